```python
import jax, jax.numpy as jnp
from jax import lax
import numpy as np

D_MODEL = 1024
BATCH = 4
SEQ = 4096
DEPTH = 4

N_META = 16
BLOCK = 128
PAD = (-N_META) % BLOCK
EPS = 1e-6
NEG_INF = -1e30

MLA_HEADS = 4
MLA_NOPE = 64
MLA_ROPE = 32
MLA_V = 64
MLA_Q_RANK = 256
MLA_KV_RANK = 128
ROPE_THETA = 10000.0

RW_HEADS = 4
RW_HEAD = 64
RW_W = RW_HEADS * RW_HEAD
RW_DECAY_LORA = 64
RW_A_LORA = 64
RW_G_LORA = 128
RW_GN_EPS = 64e-5
RW_IN = 3 * RW_W + RW_DECAY_LORA + RW_A_LORA + RW_G_LORA
RW_SPLITS = [RW_W, 2 * RW_W, 3 * RW_W, 3 * RW_W + RW_DECAY_LORA, 3 * RW_W + RW_DECAY_LORA + RW_A_LORA]

SB_HEADS = 4
SB_HEAD = 64
SB_W = SB_HEADS * SB_HEAD

GLA_HEADS = 4
GLA_DK = 32
GLA_DV = 64
GLA_GATE_LORA = 16
GLA_TAU = 16.0
GLA_QK = GLA_HEADS * GLA_DK
GLA_W = GLA_HEADS * GLA_DV
GLA_IN = 2 * GLA_QK + GLA_W + GLA_GATE_LORA + GLA_W
GLA_SPLITS = [GLA_QK, 2 * GLA_QK, 2 * GLA_QK + GLA_W, 2 * GLA_QK + GLA_W + GLA_GATE_LORA]

N_BRANCH = 4
BRANCH_W = 256
IN_SIZES = [MLA_Q_RANK, MLA_KV_RANK, MLA_ROPE, RW_IN, 3 * SB_W, GLA_IN, N_BRANCH * D_MODEL]
IN_SPLITS = [int(s) for s in np.cumsum(IN_SIZES)[:-1]]
IN_TOTAL = int(sum(IN_SIZES))

D_FF = 2816
CONV_W = 3

kernel_name = "hybrid_gated_merge_mla_rwkv7_stickbreak_gla_convffn"


def rmsnorm(x, g):
    xf = x.astype(jnp.float32)
    y = xf * lax.rsqrt(jnp.mean(xf * xf, axis=-1, keepdims=True) + EPS)
    return (y * g).astype(x.dtype)


def shift(t):
    return jnp.pad(t[:, :-1], ((0, 0), (1, 0), (0, 0)))


def pad_front(t):
    return jnp.pad(t, ((0, 0), (PAD, 0)) + ((0, 0),) * (t.ndim - 2))


def rope(x, pos):
    half = x.shape[-1] // 2
    freqs = ROPE_THETA ** (-jnp.arange(half, dtype=jnp.float32) / half)
    ang = pos.astype(jnp.float32)[:, None] * freqs[None, :]
    cos = jnp.cos(ang)[None, :, None, :]
    sin = jnp.sin(ang)[None, :, None, :]
    x1, x2 = x[..., :half], x[..., half:]
    return jnp.concatenate([x1 * cos - x2 * sin, x1 * sin + x2 * cos], axis=-1).astype(x.dtype)


def sweep_blocks(weights_fn, q, k, v):
    B, Lp, H, dk = q.shape
    nb = Lp // BLOCK
    scale = dk ** -0.5
    k_pos = jnp.arange(Lp)[None, :]
    qb = q.reshape(B, nb, BLOCK, H, dk).swapaxes(0, 1)

    def one(args):
        q_blk, i = args
        z = jnp.einsum('bqhd,bkhd->bhqk', q_blk, k).astype(jnp.float32) * scale
        q_pos = (i * BLOCK + jnp.arange(BLOCK))[:, None]
        w = weights_fn(z, q_pos, k_pos)
        return jnp.einsum('bhqk,bkhd->bqhd', w.astype(v.dtype), v)

    out = lax.map(one, (qb, jnp.arange(nb)))
    return out.swapaxes(0, 1).reshape(B, Lp, H, v.shape[-1])


def softmax_weights(z, q_pos, k_pos):
    mask = (k_pos <= q_pos) & (k_pos >= PAD)
    return jax.nn.softmax(jnp.where(mask, z, NEG_INF), axis=-1)


def stick_breaking_weights(z, q_pos, k_pos):
    mask = (k_pos < q_pos) & (k_pos >= PAD)
    log_keep = jnp.where(mask, jax.nn.log_sigmoid(-z), 0.0)
    log_later = lax.cumsum(log_keep, axis=3, reverse=True) - log_keep
    return jnp.where(mask, jnp.exp(jax.nn.log_sigmoid(z) + log_later), 0.0)


def gla_chunked(q, k, v, log_a):
    B, Lp, H, dk = q.shape
    dv = v.shape[-1]
    nc = Lp // BLOCK

    def chunks(t):
        return t.astype(jnp.float32).reshape(B, nc, BLOCK, H, t.shape[-1]).transpose(1, 0, 3, 2, 4)

    causal = jnp.tril(jnp.ones((BLOCK, BLOCK), dtype=bool))[:, :, None]

    def step(S, inp):
        qc, kc, vc, gc = inp
        b = jnp.cumsum(gc, axis=2)
        inter = jnp.einsum('bhik,bhkv->bhiv', qc * jnp.exp(b), S)
        rel = jnp.exp(jnp.where(causal, b[:, :, :, None, :] - b[:, :, None, :, :], NEG_INF))
        scores = jnp.einsum('bhik,bhisk->bhis', qc, rel * kc[:, :, None, :, :])
        intra = jnp.einsum('bhis,bhsv->bhiv', scores, vc)
        b_end = b[:, :, -1:, :]
        S = S * jnp.exp(b_end[:, :, 0, :, None]) + jnp.einsum('bhsk,bhsv->bhkv', kc * jnp.exp(b_end - b), vc)
        return S, inter + intra

    S0 = jnp.zeros((B, H, dk, dv), jnp.float32)
    _, o = lax.scan(step, S0, (chunks(q), chunks(k), chunks(v), chunks(log_a)))
    return o.transpose(1, 0, 3, 2, 4).reshape(B, Lp, H, dv).astype(v.dtype)


def mla_branch(cq, ckv, kr, pos, q_norm, w_uq, kv_norm, w_ukv):
    B, L, _ = cq.shape
    q = (rmsnorm(cq, q_norm) @ w_uq).reshape(B, L, MLA_HEADS, MLA_NOPE + MLA_ROPE)
    kv = (rmsnorm(ckv, kv_norm) @ w_ukv).reshape(B, L, MLA_HEADS, MLA_NOPE + MLA_V)
    k_rope = jnp.broadcast_to(rope(kr[:, :, None, :], pos), (B, L, MLA_HEADS, MLA_ROPE))
    q = jnp.concatenate([q[..., :MLA_NOPE], rope(q[..., MLA_NOPE:], pos)], axis=-1)
    k = jnp.concatenate([kv[..., :MLA_NOPE], k_rope], axis=-1)
    v = kv[..., MLA_NOPE:]
    o = sweep_blocks(softmax_weights, pad_front(q), pad_front(k), pad_front(v))[:, PAD:]
    return o.reshape(B, L, MLA_HEADS * MLA_V)


def rwkv7_branch(z_rw, mu, w0, w2, a0, a2, g2, k_k, k_a, r_k, ln_w, ln_b):
    B, L, _ = z_rw.shape
    z = z_rw + (shift(z_rw) - z_rw) * mu
    r, k, v, wl, al, gl = jnp.split(z, RW_SPLITS, axis=-1)
    w = -jax.nn.softplus(-(w0 + jnp.tanh(wl) @ w2)) - 0.5
    decay = jnp.exp(-jnp.exp(w.astype(jnp.float32)))
    a = jax.nn.sigmoid(a0 + al @ a2)
    g = jax.nn.sigmoid(gl) @ g2

    def heads(t):
        return t.astype(jnp.float32).reshape(B, L, RW_HEADS, RW_HEAD)

    kk = heads(k * k_k)
    kk = kk / jnp.maximum(jnp.linalg.norm(kk, axis=-1, keepdims=True), 1e-12)
    a = heads(a)
    k = heads(k) * (1.0 + (a - 1.0) * k_a.astype(jnp.float32).reshape(RW_HEADS, RW_HEAD))
    r, v, decay = heads(r), heads(v), heads(decay)

    def step(S, inp):
        r_t, w_t, k_t, v_t, kk_t, a_t = inp
        s_kk = jnp.einsum('bhvk,bhk->bhv', S, kk_t)
        S = S * w_t[:, :, None, :] - s_kk[..., None] * (kk_t * a_t)[:, :, None, :] + v_t[..., None] * k_t[:, :, None, :]
        return S, jnp.einsum('bhvk,bhk->bhv', S, r_t)

    tm = lambda t: t.swapaxes(0, 1)
    S0 = jnp.zeros((B, RW_HEADS, RW_HEAD, RW_HEAD), jnp.float32)
    _, y = lax.scan(step, S0, (tm(r), tm(decay), tm(k), tm(v), tm(kk), tm(a)))
    y = tm(y)
    mean = jnp.mean(y, axis=-1, keepdims=True)
    var = jnp.var(y, axis=-1, keepdims=True)
    y = (y - mean) * lax.rsqrt(var + RW_GN_EPS)
    y = y * ln_w.reshape(RW_HEADS, RW_HEAD) + ln_b.reshape(RW_HEADS, RW_HEAD)
    y = y + jnp.sum(r * k * r_k, axis=-1, keepdims=True) * v
    return (y.reshape(B, L, RW_W) * g).astype(z_rw.dtype)


def stick_breaking_branch(z_sb):
    B, L, _ = z_sb.shape
    q, k, v = [t.reshape(B, L, SB_HEADS, SB_HEAD) for t in jnp.split(z_sb, 3, axis=-1)]
    o = sweep_blocks(stick_breaking_weights, pad_front(q), pad_front(k), pad_front(v))[:, PAD:]
    return o.reshape(B, L, SB_W)


def gla_branch(z_gla, a2, a_b, norm_g):
    B, L, _ = z_gla.shape
    q, k, v, al, r = jnp.split(z_gla, GLA_SPLITS, axis=-1)
    q = q.reshape(B, L, GLA_HEADS, GLA_DK) * GLA_DK ** -0.5
    k = k.reshape(B, L, GLA_HEADS, GLA_DK)
    v = v.reshape(B, L, GLA_HEADS, GLA_DV)
    log_a = (jax.nn.log_sigmoid((al @ a2 + a_b).astype(jnp.float32)) / GLA_TAU).reshape(B, L, GLA_HEADS, GLA_DK)
    o = gla_chunked(pad_front(q), pad_front(k), pad_front(v), pad_front(log_a))[:, PAD:]
    o = rmsnorm(o, norm_g.reshape(GLA_HEADS, GLA_DV))
    return o.reshape(B, L, GLA_W) * jax.nn.silu(r)


def token_mixing(n, pos, w_in, mla_q_norm, mla_w_uq, mla_kv_norm, mla_w_ukv,
                 rw_mu, rw_w0, rw_w2, rw_a0, rw_a2, rw_g2, rw_k_k, rw_k_a, rw_r_k, rw_ln_w, rw_ln_b,
                 gla_a2, gla_a_b, gla_norm, gate_b, w_branch, w_out):
    B, L, _ = n.shape
    cq, ckv, kr, z_rw, z_sb, z_gla, gate_logits = jnp.split(n @ w_in, IN_SPLITS, axis=-1)
    y_mla = mla_branch(cq, ckv, kr, pos, mla_q_norm, mla_w_uq, mla_kv_norm, mla_w_ukv)
    y_rw = rwkv7_branch(z_rw, rw_mu, rw_w0, rw_w2, rw_a0, rw_a2, rw_g2, rw_k_k, rw_k_a, rw_r_k, rw_ln_w, rw_ln_b)
    y_sb = stick_breaking_branch(z_sb)
    y_gla = gla_branch(z_gla, gla_a2, gla_a_b, gla_norm)
    y = jnp.stack([y_mla, y_rw, y_sb, y_gla], axis=2)
    branch = jnp.einsum('blnc,ncd->blnd', y, w_branch)
    gates = jax.nn.sigmoid(gate_logits.reshape(B, L, N_BRANCH, D_MODEL) + gate_b)
    return jnp.sum(gates * branch, axis=2) @ w_out


def conv_ffn(n, w_ffn_in, conv_w, conv_b, w_ffn_out):
    L = n.shape[1]
    a, u = jnp.split(n @ w_ffn_in, 2, axis=-1)
    a_pad = jnp.pad(a, ((0, 0), (CONV_W - 1, 0), (0, 0)))
    a = conv_b + sum(a_pad[:, j:j + L] * conv_w[j] for j in range(CONV_W))
    return (jax.nn.silu(a) * u) @ w_ffn_out


def setup_inputs(seed: int = 0) -> dict:
    key = jax.random.key(seed)
    ks = iter(jax.random.split(key, 40))

    def nrm(shape, scale):
        return scale * jax.random.normal(next(ks), shape, jnp.float32)

    def gain(shape):
        return 1.0 + nrm(shape, 0.02)

    return {
        "x": nrm((BATCH, SEQ, D_MODEL), 1.0),
        "meta_tokens": nrm((N_META, D_MODEL), 1.0),
        "norm_mix": gain((DEPTH, D_MODEL)),
        "w_in": nrm((DEPTH, D_MODEL, IN_TOTAL), D_MODEL ** -0.5),
        "mla_q_norm": gain((DEPTH, MLA_Q_RANK)),
        "mla_w_uq": nrm((DEPTH, MLA_Q_RANK, MLA_HEADS * (MLA_NOPE + MLA_ROPE)), MLA_Q_RANK ** -0.5),
        "mla_kv_norm": gain((DEPTH, MLA_KV_RANK)),
        "mla_w_ukv": nrm((DEPTH, MLA_KV_RANK, MLA_HEADS * (MLA_NOPE + MLA_V)), MLA_KV_RANK ** -0.5),
        "rw_mu": jax.random.uniform(next(ks), (DEPTH, RW_IN), jnp.float32),
        "rw_w0": nrm((DEPTH, RW_W), 0.5),
        "rw_w2": nrm((DEPTH, RW_DECAY_LORA, RW_W), RW_DECAY_LORA ** -0.5),
        "rw_a0": nrm((DEPTH, RW_W), 0.1),
        "rw_a2": nrm((DEPTH, RW_A_LORA, RW_W), RW_A_LORA ** -0.5),
        "rw_g2": nrm((DEPTH, RW_G_LORA, RW_W), RW_G_LORA ** -0.5),
        "rw_k_k": 0.85 + nrm((DEPTH, RW_W), 0.02),
        "rw_k_a": 1.0 + nrm((DEPTH, RW_W), 0.02),
        "rw_r_k": nrm((DEPTH, RW_HEADS, RW_HEAD), 0.1),
        "rw_ln_w": gain((DEPTH, RW_W)),
        "rw_ln_b": nrm((DEPTH, RW_W), 0.02),
        "gla_a2": nrm((DEPTH, GLA_GATE_LORA, GLA_QK), GLA_GATE_LORA ** -0.5),
        "gla_a_b": nrm((DEPTH, GLA_QK), 0.1),
        "gla_norm": gain((DEPTH, GLA_W)),
        "gate_b": nrm((DEPTH, N_BRANCH, D_MODEL), 0.02),
        "w_branch": nrm((DEPTH, N_BRANCH, BRANCH_W, D_MODEL), BRANCH_W ** -0.5),
        "w_out": nrm((DEPTH, D_MODEL, D_MODEL), D_MODEL ** -0.5),
        "norm_ffn": gain((DEPTH, D_MODEL)),
        "w_ffn_in": nrm((DEPTH, D_MODEL, 2 * D_FF), D_MODEL ** -0.5),
        "ffn_conv_w": nrm((DEPTH, CONV_W, D_FF), CONV_W ** -0.5),
        "ffn_conv_b": nrm((DEPTH, D_FF), 0.02),
        "w_ffn_out": nrm((DEPTH, D_FF, D_MODEL), D_FF ** -0.5),
        "norm_final": gain((D_MODEL,)),
    }


def reference(x, meta_tokens, norm_mix, w_in, mla_q_norm, mla_w_uq, mla_kv_norm, mla_w_ukv,
              rw_mu, rw_w0, rw_w2, rw_a0, rw_a2, rw_g2, rw_k_k, rw_k_a, rw_r_k, rw_ln_w, rw_ln_b,
              gla_a2, gla_a_b, gla_norm, gate_b, w_branch, w_out,
              norm_ffn, w_ffn_in, ffn_conv_w, ffn_conv_b, w_ffn_out, norm_final):
    B = x.shape[0]
    meta = jnp.broadcast_to(meta_tokens[None].astype(x.dtype), (B, N_META, D_MODEL))
    h = jnp.concatenate([meta, x], axis=1)
    pos = jnp.arange(h.shape[1])
    for i in range(DEPTH):
        h = h + token_mixing(rmsnorm(h, norm_mix[i]), pos, w_in[i],
                             mla_q_norm[i], mla_w_uq[i], mla_kv_norm[i], mla_w_ukv[i],
                             rw_mu[i], rw_w0[i], rw_w2[i], rw_a0[i], rw_a2[i], rw_g2[i],
                             rw_k_k[i], rw_k_a[i], rw_r_k[i], rw_ln_w[i], rw_ln_b[i],
                             gla_a2[i], gla_a_b[i], gla_norm[i], gate_b[i], w_branch[i], w_out[i])
        h = h + conv_ffn(rmsnorm(h, norm_ffn[i]), w_ffn_in[i], ffn_conv_w[i], ffn_conv_b[i], w_ffn_out[i])
    return rmsnorm(h, norm_final)[:, N_META:]
```

```python
import functools

import jax
import jax.numpy as jnp
from jax import lax
from jax.experimental import pallas as pl
from jax.experimental.pallas import tpu as pltpu

F32 = jnp.float32
BF16 = jnp.bfloat16

D_MODEL = 1024
DEPTH = 4
N_META = 16
BLOCK = 128
PAD = (-N_META) % BLOCK
EPS = 1e-6
NEG_INF = -1e30

N_HEADS = 4
MLA_NOPE = 64
MLA_ROPE = 32
MLA_Q_RANK = 256
MLA_KV_RANK = 128
ROPE_THETA = 10000.0

RW_W = 256
RW_GN_EPS = 64e-5
RW_CHUNK = 64

SB_W = 256
SB_HEAD = 64

GLA_DK = 32
GLA_QK = 128
GLA_W = 256
GLA_TAU = 16.0
GLA_CHUNK = 128

D_FF = 2816
FFN_COL_CHUNK = 1408

W_MLA, W_RW, W_SB, W_GLA, W_GATE = 640, 1024, 768, 896, 4096
IN_GROUP_WIDTHS = (W_MLA, W_RW, W_SB, W_GLA, W_GATE)
IN_GROUP_DTYPES = (BF16, F32, BF16, F32, BF16)

ROW_TILE = 384
ATTN_TILE = 384
VMEM_LIMIT = 56 * 1024 * 1024


def _bf(x):
    return x.astype(BF16)


def _dot(a, b):
    return jnp.dot(_bf(a), _bf(b), preferred_element_type=F32)


def _dot_nt(a, b):
    return lax.dot_general(_bf(a), _bf(b), (((1,), (1,)), ((), ())), preferred_element_type=F32)


def _dot_tn(a, b):
    return lax.dot_general(_bf(a), _bf(b), (((0,), (0,)), ((), ())), preferred_element_type=F32)


def _split_hi_lo(x):
    hi = _bf(x)
    lo = _bf(x - hi.astype(F32))
    return hi, lo


def _dot_exact_rhs(x, m):
    hi, lo = _split_hi_lo(x)
    return jnp.dot(hi, m, preferred_element_type=F32) + jnp.dot(lo, m, preferred_element_type=F32)


def _dot_exact_lhs(m, x):
    hi, lo = _split_hi_lo(x)
    return jnp.dot(m, hi, preferred_element_type=F32) + jnp.dot(m, lo, preferred_element_type=F32)


def _iota(shape, dim):
    return lax.broadcasted_iota(jnp.int32, shape, dim)


def _log_sigmoid_pair(z):
    soft = jnp.log1p(jnp.exp(-jnp.abs(z)))
    return jnp.minimum(z, 0.0) - soft, jnp.minimum(-z, 0.0) - soft


def _div_pow2(x, d):
    assert d & (d - 1) == 0
    return lax.shift_right_logical(x, d.bit_length() - 1)


def _same_segment(shape, row_seg, col_seg):
    return _div_pow2(_iota(shape, 0), row_seg) == _div_pow2(_iota(shape, 1), col_seg)


def _segment_matrix(n, seg, value):
    return jnp.where(_same_segment((n, n), seg, seg), value, 0.0).astype(BF16)


def _head_lane_mask(width, head_width, h):
    lane = _iota((1, width), 1)
    return (lane >= h * head_width) & (lane < (h + 1) * head_width)


def _stack_heads(x, head_width):
    w = x.shape[1]
    return jnp.concatenate(
        [jnp.where(_head_lane_mask(w, head_width, h), x, 0.0) for h in range(N_HEADS)], axis=0)


def _unstack_heads(xs, c):
    return xs[0:c] + xs[c:2 * c] + xs[2 * c:3 * c] + xs[3 * c:4 * c]


def _rms(x, eps):
    return x * lax.rsqrt(jnp.mean(x * x, axis=-1, keepdims=True) + eps)


def _const_spec(shape):
    nd = len(shape)
    return pl.BlockSpec(shape, lambda *_: (0,) * nd)


def _params(sem, vmem=VMEM_LIMIT):
    return pltpu.CompilerParams(dimension_semantics=sem, vmem_limit_bytes=vmem)


def _in_proj_kernel(h_ref, g_ref, w_ref, *out_refs):
    n = _bf(_rms(h_ref[...], EPS) * g_ref[...])
    off = 0
    for o_ref, width in zip(out_refs, IN_GROUP_WIDTHS):
        for c0 in range(0, width, 1024):
            c1 = min(c0 + 1024, width)
            o_ref[:, c0:c1] = jnp.dot(
                n, w_ref[:, off + c0:off + c1], preferred_element_type=F32).astype(o_ref.dtype)
        off += width


def _in_proj(h, g, w_all):
    t = h.shape[0]
    tm = ROW_TILE
    return pl.pallas_call(
        _in_proj_kernel,
        grid=(t // tm,),
        in_specs=[pl.BlockSpec((tm, D_MODEL), lambda i: (i, 0)),
                  _const_spec((1, D_MODEL)),
                  _const_spec(w_all.shape)],
        out_specs=[pl.BlockSpec((tm, w), lambda i: (i, 0)) for w in IN_GROUP_WIDTHS],
        out_shape=[jax.ShapeDtypeStruct((t, w), dt) for w, dt in zip(IN_GROUP_WIDTHS, IN_GROUP_DTYPES)],
        compiler_params=_params(("parallel",)),
        name="in_proj",
    )(h, g, w_all)


def _mla_prep_kernel(z_ref, qg_ref, kvg_ref, wq_ref, wkv_ref, cos_ref, sin_ref, q_ref, k_ref, v_ref):
    z = z_ref[...].astype(F32)
    nq = _rms(z[:, 0:256], EPS) * qg_ref[...]
    nkv = _rms(z[:, 256:384], EPS) * kvg_ref[...]
    cos = cos_ref[...]
    sin = sin_ref[...]
    ql = _dot(nq, wq_ref[...])
    scale = (MLA_NOPE + MLA_ROPE) ** -0.5
    q = jnp.concatenate([ql[:, 0:128], ql[:, 128:256] * cos + ql[:, 512:640] * sin,
                         ql[:, 256:384], ql[:, 384:512] * cos + ql[:, 640:768] * sin], axis=1)
    q_ref[...] = _bf(q * scale)
    kvl = _dot(nkv, wkv_ref[...])
    k_rope = z[:, 384:512] * cos + z[:, 512:640] * sin
    k_rope = jnp.where(_iota((1, 128), 1) < 2 * MLA_ROPE, k_rope, 0.0)
    k_ref[...] = _bf(jnp.concatenate([kvl[:, 0:128], k_rope, kvl[:, 128:256], k_rope], axis=1))
    v_ref[...] = _bf(kvl[:, 256:512])


def _mla_prep(z_mla, qg, kvg, wq, wkv, cos, sin, lp):
    t = z_mla.shape[0]
    tm = ROW_TILE
    nb = lp // tm
    row = lambda b, i: (b * nb + i, 0)
    return pl.pallas_call(
        _mla_prep_kernel,
        grid=(t // lp, nb),
        in_specs=[pl.BlockSpec((tm, W_MLA), row),
                  _const_spec((1, MLA_Q_RANK)), _const_spec((1, MLA_KV_RANK)),
                  _const_spec(wq.shape), _const_spec(wkv.shape),
                  pl.BlockSpec((tm, 128), lambda b, i: (i, 0)),
                  pl.BlockSpec((tm, 128), lambda b, i: (i, 0))],
        out_specs=[pl.BlockSpec((tm, 512), row), pl.BlockSpec((tm, 512), row), pl.BlockSpec((tm, 256), row)],
        out_shape=[jax.ShapeDtypeStruct((t, 512), BF16), jax.ShapeDtypeStruct((t, 512), BF16),
                   jax.ShapeDtypeStruct((t, 256), BF16)],
        compiler_params=_params(("parallel", "parallel")),
        name="mla_prep",
    )(z_mla, qg, kvg, wq, wkv, cos, sin)


def _mla_attn_kernel(q_ref, k_ref, v_ref, o_ref, m_ref, l_ref, acc_ref, *, tile):
    i = pl.program_id(1)
    row = i * tile + _iota((tile, 1), 0)
    lane = _iota((1, 256), 1)
    out = jnp.zeros((tile, 256), F32)
    for h in range(N_HEADS):
        half, slot = h // 2, h % 2
        c0 = half * 256
        head_lanes = (((lane >= slot * MLA_NOPE) & (lane < (slot + 1) * MLA_NOPE))
                      | ((lane >= 128 + slot * MLA_ROPE) & (lane < 128 + (slot + 1) * MLA_ROPE)))
        qh = jnp.where(head_lanes, q_ref[:, c0:c0 + 256], jnp.zeros((), BF16))
        m_ref[...] = jnp.full((tile, 1), NEG_INF, F32)
        l_ref[...] = jnp.zeros((tile, 1), F32)
        acc_ref[...] = jnp.zeros((tile, 256), F32)

        def body(j, _):
            start = pl.multiple_of(j * tile, tile)
            s = _dot_nt(qh, k_ref[pl.ds(start, tile), c0:c0 + 256])
            col = start + _iota((1, tile), 1)
            s = jnp.where((col <= row) & (col >= PAD), s, NEG_INF)
            m_old = m_ref[...]
            m_new = jnp.maximum(m_old, jnp.max(s, axis=-1, keepdims=True))
            p = jnp.exp(s - m_new)
            alpha = jnp.exp(m_old - m_new)
            l_ref[...] = alpha * l_ref[...] + jnp.sum(p, axis=-1, keepdims=True)
            acc_ref[...] = alpha * acc_ref[...] + _dot(p, v_ref[pl.ds(start, tile), :])
            m_ref[...] = m_new
            return 0

        lax.fori_loop(0, i + 1, body, 0)
        out = out + jnp.where(_head_lane_mask(256, 64, h), acc_ref[...] / l_ref[...], 0.0)
    o_ref[...] = out


def _mla_attn(q, k, v, lp):
    t = q.shape[0]
    tile = ATTN_TILE
    nb = lp // tile
    return pl.pallas_call(
        functools.partial(_mla_attn_kernel, tile=tile),
        grid=(t // lp, nb),
        in_specs=[pl.BlockSpec((tile, 512), lambda b, i: (b * nb + i, 0)),
                  pl.BlockSpec((lp, 512), lambda b, i: (b, 0)),
                  pl.BlockSpec((lp, 256), lambda b, i: (b, 0))],
        out_specs=pl.BlockSpec((tile, 256), lambda b, i: (b * nb + i, 0)),
        out_shape=jax.ShapeDtypeStruct((t, 256), F32),
        scratch_shapes=[pltpu.VMEM((tile, 1), F32), pltpu.VMEM((tile, 1), F32), pltpu.VMEM((tile, 256), F32)],
        compiler_params=_params(("parallel", "arbitrary")),
        name="mla_attn",
    )(q, k, v)


def _sb_attn_kernel(q_ref, k_ref, v_ref, later_ref, o_ref, c_ref, acc_ref, *, tile):
    i = pl.program_id(1)
    row = i * tile + _iota((tile, 1), 0)
    out = jnp.zeros((tile, SB_W), F32)
    for h in range(N_HEADS):
        head = _head_lane_mask(SB_W, SB_HEAD, h)
        qh = jnp.where(head, q_ref[...], jnp.zeros((), BF16))
        c_ref[...] = jnp.zeros((tile, 1), F32)
        acc_ref[...] = jnp.zeros((tile, SB_W), F32)

        def body(jj, _):
            start = pl.multiple_of((i - jj) * tile, tile)
            z = _dot_nt(qh, k_ref[pl.ds(start, tile), :]) * (SB_HEAD ** -0.5)
            col = start + _iota((1, tile), 1)
            mask = (col < row) & (col >= PAD)
            log_take, log_skip = _log_sigmoid_pair(z)
            log_keep = jnp.where(mask, log_skip, 0.0)
            later = _dot_exact_rhs(log_keep, later_ref[...])
            c = c_ref[...]
            w = jnp.where(mask, jnp.exp(log_take + later + c), 0.0)
            acc_ref[...] += _dot(w, v_ref[pl.ds(start, tile), :])
            c_ref[...] = c + jnp.sum(log_keep, axis=-1, keepdims=True)
            return 0

        lax.fori_loop(0, i + 1, body, 0)
        out = out + jnp.where(head, acc_ref[...], 0.0)
    o_ref[...] = out


def _sb_attn(z_sb, later_mat, lp):
    t = z_sb.shape[0]
    tile = ATTN_TILE
    nb = lp // tile
    return pl.pallas_call(
        functools.partial(_sb_attn_kernel, tile=tile),
        grid=(t // lp, nb),
        in_specs=[pl.BlockSpec((tile, SB_W), lambda b, i: (b * nb + i, 0)),
                  pl.BlockSpec((lp, SB_W), lambda b, i: (b, 1)),
                  pl.BlockSpec((lp, SB_W), lambda b, i: (b, 2)),
                  _const_spec((tile, tile))],
        out_specs=pl.BlockSpec((tile, SB_W), lambda b, i: (b * nb + i, 0)),
        out_shape=jax.ShapeDtypeStruct((t, SB_W), F32),
        scratch_shapes=[pltpu.VMEM((tile, 1), F32), pltpu.VMEM((tile, SB_W), F32)],
        compiler_params=_params(("parallel", "arbitrary")),
        name="sb_attn",
    )(z_sb, z_sb, z_sb, later_mat)


def _rw_prep_kernel(z_ref, zprev_ref, mu_ref, w0_ref, w2_ref, a0_ref, a2_ref, g2_ref, kk_ref, ka_ref, rk_ref,
                    r_o, k_o, v_o, kap_o, beta_o, lw_o, g_o, bonus_o):
    i = pl.program_id(1)
    z = z_ref[...]
    tm = z.shape[0]
    prev = jnp.where(i == 0, 0.0, zprev_ref[7:8, :])
    shifted = jnp.where(_iota((tm, 1), 0) == 0, prev, pltpu.roll(z, 1, 0))
    zz = z + (shifted - z) * mu_ref[...]
    r, k, v = zz[:, 0:256], zz[:, 256:512], zz[:, 512:768]
    lora_in = zz[:, 768:896]
    u = w0_ref[...] + _dot(jnp.tanh(lora_in), w2_ref[...])
    w = jnp.minimum(u, 0.0) - jnp.log1p(jnp.exp(-jnp.abs(u))) - 0.5
    a = jax.nn.sigmoid(a0_ref[...] + _dot(lora_in, a2_ref[...]))
    seg = _segment_matrix(RW_W, 64, 1.0)
    kx = k * kk_ref[...]
    kap = kx / jnp.maximum(jnp.sqrt(_dot_exact_rhs(kx * kx, seg)), 1e-12)
    kmod = k * (1.0 + (a - 1.0) * ka_ref[...])
    r_o[...] = r
    k_o[...] = kmod
    v_o[...] = v
    kap_o[...] = kap
    beta_o[...] = kap * a
    lw_o[...] = -jnp.exp(w)
    g_o[...] = _dot(jax.nn.sigmoid(zz[:, 896:1024]), g2_ref[...])
    bonus_o[...] = _dot_exact_rhs(r * kmod * rk_ref[...], seg) * v


def _rw_prep(z_rw, p, lp):
    t = z_rw.shape[0]
    tm = ROW_TILE
    nb = lp // tm
    row = lambda b, i: (b * nb + i, 0)
    prev = lambda b, i: (jnp.maximum(b * (lp // 8) + i * (tm // 8) - 1, 0), 0)
    consts = [p["rw_mu"], p["rw_w0"], p["rw_w2"], p["rw_a0"], p["rw_a2"], p["rw_g2"],
              p["rw_k_k"], p["rw_k_a"], p["rw_r_k"]]
    return pl.pallas_call(
        _rw_prep_kernel,
        grid=(t // lp, nb),
        in_specs=[pl.BlockSpec((tm, W_RW), row), pl.BlockSpec((8, W_RW), prev)]
        + [_const_spec(c.shape) for c in consts],
        out_specs=[pl.BlockSpec((tm, RW_W), row)] * 8,
        out_shape=[jax.ShapeDtypeStruct((t, RW_W), F32)] * 8,
        compiler_params=_params(("parallel", "parallel")),
        name="rw_prep",
    )(z_rw, z_rw, *consts)


def _rw_scan_kernel(r_ref, k_ref, v_ref, kap_ref, beta_ref, lw_ref, y_ref, ht_ref):
    c = RW_CHUNK
    n = N_HEADS * c

    @pl.when(pl.program_id(1) == 0)
    def _():
        ht_ref[...] = jnp.zeros_like(ht_ref)

    r, k, v = r_ref[...], k_ref[...], v_ref[...]
    kap, beta, lw = kap_ref[...], beta_ref[...], lw_ref[...]
    tri = jnp.where(_iota((c, c), 0) >= _iota((c, c), 1), 1.0, 0.0).astype(BF16)
    b = _dot_exact_lhs(tri, lw)
    b_end = b[c - 1:c, :]
    grow = jnp.exp(-b)
    kap_t = kap * jnp.exp(b - lw)
    r_t = r * jnp.exp(b)
    to_end = jnp.exp(b_end - b)

    lhs = jnp.concatenate([_stack_heads(kap_t, 64), _stack_heads(r_t, 64)], axis=0)
    rhs = jnp.concatenate([_stack_heads(beta * grow, 64), _stack_heads(k * grow, 64)], axis=0)
    pair = _dot_nt(lhs, rhs)
    ti = _iota((n, n), 0) & (c - 1)
    si = _iota((n, n), 1) & (c - 1)
    a_kb = jnp.where(ti > si, pair[0:n, 0:n], 0.0)
    a_kk = jnp.where(ti > si, pair[0:n, n:2 * n], 0.0)
    a_rb = jnp.where(ti >= si, pair[n:2 * n, 0:n], 0.0)
    a_rk = jnp.where(ti >= si, pair[n:2 * n, n:2 * n], 0.0)

    eye = jnp.where(_iota((n, n), 0) == _iota((n, n), 1), 1.0, 0.0)
    inv = eye - a_kb
    power = _dot(a_kb, a_kb)
    span = 2
    while span < c:
        inv = inv + _dot(inv, power)
        span *= 2
        if span < c:
            power = _dot(power, power)

    v_st = _stack_heads(v, 64)
    w_mat = _unstack_heads(_dot(inv, _stack_heads(kap_t, 64)), c)
    u_v = _unstack_heads(_dot(inv, _dot(a_kk, v_st)), c)
    y_v = _unstack_heads(_dot(a_rk, v_st), c)

    ht = ht_ref[...]
    from_state = _dot_nt(jnp.concatenate([w_mat, r_t], axis=0), ht)
    u = from_state[0:c] + u_v
    y_ref[...] = from_state[c:2 * c] + y_v - _unstack_heads(_dot(a_rb, _stack_heads(u, 64)), c)

    upd = _dot_tn(jnp.concatenate([v, u], axis=0), jnp.concatenate([k * to_end, -(beta * to_end)], axis=0))
    ht_ref[...] = ht * jnp.exp(b_end) + jnp.where(_same_segment((RW_W, RW_W), 64, 64), upd, 0.0)


def _rw_scan(r, k, v, kap, beta, lw, lp):
    t = r.shape[0]
    c = RW_CHUNK
    nb = lp // c
    spec = pl.BlockSpec((c, RW_W), lambda b, i: (b * nb + i, 0))
    return pl.pallas_call(
        _rw_scan_kernel,
        grid=(t // lp, nb),
        in_specs=[spec] * 6,
        out_specs=spec,
        out_shape=jax.ShapeDtypeStruct((t, RW_W), F32),
        scratch_shapes=[pltpu.VMEM((RW_W, RW_W), F32)],
        compiler_params=_params(("arbitrary", "arbitrary")),
        name="rw_scan",
    )(r, k, v, kap, beta, lw)


def _gla_kernel(q_ref, k_ref, v_ref, al_ref, a2_ref, ab_ref, o_ref, st_ref):
    c = GLA_CHUNK

    @pl.when(pl.program_id(1) == 0)
    def _():
        st_ref[...] = jnp.zeros_like(st_ref)

    q = q_ref[...] * (GLA_DK ** -0.5)
    k, v = k_ref[...], v_ref[...]
    x = _dot(al_ref[...], a2_ref[...]) + ab_ref[...]
    log_a = (jnp.minimum(x, 0.0) - jnp.log1p(jnp.exp(-jnp.abs(x)))) * (1.0 / GLA_TAU)
    tri = jnp.where(_iota((c, c), 0) >= _iota((c, c), 1), 1.0, 0.0).astype(BF16)
    b = _dot_exact_lhs(tri, log_a)
    b_mid = b[c // 2 - 1:c // 2, :]
    b_end = b[c - 1:c, :]

    st = st_ref[...]
    inter = _dot_nt(q * jnp.exp(b), st)
    scores = _dot_nt(_stack_heads(q * jnp.exp(b - b_mid), GLA_DK), k * jnp.exp(b_mid - b))
    n = N_HEADS * c
    scores = jnp.where((_iota((n, c), 0) & (c - 1)) >= _iota((n, c), 1), scores, 0.0)
    per_head = _dot(scores, v)
    intra = jnp.zeros((c, GLA_W), F32)
    for h in range(N_HEADS):
        intra = intra + jnp.where(_head_lane_mask(GLA_W, 64, h), per_head[h * c:(h + 1) * c], 0.0)
    o_ref[...] = inter + intra

    upd = _dot_tn(v, k * jnp.exp(b_end - b))
    st_ref[...] = st * jnp.exp(b_end) + jnp.where(_same_segment((GLA_W, GLA_QK), 64, GLA_DK), upd, 0.0)


def _gla(z_gla, a2, ab, lp):
    t = z_gla.shape[0]
    c = GLA_CHUNK
    nb = lp // c
    return pl.pallas_call(
        _gla_kernel,
        grid=(t // lp, nb),
        in_specs=[pl.BlockSpec((c, 128), lambda b, i: (b * nb + i, 0)),
                  pl.BlockSpec((c, 128), lambda b, i: (b * nb + i, 1)),
                  pl.BlockSpec((c, 256), lambda b, i: (b * nb + i, 1)),
                  pl.BlockSpec((c, 128), lambda b, i: (b * nb + i, 6)),
                  _const_spec(a2.shape), _const_spec(ab.shape)],
        out_specs=pl.BlockSpec((c, GLA_W), lambda b, i: (b * nb + i, 0)),
        out_shape=jax.ShapeDtypeStruct((t, GLA_W), F32),
        scratch_shapes=[pltpu.VMEM((GLA_W, GLA_QK), F32)],
        compiler_params=_params(("arbitrary", "arbitrary")),
        name="gla",
    )(z_gla, z_gla, z_gla, z_gla, a2, ab)


def _merge_kernel(h_ref, ymla_ref, yrw_ref, bonus_ref, g_ref, ysb_ref, ogla_ref, rgla_ref, zg_ref,
                  lnw_ref, lnb_ref, gn_ref, gb_ref, wb_ref, wo_ref, o_ref, *, tiles_per_seq):
    tm = h_ref.shape[0]
    avg = _segment_matrix(256, 64, 1.0 / 64)

    y = yrw_ref[...]
    d = y - _dot_exact_rhs(y, avg)
    var = _dot_exact_rhs(d * d, avg)
    y_rw = (d * lax.rsqrt(var + RW_GN_EPS) * lnw_ref[...] + lnb_ref[...] + bonus_ref[...]) * g_ref[...]

    o = ogla_ref[...]
    r = rgla_ref[...]
    y_gla = o * lax.rsqrt(_dot_exact_rhs(o * o, avg) + EPS) * gn_ref[...] * (r * jax.nn.sigmoid(r))

    acc = jnp.zeros((tm, D_MODEL), F32)
    for m, y_m in enumerate((ymla_ref[...], y_rw, ysb_ref[...], y_gla)):
        gate = jax.nn.sigmoid(zg_ref[:, m * D_MODEL:(m + 1) * D_MODEL].astype(F32) + gb_ref[m:m + 1, :])
        acc = acc + gate * _dot(y_m, wb_ref[m])
    delta = _dot(acc, wo_ref[...])
    row = (pl.program_id(0) % tiles_per_seq) * tm + _iota((tm, 1), 0)
    o_ref[...] = h_ref[...] + jnp.where(row >= PAD, delta, 0.0)


def _merge(h, y_mla, y_rw, bonus, g, y_sb, o_gla, z_gla, z_gate, p, lp):
    t = h.shape[0]
    tm = ROW_TILE
    row = lambda i: (i, 0)
    w256 = pl.BlockSpec((tm, 256), row)
    consts = [p["rw_ln_w"], p["rw_ln_b"], p["gla_norm"], p["gate_b"], p["w_branch"], p["w_out"]]
    return pl.pallas_call(
        functools.partial(_merge_kernel, tiles_per_seq=lp // tm),
        grid=(t // tm,),
        in_specs=[pl.BlockSpec((tm, D_MODEL), row), w256, w256, w256, w256, w256, w256,
                  pl.BlockSpec((tm, 256), lambda i: (i, 2)),
                  pl.BlockSpec((tm, W_GATE), row)]
        + [_const_spec(c.shape) for c in consts],
        out_specs=pl.BlockSpec((tm, D_MODEL), row),
        out_shape=jax.ShapeDtypeStruct((t, D_MODEL), F32),
        compiler_params=_params(("parallel",)),
        name="merge",
    )(h, y_mla, y_rw, bonus, g, y_sb, o_gla, z_gla, z_gate, *consts)


def _ffn_kernel(h_ref, g_ref, win_ref, cw_ref, cb_ref, wout_ref, o_ref, tail_ref):
    @pl.when(pl.program_id(1) == 0)
    def _():
        tail_ref[...] = jnp.zeros_like(tail_ref)

    x = h_ref[...]
    tm = x.shape[0]
    n = _bf(_rms(x, EPS) * g_ref[...])
    rowi = _iota((tm, 1), 0)
    acc = jnp.zeros((tm, D_MODEL), F32)
    for c0 in range(0, D_FF, FFN_COL_CHUNK):
        c1 = c0 + FFN_COL_CHUNK
        a = jnp.dot(n, win_ref[:, c0:c1], preferred_element_type=F32)
        u = jnp.dot(n, win_ref[:, D_FF + c0:D_FF + c1], preferred_element_type=F32)
        prev1 = tail_ref[7:8, c0:c1]
        prev2 = tail_ref[6:7, c0:c1]
        a1 = jnp.where(rowi == 0, prev1, pltpu.roll(a, 1, 0))
        a2 = jnp.where(rowi == 0, prev2, jnp.where(rowi == 1, prev1, pltpu.roll(a, 2, 0)))
        tail_ref[:, c0:c1] = a[tm - 8:tm, :]
        conv = cb_ref[:, c0:c1] + cw_ref[0:1, c0:c1] * a2 + cw_ref[1:2, c0:c1] * a1 + cw_ref[2:3, c0:c1] * a
        acc = acc + _dot(conv * jax.nn.sigmoid(conv) * u, wout_ref[c0:c1, :])
    o_ref[...] = x + acc


def _ffn(h, g, w_in, conv_w, conv_b, w_out, lp):
    t = h.shape[0]
    tm = ROW_TILE
    nb = lp // tm
    row = lambda b, i: (b * nb + i, 0)
    return pl.pallas_call(
        _ffn_kernel,
        grid=(t // lp, nb),
        in_specs=[pl.BlockSpec((tm, D_MODEL), row), _const_spec((1, D_MODEL)), _const_spec(w_in.shape),
                  _const_spec(conv_w.shape), _const_spec(conv_b.shape), _const_spec(w_out.shape)],
        out_specs=pl.BlockSpec((tm, D_MODEL), row),
        out_shape=jax.ShapeDtypeStruct((t, D_MODEL), F32),
        scratch_shapes=[pltpu.VMEM((8, D_FF), F32)],
        compiler_params=_params(("arbitrary", "arbitrary")),
        name="conv_ffn",
    )(h, g, w_in, conv_w, conv_b, w_out)


def _final_norm_kernel(h_ref, g_ref, o_ref):
    o_ref[0] = _rms(h_ref[0], EPS) * g_ref[...]


def _final_norm(h3, g, seq):
    b = h3.shape[0]
    first = (PAD + N_META) // BLOCK
    return pl.pallas_call(
        _final_norm_kernel,
        grid=(b, seq // BLOCK),
        in_specs=[pl.BlockSpec((1, BLOCK, D_MODEL), lambda bi, i: (bi, i + first, 0)), _const_spec((1, D_MODEL))],
        out_specs=pl.BlockSpec((1, BLOCK, D_MODEL), lambda bi, i: (bi, i, 0)),
        out_shape=jax.ShapeDtypeStruct((b, seq, D_MODEL), F32),
        compiler_params=_params(("parallel", "parallel")),
        name="final_norm",
    )(h3, g)


def _rope_swap(w):
    half = w.shape[-1] // 2
    return jnp.concatenate([-w[..., half:], w[..., :half]], axis=-1)


def _layout_params(w_in, mla_w_uq, mla_w_ukv, rw_w2, rw_a2, gla_a2):
    kr = w_in[..., 384:416]
    w_mla = jnp.concatenate([w_in[..., 0:384], jnp.tile(kr, (1, 1, 4)), jnp.tile(_rope_swap(kr), (1, 1, 4))], axis=-1)
    zg = w_in[..., 2208:2992]
    w_gla = jnp.concatenate([zg[..., 0:512], zg[..., 528:784], zg[..., 512:528],
                             jnp.zeros(zg.shape[:-1] + (W_GLA - 784,), zg.dtype)], axis=-1)
    w_all = _bf(jnp.concatenate([w_mla, w_in[..., 416:1440], w_in[..., 1440:2208], w_gla, w_in[..., 2992:]], axis=-1))

    depth = w_in.shape[0]
    wuq = mla_w_uq.reshape(depth, MLA_Q_RANK, N_HEADS, MLA_NOPE + MLA_ROPE)
    nope, rope = wuq[..., :MLA_NOPE], wuq[..., MLA_NOPE:]
    rope_sw = _rope_swap(rope)
    zeros64 = jnp.zeros((depth, MLA_Q_RANK, 64), w_in.dtype)
    pair = lambda x, a, b: jnp.concatenate([x[:, :, a], x[:, :, b]], axis=-1)
    rope_pair = lambda x, a, b: jnp.concatenate([x[:, :, a], x[:, :, b], zeros64], axis=-1)
    wq = _bf(jnp.concatenate([pair(nope, 0, 1), rope_pair(rope, 0, 1), pair(nope, 2, 3), rope_pair(rope, 2, 3),
                              rope_pair(rope_sw, 0, 1), rope_pair(rope_sw, 2, 3)], axis=-1))
    wukv = mla_w_ukv.reshape(depth, MLA_KV_RANK, N_HEADS, 128)
    wkv = _bf(jnp.concatenate([wukv[..., :64].reshape(depth, MLA_KV_RANK, 256),
                               wukv[..., 64:].reshape(depth, MLA_KV_RANK, 256)], axis=-1))

    z64 = jnp.zeros_like(rw_w2)
    w2 = _bf(jnp.concatenate([rw_w2, z64], axis=1))
    a2 = _bf(jnp.concatenate([z64, rw_a2], axis=1))
    gla_a2p = _bf(jnp.concatenate([gla_a2, jnp.zeros((depth, 128 - gla_a2.shape[1], GLA_QK), gla_a2.dtype)], axis=1))
    return w_all, wq, wkv, w2, a2, gla_a2p


def _rope_tables(lp):
    half = MLA_ROPE // 2
    freqs = ROPE_THETA ** (-jnp.arange(half, dtype=F32) / half)
    pos = (jnp.arange(lp) - PAD).astype(F32)
    ang = pos[:, None] * freqs[None, :]
    return jnp.tile(jnp.cos(ang), (1, 128 // half)), jnp.tile(jnp.sin(ang), (1, 128 // half))


def kernel(x, meta_tokens, norm_mix, w_in, mla_q_norm, mla_w_uq, mla_kv_norm, mla_w_ukv, rw_mu, rw_w0, rw_w2, rw_a0, rw_a2, rw_g2, rw_k_k, rw_k_a, rw_r_k, rw_ln_w, rw_ln_b, gla_a2, gla_a_b, gla_norm, gate_b, w_branch, w_out, norm_ffn, w_ffn_in, ffn_conv_w, ffn_conv_b, w_ffn_out, norm_final):
    batch, seq, _ = x.shape
    depth = w_in.shape[0]
    lp = PAD + N_META + seq
    t = batch * lp
    assert lp % ROW_TILE == 0 and lp % ATTN_TILE == 0 and lp % GLA_CHUNK == 0 and lp % RW_CHUNK == 0

    w_all, wq, wkv, rw_w2p, rw_a2p, gla_a2p = _layout_params(w_in, mla_w_uq, mla_w_ukv, rw_w2, rw_a2, gla_a2)
    w_branch_b, w_out_b, w_ffn_in_b, w_ffn_out_b, rw_g2_b = map(_bf, (w_branch, w_out, w_ffn_in, w_ffn_out, rw_g2))
    vec = lambda a: a.reshape(depth, 1, -1)
    cos, sin = _rope_tables(lp)
    tile = ATTN_TILE
    later_mat = jnp.where(jnp.arange(tile)[:, None] > jnp.arange(tile)[None, :], 1.0, 0.0).astype(BF16)

    meta = jnp.broadcast_to(meta_tokens[None].astype(x.dtype), (batch, N_META, D_MODEL))
    h = jnp.concatenate([jnp.zeros((batch, PAD, D_MODEL), x.dtype), meta, x], axis=1).reshape(t, D_MODEL)

    for i in range(depth):
        z_mla, z_rw, z_sb, z_gla, z_gate = _in_proj(h, vec(norm_mix)[i], w_all[i])
        q, k, v = _mla_prep(z_mla, vec(mla_q_norm)[i], vec(mla_kv_norm)[i], wq[i], wkv[i], cos, sin, lp)
        y_mla = _mla_attn(q, k, v, lp)
        rw = {"rw_mu": vec(rw_mu)[i], "rw_w0": vec(rw_w0)[i], "rw_w2": rw_w2p[i], "rw_a0": vec(rw_a0)[i],
              "rw_a2": rw_a2p[i], "rw_g2": rw_g2_b[i], "rw_k_k": vec(rw_k_k)[i], "rw_k_a": vec(rw_k_a)[i],
              "rw_r_k": vec(rw_r_k)[i]}
        r_, k_, v_, kap, beta, lw, g, bonus = _rw_prep(z_rw, rw, lp)
        y_rw = _rw_scan(r_, k_, v_, kap, beta, lw, lp)
        y_sb = _sb_attn(z_sb, later_mat, lp)
        o_gla = _gla(z_gla, gla_a2p[i], vec(gla_a_b)[i], lp)
        mp = {"rw_ln_w": vec(rw_ln_w)[i], "rw_ln_b": vec(rw_ln_b)[i], "gla_norm": vec(gla_norm)[i],
              "gate_b": gate_b[i], "w_branch": w_branch_b[i], "w_out": w_out_b[i]}
        h = _merge(h, y_mla, y_rw, bonus, g, y_sb, o_gla, z_gla, z_gate, mp, lp)
        h = _ffn(h, vec(norm_ffn)[i], w_ffn_in_b[i], ffn_conv_w[i], vec(ffn_conv_b)[i], w_ffn_out_b[i], lp)
    return _final_norm(h.reshape(batch, lp, D_MODEL), norm_final.reshape(1, D_MODEL), seq)
```

```python
import functools

import jax
import jax.numpy as jnp
from jax import lax
from jax.experimental import pallas as pl
from jax.experimental.pallas import tpu as pltpu

F32 = jnp.float32
BF16 = jnp.bfloat16

D_MODEL = 1024
DEPTH = 4
N_META = 16
BLOCK = 128
PAD = (-N_META) % BLOCK
EPS = 1e-6
NEG_INF = -1e30

N_HEADS = 4
MLA_NOPE = 64
MLA_ROPE = 32
MLA_Q_RANK = 256
MLA_KV_RANK = 128
ROPE_THETA = 10000.0

RW_W = 256
RW_GN_EPS = 64e-5
RW_CHUNK = 64

SB_W = 256
SB_HEAD = 64
SB_SUFFIX_SPLIT = 256

GLA_DK = 32
GLA_QK = 128
GLA_W = 256
GLA_TAU = 16.0
GLA_CHUNK = 128

D_FF = 2816
FFN_COL_CHUNK = 1408

W_MLA, W_RW, W_SB, W_GLA, W_GATE = 640, 1024, 768, 896, 4096
IN_GROUP_WIDTHS = (W_MLA, W_RW, W_SB, W_GLA, W_GATE)
IN_GROUP_DTYPES = (BF16, F32, BF16, F32, BF16)

ROW_TILE = 384
ATTN_TILE = 384
VMEM_LIMIT = 56 * 1024 * 1024


def _bf(x):
    return x.astype(BF16)


def _dot(a, b):
    return jnp.dot(_bf(a), _bf(b), preferred_element_type=F32)


def _dot_nt(a, b):
    return lax.dot_general(_bf(a), _bf(b), (((1,), (1,)), ((), ())), preferred_element_type=F32)


def _dot_tn(a, b):
    return lax.dot_general(_bf(a), _bf(b), (((0,), (0,)), ((), ())), preferred_element_type=F32)


def _split_hi_lo(x):
    hi = _bf(x)
    lo = _bf(x - hi.astype(F32))
    return hi, lo


def _dot_exact_rhs(x, m):
    hi, lo = _split_hi_lo(x)
    return jnp.dot(hi, m, preferred_element_type=F32) + jnp.dot(lo, m, preferred_element_type=F32)


def _dot_exact_lhs(m, x):
    hi, lo = _split_hi_lo(x)
    return jnp.dot(m, hi, preferred_element_type=F32) + jnp.dot(m, lo, preferred_element_type=F32)


def _iota(shape, dim):
    return lax.broadcasted_iota(jnp.int32, shape, dim)


def _log_sigmoid_pair(z):
    log_sig = jnp.minimum(z, 0.0) - jnp.log(1.0 + jnp.exp(-jnp.abs(z)))
    return log_sig, log_sig - z


def _div_pow2(x, d):
    assert d & (d - 1) == 0
    return lax.shift_right_logical(x, d.bit_length() - 1)


def _same_segment(shape, row_seg, col_seg):
    return _div_pow2(_iota(shape, 0), row_seg) == _div_pow2(_iota(shape, 1), col_seg)


def _segment_matrix(n, seg, value):
    return jnp.where(_same_segment((n, n), seg, seg), value, 0.0).astype(BF16)


def _head_lane_mask(width, head_width, h):
    lane = _iota((1, width), 1)
    return (lane >= h * head_width) & (lane < (h + 1) * head_width)


def _stack_heads(x, head_width):
    w = x.shape[1]
    return jnp.concatenate(
        [jnp.where(_head_lane_mask(w, head_width, h), x, 0.0) for h in range(N_HEADS)], axis=0)


def _unstack_heads(xs, c):
    return xs[0:c] + xs[c:2 * c] + xs[2 * c:3 * c] + xs[3 * c:4 * c]


def _rms(x, eps):
    return x * lax.rsqrt(jnp.mean(x * x, axis=-1, keepdims=True) + eps)


def _const_spec(shape):
    nd = len(shape)
    return pl.BlockSpec(shape, lambda *_: (0,) * nd)


def _params(sem, vmem=VMEM_LIMIT):
    return pltpu.CompilerParams(dimension_semantics=sem, vmem_limit_bytes=vmem)


def _in_proj_kernel(h_ref, g_ref, w_ref, *out_refs):
    n = _bf(_rms(h_ref[...], EPS) * g_ref[...])
    off = 0
    for o_ref, width in zip(out_refs, IN_GROUP_WIDTHS):
        for c0 in range(0, width, 1024):
            c1 = min(c0 + 1024, width)
            o_ref[:, c0:c1] = jnp.dot(
                n, w_ref[:, off + c0:off + c1], preferred_element_type=F32).astype(o_ref.dtype)
        off += width


def _in_proj(h, g, w_all):
    t = h.shape[0]
    tm = ROW_TILE
    return pl.pallas_call(
        _in_proj_kernel,
        grid=(t // tm,),
        in_specs=[pl.BlockSpec((tm, D_MODEL), lambda i: (i, 0)),
                  _const_spec((1, D_MODEL)),
                  _const_spec(w_all.shape)],
        out_specs=[pl.BlockSpec((tm, w), lambda i: (i, 0)) for w in IN_GROUP_WIDTHS],
        out_shape=[jax.ShapeDtypeStruct((t, w), dt) for w, dt in zip(IN_GROUP_WIDTHS, IN_GROUP_DTYPES)],
        compiler_params=_params(("parallel",)),
        name="in_proj",
    )(h, g, w_all)


def _mla_prep_kernel(z_ref, qg_ref, kvg_ref, wq_ref, wkv_ref, cos_ref, sin_ref, q_ref, k_ref, vt_ref):
    z = z_ref[...].astype(F32)
    nq = _rms(z[:, 0:256], EPS) * qg_ref[...]
    nkv = _rms(z[:, 256:384], EPS) * kvg_ref[...]
    cos = cos_ref[...]
    sin = sin_ref[...]
    ql = _dot(nq, wq_ref[...])
    scale = (MLA_NOPE + MLA_ROPE) ** -0.5
    q = jnp.concatenate([ql[:, 0:128], ql[:, 128:256] * cos + ql[:, 512:640] * sin,
                         ql[:, 256:384], ql[:, 384:512] * cos + ql[:, 640:768] * sin], axis=1)
    q_ref[...] = _bf(q * scale)
    kvl = _dot(nkv, wkv_ref[...])
    k_rope = z[:, 384:512] * cos + z[:, 512:640] * sin
    k_rope = jnp.where(_iota((1, 128), 1) < 2 * MLA_ROPE, k_rope, 0.0)
    k_ref[...] = _bf(jnp.concatenate([kvl[:, 0:128], k_rope, kvl[:, 128:256], k_rope], axis=1))
    vt_ref[0] = _bf(kvl[:, 256:512].T)


def _mla_prep(z_mla, qg, kvg, wq, wkv, cos, sin, lp):
    t = z_mla.shape[0]
    tm = ROW_TILE
    nb = lp // tm
    row = lambda b, i: (b * nb + i, 0)
    return pl.pallas_call(
        _mla_prep_kernel,
        grid=(t // lp, nb),
        in_specs=[pl.BlockSpec((tm, W_MLA), row),
                  _const_spec((1, MLA_Q_RANK)), _const_spec((1, MLA_KV_RANK)),
                  _const_spec(wq.shape), _const_spec(wkv.shape),
                  pl.BlockSpec((tm, 128), lambda b, i: (i, 0)),
                  pl.BlockSpec((tm, 128), lambda b, i: (i, 0))],
        out_specs=[pl.BlockSpec((tm, 512), row), pl.BlockSpec((tm, 512), row),
                   pl.BlockSpec((1, 256, tm), lambda b, i: (b * nb + i, 0, 0))],
        out_shape=[jax.ShapeDtypeStruct((t, 512), BF16), jax.ShapeDtypeStruct((t, 512), BF16),
                   jax.ShapeDtypeStruct((t // tm, 256, tm), BF16)],
        compiler_params=_params(("parallel", "parallel")),
        name="mla_prep",
    )(z_mla, qg, kvg, wq, wkv, cos, sin)


def _per_head_lanes(cols):
    lane = _iota((1, 256), 1)
    return jnp.where(lane < 64, cols[0], jnp.where(lane < 128, cols[1], jnp.where(lane < 192, cols[2], cols[3])))


def _head_stacked_values(vb):
    zero = jnp.zeros((), vb.dtype)
    return jnp.concatenate([jnp.where(_head_lane_mask(256, 64, h), vb, zero) for h in range(N_HEADS)], axis=0)


def _per_head_rows(rows, n):
    return jnp.concatenate([jnp.broadcast_to(r, (64, n)) for r in rows], axis=0)


def _mla_attn_kernel(q_ref, k_ref, vt_ref, o_ref, qh_ref, m_ref, l_ref, acc_ref, *, tile):
    i = pl.program_id(1)
    q_pos = i * tile + _iota((1, tile), 1)
    lane = _iota((1, 256), 1)
    for h in range(N_HEADS):
        half, slot = h // 2, h % 2
        head_lanes = (((lane >= slot * MLA_NOPE) & (lane < (slot + 1) * MLA_NOPE))
                      | ((lane >= 128 + slot * MLA_ROPE) & (lane < 128 + (slot + 1) * MLA_ROPE)))
        qh_ref[h] = jnp.where(head_lanes, q_ref[:, half * 256:(half + 1) * 256], jnp.zeros((), BF16))
    m_ref[...] = jnp.full(m_ref.shape, NEG_INF, F32)
    l_ref[...] = jnp.zeros(l_ref.shape, F32)
    acc_ref[...] = jnp.zeros(acc_ref.shape, F32)
    value_row_head = _div_pow2(_iota((256, 1), 0), 64)

    def block(j, masked):
        start = pl.multiple_of(j * tile, tile)
        probs, alphas = [], []
        for h in range(N_HEADS):
            c0 = (h // 2) * 256
            s = _dot_nt(k_ref[pl.ds(start, tile), c0:c0 + 256], qh_ref[h])
            if masked:
                k_pos = start + _iota((tile, 1), 0)
                s = jnp.where((k_pos <= q_pos) & (k_pos >= PAD), s, NEG_INF)
            m_old = m_ref[h]
            m_new = jnp.maximum(m_old, jnp.max(s, axis=0, keepdims=True))
            p = jnp.exp(s - m_new)
            alpha = jnp.exp(m_old - m_new)
            l_ref[h] = alpha * l_ref[h] + jnp.sum(p, axis=0, keepdims=True)
            m_ref[h] = m_new
            probs.append(_bf(p))
            alphas.append(alpha)
        vt = vt_ref[j]
        zero = jnp.zeros((), BF16)
        vt_heads = jnp.concatenate([jnp.where(value_row_head == h, vt, zero) for h in range(N_HEADS)], axis=1)
        pv = jnp.dot(vt_heads, jnp.concatenate(probs, axis=0), preferred_element_type=F32)
        acc_ref[...] = _per_head_rows(alphas, tile) * acc_ref[...] + pv

    @pl.when(i > 0)
    def _():
        block(0, True)

    def interior(j, carry):
        block(j, False)
        return carry

    lax.fori_loop(1, i, interior, 0)
    block(i, True)
    o_ref[...] = (acc_ref[...] / _per_head_rows([l_ref[h] for h in range(N_HEADS)], tile)).T


def _mla_attn(q, k, vt, lp):
    t = q.shape[0]
    tile = ATTN_TILE
    nb = lp // tile
    return pl.pallas_call(
        functools.partial(_mla_attn_kernel, tile=tile),
        grid=(t // lp, nb),
        in_specs=[pl.BlockSpec((tile, 512), lambda b, i: (b * nb + i, 0)),
                  pl.BlockSpec((lp, 512), lambda b, i: (b, 0)),
                  pl.BlockSpec((nb, 256, tile), lambda b, i: (b, 0, 0))],
        out_specs=pl.BlockSpec((tile, 256), lambda b, i: (b * nb + i, 0)),
        out_shape=jax.ShapeDtypeStruct((t, 256), F32),
        scratch_shapes=[pltpu.VMEM((N_HEADS, tile, 256), BF16), pltpu.VMEM((N_HEADS, 1, tile), F32),
                        pltpu.VMEM((N_HEADS, 1, tile), F32), pltpu.VMEM((256, tile), F32)],
        compiler_params=_params(("parallel", "arbitrary")),
        name="mla_attn",
    )(q, k, vt)


def _sb_attn_kernel(q_ref, k_ref, v_ref, later_ref, o_ref, qh_ref, c_ref, acc_ref, *, tile):
    i = pl.program_id(1)
    row = i * tile + _iota((tile, 1), 0)
    q = q_ref[...] * jnp.asarray(SB_HEAD ** -0.5, BF16)
    for h in range(N_HEADS):
        qh_ref[h] = jnp.where(_head_lane_mask(SB_W, SB_HEAD, h), q, jnp.zeros((), BF16))
    c_ref[...] = jnp.zeros(c_ref.shape, F32)
    acc_ref[...] = jnp.zeros(acc_ref.shape, F32)

    def block(j, masked):
        start = pl.multiple_of(j * tile, tile)
        weights = []
        for h in range(N_HEADS):
            z = _dot_nt(qh_ref[h], k_ref[pl.ds(start, tile), :])
            log_take, log_keep = _log_sigmoid_pair(z)
            if masked:
                col = start + _iota((1, tile), 1)
                mask = (col < row) & (col >= PAD)
                log_keep = jnp.where(mask, log_keep, 0.0)
            keep16 = _bf(log_keep)
            cut = SB_SUFFIX_SPLIT
            head_sum = jnp.sum(log_keep[:, :cut], axis=-1, keepdims=True)
            tail_sum = jnp.sum(log_keep[:, cut:], axis=-1, keepdims=True)
            later = jnp.concatenate(
                [jnp.dot(keep16[:, :cut], later_ref[...], preferred_element_type=F32) + tail_sum,
                 jnp.dot(keep16[:, cut:], later_ref[:tile - cut, :tile - cut], preferred_element_type=F32)], axis=1)
            c = c_ref[h]
            w = jnp.exp(log_take + later + c)
            if masked:
                w = jnp.where(mask, w, 0.0)
            c_ref[h] = c + (head_sum + tail_sum)
            weights.append(_bf(w))
        acc_ref[...] += jnp.dot(jnp.concatenate(weights, axis=1),
                                _head_stacked_values(v_ref[pl.ds(start, tile), :]), preferred_element_type=F32)

    block(i, True)

    def interior(jj, carry):
        block(i - 1 - jj, False)
        return carry

    lax.fori_loop(0, i - 1, interior, 0)

    @pl.when(i > 0)
    def _():
        block(0, True)

    o_ref[...] = acc_ref[...]


def _sb_attn(z_sb, later_mat, lp):
    t = z_sb.shape[0]
    tile = ATTN_TILE
    nb = lp // tile
    return pl.pallas_call(
        functools.partial(_sb_attn_kernel, tile=tile),
        grid=(t // lp, nb),
        in_specs=[pl.BlockSpec((tile, SB_W), lambda b, i: (b * nb + i, 0)),
                  pl.BlockSpec((lp, SB_W), lambda b, i: (b, 1)),
                  pl.BlockSpec((lp, SB_W), lambda b, i: (b, 2)),
                  _const_spec((SB_SUFFIX_SPLIT, SB_SUFFIX_SPLIT))],
        out_specs=pl.BlockSpec((tile, SB_W), lambda b, i: (b * nb + i, 0)),
        out_shape=jax.ShapeDtypeStruct((t, SB_W), F32),
        scratch_shapes=[pltpu.VMEM((N_HEADS, tile, SB_W), BF16), pltpu.VMEM((N_HEADS, tile, 1), F32),
                        pltpu.VMEM((tile, SB_W), F32)],
        compiler_params=_params(("parallel", "arbitrary")),
        name="sb_attn",
    )(z_sb, z_sb, z_sb, later_mat)


def _rw_prep_kernel(z_ref, zprev_ref, mu_ref, w0_ref, w2_ref, a0_ref, a2_ref, g2_ref, kk_ref, ka_ref, rk_ref,
                    r_o, k_o, v_o, kap_o, beta_o, lw_o, g_o, bonus_o):
    i = pl.program_id(1)
    z = z_ref[...]
    tm = z.shape[0]
    prev = jnp.where(i == 0, 0.0, zprev_ref[7:8, :])
    shifted = jnp.where(_iota((tm, 1), 0) == 0, prev, pltpu.roll(z, 1, 0))
    zz = z + (shifted - z) * mu_ref[...]
    r, k, v = zz[:, 0:256], zz[:, 256:512], zz[:, 512:768]
    lora_in = zz[:, 768:896]
    u = w0_ref[...] + _dot(jnp.tanh(lora_in), w2_ref[...])
    w = jnp.minimum(u, 0.0) - jnp.log1p(jnp.exp(-jnp.abs(u))) - 0.5
    a = jax.nn.sigmoid(a0_ref[...] + _dot(lora_in, a2_ref[...]))
    seg = _segment_matrix(RW_W, 64, 1.0)
    kx = k * kk_ref[...]
    kap = kx / jnp.maximum(jnp.sqrt(_dot_exact_rhs(kx * kx, seg)), 1e-12)
    kmod = k * (1.0 + (a - 1.0) * ka_ref[...])
    r_o[...] = r
    k_o[...] = kmod
    v_o[...] = v
    kap_o[...] = kap
    beta_o[...] = kap * a
    lw_o[...] = -jnp.exp(w)
    g_o[...] = _dot(jax.nn.sigmoid(zz[:, 896:1024]), g2_ref[...])
    bonus_o[...] = _dot_exact_rhs(r * kmod * rk_ref[...], seg) * v


def _rw_prep(z_rw, p, lp):
    t = z_rw.shape[0]
    tm = ROW_TILE
    nb = lp // tm
    row = lambda b, i: (b * nb + i, 0)
    prev = lambda b, i: (jnp.maximum(b * (lp // 8) + i * (tm // 8) - 1, 0), 0)
    consts = [p["rw_mu"], p["rw_w0"], p["rw_w2"], p["rw_a0"], p["rw_a2"], p["rw_g2"],
              p["rw_k_k"], p["rw_k_a"], p["rw_r_k"]]
    return pl.pallas_call(
        _rw_prep_kernel,
        grid=(t // lp, nb),
        in_specs=[pl.BlockSpec((tm, W_RW), row), pl.BlockSpec((8, W_RW), prev)]
        + [_const_spec(c.shape) for c in consts],
        out_specs=[pl.BlockSpec((tm, RW_W), row)] * 8,
        out_shape=[jax.ShapeDtypeStruct((t, RW_W), F32)] * 8,
        compiler_params=_params(("parallel", "parallel")),
        name="rw_prep",
    )(z_rw, z_rw, *consts)


def _rw_scan_kernel(r_ref, k_ref, v_ref, kap_ref, beta_ref, lw_ref, y_ref, ht_ref):
    c = RW_CHUNK
    n = N_HEADS * c

    @pl.when(pl.program_id(1) == 0)
    def _():
        ht_ref[...] = jnp.zeros_like(ht_ref)

    r, k, v = r_ref[...], k_ref[...], v_ref[...]
    kap, beta, lw = kap_ref[...], beta_ref[...], lw_ref[...]
    tri = jnp.where(_iota((c, c), 0) >= _iota((c, c), 1), 1.0, 0.0).astype(BF16)
    b = _dot_exact_lhs(tri, lw)
    b_end = b[c - 1:c, :]
    grow = jnp.exp(-b)
    kap_t = kap * jnp.exp(b - lw)
    r_t = r * jnp.exp(b)
    to_end = jnp.exp(b_end - b)

    lhs = jnp.concatenate([_stack_heads(kap_t, 64), _stack_heads(r_t, 64)], axis=0)
    rhs = jnp.concatenate([_stack_heads(beta * grow, 64), _stack_heads(k * grow, 64)], axis=0)
    pair = _dot_nt(lhs, rhs)
    ti = _iota((n, n), 0) & (c - 1)
    si = _iota((n, n), 1) & (c - 1)
    a_kb = jnp.where(ti > si, pair[0:n, 0:n], 0.0)
    a_kk = jnp.where(ti > si, pair[0:n, n:2 * n], 0.0)
    a_rb = jnp.where(ti >= si, pair[n:2 * n, 0:n], 0.0)
    a_rk = jnp.where(ti >= si, pair[n:2 * n, n:2 * n], 0.0)

    eye = jnp.where(_iota((n, n), 0) == _iota((n, n), 1), 1.0, 0.0)
    inv = eye - a_kb
    power = _dot(a_kb, a_kb)
    span = 2
    while span < c:
        inv = inv + _dot(inv, power)
        span *= 2
        if span < c:
            power = _dot(power, power)

    v_st = _stack_heads(v, 64)
    w_mat = _unstack_heads(_dot(inv, _stack_heads(kap_t, 64)), c)
    u_v = _unstack_heads(_dot(inv, _dot(a_kk, v_st)), c)
    y_v = _unstack_heads(_dot(a_rk, v_st), c)

    ht = ht_ref[...]
    from_state = _dot_nt(jnp.concatenate([w_mat, r_t], axis=0), ht)
    u = from_state[0:c] + u_v
    y_ref[...] = from_state[c:2 * c] + y_v - _unstack_heads(_dot(a_rb, _stack_heads(u, 64)), c)

    upd = _dot_tn(jnp.concatenate([v, u], axis=0), jnp.concatenate([k * to_end, -(beta * to_end)], axis=0))
    ht_ref[...] = ht * jnp.exp(b_end) + jnp.where(_same_segment((RW_W, RW_W), 64, 64), upd, 0.0)


def _rw_scan(r, k, v, kap, beta, lw, lp):
    t = r.shape[0]
    c = RW_CHUNK
    nb = lp // c
    spec = pl.BlockSpec((c, RW_W), lambda b, i: (b * nb + i, 0))
    return pl.pallas_call(
        _rw_scan_kernel,
        grid=(t // lp, nb),
        in_specs=[spec] * 6,
        out_specs=spec,
        out_shape=jax.ShapeDtypeStruct((t, RW_W), F32),
        scratch_shapes=[pltpu.VMEM((RW_W, RW_W), F32)],
        compiler_params=_params(("arbitrary", "arbitrary")),
        name="rw_scan",
    )(r, k, v, kap, beta, lw)


def _gla_kernel(q_ref, k_ref, v_ref, al_ref, a2_ref, ab_ref, o_ref, st_ref):
    c = GLA_CHUNK

    @pl.when(pl.program_id(1) == 0)
    def _():
        st_ref[...] = jnp.zeros_like(st_ref)

    q = q_ref[...] * (GLA_DK ** -0.5)
    k, v = k_ref[...], v_ref[...]
    x = _dot(al_ref[...], a2_ref[...]) + ab_ref[...]
    log_a = (jnp.minimum(x, 0.0) - jnp.log1p(jnp.exp(-jnp.abs(x)))) * (1.0 / GLA_TAU)
    tri = jnp.where(_iota((c, c), 0) >= _iota((c, c), 1), 1.0, 0.0).astype(BF16)
    b = _dot_exact_lhs(tri, log_a)
    b_mid = b[c // 2 - 1:c // 2, :]
    b_end = b[c - 1:c, :]

    st = st_ref[...]
    inter = _dot_nt(q * jnp.exp(b), st)
    scores = _dot_nt(_stack_heads(q * jnp.exp(b - b_mid), GLA_DK), k * jnp.exp(b_mid - b))
    n = N_HEADS * c
    scores = jnp.where((_iota((n, c), 0) & (c - 1)) >= _iota((n, c), 1), scores, 0.0)
    per_head = _dot(scores, v)
    intra = jnp.zeros((c, GLA_W), F32)
    for h in range(N_HEADS):
        intra = intra + jnp.where(_head_lane_mask(GLA_W, 64, h), per_head[h * c:(h + 1) * c], 0.0)
    o_ref[...] = inter + intra

    upd = _dot_tn(v, k * jnp.exp(b_end - b))
    st_ref[...] = st * jnp.exp(b_end) + jnp.where(_same_segment((GLA_W, GLA_QK), 64, GLA_DK), upd, 0.0)


def _gla(z_gla, a2, ab, lp):
    t = z_gla.shape[0]
    c = GLA_CHUNK
    nb = lp // c
    return pl.pallas_call(
        _gla_kernel,
        grid=(t // lp, nb),
        in_specs=[pl.BlockSpec((c, 128), lambda b, i: (b * nb + i, 0)),
                  pl.BlockSpec((c, 128), lambda b, i: (b * nb + i, 1)),
                  pl.BlockSpec((c, 256), lambda b, i: (b * nb + i, 1)),
                  pl.BlockSpec((c, 128), lambda b, i: (b * nb + i, 6)),
                  _const_spec(a2.shape), _const_spec(ab.shape)],
        out_specs=pl.BlockSpec((c, GLA_W), lambda b, i: (b * nb + i, 0)),
        out_shape=jax.ShapeDtypeStruct((t, GLA_W), F32),
        scratch_shapes=[pltpu.VMEM((GLA_W, GLA_QK), F32)],
        compiler_params=_params(("arbitrary", "arbitrary")),
        name="gla",
    )(z_gla, z_gla, z_gla, z_gla, a2, ab)


def _merge_kernel(h_ref, ymla_ref, yrw_ref, bonus_ref, g_ref, ysb_ref, ogla_ref, rgla_ref, zg_ref,
                  lnw_ref, lnb_ref, gn_ref, gb_ref, wb_ref, wo_ref, o_ref, *, tiles_per_seq):
    tm = h_ref.shape[0]
    avg = _segment_matrix(256, 64, 1.0 / 64)

    y = yrw_ref[...]
    d = y - _dot_exact_rhs(y, avg)
    var = _dot_exact_rhs(d * d, avg)
    y_rw = (d * lax.rsqrt(var + RW_GN_EPS) * lnw_ref[...] + lnb_ref[...] + bonus_ref[...]) * g_ref[...]

    o = ogla_ref[...]
    r = rgla_ref[...]
    y_gla = o * lax.rsqrt(_dot_exact_rhs(o * o, avg) + EPS) * gn_ref[...] * (r * jax.nn.sigmoid(r))

    acc = jnp.zeros((tm, D_MODEL), F32)
    for m, y_m in enumerate((ymla_ref[...], y_rw, ysb_ref[...], y_gla)):
        gate = jax.nn.sigmoid(zg_ref[:, m * D_MODEL:(m + 1) * D_MODEL].astype(F32) + gb_ref[m:m + 1, :])
        acc = acc + gate * _dot(y_m, wb_ref[m])
    delta = _dot(acc, wo_ref[...])
    row = (pl.program_id(0) % tiles_per_seq) * tm + _iota((tm, 1), 0)
    o_ref[...] = h_ref[...] + jnp.where(row >= PAD, delta, 0.0)


def _merge(h, y_mla, y_rw, bonus, g, y_sb, o_gla, z_gla, z_gate, p, lp):
    t = h.shape[0]
    tm = ROW_TILE
    row = lambda i: (i, 0)
    w256 = pl.BlockSpec((tm, 256), row)
    consts = [p["rw_ln_w"], p["rw_ln_b"], p["gla_norm"], p["gate_b"], p["w_branch"], p["w_out"]]
    return pl.pallas_call(
        functools.partial(_merge_kernel, tiles_per_seq=lp // tm),
        grid=(t // tm,),
        in_specs=[pl.BlockSpec((tm, D_MODEL), row), w256, w256, w256, w256, w256, w256,
                  pl.BlockSpec((tm, 256), lambda i: (i, 2)),
                  pl.BlockSpec((tm, W_GATE), row)]
        + [_const_spec(c.shape) for c in consts],
        out_specs=pl.BlockSpec((tm, D_MODEL), row),
        out_shape=jax.ShapeDtypeStruct((t, D_MODEL), F32),
        compiler_params=_params(("parallel",)),
        name="merge",
    )(h, y_mla, y_rw, bonus, g, y_sb, o_gla, z_gla, z_gate, *consts)


def _ffn_kernel(h_ref, g_ref, win_ref, cw_ref, cb_ref, wout_ref, o_ref, tail_ref):
    @pl.when(pl.program_id(1) == 0)
    def _():
        tail_ref[...] = jnp.zeros_like(tail_ref)

    x = h_ref[...]
    tm = x.shape[0]
    n = _bf(_rms(x, EPS) * g_ref[...])
    rowi = _iota((tm, 1), 0)
    acc = jnp.zeros((tm, D_MODEL), F32)
    for c0 in range(0, D_FF, FFN_COL_CHUNK):
        c1 = c0 + FFN_COL_CHUNK
        a = jnp.dot(n, win_ref[:, c0:c1], preferred_element_type=F32)
        u = jnp.dot(n, win_ref[:, D_FF + c0:D_FF + c1], preferred_element_type=F32)
        prev1 = tail_ref[7:8, c0:c1]
        prev2 = tail_ref[6:7, c0:c1]
        a1 = jnp.where(rowi == 0, prev1, pltpu.roll(a, 1, 0))
        a2 = jnp.where(rowi == 0, prev2, jnp.where(rowi == 1, prev1, pltpu.roll(a, 2, 0)))
        tail_ref[:, c0:c1] = a[tm - 8:tm, :]
        conv = cb_ref[:, c0:c1] + cw_ref[0:1, c0:c1] * a2 + cw_ref[1:2, c0:c1] * a1 + cw_ref[2:3, c0:c1] * a
        acc = acc + _dot(conv * jax.nn.sigmoid(conv) * u, wout_ref[c0:c1, :])
    o_ref[...] = x + acc


def _ffn(h, g, w_in, conv_w, conv_b, w_out, lp):
    t = h.shape[0]
    tm = ROW_TILE
    nb = lp // tm
    row = lambda b, i: (b * nb + i, 0)
    return pl.pallas_call(
        _ffn_kernel,
        grid=(t // lp, nb),
        in_specs=[pl.BlockSpec((tm, D_MODEL), row), _const_spec((1, D_MODEL)), _const_spec(w_in.shape),
                  _const_spec(conv_w.shape), _const_spec(conv_b.shape), _const_spec(w_out.shape)],
        out_specs=pl.BlockSpec((tm, D_MODEL), row),
        out_shape=jax.ShapeDtypeStruct((t, D_MODEL), F32),
        scratch_shapes=[pltpu.VMEM((8, D_FF), F32)],
        compiler_params=_params(("arbitrary", "arbitrary")),
        name="conv_ffn",
    )(h, g, w_in, conv_w, conv_b, w_out)


def _final_norm_kernel(h_ref, g_ref, o_ref):
    o_ref[0] = _rms(h_ref[0], EPS) * g_ref[...]


def _final_norm(h3, g, seq):
    b = h3.shape[0]
    first = (PAD + N_META) // BLOCK
    return pl.pallas_call(
        _final_norm_kernel,
        grid=(b, seq // BLOCK),
        in_specs=[pl.BlockSpec((1, BLOCK, D_MODEL), lambda bi, i: (bi, i + first, 0)), _const_spec((1, D_MODEL))],
        out_specs=pl.BlockSpec((1, BLOCK, D_MODEL), lambda bi, i: (bi, i, 0)),
        out_shape=jax.ShapeDtypeStruct((b, seq, D_MODEL), F32),
        compiler_params=_params(("parallel", "parallel")),
        name="final_norm",
    )(h3, g)


def _rope_swap(w):
    half = w.shape[-1] // 2
    return jnp.concatenate([-w[..., half:], w[..., :half]], axis=-1)


def _layout_params(w_in, mla_w_uq, mla_w_ukv, rw_w2, rw_a2, gla_a2):
    kr = w_in[..., 384:416]
    w_mla = jnp.concatenate([w_in[..., 0:384], jnp.tile(kr, (1, 1, 4)), jnp.tile(_rope_swap(kr), (1, 1, 4))], axis=-1)
    zg = w_in[..., 2208:2992]
    w_gla = jnp.concatenate([zg[..., 0:512], zg[..., 528:784], zg[..., 512:528],
                             jnp.zeros(zg.shape[:-1] + (W_GLA - 784,), zg.dtype)], axis=-1)
    w_all = _bf(jnp.concatenate([w_mla, w_in[..., 416:1440], w_in[..., 1440:2208], w_gla, w_in[..., 2992:]], axis=-1))

    depth = w_in.shape[0]
    wuq = mla_w_uq.reshape(depth, MLA_Q_RANK, N_HEADS, MLA_NOPE + MLA_ROPE)
    nope, rope = wuq[..., :MLA_NOPE], wuq[..., MLA_NOPE:]
    rope_sw = _rope_swap(rope)
    zeros64 = jnp.zeros((depth, MLA_Q_RANK, 64), w_in.dtype)
    pair = lambda x, a, b: jnp.concatenate([x[:, :, a], x[:, :, b]], axis=-1)
    rope_pair = lambda x, a, b: jnp.concatenate([x[:, :, a], x[:, :, b], zeros64], axis=-1)
    wq = _bf(jnp.concatenate([pair(nope, 0, 1), rope_pair(rope, 0, 1), pair(nope, 2, 3), rope_pair(rope, 2, 3),
                              rope_pair(rope_sw, 0, 1), rope_pair(rope_sw, 2, 3)], axis=-1))
    wukv = mla_w_ukv.reshape(depth, MLA_KV_RANK, N_HEADS, 128)
    wkv = _bf(jnp.concatenate([wukv[..., :64].reshape(depth, MLA_KV_RANK, 256),
                               wukv[..., 64:].reshape(depth, MLA_KV_RANK, 256)], axis=-1))

    z64 = jnp.zeros_like(rw_w2)
    w2 = _bf(jnp.concatenate([rw_w2, z64], axis=1))
    a2 = _bf(jnp.concatenate([z64, rw_a2], axis=1))
    gla_a2p = _bf(jnp.concatenate([gla_a2, jnp.zeros((depth, 128 - gla_a2.shape[1], GLA_QK), gla_a2.dtype)], axis=1))
    return w_all, wq, wkv, w2, a2, gla_a2p


def _rope_tables(lp):
    half = MLA_ROPE // 2
    freqs = ROPE_THETA ** (-jnp.arange(half, dtype=F32) / half)
    pos = (jnp.arange(lp) - PAD).astype(F32)
    ang = pos[:, None] * freqs[None, :]
    return jnp.tile(jnp.cos(ang), (1, 128 // half)), jnp.tile(jnp.sin(ang), (1, 128 // half))


def kernel(x, meta_tokens, norm_mix, w_in, mla_q_norm, mla_w_uq, mla_kv_norm, mla_w_ukv, rw_mu, rw_w0, rw_w2, rw_a0, rw_a2, rw_g2, rw_k_k, rw_k_a, rw_r_k, rw_ln_w, rw_ln_b, gla_a2, gla_a_b, gla_norm, gate_b, w_branch, w_out, norm_ffn, w_ffn_in, ffn_conv_w, ffn_conv_b, w_ffn_out, norm_final):
    batch, seq, _ = x.shape
    depth = w_in.shape[0]
    lp = PAD + N_META + seq
    t = batch * lp
    assert lp % ROW_TILE == 0 and lp % ATTN_TILE == 0 and lp % GLA_CHUNK == 0 and lp % RW_CHUNK == 0

    w_all, wq, wkv, rw_w2p, rw_a2p, gla_a2p = _layout_params(w_in, mla_w_uq, mla_w_ukv, rw_w2, rw_a2, gla_a2)
    w_branch_b, w_out_b, w_ffn_in_b, w_ffn_out_b, rw_g2_b = map(_bf, (w_branch, w_out, w_ffn_in, w_ffn_out, rw_g2))
    vec = lambda a: a.reshape(depth, 1, -1)
    cos, sin = _rope_tables(lp)
    idx = jnp.arange(SB_SUFFIX_SPLIT)
    later_mat = jnp.where(idx[:, None] > idx[None, :], 1.0, 0.0).astype(BF16)

    meta = jnp.broadcast_to(meta_tokens[None].astype(x.dtype), (batch, N_META, D_MODEL))
    h = jnp.concatenate([jnp.zeros((batch, PAD, D_MODEL), x.dtype), meta, x], axis=1).reshape(t, D_MODEL)

    for i in range(depth):
        z_mla, z_rw, z_sb, z_gla, z_gate = _in_proj(h, vec(norm_mix)[i], w_all[i])
        q, k, v = _mla_prep(z_mla, vec(mla_q_norm)[i], vec(mla_kv_norm)[i], wq[i], wkv[i], cos, sin, lp)
        y_mla = _mla_attn(q, k, v, lp)
        rw = {"rw_mu": vec(rw_mu)[i], "rw_w0": vec(rw_w0)[i], "rw_w2": rw_w2p[i], "rw_a0": vec(rw_a0)[i],
              "rw_a2": rw_a2p[i], "rw_g2": rw_g2_b[i], "rw_k_k": vec(rw_k_k)[i], "rw_k_a": vec(rw_k_a)[i],
              "rw_r_k": vec(rw_r_k)[i]}
        r_, k_, v_, kap, beta, lw, g, bonus = _rw_prep(z_rw, rw, lp)
        y_rw = _rw_scan(r_, k_, v_, kap, beta, lw, lp)
        y_sb = _sb_attn(z_sb, later_mat, lp)
        o_gla = _gla(z_gla, gla_a2p[i], vec(gla_a_b)[i], lp)
        mp = {"rw_ln_w": vec(rw_ln_w)[i], "rw_ln_b": vec(rw_ln_b)[i], "gla_norm": vec(gla_norm)[i],
              "gate_b": gate_b[i], "w_branch": w_branch_b[i], "w_out": w_out_b[i]}
        h = _merge(h, y_mla, y_rw, bonus, g, y_sb, o_gla, z_gla, z_gate, mp, lp)
        h = _ffn(h, vec(norm_ffn)[i], w_ffn_in_b[i], ffn_conv_w[i], vec(ffn_conv_b)[i], w_ffn_out_b[i], lp)
    return _final_norm(h.reshape(batch, lp, D_MODEL), norm_final.reshape(1, D_MODEL), seq)
```

```python
import functools

import jax
import jax.numpy as jnp
from jax import lax
from jax.experimental import pallas as pl
from jax.experimental.pallas import tpu as pltpu

F32 = jnp.float32
BF16 = jnp.bfloat16

D_MODEL = 1024
DEPTH = 4
N_META = 16
BLOCK = 128
PAD = (-N_META) % BLOCK
EPS = 1e-6
NEG_INF = -1e30
LOG2_E = 1.4426950408889634

N_HEADS = 4
MLA_NOPE = 64
MLA_ROPE = 32
MLA_Q_RANK = 256
MLA_KV_RANK = 128
ROPE_THETA = 10000.0

RW_W = 256
RW_GN_EPS = 64e-5
RW_CHUNK = 64

SB_W = 256
SB_HEAD = 64
SB_SUFFIX_SPLIT = 256

GLA_DK = 32
GLA_QK = 128
GLA_W = 256
GLA_TAU = 16.0
GLA_CHUNK = 128

D_FF = 2816
FFN_COL_CHUNK = 1408

W_MLA, W_RW, W_SB, W_GLA, W_GATE = 640, 1024, 768, 896, 4096
IN_GROUP_WIDTHS = (W_MLA, W_RW, W_SB, W_GLA, W_GATE)
IN_GROUP_DTYPES = (BF16, F32, BF16, F32, BF16)

ROW_TILE = 384
ATTN_TILE = 384
VMEM_LIMIT = 56 * 1024 * 1024


def _bf(x):
    return x.astype(BF16)


def _dot(a, b):
    return jnp.dot(_bf(a), _bf(b), preferred_element_type=F32)


def _dot_nt(a, b):
    return lax.dot_general(_bf(a), _bf(b), (((1,), (1,)), ((), ())), preferred_element_type=F32)


def _dot_tn(a, b):
    return lax.dot_general(_bf(a), _bf(b), (((0,), (0,)), ((), ())), preferred_element_type=F32)


def _split_hi_lo(x):
    hi = _bf(x)
    lo = _bf(x - hi.astype(F32))
    return hi, lo


def _dot_exact_rhs(x, m):
    hi, lo = _split_hi_lo(x)
    return jnp.dot(hi, m, preferred_element_type=F32) + jnp.dot(lo, m, preferred_element_type=F32)


def _dot_exact_lhs(m, x):
    hi, lo = _split_hi_lo(x)
    return jnp.dot(m, hi, preferred_element_type=F32) + jnp.dot(m, lo, preferred_element_type=F32)


def _iota(shape, dim):
    return lax.broadcasted_iota(jnp.int32, shape, dim)


def _div_pow2(x, d):
    assert d & (d - 1) == 0
    return lax.shift_right_logical(x, d.bit_length() - 1)


def _same_segment(shape, row_seg, col_seg):
    return _div_pow2(_iota(shape, 0), row_seg) == _div_pow2(_iota(shape, 1), col_seg)


def _segment_matrix(n, seg, value):
    return jnp.where(_same_segment((n, n), seg, seg), value, 0.0).astype(BF16)


def _head_lane_mask(width, head_width, h):
    lane = _iota((1, width), 1)
    return (lane >= h * head_width) & (lane < (h + 1) * head_width)


def _stack_heads(x, head_width):
    w = x.shape[1]
    return jnp.concatenate(
        [jnp.where(_head_lane_mask(w, head_width, h), x, 0.0) for h in range(N_HEADS)], axis=0)


def _unstack_heads(xs, c):
    return xs[0:c] + xs[c:2 * c] + xs[2 * c:3 * c] + xs[3 * c:4 * c]


def _rms(x, eps):
    return x * lax.rsqrt(jnp.mean(x * x, axis=-1, keepdims=True) + eps)


def _const_spec(shape):
    nd = len(shape)
    return pl.BlockSpec(shape, lambda *_: (0,) * nd)


def _params(sem, vmem=VMEM_LIMIT):
    return pltpu.CompilerParams(dimension_semantics=sem, vmem_limit_bytes=vmem)


def _in_proj_kernel(h_ref, g_ref, w_ref, *out_refs):
    n = _bf(_rms(h_ref[...], EPS) * g_ref[...])
    off = 0
    for o_ref, width in zip(out_refs, IN_GROUP_WIDTHS):
        for c0 in range(0, width, 1024):
            c1 = min(c0 + 1024, width)
            o_ref[:, c0:c1] = jnp.dot(
                n, w_ref[:, off + c0:off + c1], preferred_element_type=F32).astype(o_ref.dtype)
        off += width


def _in_proj(h, g, w_all):
    t = h.shape[0]
    tm = ROW_TILE
    return pl.pallas_call(
        _in_proj_kernel,
        grid=(t // tm,),
        in_specs=[pl.BlockSpec((tm, D_MODEL), lambda i: (i, 0)),
                  _const_spec((1, D_MODEL)),
                  _const_spec(w_all.shape)],
        out_specs=[pl.BlockSpec((tm, w), lambda i: (i, 0)) for w in IN_GROUP_WIDTHS],
        out_shape=[jax.ShapeDtypeStruct((t, w), dt) for w, dt in zip(IN_GROUP_WIDTHS, IN_GROUP_DTYPES)],
        compiler_params=_params(("parallel",)),
        name="in_proj",
    )(h, g, w_all)


def _mla_prep_kernel(z_ref, qg_ref, kvg_ref, wq_ref, wkv_ref, cos_ref, sin_ref, q_ref, k_ref, vt_ref):
    z = z_ref[...].astype(F32)
    nq = _rms(z[:, 0:256], EPS) * qg_ref[...]
    nkv = _rms(z[:, 256:384], EPS) * kvg_ref[...]
    cos = cos_ref[...]
    sin = sin_ref[...]
    ql = _dot(nq, wq_ref[...])
    scale = (MLA_NOPE + MLA_ROPE) ** -0.5
    q = jnp.concatenate([ql[:, 0:128], ql[:, 128:256] * cos + ql[:, 512:640] * sin,
                         ql[:, 256:384], ql[:, 384:512] * cos + ql[:, 640:768] * sin], axis=1)
    q_ref[...] = _bf(q * scale)
    kvl = _dot(nkv, wkv_ref[...])
    k_rope = z[:, 384:512] * cos + z[:, 512:640] * sin
    k_rope = jnp.where(_iota((1, 128), 1) < 2 * MLA_ROPE, k_rope, 0.0)
    k_ref[...] = _bf(jnp.concatenate([kvl[:, 0:128], k_rope, kvl[:, 128:256], k_rope], axis=1))
    vt_ref[0] = _bf(kvl[:, 256:512].T)


def _mla_prep(z_mla, qg, kvg, wq, wkv, cos, sin, lp):
    t = z_mla.shape[0]
    tm = ROW_TILE
    nb = lp // tm
    row = lambda b, i: (b * nb + i, 0)
    return pl.pallas_call(
        _mla_prep_kernel,
        grid=(t // lp, nb),
        in_specs=[pl.BlockSpec((tm, W_MLA), row),
                  _const_spec((1, MLA_Q_RANK)), _const_spec((1, MLA_KV_RANK)),
                  _const_spec(wq.shape), _const_spec(wkv.shape),
                  pl.BlockSpec((tm, 128), lambda b, i: (i, 0)),
                  pl.BlockSpec((tm, 128), lambda b, i: (i, 0))],
        out_specs=[pl.BlockSpec((tm, 512), row), pl.BlockSpec((tm, 512), row),
                   pl.BlockSpec((1, 256, tm), lambda b, i: (b * nb + i, 0, 0))],
        out_shape=[jax.ShapeDtypeStruct((t, 512), BF16), jax.ShapeDtypeStruct((t, 512), BF16),
                   jax.ShapeDtypeStruct((t // tm, 256, tm), BF16)],
        compiler_params=_params(("parallel", "parallel")),
        name="mla_prep",
    )(z_mla, qg, kvg, wq, wkv, cos, sin)


def _per_head_lanes(cols):
    lane = _iota((1, 256), 1)
    return jnp.where(lane < 64, cols[0], jnp.where(lane < 128, cols[1], jnp.where(lane < 192, cols[2], cols[3])))


def _head_stacked_values(vb):
    zero = jnp.zeros((), vb.dtype)
    return jnp.concatenate([jnp.where(_head_lane_mask(256, 64, h), vb, zero) for h in range(N_HEADS)], axis=0)


def _per_head_rows(rows, n):
    return jnp.concatenate([jnp.broadcast_to(r, (64, n)) for r in rows], axis=0)


def _mla_attn_kernel(q_ref, k_ref, vt_ref, o_ref, qh_ref, m_ref, l_ref, acc_ref, *, tile):
    i = pl.program_id(1)
    q_pos = i * tile + _iota((1, tile), 1)
    lane = _iota((1, 256), 1)
    for h in range(N_HEADS):
        half, slot = h // 2, h % 2
        head_lanes = (((lane >= slot * MLA_NOPE) & (lane < (slot + 1) * MLA_NOPE))
                      | ((lane >= 128 + slot * MLA_ROPE) & (lane < 128 + (slot + 1) * MLA_ROPE)))
        qh_ref[h] = jnp.where(head_lanes, q_ref[:, half * 256:(half + 1) * 256], jnp.zeros((), BF16))
    m_ref[...] = jnp.full(m_ref.shape, NEG_INF, F32)
    l_ref[...] = jnp.zeros(l_ref.shape, F32)
    acc_ref[...] = jnp.zeros(acc_ref.shape, F32)
    value_row_head = _div_pow2(_iota((256, 1), 0), 64)

    def block(j, masked):
        start = pl.multiple_of(j * tile, tile)
        probs, alphas = [], []
        for h in range(N_HEADS):
            c0 = (h // 2) * 256
            s = _dot_nt(k_ref[pl.ds(start, tile), c0:c0 + 256], qh_ref[h])
            if masked:
                k_pos = start + _iota((tile, 1), 0)
                s = jnp.where((k_pos <= q_pos) & (k_pos >= PAD), s, NEG_INF)
            m_old = m_ref[h]
            m_new = jnp.maximum(m_old, jnp.max(s, axis=0, keepdims=True))
            p = jnp.exp(s - m_new)
            alpha = jnp.exp(m_old - m_new)
            l_ref[h] = alpha * l_ref[h] + jnp.sum(p, axis=0, keepdims=True)
            m_ref[h] = m_new
            probs.append(_bf(p))
            alphas.append(alpha)
        vt = vt_ref[j]
        zero = jnp.zeros((), BF16)
        vt_heads = jnp.concatenate([jnp.where(value_row_head == h, vt, zero) for h in range(N_HEADS)], axis=1)
        pv = jnp.dot(vt_heads, jnp.concatenate(probs, axis=0), preferred_element_type=F32)
        acc_ref[...] = _per_head_rows(alphas, tile) * acc_ref[...] + pv

    @pl.when(i > 0)
    def _():
        block(0, True)

    def interior(j, carry):
        block(j, False)
        return carry

    lax.fori_loop(1, i, interior, 0)
    block(i, True)
    o_ref[...] = (acc_ref[...] / _per_head_rows([l_ref[h] for h in range(N_HEADS)], tile)).T


def _mla_attn(q, k, vt, lp):
    t = q.shape[0]
    tile = ATTN_TILE
    nb = lp // tile
    return pl.pallas_call(
        functools.partial(_mla_attn_kernel, tile=tile),
        grid=(t // lp, nb),
        in_specs=[pl.BlockSpec((tile, 512), lambda b, i: (b * nb + i, 0)),
                  pl.BlockSpec((lp, 512), lambda b, i: (b, 0)),
                  pl.BlockSpec((nb, 256, tile), lambda b, i: (b, 0, 0))],
        out_specs=pl.BlockSpec((tile, 256), lambda b, i: (b * nb + i, 0)),
        out_shape=jax.ShapeDtypeStruct((t, 256), F32),
        scratch_shapes=[pltpu.VMEM((N_HEADS, tile, 256), BF16), pltpu.VMEM((N_HEADS, 1, tile), F32),
                        pltpu.VMEM((N_HEADS, 1, tile), F32), pltpu.VMEM((256, tile), F32)],
        compiler_params=_params(("parallel", "arbitrary")),
        name="mla_attn",
    )(q, k, vt)


def _sb_attn_kernel(q_ref, k_ref, v_ref, later_ref, o_ref, qh_ref, c_ref, acc_ref, *, tile):
    i = pl.program_id(1)
    row = i * tile + _iota((tile, 1), 0)
    for h in range(N_HEADS):
        qh_ref[h] = jnp.where(_head_lane_mask(SB_W, SB_HEAD, h), q_ref[...], jnp.zeros((), BF16))
    c_ref[...] = jnp.zeros(c_ref.shape, F32)
    acc_ref[...] = jnp.zeros(acc_ref.shape, F32)

    def block(j, masked):
        start = pl.multiple_of(j * tile, tile)
        heads = range(N_HEADS)
        cut = SB_SUFFIX_SPLIT
        z = [_dot_nt(qh_ref[h], k_ref[pl.ds(start, tile), :]) for h in heads]
        log_take = [jnp.minimum(z[h], 0.0) - jnp.log2(1.0 + jnp.exp2(-jnp.abs(z[h]))) for h in heads]
        log_keep = [log_take[h] - z[h] for h in heads]
        if masked:
            col = start + _iota((1, tile), 1)
            mask = (col < row) & (col >= PAD)
            log_keep = [jnp.where(mask, log_keep[h], 0.0) for h in heads]
        keep16 = [_bf(log_keep[h]) for h in heads]
        later_head = [jnp.dot(keep16[h][:, :cut], later_ref[...], preferred_element_type=F32) for h in heads]
        later_tail = [jnp.dot(keep16[h][:, cut:], later_ref[:tile - cut, :tile - cut], preferred_element_type=F32)
                      for h in heads]
        weights = []
        for h in heads:
            head_sum = jnp.sum(log_keep[h][:, :cut], axis=-1, keepdims=True)
            tail_sum = jnp.sum(log_keep[h][:, cut:], axis=-1, keepdims=True)
            c = c_ref[h]
            later = jnp.concatenate([later_head[h] + tail_sum, later_tail[h]], axis=1)
            w = jnp.exp2(log_take[h] + later + c)
            if masked:
                w = jnp.where(mask, w, 0.0)
            c_ref[h] = c + (head_sum + tail_sum)
            weights.append(_bf(w))
        acc_ref[...] += jnp.dot(jnp.concatenate(weights, axis=1),
                                _head_stacked_values(v_ref[pl.ds(start, tile), :]), preferred_element_type=F32)

    block(i, True)

    def interior(jj, carry):
        block(i - 1 - jj, False)
        return carry

    lax.fori_loop(0, i - 1, interior, 0)

    @pl.when(i > 0)
    def _():
        block(0, True)

    o_ref[...] = acc_ref[...]


def _sb_attn(z_sb, later_mat, lp):
    t = z_sb.shape[0]
    tile = ATTN_TILE
    nb = lp // tile
    return pl.pallas_call(
        functools.partial(_sb_attn_kernel, tile=tile),
        grid=(t // lp, nb),
        in_specs=[pl.BlockSpec((tile, SB_W), lambda b, i: (b * nb + i, 0)),
                  pl.BlockSpec((lp, SB_W), lambda b, i: (b, 1)),
                  pl.BlockSpec((lp, SB_W), lambda b, i: (b, 2)),
                  _const_spec((SB_SUFFIX_SPLIT, SB_SUFFIX_SPLIT))],
        out_specs=pl.BlockSpec((tile, SB_W), lambda b, i: (b * nb + i, 0)),
        out_shape=jax.ShapeDtypeStruct((t, SB_W), F32),
        scratch_shapes=[pltpu.VMEM((N_HEADS, tile, SB_W), BF16), pltpu.VMEM((N_HEADS, tile, 1), F32),
                        pltpu.VMEM((tile, SB_W), F32)],
        compiler_params=_params(("parallel", "arbitrary")),
        name="sb_attn",
    )(z_sb, z_sb, z_sb, later_mat)


def _compact_eye(c):
    return jnp.where(_iota((c, N_HEADS * c), 0) == (_iota((c, N_HEADS * c), 1) & (c - 1)), 1.0, 0.0)


def _rw_prep_kernel(z_ref, zprev_ref, mu_ref, w0_ref, w2_ref, a0_ref, a2_ref, g2_ref, kk_ref, ka_ref, rk_ref,
                    w_o, rt_o, arb_o, kbar_o, bbar_o, v_o, uv_o, yv_o, dec_o, g_o, bonus_o):
    i = pl.program_id(1)
    z = z_ref[...]
    tm = z.shape[0]
    prev = jnp.where(i == 0, 0.0, zprev_ref[7:8, :])
    shifted = jnp.where(_iota((tm, 1), 0) == 0, prev, pltpu.roll(z, 1, 0))
    zz = z + (shifted - z) * mu_ref[...]
    r, k, v = zz[:, 0:256], zz[:, 256:512], zz[:, 512:768]
    lora_in = zz[:, 768:896]
    u = w0_ref[...] + _dot(jnp.tanh(lora_in), w2_ref[...])
    w = jnp.minimum(u, 0.0) - jnp.log1p(jnp.exp(-jnp.abs(u))) - 0.5
    a = jax.nn.sigmoid(a0_ref[...] + _dot(lora_in, a2_ref[...]))
    seg = _segment_matrix(RW_W, 64, 1.0)
    kx = k * kk_ref[...]
    kap = kx / jnp.maximum(jnp.sqrt(_dot_exact_rhs(kx * kx, seg)), 1e-12)
    kmod = k * (1.0 + (a - 1.0) * ka_ref[...])
    beta = kap * a
    lw = -jnp.exp(w)
    g_o[...] = _dot(jax.nn.sigmoid(zz[:, 896:1024]), g2_ref[...])
    bonus_o[...] = _dot_exact_rhs(r * kmod * rk_ref[...], seg) * v

    c = RW_CHUNK
    same_chunk = _same_segment((tm, tm), c, c)
    lw_hi, lw_lo = _split_hi_lo(lw)
    cum = jnp.where(same_chunk & (_iota((tm, tm), 0) >= _iota((tm, tm), 1)), 1.0, 0.0).astype(BF16)
    tot = jnp.where(same_chunk, 1.0, 0.0).astype(BF16)
    b = jnp.dot(cum, lw_hi, preferred_element_type=F32) + jnp.dot(cum, lw_lo, preferred_element_type=F32)
    b_end = jnp.dot(tot, lw_hi, preferred_element_type=F32) + jnp.dot(tot, lw_lo, preferred_element_type=F32)
    grow = jnp.exp(-b)
    to_end = jnp.exp(b_end - b)
    kap_t = kap * jnp.exp(b - lw)
    r_t = r * jnp.exp(b)
    beta_g = beta * grow
    k_g = kmod * grow
    rt_o[...] = _bf(r_t)
    kbar_o[...] = _bf(kmod * to_end)
    bbar_o[...] = _bf(beta * to_end)
    v_o[...] = _bf(v)
    decay_end = jnp.exp(b_end)

    lane_pos = _iota((c, N_HEADS * c), 1) & (c - 1)
    strictly_earlier = _iota((c, N_HEADS * c), 0) > lane_pos
    not_later = _iota((c, N_HEADS * c), 0) >= lane_pos
    chunks = [slice(n * c, (n + 1) * c) for n in range(tm // c)]
    pair = [_dot_nt(jnp.concatenate([kap_t[rows], r_t[rows]], axis=0),
                    jnp.concatenate([_stack_heads(beta_g[rows], 64), _stack_heads(k_g[rows], 64)], axis=0))
            for rows in chunks]
    a_kb = [jnp.where(strictly_earlier, p[0:c, 0:4 * c], 0.0) for p in pair]
    a_kk = [jnp.where(strictly_earlier, p[0:c, 4 * c:8 * c], 0.0) for p in pair]
    a_rk = [jnp.where(not_later, p[c:2 * c, 4 * c:8 * c], 0.0) for p in pair]
    for rows, p in zip(chunks, pair):
        arb_o[rows, :] = _bf(jnp.where(not_later, p[c:2 * c, 0:4 * c], 0.0))
    inv = [_compact_eye(c) - a for a in a_kb]
    power = [_dot(a, _stack_heads(a, c)) for a in a_kb]
    span = 2
    while span < c:
        inv = [t + _dot(t, _stack_heads(p, c)) for t, p in zip(inv, power)]
        span *= 2
        if span < c:
            power = [_dot(p, _stack_heads(p, c)) for p in power]
    v_st = [_stack_heads(v[rows], 64) for rows in chunks]
    kk_v = [_dot(a, vs) for a, vs in zip(a_kk, v_st)]
    for n, rows in enumerate(chunks):
        w_o[rows, :] = _bf(_dot(inv[n], _stack_heads(kap_t[rows], 64)))
        uv_o[rows, :] = _dot(inv[n], _stack_heads(kk_v[n], 64))
        yv_o[rows, :] = _dot(a_rk[n], v_st[n])
        dec_o[n * 8:(n + 1) * 8, :] = decay_end[n * c:n * c + 8]


def _rw_prep(z_rw, p, lp):
    t = z_rw.shape[0]
    tm = ROW_TILE
    nb = lp // tm
    row = lambda b, i: (b * nb + i, 0)
    prev = lambda b, i: (jnp.maximum(b * (lp // 8) + i * (tm // 8) - 1, 0), 0)
    consts = [p["rw_mu"], p["rw_w0"], p["rw_w2"], p["rw_a0"], p["rw_a2"], p["rw_g2"],
              p["rw_k_k"], p["rw_k_a"], p["rw_r_k"]]
    return pl.pallas_call(
        _rw_prep_kernel,
        grid=(t // lp, nb),
        in_specs=[pl.BlockSpec((tm, W_RW), row), pl.BlockSpec((8, W_RW), prev)]
        + [_const_spec(c.shape) for c in consts],
        out_specs=[pl.BlockSpec((tm, RW_W), row)] * 8
        + [pl.BlockSpec((tm // RW_CHUNK * 8, RW_W), row)] + [pl.BlockSpec((tm, RW_W), row)] * 2,
        out_shape=[jax.ShapeDtypeStruct((t, RW_W), BF16)] * 6 + [jax.ShapeDtypeStruct((t, RW_W), F32)] * 2
        + [jax.ShapeDtypeStruct((t // RW_CHUNK * 8, RW_W), F32)] + [jax.ShapeDtypeStruct((t, RW_W), F32)] * 2,
        compiler_params=_params(("parallel", "parallel")),
        name="rw_prep",
    )(z_rw, z_rw, *consts)


def _rw_scan_kernel(w_ref, rt_ref, arb_ref, kbar_ref, bbar_ref, v_ref, uv_ref, yv_ref, dec_ref, y_ref, ht_ref):
    c = RW_CHUNK

    @pl.when(pl.program_id(0) == 0)
    def _():
        ht_ref[...] = jnp.zeros_like(ht_ref)

    same_head = _same_segment((RW_W, RW_W), 64, 64)
    seqs = range(ht_ref.shape[0])
    ht = [ht_ref[s] for s in seqs]
    from_state = [_dot_nt(jnp.concatenate([w_ref[s], rt_ref[s]], axis=0), ht[s]) for s in seqs]
    u = [from_state[s][0:c] + uv_ref[s] for s in seqs]
    from_u = [_dot(arb_ref[s], _stack_heads(u[s], 64)) for s in seqs]
    upd = [_dot_tn(jnp.concatenate([v_ref[s], _bf(u[s])], axis=0),
                   jnp.concatenate([kbar_ref[s], -bbar_ref[s]], axis=0)) for s in seqs]
    for s in seqs:
        y_ref[s] = from_state[s][c:2 * c] + yv_ref[s] - from_u[s]
        ht_ref[s] = ht[s] * dec_ref[s, 0:1, :] + jnp.where(same_head, upd[s], 0.0)


def _rw_scan(w, rt, arb, kbar, bbar, v, uv, yv, dec, batch):
    t = w.shape[0]
    lp = t // batch
    c = RW_CHUNK
    as_seq = lambda a: a.reshape(batch, a.shape[0] // batch, RW_W)
    spec = pl.BlockSpec((batch, c, RW_W), lambda i: (0, i, 0))
    y = pl.pallas_call(
        _rw_scan_kernel,
        grid=(lp // c,),
        in_specs=[spec] * 8 + [pl.BlockSpec((batch, 8, RW_W), lambda i: (0, i, 0))],
        out_specs=spec,
        out_shape=jax.ShapeDtypeStruct((batch, lp, RW_W), F32),
        scratch_shapes=[pltpu.VMEM((batch, RW_W, RW_W), F32)],
        compiler_params=_params(("arbitrary",)),
        name="rw_scan",
    )(*map(as_seq, (w, rt, arb, kbar, bbar, v, uv, yv, dec)))
    return y.reshape(t, RW_W)


def _gla_kernel(q_ref, k_ref, v_ref, al_ref, a2_ref, ab_ref, o_ref, st_ref):
    c = GLA_CHUNK

    @pl.when(pl.program_id(1) == 0)
    def _():
        st_ref[...] = jnp.zeros_like(st_ref)

    q = q_ref[...] * (GLA_DK ** -0.5)
    k, v = k_ref[...], v_ref[...]
    x = _dot(al_ref[...], a2_ref[...]) + ab_ref[...]
    log_a = (jnp.minimum(x, 0.0) - jnp.log1p(jnp.exp(-jnp.abs(x)))) * (1.0 / GLA_TAU)
    tri = jnp.where(_iota((c, c), 0) >= _iota((c, c), 1), 1.0, 0.0).astype(BF16)
    b = _dot_exact_lhs(tri, log_a)
    b_mid = b[c // 2 - 1:c // 2, :]
    b_end = b[c - 1:c, :]

    st = st_ref[...]
    inter = _dot_nt(q * jnp.exp(b), st)
    scores = _dot_nt(_stack_heads(q * jnp.exp(b - b_mid), GLA_DK), k * jnp.exp(b_mid - b))
    n = N_HEADS * c
    scores = jnp.where((_iota((n, c), 0) & (c - 1)) >= _iota((n, c), 1), scores, 0.0)
    per_head = _dot(scores, v)
    intra = jnp.zeros((c, GLA_W), F32)
    for h in range(N_HEADS):
        intra = intra + jnp.where(_head_lane_mask(GLA_W, 64, h), per_head[h * c:(h + 1) * c], 0.0)
    o_ref[...] = inter + intra

    upd = _dot_tn(v, k * jnp.exp(b_end - b))
    st_ref[...] = st * jnp.exp(b_end) + jnp.where(_same_segment((GLA_W, GLA_QK), 64, GLA_DK), upd, 0.0)


def _gla(z_gla, a2, ab, lp):
    t = z_gla.shape[0]
    c = GLA_CHUNK
    nb = lp // c
    return pl.pallas_call(
        _gla_kernel,
        grid=(t // lp, nb),
        in_specs=[pl.BlockSpec((c, 128), lambda b, i: (b * nb + i, 0)),
                  pl.BlockSpec((c, 128), lambda b, i: (b * nb + i, 1)),
                  pl.BlockSpec((c, 256), lambda b, i: (b * nb + i, 1)),
                  pl.BlockSpec((c, 128), lambda b, i: (b * nb + i, 6)),
                  _const_spec(a2.shape), _const_spec(ab.shape)],
        out_specs=pl.BlockSpec((c, GLA_W), lambda b, i: (b * nb + i, 0)),
        out_shape=jax.ShapeDtypeStruct((t, GLA_W), F32),
        scratch_shapes=[pltpu.VMEM((GLA_W, GLA_QK), F32)],
        compiler_params=_params(("arbitrary", "arbitrary")),
        name="gla",
    )(z_gla, z_gla, z_gla, z_gla, a2, ab)


def _merge_kernel(h_ref, ymla_ref, yrw_ref, bonus_ref, g_ref, ysb_ref, ogla_ref, rgla_ref, zg_ref,
                  lnw_ref, lnb_ref, gn_ref, gb_ref, wb_ref, wo_ref, o_ref, *, tiles_per_seq):
    tm = h_ref.shape[0]
    avg = _segment_matrix(256, 64, 1.0 / 64)

    y = yrw_ref[...]
    d = y - _dot_exact_rhs(y, avg)
    var = _dot_exact_rhs(d * d, avg)
    y_rw = (d * lax.rsqrt(var + RW_GN_EPS) * lnw_ref[...] + lnb_ref[...] + bonus_ref[...]) * g_ref[...]

    o = ogla_ref[...]
    r = rgla_ref[...]
    y_gla = o * lax.rsqrt(_dot_exact_rhs(o * o, avg) + EPS) * gn_ref[...] * (r * jax.nn.sigmoid(r))

    acc = jnp.zeros((tm, D_MODEL), F32)
    for m, y_m in enumerate((ymla_ref[...], y_rw, ysb_ref[...], y_gla)):
        gate = jax.nn.sigmoid(zg_ref[:, m * D_MODEL:(m + 1) * D_MODEL].astype(F32) + gb_ref[m:m + 1, :])
        acc = acc + gate * _dot(y_m, wb_ref[m])
    delta = _dot(acc, wo_ref[...])
    row = (pl.program_id(0) % tiles_per_seq) * tm + _iota((tm, 1), 0)
    o_ref[...] = h_ref[...] + jnp.where(row >= PAD, delta, 0.0)


def _merge(h, y_mla, y_rw, bonus, g, y_sb, o_gla, z_gla, z_gate, p, lp):
    t = h.shape[0]
    tm = ROW_TILE
    row = lambda i: (i, 0)
    w256 = pl.BlockSpec((tm, 256), row)
    consts = [p["rw_ln_w"], p["rw_ln_b"], p["gla_norm"], p["gate_b"], p["w_branch"], p["w_out"]]
    return pl.pallas_call(
        functools.partial(_merge_kernel, tiles_per_seq=lp // tm),
        grid=(t // tm,),
        in_specs=[pl.BlockSpec((tm, D_MODEL), row), w256, w256, w256, w256, w256, w256,
                  pl.BlockSpec((tm, 256), lambda i: (i, 2)),
                  pl.BlockSpec((tm, W_GATE), row)]
        + [_const_spec(c.shape) for c in consts],
        out_specs=pl.BlockSpec((tm, D_MODEL), row),
        out_shape=jax.ShapeDtypeStruct((t, D_MODEL), F32),
        compiler_params=_params(("parallel",)),
        name="merge",
    )(h, y_mla, y_rw, bonus, g, y_sb, o_gla, z_gla, z_gate, *consts)


def _ffn_kernel(h_ref, g_ref, win_ref, cw_ref, cb_ref, wout_ref, o_ref, tail_ref):
    @pl.when(pl.program_id(1) == 0)
    def _():
        tail_ref[...] = jnp.zeros_like(tail_ref)

    x = h_ref[...]
    tm = x.shape[0]
    n = _bf(_rms(x, EPS) * g_ref[...])
    rowi = _iota((tm, 1), 0)
    acc = jnp.zeros((tm, D_MODEL), F32)
    for c0 in range(0, D_FF, FFN_COL_CHUNK):
        c1 = c0 + FFN_COL_CHUNK
        a = jnp.dot(n, win_ref[:, c0:c1], preferred_element_type=F32)
        u = jnp.dot(n, win_ref[:, D_FF + c0:D_FF + c1], preferred_element_type=F32)
        prev1 = tail_ref[7:8, c0:c1]
        prev2 = tail_ref[6:7, c0:c1]
        a1 = jnp.where(rowi == 0, prev1, pltpu.roll(a, 1, 0))
        a2 = jnp.where(rowi == 0, prev2, jnp.where(rowi == 1, prev1, pltpu.roll(a, 2, 0)))
        tail_ref[:, c0:c1] = a[tm - 8:tm, :]
        conv = cb_ref[:, c0:c1] + cw_ref[0:1, c0:c1] * a2 + cw_ref[1:2, c0:c1] * a1 + cw_ref[2:3, c0:c1] * a
        acc = acc + _dot(conv * jax.nn.sigmoid(conv) * u, wout_ref[c0:c1, :])
    o_ref[...] = x + acc


def _ffn(h, g, w_in, conv_w, conv_b, w_out, lp):
    t = h.shape[0]
    tm = ROW_TILE
    nb = lp // tm
    row = lambda b, i: (b * nb + i, 0)
    return pl.pallas_call(
        _ffn_kernel,
        grid=(t // lp, nb),
        in_specs=[pl.BlockSpec((tm, D_MODEL), row), _const_spec((1, D_MODEL)), _const_spec(w_in.shape),
                  _const_spec(conv_w.shape), _const_spec(conv_b.shape), _const_spec(w_out.shape)],
        out_specs=pl.BlockSpec((tm, D_MODEL), row),
        out_shape=jax.ShapeDtypeStruct((t, D_MODEL), F32),
        scratch_shapes=[pltpu.VMEM((8, D_FF), F32)],
        compiler_params=_params(("arbitrary", "arbitrary")),
        name="conv_ffn",
    )(h, g, w_in, conv_w, conv_b, w_out)


def _final_norm_kernel(h_ref, g_ref, o_ref):
    o_ref[0] = _rms(h_ref[0], EPS) * g_ref[...]


def _final_norm(h3, g, seq):
    b = h3.shape[0]
    first = (PAD + N_META) // BLOCK
    return pl.pallas_call(
        _final_norm_kernel,
        grid=(b, seq // BLOCK),
        in_specs=[pl.BlockSpec((1, BLOCK, D_MODEL), lambda bi, i: (bi, i + first, 0)), _const_spec((1, D_MODEL))],
        out_specs=pl.BlockSpec((1, BLOCK, D_MODEL), lambda bi, i: (bi, i, 0)),
        out_shape=jax.ShapeDtypeStruct((b, seq, D_MODEL), F32),
        compiler_params=_params(("parallel", "parallel")),
        name="final_norm",
    )(h3, g)


def _rope_swap(w):
    half = w.shape[-1] // 2
    return jnp.concatenate([-w[..., half:], w[..., :half]], axis=-1)


def _layout_params(w_in, mla_w_uq, mla_w_ukv, rw_w2, rw_a2, gla_a2):
    kr = w_in[..., 384:416]
    w_mla = jnp.concatenate([w_in[..., 0:384], jnp.tile(kr, (1, 1, 4)), jnp.tile(_rope_swap(kr), (1, 1, 4))], axis=-1)
    zg = w_in[..., 2208:2992]
    w_gla = jnp.concatenate([zg[..., 0:512], zg[..., 528:784], zg[..., 512:528],
                             jnp.zeros(zg.shape[:-1] + (W_GLA - 784,), zg.dtype)], axis=-1)
    sb_q = w_in[..., 1440:1440 + SB_W] * (SB_HEAD ** -0.5 * LOG2_E)
    w_all = _bf(jnp.concatenate([w_mla, w_in[..., 416:1440], sb_q, w_in[..., 1440 + SB_W:2208], w_gla,
                                 w_in[..., 2992:]], axis=-1))

    depth = w_in.shape[0]
    wuq = mla_w_uq.reshape(depth, MLA_Q_RANK, N_HEADS, MLA_NOPE + MLA_ROPE)
    nope, rope = wuq[..., :MLA_NOPE], wuq[..., MLA_NOPE:]
    rope_sw = _rope_swap(rope)
    zeros64 = jnp.zeros((depth, MLA_Q_RANK, 64), w_in.dtype)
    pair = lambda x, a, b: jnp.concatenate([x[:, :, a], x[:, :, b]], axis=-1)
    rope_pair = lambda x, a, b: jnp.concatenate([x[:, :, a], x[:, :, b], zeros64], axis=-1)
    wq = _bf(jnp.concatenate([pair(nope, 0, 1), rope_pair(rope, 0, 1), pair(nope, 2, 3), rope_pair(rope, 2, 3),
                              rope_pair(rope_sw, 0, 1), rope_pair(rope_sw, 2, 3)], axis=-1))
    wukv = mla_w_ukv.reshape(depth, MLA_KV_RANK, N_HEADS, 128)
    wkv = _bf(jnp.concatenate([wukv[..., :64].reshape(depth, MLA_KV_RANK, 256),
                               wukv[..., 64:].reshape(depth, MLA_KV_RANK, 256)], axis=-1))

    z64 = jnp.zeros_like(rw_w2)
    w2 = _bf(jnp.concatenate([rw_w2, z64], axis=1))
    a2 = _bf(jnp.concatenate([z64, rw_a2], axis=1))
    gla_a2p = _bf(jnp.concatenate([gla_a2, jnp.zeros((depth, 128 - gla_a2.shape[1], GLA_QK), gla_a2.dtype)], axis=1))
    return w_all, wq, wkv, w2, a2, gla_a2p


def _rope_tables(lp):
    half = MLA_ROPE // 2
    freqs = ROPE_THETA ** (-jnp.arange(half, dtype=F32) / half)
    pos = (jnp.arange(lp) - PAD).astype(F32)
    ang = pos[:, None] * freqs[None, :]
    return jnp.tile(jnp.cos(ang), (1, 128 // half)), jnp.tile(jnp.sin(ang), (1, 128 // half))


def kernel(x, meta_tokens, norm_mix, w_in, mla_q_norm, mla_w_uq, mla_kv_norm, mla_w_ukv, rw_mu, rw_w0, rw_w2, rw_a0, rw_a2, rw_g2, rw_k_k, rw_k_a, rw_r_k, rw_ln_w, rw_ln_b, gla_a2, gla_a_b, gla_norm, gate_b, w_branch, w_out, norm_ffn, w_ffn_in, ffn_conv_w, ffn_conv_b, w_ffn_out, norm_final):
    batch, seq, _ = x.shape
    depth = w_in.shape[0]
    lp = PAD + N_META + seq
    t = batch * lp
    assert lp % ROW_TILE == 0 and lp % ATTN_TILE == 0 and lp % GLA_CHUNK == 0 and lp % RW_CHUNK == 0

    w_all, wq, wkv, rw_w2p, rw_a2p, gla_a2p = _layout_params(w_in, mla_w_uq, mla_w_ukv, rw_w2, rw_a2, gla_a2)
    w_branch_b, w_out_b, w_ffn_in_b, w_ffn_out_b, rw_g2_b = map(_bf, (w_branch, w_out, w_ffn_in, w_ffn_out, rw_g2))
    vec = lambda a: a.reshape(depth, 1, -1)
    cos, sin = _rope_tables(lp)
    idx = jnp.arange(SB_SUFFIX_SPLIT)
    later_mat = jnp.where(idx[:, None] > idx[None, :], 1.0, 0.0).astype(BF16)

    meta = jnp.broadcast_to(meta_tokens[None].astype(x.dtype), (batch, N_META, D_MODEL))
    h = jnp.concatenate([jnp.zeros((batch, PAD, D_MODEL), x.dtype), meta, x], axis=1).reshape(t, D_MODEL)

    for i in range(depth):
        z_mla, z_rw, z_sb, z_gla, z_gate = _in_proj(h, vec(norm_mix)[i], w_all[i])
        q, k, v = _mla_prep(z_mla, vec(mla_q_norm)[i], vec(mla_kv_norm)[i], wq[i], wkv[i], cos, sin, lp)
        y_mla = _mla_attn(q, k, v, lp)
        rw = {"rw_mu": vec(rw_mu)[i], "rw_w0": vec(rw_w0)[i], "rw_w2": rw_w2p[i], "rw_a0": vec(rw_a0)[i],
              "rw_a2": rw_a2p[i], "rw_g2": rw_g2_b[i], "rw_k_k": vec(rw_k_k)[i], "rw_k_a": vec(rw_k_a)[i],
              "rw_r_k": vec(rw_r_k)[i]}
        *chunk_terms, g, bonus = _rw_prep(z_rw, rw, lp)
        y_rw = _rw_scan(*chunk_terms, batch)
        y_sb = _sb_attn(z_sb, later_mat, lp)
        o_gla = _gla(z_gla, gla_a2p[i], vec(gla_a_b)[i], lp)
        mp = {"rw_ln_w": vec(rw_ln_w)[i], "rw_ln_b": vec(rw_ln_b)[i], "gla_norm": vec(gla_norm)[i],
              "gate_b": gate_b[i], "w_branch": w_branch_b[i], "w_out": w_out_b[i]}
        h = _merge(h, y_mla, y_rw, bonus, g, y_sb, o_gla, z_gla, z_gate, mp, lp)
        h = _ffn(h, vec(norm_ffn)[i], w_ffn_in_b[i], ffn_conv_w[i], vec(ffn_conv_b)[i], w_ffn_out_b[i], lp)
    return _final_norm(h.reshape(batch, lp, D_MODEL), norm_final.reshape(1, D_MODEL), seq)
```

```python
import functools

import jax
import jax.numpy as jnp
from jax import lax
from jax.experimental import pallas as pl
from jax.experimental.pallas import tpu as pltpu

F32 = jnp.float32
BF16 = jnp.bfloat16

D_MODEL = 1024
DEPTH = 4
N_META = 16
BLOCK = 128
PAD = (-N_META) % BLOCK
EPS = 1e-6
NEG_INF = -1e30
LOG2_E = 1.4426950408889634

N_HEADS = 4
MLA_NOPE = 64
MLA_ROPE = 32
MLA_Q_RANK = 256
MLA_KV_RANK = 128
ROPE_THETA = 10000.0

RW_W = 256
RW_GN_EPS = 64e-5
RW_CHUNK = 64

SB_W = 256
SB_HEAD = 64
SB_SUFFIX_SPLIT = 256

GLA_DK = 32
GLA_QK = 128
GLA_W = 256
GLA_TAU = 16.0
GLA_CHUNK = 128

D_FF = 2816
FFN_COL_CHUNK = 1408

W_MLA, W_RW, W_SB, W_GLA, W_GATE = 640, 1024, 768, 896, 4096
IN_GROUP_WIDTHS = (W_MLA, W_RW, W_SB, W_GLA, W_GATE)
IN_GROUP_DTYPES = (BF16, F32, BF16, F32, BF16)

ROW_TILE = 384
ATTN_TILE = 384
VMEM_LIMIT = 56 * 1024 * 1024


def _bf(x):
    return x.astype(BF16)


def _dot(a, b):
    return jnp.dot(_bf(a), _bf(b), preferred_element_type=F32)


def _dot_nt(a, b):
    return lax.dot_general(_bf(a), _bf(b), (((1,), (1,)), ((), ())), preferred_element_type=F32)


def _dot_tn(a, b):
    return lax.dot_general(_bf(a), _bf(b), (((0,), (0,)), ((), ())), preferred_element_type=F32)


def _split_hi_lo(x):
    hi = _bf(x)
    lo = _bf(x - hi.astype(F32))
    return hi, lo


def _dot_exact_rhs(x, m):
    hi, lo = _split_hi_lo(x)
    return jnp.dot(hi, m, preferred_element_type=F32) + jnp.dot(lo, m, preferred_element_type=F32)


def _dot_exact_lhs(m, x):
    hi, lo = _split_hi_lo(x)
    return jnp.dot(m, hi, preferred_element_type=F32) + jnp.dot(m, lo, preferred_element_type=F32)


def _iota(shape, dim):
    return lax.broadcasted_iota(jnp.int32, shape, dim)


def _div_pow2(x, d):
    assert d & (d - 1) == 0
    return lax.shift_right_logical(x, d.bit_length() - 1)


def _same_segment(shape, row_seg, col_seg):
    return _div_pow2(_iota(shape, 0), row_seg) == _div_pow2(_iota(shape, 1), col_seg)


def _segment_matrix(n, seg, value):
    return jnp.where(_same_segment((n, n), seg, seg), value, 0.0).astype(BF16)


def _head_lane_mask(width, head_width, h):
    lane = _iota((1, width), 1)
    return (lane >= h * head_width) & (lane < (h + 1) * head_width)


def _stack_heads(x, head_width):
    w = x.shape[1]
    return jnp.concatenate(
        [jnp.where(_head_lane_mask(w, head_width, h), x, 0.0) for h in range(N_HEADS)], axis=0)


def _unstack_heads(xs, c):
    return xs[0:c] + xs[c:2 * c] + xs[2 * c:3 * c] + xs[3 * c:4 * c]


def _rms(x, eps):
    return x * lax.rsqrt(jnp.mean(x * x, axis=-1, keepdims=True) + eps)


def _const_spec(shape):
    nd = len(shape)
    return pl.BlockSpec(shape, lambda *_: (0,) * nd)


def _params(sem, vmem=VMEM_LIMIT):
    return pltpu.CompilerParams(dimension_semantics=sem, vmem_limit_bytes=vmem)


def _in_proj_kernel(h_ref, g_ref, w_ref, *out_refs):
    n = _bf(_rms(h_ref[...], EPS) * g_ref[...])
    off = 0
    for o_ref, width in zip(out_refs, IN_GROUP_WIDTHS):
        for c0 in range(0, width, 1024):
            c1 = min(c0 + 1024, width)
            o_ref[:, c0:c1] = jnp.dot(
                n, w_ref[:, off + c0:off + c1], preferred_element_type=F32).astype(o_ref.dtype)
        off += width


def _in_proj(h, g, w_all):
    t = h.shape[0]
    tm = ROW_TILE
    return pl.pallas_call(
        _in_proj_kernel,
        grid=(t // tm,),
        in_specs=[pl.BlockSpec((tm, D_MODEL), lambda i: (i, 0)),
                  _const_spec((1, D_MODEL)),
                  _const_spec(w_all.shape)],
        out_specs=[pl.BlockSpec((tm, w), lambda i: (i, 0)) for w in IN_GROUP_WIDTHS],
        out_shape=[jax.ShapeDtypeStruct((t, w), dt) for w, dt in zip(IN_GROUP_WIDTHS, IN_GROUP_DTYPES)],
        compiler_params=_params(("parallel",)),
        name="in_proj",
    )(h, g, w_all)


def _mla_prep_kernel(z_ref, qg_ref, kvg_ref, wq_ref, wkv_ref, cos_ref, sin_ref, q_ref, k_ref, vt_ref):
    z = z_ref[...].astype(F32)
    nq = _rms(z[:, 0:256], EPS) * qg_ref[...]
    nkv = _rms(z[:, 256:384], EPS) * kvg_ref[...]
    cos = cos_ref[...]
    sin = sin_ref[...]
    ql = _dot(nq, wq_ref[...])
    scale = (MLA_NOPE + MLA_ROPE) ** -0.5
    q = jnp.concatenate([ql[:, 0:128], ql[:, 128:256] * cos + ql[:, 512:640] * sin,
                         ql[:, 256:384], ql[:, 384:512] * cos + ql[:, 640:768] * sin], axis=1)
    q_ref[...] = _bf(q * scale)
    kvl = _dot(nkv, wkv_ref[...])
    k_rope = z[:, 384:512] * cos + z[:, 512:640] * sin
    k_rope = jnp.where(_iota((1, 128), 1) < 2 * MLA_ROPE, k_rope, 0.0)
    k_ref[...] = _bf(jnp.concatenate([kvl[:, 0:128], k_rope, kvl[:, 128:256], k_rope], axis=1))
    vt_ref[0] = _bf(kvl[:, 256:512].T)


def _mla_prep(z_mla, qg, kvg, wq, wkv, cos, sin, lp):
    t = z_mla.shape[0]
    tm = ROW_TILE
    nb = lp // tm
    row = lambda b, i: (b * nb + i, 0)
    return pl.pallas_call(
        _mla_prep_kernel,
        grid=(t // lp, nb),
        in_specs=[pl.BlockSpec((tm, W_MLA), row),
                  _const_spec((1, MLA_Q_RANK)), _const_spec((1, MLA_KV_RANK)),
                  _const_spec(wq.shape), _const_spec(wkv.shape),
                  pl.BlockSpec((tm, 128), lambda b, i: (i, 0)),
                  pl.BlockSpec((tm, 128), lambda b, i: (i, 0))],
        out_specs=[pl.BlockSpec((tm, 512), row), pl.BlockSpec((tm, 512), row),
                   pl.BlockSpec((1, 256, tm), lambda b, i: (b * nb + i, 0, 0))],
        out_shape=[jax.ShapeDtypeStruct((t, 512), BF16), jax.ShapeDtypeStruct((t, 512), BF16),
                   jax.ShapeDtypeStruct((t // tm, 256, tm), BF16)],
        compiler_params=_params(("parallel", "parallel")),
        name="mla_prep",
    )(z_mla, qg, kvg, wq, wkv, cos, sin)


def _per_head_lanes(cols):
    lane = _iota((1, 256), 1)
    return jnp.where(lane < 64, cols[0], jnp.where(lane < 128, cols[1], jnp.where(lane < 192, cols[2], cols[3])))


def _head_stacked_values(vb):
    zero = jnp.zeros((), vb.dtype)
    return jnp.concatenate([jnp.where(_head_lane_mask(256, 64, h), vb, zero) for h in range(N_HEADS)], axis=0)


def _per_head_rows(rows, n):
    return jnp.concatenate([jnp.broadcast_to(r, (64, n)) for r in rows], axis=0)


def _mla_attn_kernel(q_ref, k_ref, vt_ref, o_ref, qh_ref, m_ref, l_ref, acc_ref, p_ref, alpha_ref, *, tile):
    i = pl.program_id(1)
    q_pos = i * tile + _iota((1, tile), 1)
    lane = _iota((1, 256), 1)
    for h in range(N_HEADS):
        half, slot = h // 2, h % 2
        head_lanes = (((lane >= slot * MLA_NOPE) & (lane < (slot + 1) * MLA_NOPE))
                      | ((lane >= 128 + slot * MLA_ROPE) & (lane < 128 + (slot + 1) * MLA_ROPE)))
        qh_ref[h] = jnp.where(head_lanes, q_ref[:, half * 256:(half + 1) * 256], jnp.zeros((), BF16))
    m_ref[...] = jnp.full(m_ref.shape, NEG_INF, F32)
    l_ref[...] = jnp.zeros(l_ref.shape, F32)
    acc_ref[...] = jnp.zeros(acc_ref.shape, F32)
    value_row_head = _div_pow2(_iota((256, 1), 0), 64)

    def add_values(j, slot):
        vt = vt_ref[j]
        zero = jnp.zeros((), BF16)
        vt_heads = jnp.concatenate([jnp.where(value_row_head == h, vt, zero) for h in range(N_HEADS)], axis=1)
        pv = jnp.dot(vt_heads, p_ref[slot], preferred_element_type=F32)
        acc_ref[...] = _per_head_rows([alpha_ref[slot, h] for h in range(N_HEADS)], tile) * acc_ref[...] + pv

    def block(j, masked, prev, slot):
        start = pl.multiple_of(j * tile, tile)
        heads = range(N_HEADS)
        s = [_dot_nt(k_ref[pl.ds(start, tile), (h // 2) * 256:(h // 2 + 1) * 256], qh_ref[h]) for h in heads]
        if prev is not None:
            add_values(prev, 1 - slot)
        for h in heads:
            s_h = s[h]
            if masked:
                k_pos = start + _iota((tile, 1), 0)
                s_h = jnp.where((k_pos <= q_pos) & (k_pos >= PAD), s_h, NEG_INF)
            m_old = m_ref[h]
            m_new = jnp.maximum(m_old, jnp.max(s_h, axis=0, keepdims=True))
            p = jnp.exp(s_h - m_new)
            alpha = jnp.exp(m_old - m_new)
            l_ref[h] = alpha * l_ref[h] + jnp.sum(p, axis=0, keepdims=True)
            m_ref[h] = m_new
            p_ref[slot, h * tile:(h + 1) * tile, :] = _bf(p)
            alpha_ref[slot, h] = alpha

    block(i, True, None, 0)

    @pl.when(i > 0)
    def _():
        block(0, True, i, 1)

    interior_blocks = jnp.maximum(i - 1, 0)

    def interior_pair(t, carry):
        j = 1 + 2 * t
        block(j, False, j - 1, 0)
        block(j + 1, False, j, 1)
        return carry

    lax.fori_loop(0, lax.shift_right_logical(interior_blocks, 1), interior_pair, 0)
    odd = (interior_blocks & 1) == 1

    @pl.when(odd)
    def _():
        block(i - 1, False, i - 2, 0)

    last = jnp.maximum(i - 1, 0)
    last_in_slot0 = (i == 0) | odd

    @pl.when(last_in_slot0)
    def _():
        add_values(last, 0)

    @pl.when(jnp.logical_not(last_in_slot0))
    def _():
        add_values(last, 1)
    o_ref[...] = (acc_ref[...] / _per_head_rows([l_ref[h] for h in range(N_HEADS)], tile)).T


def _mla_attn(q, k, vt, lp):
    t = q.shape[0]
    tile = ATTN_TILE
    nb = lp // tile
    return pl.pallas_call(
        functools.partial(_mla_attn_kernel, tile=tile),
        grid=(t // lp, nb),
        in_specs=[pl.BlockSpec((tile, 512), lambda b, i: (b * nb + i, 0)),
                  pl.BlockSpec((lp, 512), lambda b, i: (b, 0)),
                  pl.BlockSpec((nb, 256, tile), lambda b, i: (b, 0, 0))],
        out_specs=pl.BlockSpec((tile, 256), lambda b, i: (b * nb + i, 0)),
        out_shape=jax.ShapeDtypeStruct((t, 256), F32),
        scratch_shapes=[pltpu.VMEM((N_HEADS, tile, 256), BF16), pltpu.VMEM((N_HEADS, 1, tile), F32),
                        pltpu.VMEM((N_HEADS, 1, tile), F32), pltpu.VMEM((256, tile), F32),
                        pltpu.VMEM((2, N_HEADS * tile, tile), BF16), pltpu.VMEM((2, N_HEADS, 1, tile), F32)],
        compiler_params=_params(("parallel", "arbitrary")),
        name="mla_attn",
    )(q, k, vt)


def _sb_attn_kernel(q_ref, k_ref, v_ref, later_ref, o_ref, qh_ref, c_ref, acc_ref, wts_ref, *, tile):
    i = pl.program_id(1)
    row = i * tile + _iota((tile, 1), 0)
    for h in range(N_HEADS):
        qh_ref[h] = jnp.where(_head_lane_mask(SB_W, SB_HEAD, h), q_ref[...], jnp.zeros((), BF16))
    c_ref[...] = jnp.zeros(c_ref.shape, F32)
    acc_ref[...] = jnp.zeros(acc_ref.shape, F32)

    def add_values(j, slot):
        start = pl.multiple_of(j * tile, tile)
        acc_ref[...] += jnp.dot(wts_ref[slot], _head_stacked_values(v_ref[pl.ds(start, tile), :]),
                                preferred_element_type=F32)

    def block(j, masked, prev, slot):
        start = pl.multiple_of(j * tile, tile)
        heads = range(N_HEADS)
        cut = SB_SUFFIX_SPLIT
        z = [_dot_nt(qh_ref[h], k_ref[pl.ds(start, tile), :]) for h in heads]
        if prev is not None:
            add_values(prev, 1 - slot)
        log_take = [jnp.minimum(z[h], 0.0) - jnp.log2(1.0 + jnp.exp2(-jnp.abs(z[h]))) for h in heads]
        log_keep = [log_take[h] - z[h] for h in heads]
        if masked:
            col = start + _iota((1, tile), 1)
            mask = (col < row) & (col >= PAD)
            log_keep = [jnp.where(mask, log_keep[h], 0.0) for h in heads]
        keep16 = [_bf(log_keep[h]) for h in heads]
        later_head = [jnp.dot(keep16[h][:, :cut], later_ref[...], preferred_element_type=F32) for h in heads]
        later_tail = [jnp.dot(keep16[h][:, cut:], later_ref[:tile - cut, :tile - cut], preferred_element_type=F32)
                      for h in heads]
        for h in heads:
            head_sum = jnp.sum(log_keep[h][:, :cut], axis=-1, keepdims=True)
            tail_sum = jnp.sum(log_keep[h][:, cut:], axis=-1, keepdims=True)
            c = c_ref[h]
            later = jnp.concatenate([later_head[h] + tail_sum, later_tail[h]], axis=1)
            w = jnp.exp2(log_take[h] + later + c)
            if masked:
                w = jnp.where(mask, w, 0.0)
            c_ref[h] = c + (head_sum + tail_sum)
            wts_ref[slot, :, h * tile:(h + 1) * tile] = _bf(w)

    block(i, True, None, 0)
    interior_blocks = jnp.maximum(i - 1, 0)

    def interior_pair(t, carry):
        j = i - 1 - 2 * t
        block(j, False, j + 1, 1)
        block(j - 1, False, j, 0)
        return carry

    lax.fori_loop(0, lax.shift_right_logical(interior_blocks, 1), interior_pair, 0)
    odd = (interior_blocks & 1) == 1

    @pl.when(odd)
    def _():
        block(1, False, 2, 1)
        block(0, True, 1, 0)
        add_values(0, 0)

    @pl.when(jnp.logical_not(odd) & (i > 0))
    def _():
        block(0, True, 1, 1)
        add_values(0, 1)

    @pl.when(i == 0)
    def _():
        add_values(0, 0)
    o_ref[...] = acc_ref[...]


def _sb_attn(z_sb, later_mat, lp):
    t = z_sb.shape[0]
    tile = ATTN_TILE
    nb = lp // tile
    return pl.pallas_call(
        functools.partial(_sb_attn_kernel, tile=tile),
        grid=(t // lp, nb),
        in_specs=[pl.BlockSpec((tile, SB_W), lambda b, i: (b * nb + i, 0)),
                  pl.BlockSpec((lp, SB_W), lambda b, i: (b, 1)),
                  pl.BlockSpec((lp, SB_W), lambda b, i: (b, 2)),
                  _const_spec((SB_SUFFIX_SPLIT, SB_SUFFIX_SPLIT))],
        out_specs=pl.BlockSpec((tile, SB_W), lambda b, i: (b * nb + i, 0)),
        out_shape=jax.ShapeDtypeStruct((t, SB_W), F32),
        scratch_shapes=[pltpu.VMEM((N_HEADS, tile, SB_W), BF16), pltpu.VMEM((N_HEADS, tile, 1), F32),
                        pltpu.VMEM((tile, SB_W), F32), pltpu.VMEM((2, tile, N_HEADS * tile), BF16)],
        compiler_params=_params(("parallel", "arbitrary")),
        name="sb_attn",
    )(z_sb, z_sb, z_sb, later_mat)


def _compact_eye(c):
    return jnp.where(_iota((c, N_HEADS * c), 0) == (_iota((c, N_HEADS * c), 1) & (c - 1)), 1.0, 0.0)


def _rw_prep_kernel(z_ref, zprev_ref, mu_ref, w0_ref, w2_ref, a0_ref, a2_ref, g2_ref, kk_ref, ka_ref, rk_ref,
                    w_o, rt_o, arb_o, kbar_o, bbar_o, v_o, uv_o, yv_o, dec_o, g_o, bonus_o):
    i = pl.program_id(1)
    z = z_ref[...]
    tm = z.shape[0]
    prev = jnp.where(i == 0, 0.0, zprev_ref[7:8, :])
    shifted = jnp.where(_iota((tm, 1), 0) == 0, prev, pltpu.roll(z, 1, 0))
    zz = z + (shifted - z) * mu_ref[...]
    r, k, v = zz[:, 0:256], zz[:, 256:512], zz[:, 512:768]
    lora_in = zz[:, 768:896]
    u = w0_ref[...] + _dot(jnp.tanh(lora_in), w2_ref[...])
    w = jnp.minimum(u, 0.0) - jnp.log1p(jnp.exp(-jnp.abs(u))) - 0.5
    a = jax.nn.sigmoid(a0_ref[...] + _dot(lora_in, a2_ref[...]))
    seg = _segment_matrix(RW_W, 64, 1.0)
    kx = k * kk_ref[...]
    kap = kx / jnp.maximum(jnp.sqrt(_dot_exact_rhs(kx * kx, seg)), 1e-12)
    kmod = k * (1.0 + (a - 1.0) * ka_ref[...])
    beta = kap * a
    lw = -jnp.exp(w)
    g_o[...] = _dot(jax.nn.sigmoid(zz[:, 896:1024]), g2_ref[...])
    bonus_o[...] = _dot_exact_rhs(r * kmod * rk_ref[...], seg) * v

    c = RW_CHUNK
    same_chunk = _same_segment((tm, tm), c, c)
    lw_hi, lw_lo = _split_hi_lo(lw)
    cum = jnp.where(same_chunk & (_iota((tm, tm), 0) >= _iota((tm, tm), 1)), 1.0, 0.0).astype(BF16)
    tot = jnp.where(same_chunk, 1.0, 0.0).astype(BF16)
    b = jnp.dot(cum, lw_hi, preferred_element_type=F32) + jnp.dot(cum, lw_lo, preferred_element_type=F32)
    b_end = jnp.dot(tot, lw_hi, preferred_element_type=F32) + jnp.dot(tot, lw_lo, preferred_element_type=F32)
    grow = jnp.exp(-b)
    to_end = jnp.exp(b_end - b)
    kap_t = kap * jnp.exp(b - lw)
    r_t = r * jnp.exp(b)
    beta_g = beta * grow
    k_g = kmod * grow
    rt_o[...] = _bf(r_t)
    kbar_o[...] = _bf(kmod * to_end)
    bbar_o[...] = _bf(beta * to_end)
    v_o[...] = _bf(v)
    decay_end = jnp.exp(b_end)

    lane_pos = _iota((c, N_HEADS * c), 1) & (c - 1)
    strictly_earlier = _iota((c, N_HEADS * c), 0) > lane_pos
    not_later = _iota((c, N_HEADS * c), 0) >= lane_pos
    chunks = [slice(n * c, (n + 1) * c) for n in range(tm // c)]
    pair = [_dot_nt(jnp.concatenate([kap_t[rows], r_t[rows]], axis=0),
                    jnp.concatenate([_stack_heads(beta_g[rows], 64), _stack_heads(k_g[rows], 64)], axis=0))
            for rows in chunks]
    a_kb = [jnp.where(strictly_earlier, p[0:c, 0:4 * c], 0.0) for p in pair]
    a_kk = [jnp.where(strictly_earlier, p[0:c, 4 * c:8 * c], 0.0) for p in pair]
    a_rk = [jnp.where(not_later, p[c:2 * c, 4 * c:8 * c], 0.0) for p in pair]
    for rows, p in zip(chunks, pair):
        arb_o[rows, :] = _bf(jnp.where(not_later, p[c:2 * c, 0:4 * c], 0.0))
    inv = [_compact_eye(c) - a for a in a_kb]
    power = [_dot(a, _stack_heads(a, c)) for a in a_kb]
    span = 2
    while span < c:
        inv = [t + _dot(t, _stack_heads(p, c)) for t, p in zip(inv, power)]
        span *= 2
        if span < c:
            power = [_dot(p, _stack_heads(p, c)) for p in power]
    v_st = [_stack_heads(v[rows], 64) for rows in chunks]
    kk_v = [_dot(a, vs) for a, vs in zip(a_kk, v_st)]
    for n, rows in enumerate(chunks):
        w_o[rows, :] = _bf(_dot(inv[n], _stack_heads(kap_t[rows], 64)))
        uv_o[rows, :] = _dot(inv[n], _stack_heads(kk_v[n], 64))
        yv_o[rows, :] = _dot(a_rk[n], v_st[n])
        dec_o[n * 8:(n + 1) * 8, :] = decay_end[n * c:n * c + 8]


def _rw_prep(z_rw, p, lp):
    t = z_rw.shape[0]
    tm = ROW_TILE
    nb = lp // tm
    row = lambda b, i: (b * nb + i, 0)
    prev = lambda b, i: (jnp.maximum(b * (lp // 8) + i * (tm // 8) - 1, 0), 0)
    consts = [p["rw_mu"], p["rw_w0"], p["rw_w2"], p["rw_a0"], p["rw_a2"], p["rw_g2"],
              p["rw_k_k"], p["rw_k_a"], p["rw_r_k"]]
    return pl.pallas_call(
        _rw_prep_kernel,
        grid=(t // lp, nb),
        in_specs=[pl.BlockSpec((tm, W_RW), row), pl.BlockSpec((8, W_RW), prev)]
        + [_const_spec(c.shape) for c in consts],
        out_specs=[pl.BlockSpec((tm, RW_W), row)] * 8
        + [pl.BlockSpec((tm // RW_CHUNK * 8, RW_W), row)] + [pl.BlockSpec((tm, RW_W), row)] * 2,
        out_shape=[jax.ShapeDtypeStruct((t, RW_W), BF16)] * 6 + [jax.ShapeDtypeStruct((t, RW_W), F32)] * 2
        + [jax.ShapeDtypeStruct((t // RW_CHUNK * 8, RW_W), F32)] + [jax.ShapeDtypeStruct((t, RW_W), F32)] * 2,
        compiler_params=_params(("parallel", "parallel")),
        name="rw_prep",
    )(z_rw, z_rw, *consts)


def _rw_scan_kernel(w_ref, rt_ref, arb_ref, kbar_ref, bbar_ref, v_ref, uv_ref, yv_ref, dec_ref, y_ref, ht_ref):
    c = RW_CHUNK

    @pl.when(pl.program_id(0) == 0)
    def _():
        ht_ref[...] = jnp.zeros_like(ht_ref)

    same_head = _same_segment((RW_W, RW_W), 64, 64)
    seqs = range(ht_ref.shape[0])
    ht = [ht_ref[s] for s in seqs]
    from_state = [_dot_nt(jnp.concatenate([w_ref[s], rt_ref[s]], axis=0), ht[s]) for s in seqs]
    u = [from_state[s][0:c] + uv_ref[s] for s in seqs]
    from_u = [_dot(arb_ref[s], _stack_heads(u[s], 64)) for s in seqs]
    upd = [_dot_tn(jnp.concatenate([v_ref[s], _bf(u[s])], axis=0),
                   jnp.concatenate([kbar_ref[s], -bbar_ref[s]], axis=0)) for s in seqs]
    for s in seqs:
        y_ref[s] = from_state[s][c:2 * c] + yv_ref[s] - from_u[s]
        ht_ref[s] = ht[s] * dec_ref[s, 0:1, :] + jnp.where(same_head, upd[s], 0.0)


def _rw_scan(w, rt, arb, kbar, bbar, v, uv, yv, dec, batch):
    t = w.shape[0]
    lp = t // batch
    c = RW_CHUNK
    as_seq = lambda a: a.reshape(batch, a.shape[0] // batch, RW_W)
    spec = pl.BlockSpec((batch, c, RW_W), lambda i: (0, i, 0))
    y = pl.pallas_call(
        _rw_scan_kernel,
        grid=(lp // c,),
        in_specs=[spec] * 8 + [pl.BlockSpec((batch, 8, RW_W), lambda i: (0, i, 0))],
        out_specs=spec,
        out_shape=jax.ShapeDtypeStruct((batch, lp, RW_W), F32),
        scratch_shapes=[pltpu.VMEM((batch, RW_W, RW_W), F32)],
        compiler_params=_params(("arbitrary",)),
        name="rw_scan",
    )(*map(as_seq, (w, rt, arb, kbar, bbar, v, uv, yv, dec)))
    return y.reshape(t, RW_W)


def _gla_kernel(q_ref, k_ref, v_ref, al_ref, a2_ref, ab_ref, o_ref, st_ref):
    c = GLA_CHUNK
    n = N_HEADS * c
    seqs = range(st_ref.shape[0])

    @pl.when(pl.program_id(0) == 0)
    def _():
        st_ref[...] = jnp.zeros_like(st_ref)

    tri = jnp.where(_iota((c, c), 0) >= _iota((c, c), 1), 1.0, 0.0).astype(BF16)
    causal = (_iota((n, c), 0) & (c - 1)) >= _iota((n, c), 1)
    same_head = _same_segment((GLA_W, GLA_QK), 64, GLA_DK)
    x = [_dot(al_ref[s], a2_ref[...]) + ab_ref[...] for s in seqs]
    log_a = [(jnp.minimum(x[s], 0.0) - jnp.log1p(jnp.exp(-jnp.abs(x[s])))) * (1.0 / GLA_TAU) for s in seqs]
    b = [_dot_exact_lhs(tri, log_a[s]) for s in seqs]
    q = [q_ref[s] * (GLA_DK ** -0.5) for s in seqs]
    st = [st_ref[s] for s in seqs]
    inter = [_dot_nt(q[s] * jnp.exp(b[s]), st[s]) for s in seqs]
    scores = []
    for s in seqs:
        b_mid = b[s][c // 2 - 1:c // 2, :]
        sc = _dot_nt(_stack_heads(q[s] * jnp.exp(b[s] - b_mid), GLA_DK), k_ref[s] * jnp.exp(b_mid - b[s]))
        scores.append(jnp.where(causal, sc, 0.0))
    per_head = [_dot(scores[s], v_ref[s]) for s in seqs]
    upd = [_dot_tn(v_ref[s], k_ref[s] * jnp.exp(b[s][c - 1:c, :] - b[s])) for s in seqs]
    for s in seqs:
        intra = jnp.zeros((c, GLA_W), F32)
        for h in range(N_HEADS):
            intra = intra + jnp.where(_head_lane_mask(GLA_W, 64, h), per_head[s][h * c:(h + 1) * c], 0.0)
        o_ref[s] = inter[s] + intra
        st_ref[s] = st[s] * jnp.exp(b[s][c - 1:c, :]) + jnp.where(same_head, upd[s], 0.0)


def _gla(z_gla, a2, ab, batch):
    t = z_gla.shape[0]
    lp = t // batch
    c = GLA_CHUNK
    z3 = z_gla.reshape(batch, lp, W_GLA)
    o = pl.pallas_call(
        _gla_kernel,
        grid=(lp // c,),
        in_specs=[pl.BlockSpec((batch, c, 128), lambda i: (0, i, 0)),
                  pl.BlockSpec((batch, c, 128), lambda i: (0, i, 1)),
                  pl.BlockSpec((batch, c, 256), lambda i: (0, i, 1)),
                  pl.BlockSpec((batch, c, 128), lambda i: (0, i, 6)),
                  _const_spec(a2.shape), _const_spec(ab.shape)],
        out_specs=pl.BlockSpec((batch, c, GLA_W), lambda i: (0, i, 0)),
        out_shape=jax.ShapeDtypeStruct((batch, lp, GLA_W), F32),
        scratch_shapes=[pltpu.VMEM((batch, GLA_W, GLA_QK), F32)],
        compiler_params=_params(("arbitrary",)),
        name="gla",
    )(z3, z3, z3, z3, a2, ab)
    return o.reshape(t, GLA_W)


def _merge_kernel(h_ref, ymla_ref, yrw_ref, bonus_ref, g_ref, ysb_ref, ogla_ref, rgla_ref, zg_ref,
                  lnw_ref, lnb_ref, gn_ref, gb_ref, wb_ref, wo_ref, o_ref, *, tiles_per_seq):
    tm = h_ref.shape[0]
    avg = _segment_matrix(256, 64, 1.0 / 64)

    y = yrw_ref[...]
    d = y - _dot_exact_rhs(y, avg)
    var = _dot_exact_rhs(d * d, avg)
    y_rw = (d * lax.rsqrt(var + RW_GN_EPS) * lnw_ref[...] + lnb_ref[...] + bonus_ref[...]) * g_ref[...]

    o = ogla_ref[...]
    r = rgla_ref[...]
    y_gla = o * lax.rsqrt(_dot_exact_rhs(o * o, avg) + EPS) * gn_ref[...] * (r * jax.nn.sigmoid(r))

    acc = jnp.zeros((tm, D_MODEL), F32)
    for m, y_m in enumerate((ymla_ref[...], y_rw, ysb_ref[...], y_gla)):
        gate = jax.nn.sigmoid(zg_ref[:, m * D_MODEL:(m + 1) * D_MODEL].astype(F32) + gb_ref[m:m + 1, :])
        acc = acc + gate * _dot(y_m, wb_ref[m])
    delta = _dot(acc, wo_ref[...])
    row = (pl.program_id(0) % tiles_per_seq) * tm + _iota((tm, 1), 0)
    o_ref[...] = h_ref[...] + jnp.where(row >= PAD, delta, 0.0)


def _merge(h, y_mla, y_rw, bonus, g, y_sb, o_gla, z_gla, z_gate, p, lp):
    t = h.shape[0]
    tm = ROW_TILE
    row = lambda i: (i, 0)
    w256 = pl.BlockSpec((tm, 256), row)
    consts = [p["rw_ln_w"], p["rw_ln_b"], p["gla_norm"], p["gate_b"], p["w_branch"], p["w_out"]]
    return pl.pallas_call(
        functools.partial(_merge_kernel, tiles_per_seq=lp // tm),
        grid=(t // tm,),
        in_specs=[pl.BlockSpec((tm, D_MODEL), row), w256, w256, w256, w256, w256, w256,
                  pl.BlockSpec((tm, 256), lambda i: (i, 2)),
                  pl.BlockSpec((tm, W_GATE), row)]
        + [_const_spec(c.shape) for c in consts],
        out_specs=pl.BlockSpec((tm, D_MODEL), row),
        out_shape=jax.ShapeDtypeStruct((t, D_MODEL), F32),
        compiler_params=_params(("parallel",)),
        name="merge",
    )(h, y_mla, y_rw, bonus, g, y_sb, o_gla, z_gla, z_gate, *consts)


def _ffn_kernel(h_ref, g_ref, win_ref, cw_ref, cb_ref, wout_ref, o_ref, tail_ref):
    @pl.when(pl.program_id(1) == 0)
    def _():
        tail_ref[...] = jnp.zeros_like(tail_ref)

    x = h_ref[...]
    tm = x.shape[0]
    n = _bf(_rms(x, EPS) * g_ref[...])
    rowi = _iota((tm, 1), 0)
    acc = jnp.zeros((tm, D_MODEL), F32)
    for c0 in range(0, D_FF, FFN_COL_CHUNK):
        c1 = c0 + FFN_COL_CHUNK
        a = jnp.dot(n, win_ref[:, c0:c1], preferred_element_type=F32)
        u = jnp.dot(n, win_ref[:, D_FF + c0:D_FF + c1], preferred_element_type=F32)
        prev1 = tail_ref[7:8, c0:c1]
        prev2 = tail_ref[6:7, c0:c1]
        a1 = jnp.where(rowi == 0, prev1, pltpu.roll(a, 1, 0))
        a2 = jnp.where(rowi == 0, prev2, jnp.where(rowi == 1, prev1, pltpu.roll(a, 2, 0)))
        tail_ref[:, c0:c1] = a[tm - 8:tm, :]
        conv = cb_ref[:, c0:c1] + cw_ref[0:1, c0:c1] * a2 + cw_ref[1:2, c0:c1] * a1 + cw_ref[2:3, c0:c1] * a
        acc = acc + _dot(conv * jax.nn.sigmoid(conv) * u, wout_ref[c0:c1, :])
    o_ref[...] = x + acc


def _ffn(h, g, w_in, conv_w, conv_b, w_out, lp):
    t = h.shape[0]
    tm = ROW_TILE
    nb = lp // tm
    row = lambda b, i: (b * nb + i, 0)
    return pl.pallas_call(
        _ffn_kernel,
        grid=(t // lp, nb),
        in_specs=[pl.BlockSpec((tm, D_MODEL), row), _const_spec((1, D_MODEL)), _const_spec(w_in.shape),
                  _const_spec(conv_w.shape), _const_spec(conv_b.shape), _const_spec(w_out.shape)],
        out_specs=pl.BlockSpec((tm, D_MODEL), row),
        out_shape=jax.ShapeDtypeStruct((t, D_MODEL), F32),
        scratch_shapes=[pltpu.VMEM((8, D_FF), F32)],
        compiler_params=_params(("arbitrary", "arbitrary")),
        name="conv_ffn",
    )(h, g, w_in, conv_w, conv_b, w_out)


def _final_norm_kernel(h_ref, g_ref, o_ref):
    o_ref[0] = _rms(h_ref[0], EPS) * g_ref[...]


def _final_norm(h3, g, seq):
    b = h3.shape[0]
    first = (PAD + N_META) // BLOCK
    return pl.pallas_call(
        _final_norm_kernel,
        grid=(b, seq // BLOCK),
        in_specs=[pl.BlockSpec((1, BLOCK, D_MODEL), lambda bi, i: (bi, i + first, 0)), _const_spec((1, D_MODEL))],
        out_specs=pl.BlockSpec((1, BLOCK, D_MODEL), lambda bi, i: (bi, i, 0)),
        out_shape=jax.ShapeDtypeStruct((b, seq, D_MODEL), F32),
        compiler_params=_params(("parallel", "parallel")),
        name="final_norm",
    )(h3, g)


def _rope_swap(w):
    half = w.shape[-1] // 2
    return jnp.concatenate([-w[..., half:], w[..., :half]], axis=-1)


def _layout_params(w_in, mla_w_uq, mla_w_ukv, rw_w2, rw_a2, gla_a2):
    kr = w_in[..., 384:416]
    w_mla = jnp.concatenate([w_in[..., 0:384], jnp.tile(kr, (1, 1, 4)), jnp.tile(_rope_swap(kr), (1, 1, 4))], axis=-1)
    zg = w_in[..., 2208:2992]
    w_gla = jnp.concatenate([zg[..., 0:512], zg[..., 528:784], zg[..., 512:528],
                             jnp.zeros(zg.shape[:-1] + (W_GLA - 784,), zg.dtype)], axis=-1)
    sb_q = w_in[..., 1440:1440 + SB_W] * (SB_HEAD ** -0.5 * LOG2_E)
    w_all = _bf(jnp.concatenate([w_mla, w_in[..., 416:1440], sb_q, w_in[..., 1440 + SB_W:2208], w_gla,
                                 w_in[..., 2992:]], axis=-1))

    depth = w_in.shape[0]
    wuq = mla_w_uq.reshape(depth, MLA_Q_RANK, N_HEADS, MLA_NOPE + MLA_ROPE)
    nope, rope = wuq[..., :MLA_NOPE], wuq[..., MLA_NOPE:]
    rope_sw = _rope_swap(rope)
    zeros64 = jnp.zeros((depth, MLA_Q_RANK, 64), w_in.dtype)
    pair = lambda x, a, b: jnp.concatenate([x[:, :, a], x[:, :, b]], axis=-1)
    rope_pair = lambda x, a, b: jnp.concatenate([x[:, :, a], x[:, :, b], zeros64], axis=-1)
    wq = _bf(jnp.concatenate([pair(nope, 0, 1), rope_pair(rope, 0, 1), pair(nope, 2, 3), rope_pair(rope, 2, 3),
                              rope_pair(rope_sw, 0, 1), rope_pair(rope_sw, 2, 3)], axis=-1))
    wukv = mla_w_ukv.reshape(depth, MLA_KV_RANK, N_HEADS, 128)
    wkv = _bf(jnp.concatenate([wukv[..., :64].reshape(depth, MLA_KV_RANK, 256),
                               wukv[..., 64:].reshape(depth, MLA_KV_RANK, 256)], axis=-1))

    z64 = jnp.zeros_like(rw_w2)
    w2 = _bf(jnp.concatenate([rw_w2, z64], axis=1))
    a2 = _bf(jnp.concatenate([z64, rw_a2], axis=1))
    gla_a2p = _bf(jnp.concatenate([gla_a2, jnp.zeros((depth, 128 - gla_a2.shape[1], GLA_QK), gla_a2.dtype)], axis=1))
    return w_all, wq, wkv, w2, a2, gla_a2p


def _rope_tables(lp):
    half = MLA_ROPE // 2
    freqs = ROPE_THETA ** (-jnp.arange(half, dtype=F32) / half)
    pos = (jnp.arange(lp) - PAD).astype(F32)
    ang = pos[:, None] * freqs[None, :]
    return jnp.tile(jnp.cos(ang), (1, 128 // half)), jnp.tile(jnp.sin(ang), (1, 128 // half))


def kernel(x, meta_tokens, norm_mix, w_in, mla_q_norm, mla_w_uq, mla_kv_norm, mla_w_ukv, rw_mu, rw_w0, rw_w2, rw_a0, rw_a2, rw_g2, rw_k_k, rw_k_a, rw_r_k, rw_ln_w, rw_ln_b, gla_a2, gla_a_b, gla_norm, gate_b, w_branch, w_out, norm_ffn, w_ffn_in, ffn_conv_w, ffn_conv_b, w_ffn_out, norm_final):
    batch, seq, _ = x.shape
    depth = w_in.shape[0]
    lp = PAD + N_META + seq
    t = batch * lp
    assert lp % ROW_TILE == 0 and lp % ATTN_TILE == 0 and lp % GLA_CHUNK == 0 and lp % RW_CHUNK == 0

    w_all, wq, wkv, rw_w2p, rw_a2p, gla_a2p = _layout_params(w_in, mla_w_uq, mla_w_ukv, rw_w2, rw_a2, gla_a2)
    w_branch_b, w_out_b, w_ffn_in_b, w_ffn_out_b, rw_g2_b = map(_bf, (w_branch, w_out, w_ffn_in, w_ffn_out, rw_g2))
    vec = lambda a: a.reshape(depth, 1, -1)
    cos, sin = _rope_tables(lp)
    idx = jnp.arange(SB_SUFFIX_SPLIT)
    later_mat = jnp.where(idx[:, None] > idx[None, :], 1.0, 0.0).astype(BF16)

    meta = jnp.broadcast_to(meta_tokens[None].astype(x.dtype), (batch, N_META, D_MODEL))
    h = jnp.concatenate([jnp.zeros((batch, PAD, D_MODEL), x.dtype), meta, x], axis=1).reshape(t, D_MODEL)

    for i in range(depth):
        z_mla, z_rw, z_sb, z_gla, z_gate = _in_proj(h, vec(norm_mix)[i], w_all[i])
        q, k, v = _mla_prep(z_mla, vec(mla_q_norm)[i], vec(mla_kv_norm)[i], wq[i], wkv[i], cos, sin, lp)
        y_mla = _mla_attn(q, k, v, lp)
        rw = {"rw_mu": vec(rw_mu)[i], "rw_w0": vec(rw_w0)[i], "rw_w2": rw_w2p[i], "rw_a0": vec(rw_a0)[i],
              "rw_a2": rw_a2p[i], "rw_g2": rw_g2_b[i], "rw_k_k": vec(rw_k_k)[i], "rw_k_a": vec(rw_k_a)[i],
              "rw_r_k": vec(rw_r_k)[i]}
        *chunk_terms, g, bonus = _rw_prep(z_rw, rw, lp)
        y_rw = _rw_scan(*chunk_terms, batch)
        y_sb = _sb_attn(z_sb, later_mat, lp)
        o_gla = _gla(z_gla, gla_a2p[i], vec(gla_a_b)[i], batch)
        mp = {"rw_ln_w": vec(rw_ln_w)[i], "rw_ln_b": vec(rw_ln_b)[i], "gla_norm": vec(gla_norm)[i],
              "gate_b": gate_b[i], "w_branch": w_branch_b[i], "w_out": w_out_b[i]}
        h = _merge(h, y_mla, y_rw, bonus, g, y_sb, o_gla, z_gla, z_gate, mp, lp)
        h = _ffn(h, vec(norm_ffn)[i], w_ffn_in_b[i], ffn_conv_w[i], vec(ffn_conv_b)[i], w_ffn_out_b[i], lp)
    return _final_norm(h.reshape(batch, lp, D_MODEL), norm_final.reshape(1, D_MODEL), seq)
```

```python
import functools

import jax
import jax.numpy as jnp
from jax import lax
from jax.experimental import pallas as pl
from jax.experimental.pallas import tpu as pltpu

F32 = jnp.float32
BF16 = jnp.bfloat16

D_MODEL = 1024
DEPTH = 4
N_META = 16
BLOCK = 128
PAD = (-N_META) % BLOCK
EPS = 1e-6
NEG_INF = -1e30
LOG2_E = 1.4426950408889634

N_HEADS = 4
MLA_NOPE = 64
MLA_ROPE = 32
MLA_Q_RANK = 256
MLA_KV_RANK = 128
ROPE_THETA = 10000.0

RW_W = 256
RW_GN_EPS = 64e-5
RW_CHUNK = 64

SB_W = 256
SB_HEAD = 64
SB_SUFFIX_SPLIT = 256

GLA_DK = 32
GLA_QK = 128
GLA_W = 256
GLA_TAU = 16.0
GLA_CHUNK = 128

D_FF = 2816
FFN_COL_CHUNK = 1408

W_MLA, W_RW, W_SB, W_GLA, W_GATE = 640, 1024, 768, 896, 4096
IN_GROUP_WIDTHS = (W_MLA, W_RW, W_SB, W_GLA, W_GATE)
IN_GROUP_DTYPES = (BF16, F32, BF16, F32, BF16)

ROW_TILE = 384
ATTN_TILE = 384
VMEM_LIMIT = 56 * 1024 * 1024


def _bf(x):
    return x.astype(BF16)


def _dot(a, b):
    return jnp.dot(_bf(a), _bf(b), preferred_element_type=F32)


def _dot_nt(a, b):
    return lax.dot_general(_bf(a), _bf(b), (((1,), (1,)), ((), ())), preferred_element_type=F32)


def _dot_tn(a, b):
    return lax.dot_general(_bf(a), _bf(b), (((0,), (0,)), ((), ())), preferred_element_type=F32)


def _split_hi_lo(x):
    hi = _bf(x)
    lo = _bf(x - hi.astype(F32))
    return hi, lo


def _dot_exact_rhs(x, m):
    hi, lo = _split_hi_lo(x)
    return jnp.dot(hi, m, preferred_element_type=F32) + jnp.dot(lo, m, preferred_element_type=F32)


def _dot_exact_lhs(m, x):
    hi, lo = _split_hi_lo(x)
    return jnp.dot(m, hi, preferred_element_type=F32) + jnp.dot(m, lo, preferred_element_type=F32)


def _iota(shape, dim):
    return lax.broadcasted_iota(jnp.int32, shape, dim)


def _div_pow2(x, d):
    assert d & (d - 1) == 0
    return lax.shift_right_logical(x, d.bit_length() - 1)


def _same_segment(shape, row_seg, col_seg):
    return _div_pow2(_iota(shape, 0), row_seg) == _div_pow2(_iota(shape, 1), col_seg)


def _segment_matrix(n, seg, value):
    return jnp.where(_same_segment((n, n), seg, seg), value, 0.0).astype(BF16)


def _head_lane_mask(width, head_width, h):
    lane = _iota((1, width), 1)
    return (lane >= h * head_width) & (lane < (h + 1) * head_width)


def _stack_heads(x, head_width):
    w = x.shape[1]
    return jnp.concatenate(
        [jnp.where(_head_lane_mask(w, head_width, h), x, 0.0) for h in range(N_HEADS)], axis=0)


def _unstack_heads(xs, c):
    return xs[0:c] + xs[c:2 * c] + xs[2 * c:3 * c] + xs[3 * c:4 * c]


def _rms(x, eps):
    return x * lax.rsqrt(jnp.mean(x * x, axis=-1, keepdims=True) + eps)


def _const_spec(shape):
    nd = len(shape)
    return pl.BlockSpec(shape, lambda *_: (0,) * nd)


def _params(sem, vmem=VMEM_LIMIT):
    return pltpu.CompilerParams(dimension_semantics=sem, vmem_limit_bytes=vmem)


def _in_proj_kernel(h_ref, g_ref, w_ref, *out_refs):
    *group_refs, sb_values_ref = out_refs
    n = _bf(_rms(h_ref[...], EPS) * g_ref[...])
    off = 0
    for o_ref, width in zip(group_refs, IN_GROUP_WIDTHS):
        for c0 in range(0, width, 1024):
            c1 = min(c0 + 1024, width)
            o_ref[:, c0:c1] = jnp.dot(
                n, w_ref[:, off + c0:off + c1], preferred_element_type=F32).astype(o_ref.dtype)
        off += width
    sb_values_ref[0] = _head_stacked_values(group_refs[2][:, 2 * SB_W:3 * SB_W])


def _in_proj(h, g, w_all):
    t = h.shape[0]
    tm = ROW_TILE
    return pl.pallas_call(
        _in_proj_kernel,
        grid=(t // tm,),
        in_specs=[pl.BlockSpec((tm, D_MODEL), lambda i: (i, 0)),
                  _const_spec((1, D_MODEL)),
                  _const_spec(w_all.shape)],
        out_specs=[pl.BlockSpec((tm, w), lambda i: (i, 0)) for w in IN_GROUP_WIDTHS]
        + [pl.BlockSpec((1, N_HEADS * tm, SB_W), lambda i: (i, 0, 0))],
        out_shape=[jax.ShapeDtypeStruct((t, w), dt) for w, dt in zip(IN_GROUP_WIDTHS, IN_GROUP_DTYPES)]
        + [jax.ShapeDtypeStruct((t // tm, N_HEADS * tm, SB_W), BF16)],
        compiler_params=_params(("parallel",)),
        name="in_proj",
    )(h, g, w_all)


def _mla_prep_kernel(z_ref, qg_ref, kvg_ref, wq_ref, wkv_ref, cos_ref, sin_ref, q_ref, k_ref, vt_ref):
    z = z_ref[...].astype(F32)
    nq = _rms(z[:, 0:256], EPS) * qg_ref[...]
    nkv = _rms(z[:, 256:384], EPS) * kvg_ref[...]
    cos = cos_ref[...]
    sin = sin_ref[...]
    ql = _dot(nq, wq_ref[...])
    scale = (MLA_NOPE + MLA_ROPE) ** -0.5
    q = jnp.concatenate([ql[:, 0:128], ql[:, 128:256] * cos + ql[:, 512:640] * sin,
                         ql[:, 256:384], ql[:, 384:512] * cos + ql[:, 640:768] * sin], axis=1)
    q_ref[...] = _bf(q * scale)
    kvl = _dot(nkv, wkv_ref[...])
    k_rope = z[:, 384:512] * cos + z[:, 512:640] * sin
    k_rope = jnp.where(_iota((1, 128), 1) < 2 * MLA_ROPE, k_rope, 0.0)
    k_ref[...] = _bf(jnp.concatenate([kvl[:, 0:128], k_rope, kvl[:, 128:256], k_rope], axis=1))
    vt = _bf(kvl[:, 256:512].T)
    row_head = _div_pow2(_iota((256, 1), 0), 64)
    vt_ref[0] = jnp.concatenate([jnp.where(row_head == h, vt, jnp.zeros((), BF16)) for h in range(N_HEADS)], axis=1)


def _mla_prep(z_mla, qg, kvg, wq, wkv, cos, sin, lp):
    t = z_mla.shape[0]
    tm = ROW_TILE
    nb = lp // tm
    row = lambda b, i: (b * nb + i, 0)
    return pl.pallas_call(
        _mla_prep_kernel,
        grid=(t // lp, nb),
        in_specs=[pl.BlockSpec((tm, W_MLA), row),
                  _const_spec((1, MLA_Q_RANK)), _const_spec((1, MLA_KV_RANK)),
                  _const_spec(wq.shape), _const_spec(wkv.shape),
                  pl.BlockSpec((tm, 128), lambda b, i: (i, 0)),
                  pl.BlockSpec((tm, 128), lambda b, i: (i, 0))],
        out_specs=[pl.BlockSpec((tm, 512), row), pl.BlockSpec((tm, 512), row),
                   pl.BlockSpec((1, 256, N_HEADS * tm), lambda b, i: (b * nb + i, 0, 0))],
        out_shape=[jax.ShapeDtypeStruct((t, 512), BF16), jax.ShapeDtypeStruct((t, 512), BF16),
                   jax.ShapeDtypeStruct((t // tm, 256, N_HEADS * tm), BF16)],
        compiler_params=_params(("parallel", "parallel")),
        name="mla_prep",
    )(z_mla, qg, kvg, wq, wkv, cos, sin)


def _per_head_lanes(cols):
    lane = _iota((1, 256), 1)
    return jnp.where(lane < 64, cols[0], jnp.where(lane < 128, cols[1], jnp.where(lane < 192, cols[2], cols[3])))


def _head_stacked_values(vb):
    zero = jnp.zeros((), vb.dtype)
    return jnp.concatenate([jnp.where(_head_lane_mask(256, 64, h), vb, zero) for h in range(N_HEADS)], axis=0)


def _per_head_rows(rows, n):
    return jnp.concatenate([jnp.broadcast_to(r, (64, n)) for r in rows], axis=0)


def _mla_attn_kernel(q_ref, k_ref, vt_ref, o_ref, qh_ref, m_ref, l_ref, acc_ref, p_ref, alpha_ref, *, tile):
    i = pl.program_id(1)
    q_pos = i * tile + _iota((1, tile), 1)
    lane = _iota((1, 256), 1)
    for h in range(N_HEADS):
        half, slot = h // 2, h % 2
        head_lanes = (((lane >= slot * MLA_NOPE) & (lane < (slot + 1) * MLA_NOPE))
                      | ((lane >= 128 + slot * MLA_ROPE) & (lane < 128 + (slot + 1) * MLA_ROPE)))
        qh_ref[half, slot * tile:(slot + 1) * tile, :] = jnp.where(
            head_lanes, q_ref[:, half * 256:(half + 1) * 256], jnp.zeros((), BF16))
    m_ref[...] = jnp.full(m_ref.shape, NEG_INF, F32)
    l_ref[...] = jnp.zeros(l_ref.shape, F32)
    acc_ref[...] = jnp.zeros(acc_ref.shape, F32)

    def add_values(j, slot):
        pv = jnp.dot(vt_ref[j], p_ref[slot], preferred_element_type=F32)
        acc_ref[...] = _per_head_rows([alpha_ref[slot, h] for h in range(N_HEADS)], tile) * acc_ref[...] + pv

    def block(j, masked, prev, slot):
        start = pl.multiple_of(j * tile, tile)
        heads = range(N_HEADS)
        s_pair = [_dot_nt(k_ref[pl.ds(start, tile), half * 256:(half + 1) * 256], qh_ref[half]) for half in (0, 1)]
        s = [s_pair[h // 2][:, (h % 2) * tile:(h % 2 + 1) * tile] for h in heads]
        if prev is not None:
            add_values(prev, 1 - slot)
        for h in heads:
            s_h = s[h]
            if masked:
                k_pos = start + _iota((tile, 1), 0)
                s_h = jnp.where((k_pos <= q_pos) & (k_pos >= PAD), s_h, NEG_INF)
            m_old = m_ref[h]
            m_new = jnp.maximum(m_old, jnp.max(s_h, axis=0, keepdims=True))
            p = jnp.exp(s_h - m_new)
            alpha = jnp.exp(m_old - m_new)
            l_ref[h] = alpha * l_ref[h] + jnp.sum(p, axis=0, keepdims=True)
            m_ref[h] = m_new
            p_ref[slot, h * tile:(h + 1) * tile, :] = _bf(p)
            alpha_ref[slot, h] = alpha

    block(i, True, None, 0)

    @pl.when(i > 0)
    def _():
        block(0, True, i, 1)

    interior_blocks = jnp.maximum(i - 1, 0)

    def interior_pair(t, carry):
        j = 1 + 2 * t
        block(j, False, j - 1, 0)
        block(j + 1, False, j, 1)
        return carry

    lax.fori_loop(0, lax.shift_right_logical(interior_blocks, 1), interior_pair, 0)
    odd = (interior_blocks & 1) == 1

    @pl.when(odd)
    def _():
        block(i - 1, False, i - 2, 0)

    last = jnp.maximum(i - 1, 0)
    last_in_slot0 = (i == 0) | odd

    @pl.when(last_in_slot0)
    def _():
        add_values(last, 0)

    @pl.when(jnp.logical_not(last_in_slot0))
    def _():
        add_values(last, 1)
    o_ref[...] = (acc_ref[...] / _per_head_rows([l_ref[h] for h in range(N_HEADS)], tile)).T


def _mla_attn(q, k, vt, lp):
    t = q.shape[0]
    tile = ATTN_TILE
    nb = lp // tile
    return pl.pallas_call(
        functools.partial(_mla_attn_kernel, tile=tile),
        grid=(t // lp, nb),
        in_specs=[pl.BlockSpec((tile, 512), lambda b, i: (b * nb + i, 0)),
                  pl.BlockSpec((lp, 512), lambda b, i: (b, 0)),
                  pl.BlockSpec((nb, 256, N_HEADS * tile), lambda b, i: (b, 0, 0))],
        out_specs=pl.BlockSpec((tile, 256), lambda b, i: (b * nb + i, 0)),
        out_shape=jax.ShapeDtypeStruct((t, 256), F32),
        scratch_shapes=[pltpu.VMEM((2, 2 * tile, 256), BF16), pltpu.VMEM((N_HEADS, 1, tile), F32),
                        pltpu.VMEM((N_HEADS, 1, tile), F32), pltpu.VMEM((256, tile), F32),
                        pltpu.VMEM((2, N_HEADS * tile, tile), BF16), pltpu.VMEM((2, N_HEADS, 1, tile), F32)],
        compiler_params=_params(("parallel", "arbitrary")),
        name="mla_attn",
    )(q, k, vt)


def _sb_attn_kernel(q_ref, k_ref, v_ref, later_ref, o_ref, qh_ref, c_ref, acc_ref, wts_ref, *, tile):
    i = pl.program_id(1)
    row = i * tile + _iota((tile, 1), 0)
    for h in range(N_HEADS):
        qh_ref[h] = jnp.where(_head_lane_mask(SB_W, SB_HEAD, h), q_ref[...], jnp.zeros((), BF16))
    c_ref[...] = jnp.zeros(c_ref.shape, F32)
    acc_ref[...] = jnp.zeros(acc_ref.shape, F32)

    def add_values(j, slot):
        acc_ref[...] += jnp.dot(wts_ref[slot], v_ref[j], preferred_element_type=F32)

    def block(j, masked, prev, slot):
        start = pl.multiple_of(j * tile, tile)
        heads = range(N_HEADS)
        cut = SB_SUFFIX_SPLIT
        z = [_dot_nt(qh_ref[h], k_ref[pl.ds(start, tile), :]) for h in heads]
        if prev is not None:
            add_values(prev, 1 - slot)
        log_take = [jnp.minimum(z[h], 0.0) - jnp.log2(1.0 + jnp.exp2(-jnp.abs(z[h]))) for h in heads]
        log_keep = [log_take[h] - z[h] for h in heads]
        if masked:
            mask = (start + _iota((1, tile), 1)) < row
            log_keep = [jnp.where(mask, log_keep[h], 0.0) for h in heads]
        keep16 = [_bf(log_keep[h]) for h in heads]
        later_head = [jnp.dot(keep16[h][:, :cut], later_ref[...], preferred_element_type=F32) for h in heads]
        later_tail = [jnp.dot(keep16[h][:, cut:], later_ref[:tile - cut, :tile - cut], preferred_element_type=F32)
                      for h in heads]
        for h in heads:
            head_sum = jnp.sum(log_keep[h][:, :cut], axis=-1, keepdims=True)
            tail_sum = jnp.sum(log_keep[h][:, cut:], axis=-1, keepdims=True)
            c = c_ref[h]
            later = jnp.concatenate([later_head[h] + tail_sum, later_tail[h]], axis=1)
            w = jnp.exp2(log_take[h] + later + c)
            if masked:
                w = jnp.where(mask, w, 0.0)
            c_ref[h] = c + (head_sum + tail_sum)
            wts_ref[slot, :, h * tile:(h + 1) * tile] = _bf(w)

    block(i, True, None, 0)

    def below_pair(t, carry):
        j = i - 1 - 2 * t
        block(j, False, j + 1, 1)
        block(j - 1, False, j, 0)
        return carry

    lax.fori_loop(0, lax.shift_right_logical(i, 1), below_pair, 0)
    odd = (i & 1) == 1

    @pl.when(odd)
    def _():
        block(0, False, 1, 1)
        add_values(0, 1)

    @pl.when(jnp.logical_not(odd))
    def _():
        add_values(0, 0)
    o_ref[...] = acc_ref[...]


def _sb_attn(z_sb, sb_values, later_mat, lp):
    t = z_sb.shape[0]
    tile = ATTN_TILE
    nb = lp // tile
    return pl.pallas_call(
        functools.partial(_sb_attn_kernel, tile=tile),
        grid=(t // lp, nb),
        in_specs=[pl.BlockSpec((tile, SB_W), lambda b, i: (b * nb + i, 0)),
                  pl.BlockSpec((lp, SB_W), lambda b, i: (b, 1)),
                  pl.BlockSpec((nb, N_HEADS * tile, SB_W), lambda b, i: (b, 0, 0)),
                  _const_spec((SB_SUFFIX_SPLIT, SB_SUFFIX_SPLIT))],
        out_specs=pl.BlockSpec((tile, SB_W), lambda b, i: (b * nb + i, 0)),
        out_shape=jax.ShapeDtypeStruct((t, SB_W), F32),
        scratch_shapes=[pltpu.VMEM((N_HEADS, tile, SB_W), BF16), pltpu.VMEM((N_HEADS, tile, 1), F32),
                        pltpu.VMEM((tile, SB_W), F32), pltpu.VMEM((2, tile, N_HEADS * tile), BF16)],
        compiler_params=_params(("parallel", "arbitrary")),
        name="sb_attn",
    )(z_sb, z_sb, sb_values, later_mat)


def _compact_eye(c):
    return jnp.where(_iota((c, N_HEADS * c), 0) == (_iota((c, N_HEADS * c), 1) & (c - 1)), 1.0, 0.0)


def _rw_prep_kernel(z_ref, zprev_ref, mu_ref, w0_ref, w2_ref, a0_ref, a2_ref, g2_ref, kk_ref, ka_ref, rk_ref,
                    w_o, rt_o, arb_o, kbar_o, bbar_o, v_o, uv_o, yv_o, dec_o, g_o, bonus_o):
    i = pl.program_id(1)
    z = z_ref[...]
    tm = z.shape[0]
    prev = jnp.where(i == 0, 0.0, zprev_ref[7:8, :])
    shifted = jnp.where(_iota((tm, 1), 0) == 0, prev, pltpu.roll(z, 1, 0))
    zz = z + (shifted - z) * mu_ref[...]
    r, k, v = zz[:, 0:256], zz[:, 256:512], zz[:, 512:768]
    lora_in = zz[:, 768:896]
    u = w0_ref[...] + _dot(jnp.tanh(lora_in), w2_ref[...])
    w = jnp.minimum(u, 0.0) - jnp.log1p(jnp.exp(-jnp.abs(u))) - 0.5
    a = jax.nn.sigmoid(a0_ref[...] + _dot(lora_in, a2_ref[...]))
    seg = _segment_matrix(RW_W, 64, 1.0)
    kx = k * kk_ref[...]
    kap = kx / jnp.maximum(jnp.sqrt(_dot_exact_rhs(kx * kx, seg)), 1e-12)
    kmod = k * (1.0 + (a - 1.0) * ka_ref[...])
    beta = kap * a
    lw = -jnp.exp(w)
    g_o[...] = _dot(jax.nn.sigmoid(zz[:, 896:1024]), g2_ref[...])
    bonus_o[...] = _dot_exact_rhs(r * kmod * rk_ref[...], seg) * v

    c = RW_CHUNK
    same_chunk = _same_segment((tm, tm), c, c)
    lw_hi, lw_lo = _split_hi_lo(lw)
    cum = jnp.where(same_chunk & (_iota((tm, tm), 0) >= _iota((tm, tm), 1)), 1.0, 0.0).astype(BF16)
    tot = jnp.where(same_chunk, 1.0, 0.0).astype(BF16)
    b = jnp.dot(cum, lw_hi, preferred_element_type=F32) + jnp.dot(cum, lw_lo, preferred_element_type=F32)
    b_end = jnp.dot(tot, lw_hi, preferred_element_type=F32) + jnp.dot(tot, lw_lo, preferred_element_type=F32)
    grow = jnp.exp(-b)
    to_end = jnp.exp(b_end - b)
    kap_t = kap * jnp.exp(b - lw)
    r_t = r * jnp.exp(b)
    beta_g = beta * grow
    k_g = kmod * grow
    rt_o[...] = _bf(r_t)
    kbar_o[...] = _bf(kmod * to_end)
    bbar_o[...] = _bf(beta * to_end)
    v_o[...] = _bf(v)
    decay_end = jnp.exp(b_end)

    lane_pos = _iota((c, N_HEADS * c), 1) & (c - 1)
    strictly_earlier = _iota((c, N_HEADS * c), 0) > lane_pos
    not_later = _iota((c, N_HEADS * c), 0) >= lane_pos
    chunks = [slice(n * c, (n + 1) * c) for n in range(tm // c)]
    pair = [_dot_nt(jnp.concatenate([kap_t[rows], r_t[rows]], axis=0),
                    jnp.concatenate([_stack_heads(beta_g[rows], 64), _stack_heads(k_g[rows], 64)], axis=0))
            for rows in chunks]
    a_kb = [jnp.where(strictly_earlier, p[0:c, 0:4 * c], 0.0) for p in pair]
    a_kk = [jnp.where(strictly_earlier, p[0:c, 4 * c:8 * c], 0.0) for p in pair]
    a_rk = [jnp.where(not_later, p[c:2 * c, 4 * c:8 * c], 0.0) for p in pair]
    for rows, p in zip(chunks, pair):
        arb_o[rows, :] = _bf(jnp.where(not_later, p[c:2 * c, 0:4 * c], 0.0))
    inv = [_compact_eye(c) - a for a in a_kb]
    power = [_dot(a, _stack_heads(a, c)) for a in a_kb]
    span = 2
    while span < c:
        inv = [t + _dot(t, _stack_heads(p, c)) for t, p in zip(inv, power)]
        span *= 2
        if span < c:
            power = [_dot(p, _stack_heads(p, c)) for p in power]
    v_st = [_stack_heads(v[rows], 64) for rows in chunks]
    kk_v = [_dot(a, vs) for a, vs in zip(a_kk, v_st)]
    for n, rows in enumerate(chunks):
        w_o[rows, :] = _bf(_dot(inv[n], _stack_heads(kap_t[rows], 64)))
        uv_o[rows, :] = _dot(inv[n], _stack_heads(kk_v[n], 64))
        yv_o[rows, :] = _dot(a_rk[n], v_st[n])
        dec_o[n * 8:(n + 1) * 8, :] = decay_end[n * c:n * c + 8]


def _rw_prep(z_rw, p, lp):
    t = z_rw.shape[0]
    tm = ROW_TILE
    nb = lp // tm
    row = lambda b, i: (b * nb + i, 0)
    prev = lambda b, i: (jnp.maximum(b * (lp // 8) + i * (tm // 8) - 1, 0), 0)
    consts = [p["rw_mu"], p["rw_w0"], p["rw_w2"], p["rw_a0"], p["rw_a2"], p["rw_g2"],
              p["rw_k_k"], p["rw_k_a"], p["rw_r_k"]]
    return pl.pallas_call(
        _rw_prep_kernel,
        grid=(t // lp, nb),
        in_specs=[pl.BlockSpec((tm, W_RW), row), pl.BlockSpec((8, W_RW), prev)]
        + [_const_spec(c.shape) for c in consts],
        out_specs=[pl.BlockSpec((tm, RW_W), row)] * 8
        + [pl.BlockSpec((tm // RW_CHUNK * 8, RW_W), row)] + [pl.BlockSpec((tm, RW_W), row)] * 2,
        out_shape=[jax.ShapeDtypeStruct((t, RW_W), BF16)] * 6 + [jax.ShapeDtypeStruct((t, RW_W), F32)] * 2
        + [jax.ShapeDtypeStruct((t // RW_CHUNK * 8, RW_W), F32)] + [jax.ShapeDtypeStruct((t, RW_W), F32)] * 2,
        compiler_params=_params(("parallel", "parallel")),
        name="rw_prep",
    )(z_rw, z_rw, *consts)


def _rw_scan_kernel(w_ref, rt_ref, arb_ref, kbar_ref, bbar_ref, v_ref, uv_ref, yv_ref, dec_ref, y_ref, ht_ref):
    c = RW_CHUNK

    @pl.when(pl.program_id(0) == 0)
    def _():
        ht_ref[...] = jnp.zeros_like(ht_ref)

    same_head = _same_segment((RW_W, RW_W), 64, 64)
    seqs = range(ht_ref.shape[0])
    ht = [ht_ref[s] for s in seqs]
    from_state = [_dot_nt(jnp.concatenate([w_ref[s], rt_ref[s]], axis=0), ht[s]) for s in seqs]
    u = [from_state[s][0:c] + uv_ref[s] for s in seqs]
    from_u = [_dot(arb_ref[s], _stack_heads(u[s], 64)) for s in seqs]
    upd = [_dot_tn(jnp.concatenate([v_ref[s], _bf(u[s])], axis=0),
                   jnp.concatenate([kbar_ref[s], -bbar_ref[s]], axis=0)) for s in seqs]
    for s in seqs:
        y_ref[s] = from_state[s][c:2 * c] + yv_ref[s] - from_u[s]
        ht_ref[s] = ht[s] * dec_ref[s, 0:1, :] + jnp.where(same_head, upd[s], 0.0)


def _rw_scan(w, rt, arb, kbar, bbar, v, uv, yv, dec, batch):
    t = w.shape[0]
    lp = t // batch
    c = RW_CHUNK
    as_seq = lambda a: a.reshape(batch, a.shape[0] // batch, RW_W)
    spec = pl.BlockSpec((batch, c, RW_W), lambda i: (0, i, 0))
    y = pl.pallas_call(
        _rw_scan_kernel,
        grid=(lp // c,),
        in_specs=[spec] * 8 + [pl.BlockSpec((batch, 8, RW_W), lambda i: (0, i, 0))],
        out_specs=spec,
        out_shape=jax.ShapeDtypeStruct((batch, lp, RW_W), F32),
        scratch_shapes=[pltpu.VMEM((batch, RW_W, RW_W), F32)],
        compiler_params=_params(("arbitrary",)),
        name="rw_scan",
    )(*map(as_seq, (w, rt, arb, kbar, bbar, v, uv, yv, dec)))
    return y.reshape(t, RW_W)


def _gla_kernel(q_ref, k_ref, v_ref, al_ref, a2_ref, ab_ref, o_ref, st_ref):
    c = GLA_CHUNK
    n = N_HEADS * c
    seqs = range(st_ref.shape[0])

    @pl.when(pl.program_id(0) == 0)
    def _():
        st_ref[...] = jnp.zeros_like(st_ref)

    tri = jnp.where(_iota((c, c), 0) >= _iota((c, c), 1), 1.0, 0.0).astype(BF16)
    causal = (_iota((n, c), 0) & (c - 1)) >= _iota((n, c), 1)
    same_head = _same_segment((GLA_W, GLA_QK), 64, GLA_DK)
    x = [_dot(al_ref[s], a2_ref[...]) + ab_ref[...] for s in seqs]
    log_a = [(jnp.minimum(x[s], 0.0) - jnp.log1p(jnp.exp(-jnp.abs(x[s])))) * (1.0 / GLA_TAU) for s in seqs]
    b = [_dot_exact_lhs(tri, log_a[s]) for s in seqs]
    q = [q_ref[s] * (GLA_DK ** -0.5) for s in seqs]
    st = [st_ref[s] for s in seqs]
    inter = [_dot_nt(q[s] * jnp.exp(b[s]), st[s]) for s in seqs]
    scores = []
    for s in seqs:
        b_mid = b[s][c // 2 - 1:c // 2, :]
        sc = _dot_nt(_stack_heads(q[s] * jnp.exp(b[s] - b_mid), GLA_DK), k_ref[s] * jnp.exp(b_mid - b[s]))
        scores.append(jnp.where(causal, sc, 0.0))
    per_head = [_dot(scores[s], v_ref[s]) for s in seqs]
    upd = [_dot_tn(v_ref[s], k_ref[s] * jnp.exp(b[s][c - 1:c, :] - b[s])) for s in seqs]
    for s in seqs:
        intra = jnp.zeros((c, GLA_W), F32)
        for h in range(N_HEADS):
            intra = intra + jnp.where(_head_lane_mask(GLA_W, 64, h), per_head[s][h * c:(h + 1) * c], 0.0)
        o_ref[s] = inter[s] + intra
        st_ref[s] = st[s] * jnp.exp(b[s][c - 1:c, :]) + jnp.where(same_head, upd[s], 0.0)


def _gla(z_gla, a2, ab, batch):
    t = z_gla.shape[0]
    lp = t // batch
    c = GLA_CHUNK
    z3 = z_gla.reshape(batch, lp, W_GLA)
    o = pl.pallas_call(
        _gla_kernel,
        grid=(lp // c,),
        in_specs=[pl.BlockSpec((batch, c, 128), lambda i: (0, i, 0)),
                  pl.BlockSpec((batch, c, 128), lambda i: (0, i, 1)),
                  pl.BlockSpec((batch, c, 256), lambda i: (0, i, 1)),
                  pl.BlockSpec((batch, c, 128), lambda i: (0, i, 6)),
                  _const_spec(a2.shape), _const_spec(ab.shape)],
        out_specs=pl.BlockSpec((batch, c, GLA_W), lambda i: (0, i, 0)),
        out_shape=jax.ShapeDtypeStruct((batch, lp, GLA_W), F32),
        scratch_shapes=[pltpu.VMEM((batch, GLA_W, GLA_QK), F32)],
        compiler_params=_params(("arbitrary",)),
        name="gla",
    )(z3, z3, z3, z3, a2, ab)
    return o.reshape(t, GLA_W)


def _merge_kernel(h_ref, ymla_ref, yrw_ref, bonus_ref, g_ref, ysb_ref, ogla_ref, rgla_ref, zg_ref,
                  lnw_ref, lnb_ref, gn_ref, gb_ref, wb_ref, wo_ref, o_ref, *, tiles_per_seq):
    tm = h_ref.shape[0]
    avg = _segment_matrix(256, 64, 1.0 / 64)

    y = yrw_ref[...]
    d = y - _dot_exact_rhs(y, avg)
    var = _dot_exact_rhs(d * d, avg)
    y_rw = (d * lax.rsqrt(var + RW_GN_EPS) * lnw_ref[...] + lnb_ref[...] + bonus_ref[...]) * g_ref[...]

    o = ogla_ref[...]
    r = rgla_ref[...]
    y_gla = o * lax.rsqrt(_dot_exact_rhs(o * o, avg) + EPS) * gn_ref[...] * (r * jax.nn.sigmoid(r))

    acc = jnp.zeros((tm, D_MODEL), F32)
    for m, y_m in enumerate((ymla_ref[...], y_rw, ysb_ref[...], y_gla)):
        gate = jax.nn.sigmoid(zg_ref[:, m * D_MODEL:(m + 1) * D_MODEL].astype(F32) + gb_ref[m:m + 1, :])
        acc = acc + gate * _dot(y_m, wb_ref[m])
    delta = _dot(acc, wo_ref[...])
    row = (pl.program_id(0) % tiles_per_seq) * tm + _iota((tm, 1), 0)
    o_ref[...] = h_ref[...] + jnp.where(row >= PAD, delta, 0.0)


def _merge(h, y_mla, y_rw, bonus, g, y_sb, o_gla, z_gla, z_gate, p, lp):
    t = h.shape[0]
    tm = ROW_TILE
    row = lambda i: (i, 0)
    w256 = pl.BlockSpec((tm, 256), row)
    consts = [p["rw_ln_w"], p["rw_ln_b"], p["gla_norm"], p["gate_b"], p["w_branch"], p["w_out"]]
    return pl.pallas_call(
        functools.partial(_merge_kernel, tiles_per_seq=lp // tm),
        grid=(t // tm,),
        in_specs=[pl.BlockSpec((tm, D_MODEL), row), w256, w256, w256, w256, w256, w256,
                  pl.BlockSpec((tm, 256), lambda i: (i, 2)),
                  pl.BlockSpec((tm, W_GATE), row)]
        + [_const_spec(c.shape) for c in consts],
        out_specs=pl.BlockSpec((tm, D_MODEL), row),
        out_shape=jax.ShapeDtypeStruct((t, D_MODEL), F32),
        compiler_params=_params(("parallel",)),
        name="merge",
    )(h, y_mla, y_rw, bonus, g, y_sb, o_gla, z_gla, z_gate, *consts)


def _ffn_kernel(h_ref, g_ref, win_ref, cw_ref, cb_ref, wout_ref, o_ref, tail_ref):
    @pl.when(pl.program_id(1) == 0)
    def _():
        tail_ref[...] = jnp.zeros_like(tail_ref)

    x = h_ref[...]
    tm = x.shape[0]
    n = _bf(_rms(x, EPS) * g_ref[...])
    rowi = _iota((tm, 1), 0)
    acc = jnp.zeros((tm, D_MODEL), F32)
    for c0 in range(0, D_FF, FFN_COL_CHUNK):
        c1 = c0 + FFN_COL_CHUNK
        a = jnp.dot(n, win_ref[:, c0:c1], preferred_element_type=F32)
        u = jnp.dot(n, win_ref[:, D_FF + c0:D_FF + c1], preferred_element_type=F32)
        prev1 = tail_ref[7:8, c0:c1]
        prev2 = tail_ref[6:7, c0:c1]
        a1 = jnp.where(rowi == 0, prev1, pltpu.roll(a, 1, 0))
        a2 = jnp.where(rowi == 0, prev2, jnp.where(rowi == 1, prev1, pltpu.roll(a, 2, 0)))
        tail_ref[:, c0:c1] = a[tm - 8:tm, :]
        conv = cb_ref[:, c0:c1] + cw_ref[0:1, c0:c1] * a2 + cw_ref[1:2, c0:c1] * a1 + cw_ref[2:3, c0:c1] * a
        acc = acc + _dot(conv * jax.nn.sigmoid(conv) * u, wout_ref[c0:c1, :])
    o_ref[...] = x + acc


def _ffn(h, g, w_in, conv_w, conv_b, w_out, lp):
    t = h.shape[0]
    tm = ROW_TILE
    nb = lp // tm
    row = lambda b, i: (b * nb + i, 0)
    return pl.pallas_call(
        _ffn_kernel,
        grid=(t // lp, nb),
        in_specs=[pl.BlockSpec((tm, D_MODEL), row), _const_spec((1, D_MODEL)), _const_spec(w_in.shape),
                  _const_spec(conv_w.shape), _const_spec(conv_b.shape), _const_spec(w_out.shape)],
        out_specs=pl.BlockSpec((tm, D_MODEL), row),
        out_shape=jax.ShapeDtypeStruct((t, D_MODEL), F32),
        scratch_shapes=[pltpu.VMEM((8, D_FF), F32)],
        compiler_params=_params(("arbitrary", "arbitrary")),
        name="conv_ffn",
    )(h, g, w_in, conv_w, conv_b, w_out)


def _final_norm_kernel(h_ref, g_ref, o_ref):
    o_ref[0] = _rms(h_ref[0], EPS) * g_ref[...]


def _final_norm(h3, g, seq):
    b = h3.shape[0]
    first = (PAD + N_META) // BLOCK
    return pl.pallas_call(
        _final_norm_kernel,
        grid=(b, seq // BLOCK),
        in_specs=[pl.BlockSpec((1, BLOCK, D_MODEL), lambda bi, i: (bi, i + first, 0)), _const_spec((1, D_MODEL))],
        out_specs=pl.BlockSpec((1, BLOCK, D_MODEL), lambda bi, i: (bi, i, 0)),
        out_shape=jax.ShapeDtypeStruct((b, seq, D_MODEL), F32),
        compiler_params=_params(("parallel", "parallel")),
        name="final_norm",
    )(h3, g)


def _rope_swap(w):
    half = w.shape[-1] // 2
    return jnp.concatenate([-w[..., half:], w[..., :half]], axis=-1)


def _layout_params(w_in, mla_w_uq, mla_w_ukv, rw_w2, rw_a2, gla_a2):
    kr = w_in[..., 384:416]
    w_mla = jnp.concatenate([w_in[..., 0:384], jnp.tile(kr, (1, 1, 4)), jnp.tile(_rope_swap(kr), (1, 1, 4))], axis=-1)
    zg = w_in[..., 2208:2992]
    w_gla = jnp.concatenate([zg[..., 0:512], zg[..., 528:784], zg[..., 512:528],
                             jnp.zeros(zg.shape[:-1] + (W_GLA - 784,), zg.dtype)], axis=-1)
    sb_q = w_in[..., 1440:1440 + SB_W] * (SB_HEAD ** -0.5 * LOG2_E)
    w_all = _bf(jnp.concatenate([w_mla, w_in[..., 416:1440], sb_q, w_in[..., 1440 + SB_W:2208], w_gla,
                                 w_in[..., 2992:]], axis=-1))

    depth = w_in.shape[0]
    wuq = mla_w_uq.reshape(depth, MLA_Q_RANK, N_HEADS, MLA_NOPE + MLA_ROPE)
    nope, rope = wuq[..., :MLA_NOPE], wuq[..., MLA_NOPE:]
    rope_sw = _rope_swap(rope)
    zeros64 = jnp.zeros((depth, MLA_Q_RANK, 64), w_in.dtype)
    pair = lambda x, a, b: jnp.concatenate([x[:, :, a], x[:, :, b]], axis=-1)
    rope_pair = lambda x, a, b: jnp.concatenate([x[:, :, a], x[:, :, b], zeros64], axis=-1)
    wq = _bf(jnp.concatenate([pair(nope, 0, 1), rope_pair(rope, 0, 1), pair(nope, 2, 3), rope_pair(rope, 2, 3),
                              rope_pair(rope_sw, 0, 1), rope_pair(rope_sw, 2, 3)], axis=-1))
    wukv = mla_w_ukv.reshape(depth, MLA_KV_RANK, N_HEADS, 128)
    wkv = _bf(jnp.concatenate([wukv[..., :64].reshape(depth, MLA_KV_RANK, 256),
                               wukv[..., 64:].reshape(depth, MLA_KV_RANK, 256)], axis=-1))

    z64 = jnp.zeros_like(rw_w2)
    w2 = _bf(jnp.concatenate([rw_w2, z64], axis=1))
    a2 = _bf(jnp.concatenate([z64, rw_a2], axis=1))
    gla_a2p = _bf(jnp.concatenate([gla_a2, jnp.zeros((depth, 128 - gla_a2.shape[1], GLA_QK), gla_a2.dtype)], axis=1))
    return w_all, wq, wkv, w2, a2, gla_a2p


def _rope_tables(lp):
    half = MLA_ROPE // 2
    freqs = ROPE_THETA ** (-jnp.arange(half, dtype=F32) / half)
    pos = (jnp.arange(lp) - PAD).astype(F32)
    ang = pos[:, None] * freqs[None, :]
    return jnp.tile(jnp.cos(ang), (1, 128 // half)), jnp.tile(jnp.sin(ang), (1, 128 // half))


def kernel(x, meta_tokens, norm_mix, w_in, mla_q_norm, mla_w_uq, mla_kv_norm, mla_w_ukv, rw_mu, rw_w0, rw_w2, rw_a0, rw_a2, rw_g2, rw_k_k, rw_k_a, rw_r_k, rw_ln_w, rw_ln_b, gla_a2, gla_a_b, gla_norm, gate_b, w_branch, w_out, norm_ffn, w_ffn_in, ffn_conv_w, ffn_conv_b, w_ffn_out, norm_final):
    batch, seq, _ = x.shape
    depth = w_in.shape[0]
    lp = PAD + N_META + seq
    t = batch * lp
    assert lp % ROW_TILE == 0 and lp % ATTN_TILE == 0 and lp % GLA_CHUNK == 0 and lp % RW_CHUNK == 0

    w_all, wq, wkv, rw_w2p, rw_a2p, gla_a2p = _layout_params(w_in, mla_w_uq, mla_w_ukv, rw_w2, rw_a2, gla_a2)
    w_branch_b, w_out_b, w_ffn_in_b, w_ffn_out_b, rw_g2_b = map(_bf, (w_branch, w_out, w_ffn_in, w_ffn_out, rw_g2))
    vec = lambda a: a.reshape(depth, 1, -1)
    cos, sin = _rope_tables(lp)
    idx = jnp.arange(SB_SUFFIX_SPLIT)
    later_mat = jnp.where(idx[:, None] > idx[None, :], 1.0, 0.0).astype(BF16)

    meta = jnp.broadcast_to(meta_tokens[None].astype(x.dtype), (batch, N_META, D_MODEL))
    h = jnp.concatenate([jnp.zeros((batch, PAD, D_MODEL), x.dtype), meta, x], axis=1).reshape(t, D_MODEL)

    for i in range(depth):
        z_mla, z_rw, z_sb, z_gla, z_gate, sb_values = _in_proj(h, vec(norm_mix)[i], w_all[i])
        q, k, v = _mla_prep(z_mla, vec(mla_q_norm)[i], vec(mla_kv_norm)[i], wq[i], wkv[i], cos, sin, lp)
        y_mla = _mla_attn(q, k, v, lp)
        rw = {"rw_mu": vec(rw_mu)[i], "rw_w0": vec(rw_w0)[i], "rw_w2": rw_w2p[i], "rw_a0": vec(rw_a0)[i],
              "rw_a2": rw_a2p[i], "rw_g2": rw_g2_b[i], "rw_k_k": vec(rw_k_k)[i], "rw_k_a": vec(rw_k_a)[i],
              "rw_r_k": vec(rw_r_k)[i]}
        *chunk_terms, g, bonus = _rw_prep(z_rw, rw, lp)
        y_rw = _rw_scan(*chunk_terms, batch)
        y_sb = _sb_attn(z_sb, sb_values, later_mat, lp)
        o_gla = _gla(z_gla, gla_a2p[i], vec(gla_a_b)[i], batch)
        mp = {"rw_ln_w": vec(rw_ln_w)[i], "rw_ln_b": vec(rw_ln_b)[i], "gla_norm": vec(gla_norm)[i],
              "gate_b": gate_b[i], "w_branch": w_branch_b[i], "w_out": w_out_b[i]}
        h = _merge(h, y_mla, y_rw, bonus, g, y_sb, o_gla, z_gla, z_gate, mp, lp)
        h = _ffn(h, vec(norm_ffn)[i], w_ffn_in_b[i], ffn_conv_w[i], vec(ffn_conv_b)[i], w_ffn_out_b[i], lp)
    return _final_norm(h.reshape(batch, lp, D_MODEL), norm_final.reshape(1, D_MODEL), seq)
```

```python
import functools

import jax
import jax.numpy as jnp
from jax import lax
from jax.experimental import pallas as pl
from jax.experimental.pallas import tpu as pltpu

F32 = jnp.float32
BF16 = jnp.bfloat16

D_MODEL = 1024
DEPTH = 4
N_META = 16
BLOCK = 128
PAD = (-N_META) % BLOCK
EPS = 1e-6
NEG_INF = -1e30
LOG2_E = 1.4426950408889634

N_HEADS = 4
MLA_NOPE = 64
MLA_ROPE = 32
MLA_Q_RANK = 256
MLA_KV_RANK = 128
ROPE_THETA = 10000.0

RW_W = 256
RW_GN_EPS = 64e-5
RW_CHUNK = 64

SB_W = 256
SB_HEAD = 64
SB_SUFFIX_SPLIT = 256

GLA_DK = 32
GLA_QK = 128
GLA_W = 256
GLA_TAU = 16.0
GLA_CHUNK = 128

D_FF = 2816
FFN_COL_CHUNK = 1408

W_MLA, W_RW, W_SB, W_GLA, W_GATE = 640, 1024, 768, 896, 4096
IN_GROUP_WIDTHS = (W_MLA, W_RW, W_SB, W_GLA, W_GATE)
IN_GROUP_DTYPES = (BF16, F32, BF16, F32, BF16)

ROW_TILE = 384
ATTN_TILE = 384
FINAL_ROWS = 1024
VMEM_LIMIT = 56 * 1024 * 1024


def _bf(x):
    return x.astype(BF16)


def _dot(a, b):
    return jnp.dot(_bf(a), _bf(b), preferred_element_type=F32)


def _dot_nt(a, b):
    return lax.dot_general(_bf(a), _bf(b), (((1,), (1,)), ((), ())), preferred_element_type=F32)


def _dot_tn(a, b):
    return lax.dot_general(_bf(a), _bf(b), (((0,), (0,)), ((), ())), preferred_element_type=F32)


def _split_hi_lo(x):
    hi = _bf(x)
    lo = _bf(x - hi.astype(F32))
    return hi, lo


def _dot_exact_rhs(x, m):
    hi, lo = _split_hi_lo(x)
    return jnp.dot(hi, m, preferred_element_type=F32) + jnp.dot(lo, m, preferred_element_type=F32)


def _dot_exact_lhs(m, x):
    hi, lo = _split_hi_lo(x)
    return jnp.dot(m, hi, preferred_element_type=F32) + jnp.dot(m, lo, preferred_element_type=F32)


def _iota(shape, dim):
    return lax.broadcasted_iota(jnp.int32, shape, dim)


def _div_pow2(x, d):
    assert d & (d - 1) == 0
    return lax.shift_right_logical(x, d.bit_length() - 1)


def _same_segment(shape, row_seg, col_seg):
    return _div_pow2(_iota(shape, 0), row_seg) == _div_pow2(_iota(shape, 1), col_seg)


def _segment_matrix(n, seg, value):
    return jnp.where(_same_segment((n, n), seg, seg), value, 0.0).astype(BF16)


def _head_lane_mask(width, head_width, h):
    lane = _iota((1, width), 1)
    return (lane >= h * head_width) & (lane < (h + 1) * head_width)


def _stack_heads(x, head_width):
    w = x.shape[1]
    return jnp.concatenate(
        [jnp.where(_head_lane_mask(w, head_width, h), x, 0.0) for h in range(N_HEADS)], axis=0)


def _unstack_heads(xs, c):
    return xs[0:c] + xs[c:2 * c] + xs[2 * c:3 * c] + xs[3 * c:4 * c]


def _rms(x, eps):
    return x * lax.rsqrt(jnp.mean(x * x, axis=-1, keepdims=True) + eps)


def _const_spec(shape):
    nd = len(shape)
    return pl.BlockSpec(shape, lambda *_: (0,) * nd)


def _layer_spec(stacked, layer):
    nd = stacked.ndim - 1
    return pl.BlockSpec((None,) + stacked.shape[1:], lambda *_: (layer,) + (0,) * nd)


def _params(sem, vmem=VMEM_LIMIT):
    return pltpu.CompilerParams(dimension_semantics=sem, vmem_limit_bytes=vmem)


def _in_proj_kernel(h_ref, g_ref, w_ref, *out_refs):
    *group_refs, sb_values_ref = out_refs
    n = _bf(_rms(h_ref[...], EPS) * g_ref[...])
    off = 0
    for o_ref, width in zip(group_refs, IN_GROUP_WIDTHS):
        for c0 in range(0, width, 1024):
            c1 = min(c0 + 1024, width)
            o_ref[:, c0:c1] = jnp.dot(
                n, w_ref[:, off + c0:off + c1], preferred_element_type=F32).astype(o_ref.dtype)
        off += width
    sb_values_ref[0] = _head_stacked_values(group_refs[2][:, 2 * SB_W:3 * SB_W])


def _in_proj(h, g, w_all, layer):
    t = h.shape[0]
    tm = ROW_TILE
    return pl.pallas_call(
        _in_proj_kernel,
        grid=(t // tm,),
        in_specs=[pl.BlockSpec((tm, D_MODEL), lambda i: (i, 0)),
                  _const_spec((1, D_MODEL)),
                  _layer_spec(w_all, layer)],
        out_specs=[pl.BlockSpec((tm, w), lambda i: (i, 0)) for w in IN_GROUP_WIDTHS]
        + [pl.BlockSpec((1, N_HEADS * tm, SB_W), lambda i: (i, 0, 0))],
        out_shape=[jax.ShapeDtypeStruct((t, w), dt) for w, dt in zip(IN_GROUP_WIDTHS, IN_GROUP_DTYPES)]
        + [jax.ShapeDtypeStruct((t // tm, N_HEADS * tm, SB_W), BF16)],
        compiler_params=_params(("parallel",)),
        name="in_proj",
    )(h, g, w_all)


def _mla_prep_kernel(z_ref, qg_ref, kvg_ref, wq_ref, wkv_ref, cos_ref, sin_ref, q_ref, k_ref, vt_ref):
    z = z_ref[...].astype(F32)
    nq = _rms(z[:, 0:256], EPS) * qg_ref[...]
    nkv = _rms(z[:, 256:384], EPS) * kvg_ref[...]
    cos = cos_ref[...]
    sin = sin_ref[...]
    ql = _dot(nq, wq_ref[...])
    scale = (MLA_NOPE + MLA_ROPE) ** -0.5
    q = jnp.concatenate([ql[:, 0:128], ql[:, 128:256] * cos + ql[:, 512:640] * sin,
                         ql[:, 256:384], ql[:, 384:512] * cos + ql[:, 640:768] * sin], axis=1)
    q_ref[...] = _bf(q * scale)
    kvl = _dot(nkv, wkv_ref[...])
    k_rope = z[:, 384:512] * cos + z[:, 512:640] * sin
    k_rope = jnp.where(_iota((1, 128), 1) < 2 * MLA_ROPE, k_rope, 0.0)
    k_ref[...] = _bf(jnp.concatenate([kvl[:, 0:128], k_rope, kvl[:, 128:256], k_rope], axis=1))
    vt = _bf(kvl[:, 256:512].T)
    row_head = _div_pow2(_iota((256, 1), 0), 64)
    vt_ref[0] = jnp.concatenate([jnp.where(row_head == h, vt, jnp.zeros((), BF16)) for h in range(N_HEADS)], axis=1)


def _mla_prep(z_mla, qg, kvg, wq, wkv, layer, cos, sin, lp):
    t = z_mla.shape[0]
    tm = ROW_TILE
    nb = lp // tm
    row = lambda b, i: (b * nb + i, 0)
    return pl.pallas_call(
        _mla_prep_kernel,
        grid=(t // lp, nb),
        in_specs=[pl.BlockSpec((tm, W_MLA), row),
                  _const_spec((1, MLA_Q_RANK)), _const_spec((1, MLA_KV_RANK)),
                  _layer_spec(wq, layer), _layer_spec(wkv, layer),
                  pl.BlockSpec((tm, 128), lambda b, i: (i, 0)),
                  pl.BlockSpec((tm, 128), lambda b, i: (i, 0))],
        out_specs=[pl.BlockSpec((tm, 512), row), pl.BlockSpec((tm, 512), row),
                   pl.BlockSpec((1, 256, N_HEADS * tm), lambda b, i: (b * nb + i, 0, 0))],
        out_shape=[jax.ShapeDtypeStruct((t, 512), BF16), jax.ShapeDtypeStruct((t, 512), BF16),
                   jax.ShapeDtypeStruct((t // tm, 256, N_HEADS * tm), BF16)],
        compiler_params=_params(("parallel", "parallel")),
        name="mla_prep",
    )(z_mla, qg, kvg, wq, wkv, cos, sin)


def _per_head_lanes(cols):
    lane = _iota((1, 256), 1)
    return jnp.where(lane < 64, cols[0], jnp.where(lane < 128, cols[1], jnp.where(lane < 192, cols[2], cols[3])))


def _head_stacked_values(vb):
    zero = jnp.zeros((), vb.dtype)
    return jnp.concatenate([jnp.where(_head_lane_mask(256, 64, h), vb, zero) for h in range(N_HEADS)], axis=0)


def _per_head_rows(rows, n):
    return jnp.concatenate([jnp.broadcast_to(r, (64, n)) for r in rows], axis=0)


def _mla_attn_kernel(q_ref, k_ref, vt_ref, o_ref, qh_ref, m_ref, l_ref, acc_ref, p_ref, alpha_ref, *, tile):
    i = pl.program_id(1)
    q_pos = i * tile + _iota((1, tile), 1)
    lane = _iota((1, 256), 1)
    for h in range(N_HEADS):
        half, slot = h // 2, h % 2
        head_lanes = (((lane >= slot * MLA_NOPE) & (lane < (slot + 1) * MLA_NOPE))
                      | ((lane >= 128 + slot * MLA_ROPE) & (lane < 128 + (slot + 1) * MLA_ROPE)))
        qh_ref[half, slot * tile:(slot + 1) * tile, :] = jnp.where(
            head_lanes, q_ref[:, half * 256:(half + 1) * 256], jnp.zeros((), BF16))
    m_ref[...] = jnp.full(m_ref.shape, NEG_INF, F32)
    l_ref[...] = jnp.zeros(l_ref.shape, F32)
    acc_ref[...] = jnp.zeros(acc_ref.shape, F32)

    def add_values(j, slot):
        pv = jnp.dot(vt_ref[j], p_ref[slot], preferred_element_type=F32)
        acc_ref[...] = _per_head_rows([alpha_ref[slot, h] for h in range(N_HEADS)], tile) * acc_ref[...] + pv

    def block(j, masked, prev, slot):
        start = pl.multiple_of(j * tile, tile)
        heads = range(N_HEADS)
        s_pair = [_dot_nt(k_ref[pl.ds(start, tile), half * 256:(half + 1) * 256], qh_ref[half]) for half in (0, 1)]
        s = [s_pair[h // 2][:, (h % 2) * tile:(h % 2 + 1) * tile] for h in heads]
        if prev is not None:
            add_values(prev, 1 - slot)
        for h in heads:
            s_h = s[h]
            if masked:
                k_pos = start + _iota((tile, 1), 0)
                s_h = jnp.where((k_pos <= q_pos) & (k_pos >= PAD), s_h, NEG_INF)
            m_old = m_ref[h]
            m_new = jnp.maximum(m_old, jnp.max(s_h, axis=0, keepdims=True))
            p = jnp.exp(s_h - m_new)
            alpha = jnp.exp(m_old - m_new)
            l_ref[h] = alpha * l_ref[h] + jnp.sum(p, axis=0, keepdims=True)
            m_ref[h] = m_new
            p_ref[slot, h * tile:(h + 1) * tile, :] = _bf(p)
            alpha_ref[slot, h] = alpha

    block(i, True, None, 0)

    @pl.when(i > 0)
    def _():
        block(0, True, i, 1)

    interior_blocks = jnp.maximum(i - 1, 0)

    def interior_pair(t, carry):
        j = 1 + 2 * t
        block(j, False, j - 1, 0)
        block(j + 1, False, j, 1)
        return carry

    lax.fori_loop(0, lax.shift_right_logical(interior_blocks, 1), interior_pair, 0)
    odd = (interior_blocks & 1) == 1

    @pl.when(odd)
    def _():
        block(i - 1, False, i - 2, 0)

    last = jnp.maximum(i - 1, 0)
    last_in_slot0 = (i == 0) | odd

    @pl.when(last_in_slot0)
    def _():
        add_values(last, 0)

    @pl.when(jnp.logical_not(last_in_slot0))
    def _():
        add_values(last, 1)
    o_ref[...] = (acc_ref[...] / _per_head_rows([l_ref[h] for h in range(N_HEADS)], tile)).T


def _mla_attn(q, k, vt, lp):
    t = q.shape[0]
    tile = ATTN_TILE
    nb = lp // tile
    return pl.pallas_call(
        functools.partial(_mla_attn_kernel, tile=tile),
        grid=(t // lp, nb),
        in_specs=[pl.BlockSpec((tile, 512), lambda b, i: (b * nb + i, 0)),
                  pl.BlockSpec((lp, 512), lambda b, i: (b, 0)),
                  pl.BlockSpec((nb, 256, N_HEADS * tile), lambda b, i: (b, 0, 0))],
        out_specs=pl.BlockSpec((tile, 256), lambda b, i: (b * nb + i, 0)),
        out_shape=jax.ShapeDtypeStruct((t, 256), F32),
        scratch_shapes=[pltpu.VMEM((2, 2 * tile, 256), BF16), pltpu.VMEM((N_HEADS, 1, tile), F32),
                        pltpu.VMEM((N_HEADS, 1, tile), F32), pltpu.VMEM((256, tile), F32),
                        pltpu.VMEM((2, N_HEADS * tile, tile), BF16), pltpu.VMEM((2, N_HEADS, 1, tile), F32)],
        compiler_params=_params(("parallel", "arbitrary")),
        name="mla_attn",
    )(q, k, vt)


def _sb_attn_kernel(q_ref, k_ref, v_ref, later_ref, o_ref, qh_ref, c_ref, acc_ref, wts_ref, *, tile):
    i = pl.program_id(1)
    row = i * tile + _iota((tile, 1), 0)
    for h in range(N_HEADS):
        qh_ref[h] = jnp.where(_head_lane_mask(SB_W, SB_HEAD, h), q_ref[...], jnp.zeros((), BF16))
    c_ref[...] = jnp.zeros(c_ref.shape, F32)
    acc_ref[...] = jnp.zeros(acc_ref.shape, F32)

    def add_values(j, slot):
        acc_ref[...] += jnp.dot(wts_ref[slot], v_ref[j], preferred_element_type=F32)

    def block(j, masked, prev, slot):
        start = pl.multiple_of(j * tile, tile)
        heads = range(N_HEADS)
        cut = SB_SUFFIX_SPLIT
        z = [_dot_nt(qh_ref[h], k_ref[pl.ds(start, tile), :]) for h in heads]
        if prev is not None:
            add_values(prev, 1 - slot)
        log_take = [jnp.minimum(z[h], 0.0) - jnp.log2(1.0 + jnp.exp2(-jnp.abs(z[h]))) for h in heads]
        log_keep = [log_take[h] - z[h] for h in heads]
        if masked:
            mask = (start + _iota((1, tile), 1)) < row
            log_keep = [jnp.where(mask, log_keep[h], 0.0) for h in heads]
        keep16 = [_bf(log_keep[h]) for h in heads]
        later_head = [jnp.dot(keep16[h][:, :cut], later_ref[...], preferred_element_type=F32) for h in heads]
        later_tail = [jnp.dot(keep16[h][:, cut:], later_ref[:tile - cut, :tile - cut], preferred_element_type=F32)
                      for h in heads]
        for h in heads:
            head_sum = jnp.sum(log_keep[h][:, :cut], axis=-1, keepdims=True)
            tail_sum = jnp.sum(log_keep[h][:, cut:], axis=-1, keepdims=True)
            c = c_ref[h]
            later = jnp.concatenate([later_head[h] + tail_sum, later_tail[h]], axis=1)
            w = jnp.exp2(log_take[h] + later + c)
            if masked:
                w = jnp.where(mask, w, 0.0)
            c_ref[h] = c + (head_sum + tail_sum)
            wts_ref[slot, :, h * tile:(h + 1) * tile] = _bf(w)

    block(i, True, None, 0)

    def below_pair(t, carry):
        j = i - 1 - 2 * t
        block(j, False, j + 1, 1)
        block(j - 1, False, j, 0)
        return carry

    lax.fori_loop(0, lax.shift_right_logical(i, 1), below_pair, 0)
    odd = (i & 1) == 1

    @pl.when(odd)
    def _():
        block(0, False, 1, 1)
        add_values(0, 1)

    @pl.when(jnp.logical_not(odd))
    def _():
        add_values(0, 0)
    o_ref[...] = acc_ref[...]


def _sb_attn(z_sb, sb_values, later_mat, lp):
    t = z_sb.shape[0]
    tile = ATTN_TILE
    nb = lp // tile
    return pl.pallas_call(
        functools.partial(_sb_attn_kernel, tile=tile),
        grid=(t // lp, nb),
        in_specs=[pl.BlockSpec((tile, SB_W), lambda b, i: (b * nb + i, 0)),
                  pl.BlockSpec((lp, SB_W), lambda b, i: (b, 1)),
                  pl.BlockSpec((nb, N_HEADS * tile, SB_W), lambda b, i: (b, 0, 0)),
                  _const_spec((SB_SUFFIX_SPLIT, SB_SUFFIX_SPLIT))],
        out_specs=pl.BlockSpec((tile, SB_W), lambda b, i: (b * nb + i, 0)),
        out_shape=jax.ShapeDtypeStruct((t, SB_W), F32),
        scratch_shapes=[pltpu.VMEM((N_HEADS, tile, SB_W), BF16), pltpu.VMEM((N_HEADS, tile, 1), F32),
                        pltpu.VMEM((tile, SB_W), F32), pltpu.VMEM((2, tile, N_HEADS * tile), BF16)],
        compiler_params=_params(("parallel", "arbitrary")),
        name="sb_attn",
    )(z_sb, z_sb, sb_values, later_mat)


def _compact_eye(c):
    return jnp.where(_iota((c, N_HEADS * c), 0) == (_iota((c, N_HEADS * c), 1) & (c - 1)), 1.0, 0.0)


def _rw_prep_kernel(z_ref, zprev_ref, mu_ref, w0_ref, w2_ref, a0_ref, a2_ref, g2_ref, kk_ref, ka_ref, rk_ref,
                    w_o, rt_o, arb_o, kbar_o, bbar_o, v_o, uv_o, yv_o, dec_o, g_o, bonus_o):
    i = pl.program_id(1)
    z = z_ref[...]
    tm = z.shape[0]
    prev = jnp.where(i == 0, 0.0, zprev_ref[7:8, :])
    shifted = jnp.where(_iota((tm, 1), 0) == 0, prev, pltpu.roll(z, 1, 0))
    zz = z + (shifted - z) * mu_ref[...]
    r, k, v = zz[:, 0:256], zz[:, 256:512], zz[:, 512:768]
    lora_in = zz[:, 768:896]
    u = w0_ref[...] + _dot(jnp.tanh(lora_in), w2_ref[...])
    w = jnp.minimum(u, 0.0) - jnp.log1p(jnp.exp(-jnp.abs(u))) - 0.5
    a = jax.nn.sigmoid(a0_ref[...] + _dot(lora_in, a2_ref[...]))
    seg = _segment_matrix(RW_W, 64, 1.0)
    kx = k * kk_ref[...]
    kap = kx / jnp.maximum(jnp.sqrt(_dot_exact_rhs(kx * kx, seg)), 1e-12)
    kmod = k * (1.0 + (a - 1.0) * ka_ref[...])
    beta = kap * a
    lw = -jnp.exp(w)
    g_o[...] = _dot(jax.nn.sigmoid(zz[:, 896:1024]), g2_ref[...])
    bonus_o[...] = _dot_exact_rhs(r * kmod * rk_ref[...], seg) * v

    c = RW_CHUNK
    same_chunk = _same_segment((tm, tm), c, c)
    lw_hi, lw_lo = _split_hi_lo(lw)
    cum = jnp.where(same_chunk & (_iota((tm, tm), 0) >= _iota((tm, tm), 1)), 1.0, 0.0).astype(BF16)
    tot = jnp.where(same_chunk, 1.0, 0.0).astype(BF16)
    b = jnp.dot(cum, lw_hi, preferred_element_type=F32) + jnp.dot(cum, lw_lo, preferred_element_type=F32)
    b_end = jnp.dot(tot, lw_hi, preferred_element_type=F32) + jnp.dot(tot, lw_lo, preferred_element_type=F32)
    grow = jnp.exp(-b)
    to_end = jnp.exp(b_end - b)
    kap_t = kap * jnp.exp(b - lw)
    r_t = r * jnp.exp(b)
    beta_g = beta * grow
    k_g = kmod * grow
    rt_o[...] = _bf(r_t)
    kbar_o[...] = _bf(kmod * to_end)
    bbar_o[...] = _bf(beta * to_end)
    v_o[...] = _bf(v)
    decay_end = jnp.exp(b_end)

    lane_pos = _iota((c, N_HEADS * c), 1) & (c - 1)
    strictly_earlier = _iota((c, N_HEADS * c), 0) > lane_pos
    not_later = _iota((c, N_HEADS * c), 0) >= lane_pos
    chunks = [slice(n * c, (n + 1) * c) for n in range(tm // c)]
    pair = [_dot_nt(jnp.concatenate([kap_t[rows], r_t[rows]], axis=0),
                    jnp.concatenate([_stack_heads(beta_g[rows], 64), _stack_heads(k_g[rows], 64)], axis=0))
            for rows in chunks]
    a_kb = [jnp.where(strictly_earlier, p[0:c, 0:4 * c], 0.0) for p in pair]
    a_kk = [jnp.where(strictly_earlier, p[0:c, 4 * c:8 * c], 0.0) for p in pair]
    a_rk = [jnp.where(not_later, p[c:2 * c, 4 * c:8 * c], 0.0) for p in pair]
    for rows, p in zip(chunks, pair):
        arb_o[rows, :] = _bf(jnp.where(not_later, p[c:2 * c, 0:4 * c], 0.0))
    inv = [_compact_eye(c) - a for a in a_kb]
    power = [_dot(a, _stack_heads(a, c)) for a in a_kb]
    span = 2
    while span < c:
        inv = [t + _dot(t, _stack_heads(p, c)) for t, p in zip(inv, power)]
        span *= 2
        if span < c:
            power = [_dot(p, _stack_heads(p, c)) for p in power]
    v_st = [_stack_heads(v[rows], 64) for rows in chunks]
    kk_v = [_dot(a, vs) for a, vs in zip(a_kk, v_st)]
    for n, rows in enumerate(chunks):
        w_o[rows, :] = _bf(_dot(inv[n], _stack_heads(kap_t[rows], 64)))
        uv_o[rows, :] = _dot(inv[n], _stack_heads(kk_v[n], 64))
        yv_o[rows, :] = _dot(a_rk[n], v_st[n])
        dec_o[n * 8:(n + 1) * 8, :] = decay_end[n * c:n * c + 8]


def _rw_prep(z_rw, p, lp):
    t = z_rw.shape[0]
    tm = ROW_TILE
    nb = lp // tm
    row = lambda b, i: (b * nb + i, 0)
    prev = lambda b, i: (jnp.maximum(b * (lp // 8) + i * (tm // 8) - 1, 0), 0)
    consts = [p["rw_mu"], p["rw_w0"], p["rw_w2"], p["rw_a0"], p["rw_a2"], p["rw_g2"],
              p["rw_k_k"], p["rw_k_a"], p["rw_r_k"]]
    return pl.pallas_call(
        _rw_prep_kernel,
        grid=(t // lp, nb),
        in_specs=[pl.BlockSpec((tm, W_RW), row), pl.BlockSpec((8, W_RW), prev)]
        + [_const_spec(c.shape) for c in consts],
        out_specs=[pl.BlockSpec((tm, RW_W), row)] * 8
        + [pl.BlockSpec((tm // RW_CHUNK * 8, RW_W), row)] + [pl.BlockSpec((tm, RW_W), row)] * 2,
        out_shape=[jax.ShapeDtypeStruct((t, RW_W), BF16)] * 6 + [jax.ShapeDtypeStruct((t, RW_W), F32)] * 2
        + [jax.ShapeDtypeStruct((t // RW_CHUNK * 8, RW_W), F32)] + [jax.ShapeDtypeStruct((t, RW_W), F32)] * 2,
        compiler_params=_params(("parallel", "parallel")),
        name="rw_prep",
    )(z_rw, z_rw, *consts)


def _rw_scan_kernel(w_ref, rt_ref, arb_ref, kbar_ref, bbar_ref, v_ref, uv_ref, yv_ref, dec_ref, y_ref, ht_ref):
    c = RW_CHUNK

    @pl.when(pl.program_id(0) == 0)
    def _():
        ht_ref[...] = jnp.zeros_like(ht_ref)

    same_head = _same_segment((RW_W, RW_W), 64, 64)
    seqs = range(ht_ref.shape[0])
    ht = [ht_ref[s] for s in seqs]
    from_state = [_dot_nt(jnp.concatenate([w_ref[s], rt_ref[s]], axis=0), ht[s]) for s in seqs]
    u = [from_state[s][0:c] + uv_ref[s] for s in seqs]
    from_u = [_dot(arb_ref[s], _stack_heads(u[s], 64)) for s in seqs]
    upd = [_dot_tn(jnp.concatenate([v_ref[s], _bf(u[s])], axis=0),
                   jnp.concatenate([kbar_ref[s], -bbar_ref[s]], axis=0)) for s in seqs]
    for s in seqs:
        y_ref[s] = from_state[s][c:2 * c] + yv_ref[s] - from_u[s]
        ht_ref[s] = ht[s] * dec_ref[s, 0:1, :] + jnp.where(same_head, upd[s], 0.0)


def _rw_scan(w, rt, arb, kbar, bbar, v, uv, yv, dec, batch):
    t = w.shape[0]
    lp = t // batch
    c = RW_CHUNK
    as_seq = lambda a: a.reshape(batch, a.shape[0] // batch, RW_W)
    spec = pl.BlockSpec((batch, c, RW_W), lambda i: (0, i, 0))
    y = pl.pallas_call(
        _rw_scan_kernel,
        grid=(lp // c,),
        in_specs=[spec] * 8 + [pl.BlockSpec((batch, 8, RW_W), lambda i: (0, i, 0))],
        out_specs=spec,
        out_shape=jax.ShapeDtypeStruct((batch, lp, RW_W), F32),
        scratch_shapes=[pltpu.VMEM((batch, RW_W, RW_W), F32)],
        compiler_params=_params(("arbitrary",)),
        name="rw_scan",
    )(*map(as_seq, (w, rt, arb, kbar, bbar, v, uv, yv, dec)))
    return y.reshape(t, RW_W)


def _gla_kernel(q_ref, k_ref, v_ref, al_ref, a2_ref, ab_ref, o_ref, st_ref):
    c = GLA_CHUNK
    n = N_HEADS * c
    seqs = range(st_ref.shape[0])

    @pl.when(pl.program_id(0) == 0)
    def _():
        st_ref[...] = jnp.zeros_like(st_ref)

    tri = jnp.where(_iota((c, c), 0) >= _iota((c, c), 1), 1.0, 0.0).astype(BF16)
    causal = (_iota((n, c), 0) & (c - 1)) >= _iota((n, c), 1)
    same_head = _same_segment((GLA_W, GLA_QK), 64, GLA_DK)
    x = [_dot(al_ref[s], a2_ref[...]) + ab_ref[...] for s in seqs]
    log_a = [(jnp.minimum(x[s], 0.0) - jnp.log1p(jnp.exp(-jnp.abs(x[s])))) * (1.0 / GLA_TAU) for s in seqs]
    b = [_dot_exact_lhs(tri, log_a[s]) for s in seqs]
    q = [q_ref[s] * (GLA_DK ** -0.5) for s in seqs]
    st = [st_ref[s] for s in seqs]
    inter = [_dot_nt(q[s] * jnp.exp(b[s]), st[s]) for s in seqs]
    scores = []
    for s in seqs:
        b_mid = b[s][c // 2 - 1:c // 2, :]
        sc = _dot_nt(_stack_heads(q[s] * jnp.exp(b[s] - b_mid), GLA_DK), k_ref[s] * jnp.exp(b_mid - b[s]))
        scores.append(jnp.where(causal, sc, 0.0))
    per_head = [_dot(scores[s], v_ref[s]) for s in seqs]
    upd = [_dot_tn(v_ref[s], k_ref[s] * jnp.exp(b[s][c - 1:c, :] - b[s])) for s in seqs]
    for s in seqs:
        intra = jnp.zeros((c, GLA_W), F32)
        for h in range(N_HEADS):
            intra = intra + jnp.where(_head_lane_mask(GLA_W, 64, h), per_head[s][h * c:(h + 1) * c], 0.0)
        o_ref[s] = inter[s] + intra
        st_ref[s] = st[s] * jnp.exp(b[s][c - 1:c, :]) + jnp.where(same_head, upd[s], 0.0)


def _gla(z_gla, a2, ab, batch):
    t = z_gla.shape[0]
    lp = t // batch
    c = GLA_CHUNK
    z3 = z_gla.reshape(batch, lp, W_GLA)
    o = pl.pallas_call(
        _gla_kernel,
        grid=(lp // c,),
        in_specs=[pl.BlockSpec((batch, c, 128), lambda i: (0, i, 0)),
                  pl.BlockSpec((batch, c, 128), lambda i: (0, i, 1)),
                  pl.BlockSpec((batch, c, 256), lambda i: (0, i, 1)),
                  pl.BlockSpec((batch, c, 128), lambda i: (0, i, 6)),
                  _const_spec(a2.shape), _const_spec(ab.shape)],
        out_specs=pl.BlockSpec((batch, c, GLA_W), lambda i: (0, i, 0)),
        out_shape=jax.ShapeDtypeStruct((batch, lp, GLA_W), F32),
        scratch_shapes=[pltpu.VMEM((batch, GLA_W, GLA_QK), F32)],
        compiler_params=_params(("arbitrary",)),
        name="gla",
    )(z3, z3, z3, z3, a2, ab)
    return o.reshape(t, GLA_W)


def _merge_kernel(h_ref, ymla_ref, yrw_ref, bonus_ref, g_ref, ysb_ref, ogla_ref, rgla_ref, zg_ref,
                  lnw_ref, lnb_ref, gn_ref, gb_ref, wb_ref, wo_ref, o_ref, *, tiles_per_seq):
    tm = h_ref.shape[0]
    avg = _segment_matrix(256, 64, 1.0 / 64)

    y = yrw_ref[...]
    d = y - _dot_exact_rhs(y, avg)
    var = _dot_exact_rhs(d * d, avg)
    y_rw = (d * lax.rsqrt(var + RW_GN_EPS) * lnw_ref[...] + lnb_ref[...] + bonus_ref[...]) * g_ref[...]

    o = ogla_ref[...]
    r = rgla_ref[...]
    y_gla = o * lax.rsqrt(_dot_exact_rhs(o * o, avg) + EPS) * gn_ref[...] * (r * jax.nn.sigmoid(r))

    acc = jnp.zeros((tm, D_MODEL), F32)
    for m, y_m in enumerate((ymla_ref[...], y_rw, ysb_ref[...], y_gla)):
        gate = jax.nn.sigmoid(zg_ref[:, m * D_MODEL:(m + 1) * D_MODEL].astype(F32) + gb_ref[m:m + 1, :])
        acc = acc + gate * _dot(y_m, wb_ref[m])
    delta = _dot(acc, wo_ref[...])
    row = (pl.program_id(0) % tiles_per_seq) * tm + _iota((tm, 1), 0)
    o_ref[...] = h_ref[...] + jnp.where(row >= PAD, delta, 0.0)


def _merge(h, y_mla, y_rw, bonus, g, y_sb, o_gla, z_gla, z_gate, p, lp):
    t = h.shape[0]
    tm = ROW_TILE
    row = lambda i: (i, 0)
    w256 = pl.BlockSpec((tm, 256), row)
    consts = [p["rw_ln_w"], p["rw_ln_b"], p["gla_norm"], p["gate_b"]]
    return pl.pallas_call(
        functools.partial(_merge_kernel, tiles_per_seq=lp // tm),
        grid=(t // tm,),
        in_specs=[pl.BlockSpec((tm, D_MODEL), row), w256, w256, w256, w256, w256, w256,
                  pl.BlockSpec((tm, 256), lambda i: (i, 2)),
                  pl.BlockSpec((tm, W_GATE), row)]
        + [_const_spec(c.shape) for c in consts]
        + [_layer_spec(p["w_branch"], p["layer"]), _layer_spec(p["w_out"], p["layer"])],
        out_specs=pl.BlockSpec((tm, D_MODEL), row),
        out_shape=jax.ShapeDtypeStruct((t, D_MODEL), F32),
        compiler_params=_params(("parallel",)),
        name="merge",
    )(h, y_mla, y_rw, bonus, g, y_sb, o_gla, z_gla, z_gate, *consts, p["w_branch"], p["w_out"])


def _ffn_kernel(h_ref, g_ref, win_ref, cw_ref, cb_ref, wout_ref, o_ref, tail_ref):
    @pl.when(pl.program_id(1) == 0)
    def _():
        tail_ref[...] = jnp.zeros_like(tail_ref)

    x = h_ref[...]
    tm = x.shape[0]
    n = _bf(_rms(x, EPS) * g_ref[...])
    rowi = _iota((tm, 1), 0)
    acc = jnp.zeros((tm, D_MODEL), F32)
    for c0 in range(0, D_FF, FFN_COL_CHUNK):
        c1 = c0 + FFN_COL_CHUNK
        a = jnp.dot(n, win_ref[:, c0:c1], preferred_element_type=F32)
        u = jnp.dot(n, win_ref[:, D_FF + c0:D_FF + c1], preferred_element_type=F32)
        prev1 = tail_ref[7:8, c0:c1]
        prev2 = tail_ref[6:7, c0:c1]
        a1 = jnp.where(rowi == 0, prev1, pltpu.roll(a, 1, 0))
        a2 = jnp.where(rowi == 0, prev2, jnp.where(rowi == 1, prev1, pltpu.roll(a, 2, 0)))
        tail_ref[:, c0:c1] = a[tm - 8:tm, :]
        conv = cb_ref[:, c0:c1] + cw_ref[0:1, c0:c1] * a2 + cw_ref[1:2, c0:c1] * a1 + cw_ref[2:3, c0:c1] * a
        acc = acc + _dot(conv * jax.nn.sigmoid(conv) * u, wout_ref[c0:c1, :])
    o_ref[...] = x + acc


def _ffn(h, g, w_in, conv_w, conv_b, w_out, layer, lp):
    t = h.shape[0]
    tm = ROW_TILE
    nb = lp // tm
    row = lambda b, i: (b * nb + i, 0)
    return pl.pallas_call(
        _ffn_kernel,
        grid=(t // lp, nb),
        in_specs=[pl.BlockSpec((tm, D_MODEL), row), _const_spec((1, D_MODEL)), _layer_spec(w_in, layer),
                  _const_spec(conv_w.shape), _const_spec(conv_b.shape), _layer_spec(w_out, layer)],
        out_specs=pl.BlockSpec((tm, D_MODEL), row),
        out_shape=jax.ShapeDtypeStruct((t, D_MODEL), F32),
        scratch_shapes=[pltpu.VMEM((8, D_FF), F32)],
        compiler_params=_params(("arbitrary", "arbitrary")),
        name="conv_ffn",
    )(h, g, w_in, conv_w, conv_b, w_out)


def _final_norm_kernel(h_ref, g_ref, o_ref):
    o_ref[0] = _rms(h_ref[...], EPS) * g_ref[...]


def _final_norm(h, g, batch, seq):
    lp = h.shape[0] // batch
    rows = FINAL_ROWS
    assert seq % rows == 0
    first = lambda bi, i: (pl.multiple_of(bi * lp + (PAD + N_META) + i * rows, BLOCK), 0)
    return pl.pallas_call(
        _final_norm_kernel,
        grid=(batch, seq // rows),
        in_specs=[pl.BlockSpec((pl.Element(rows), pl.Element(D_MODEL)), first), _const_spec((1, D_MODEL))],
        out_specs=pl.BlockSpec((1, rows, D_MODEL), lambda bi, i: (bi, i, 0)),
        out_shape=jax.ShapeDtypeStruct((batch, seq, D_MODEL), F32),
        compiler_params=_params(("parallel", "parallel")),
        name="final_norm",
    )(h, g)


def _rope_swap(w):
    half = w.shape[-1] // 2
    return jnp.concatenate([-w[..., half:], w[..., :half]], axis=-1)


def _layout_params(w_in, mla_w_uq, mla_w_ukv, rw_w2, rw_a2, gla_a2):
    kr = w_in[..., 384:416]
    w_mla = jnp.concatenate([w_in[..., 0:384], jnp.tile(kr, (1, 1, 4)), jnp.tile(_rope_swap(kr), (1, 1, 4))], axis=-1)
    zg = w_in[..., 2208:2992]
    w_gla = jnp.concatenate([zg[..., 0:512], zg[..., 528:784], zg[..., 512:528],
                             jnp.zeros(zg.shape[:-1] + (W_GLA - 784,), zg.dtype)], axis=-1)
    sb_q = w_in[..., 1440:1440 + SB_W] * (SB_HEAD ** -0.5 * LOG2_E)
    w_all = _bf(jnp.concatenate([w_mla, w_in[..., 416:1440], sb_q, w_in[..., 1440 + SB_W:2208], w_gla,
                                 w_in[..., 2992:]], axis=-1))

    depth = w_in.shape[0]
    wuq = mla_w_uq.reshape(depth, MLA_Q_RANK, N_HEADS, MLA_NOPE + MLA_ROPE)
    nope, rope = wuq[..., :MLA_NOPE], wuq[..., MLA_NOPE:]
    rope_sw = _rope_swap(rope)
    zeros64 = jnp.zeros((depth, MLA_Q_RANK, 64), w_in.dtype)
    pair = lambda x, a, b: jnp.concatenate([x[:, :, a], x[:, :, b]], axis=-1)
    rope_pair = lambda x, a, b: jnp.concatenate([x[:, :, a], x[:, :, b], zeros64], axis=-1)
    wq = _bf(jnp.concatenate([pair(nope, 0, 1), rope_pair(rope, 0, 1), pair(nope, 2, 3), rope_pair(rope, 2, 3),
                              rope_pair(rope_sw, 0, 1), rope_pair(rope_sw, 2, 3)], axis=-1))
    wukv = mla_w_ukv.reshape(depth, MLA_KV_RANK, N_HEADS, 128)
    wkv = _bf(jnp.concatenate([wukv[..., :64].reshape(depth, MLA_KV_RANK, 256),
                               wukv[..., 64:].reshape(depth, MLA_KV_RANK, 256)], axis=-1))

    z64 = jnp.zeros_like(rw_w2)
    w2 = _bf(jnp.concatenate([rw_w2, z64], axis=1))
    a2 = _bf(jnp.concatenate([z64, rw_a2], axis=1))
    gla_a2p = _bf(jnp.concatenate([gla_a2, jnp.zeros((depth, 128 - gla_a2.shape[1], GLA_QK), gla_a2.dtype)], axis=1))
    return w_all, wq, wkv, w2, a2, gla_a2p


def _rope_tables(lp):
    half = MLA_ROPE // 2
    freqs = ROPE_THETA ** (-jnp.arange(half, dtype=F32) / half)
    pos = (jnp.arange(lp) - PAD).astype(F32)
    ang = pos[:, None] * freqs[None, :]
    return jnp.tile(jnp.cos(ang), (1, 128 // half)), jnp.tile(jnp.sin(ang), (1, 128 // half))


def kernel(x, meta_tokens, norm_mix, w_in, mla_q_norm, mla_w_uq, mla_kv_norm, mla_w_ukv, rw_mu, rw_w0, rw_w2, rw_a0, rw_a2, rw_g2, rw_k_k, rw_k_a, rw_r_k, rw_ln_w, rw_ln_b, gla_a2, gla_a_b, gla_norm, gate_b, w_branch, w_out, norm_ffn, w_ffn_in, ffn_conv_w, ffn_conv_b, w_ffn_out, norm_final):
    batch, seq, _ = x.shape
    depth = w_in.shape[0]
    lp = PAD + N_META + seq
    t = batch * lp
    assert lp % ROW_TILE == 0 and lp % ATTN_TILE == 0 and lp % GLA_CHUNK == 0 and lp % RW_CHUNK == 0

    w_all, wq, wkv, rw_w2p, rw_a2p, gla_a2p = _layout_params(w_in, mla_w_uq, mla_w_ukv, rw_w2, rw_a2, gla_a2)
    w_branch_b, w_out_b, w_ffn_in_b, w_ffn_out_b, rw_g2_b = map(_bf, (w_branch, w_out, w_ffn_in, w_ffn_out, rw_g2))
    vec = lambda a: a.reshape(depth, 1, -1)
    cos, sin = _rope_tables(lp)
    idx = jnp.arange(SB_SUFFIX_SPLIT)
    later_mat = jnp.where(idx[:, None] > idx[None, :], 1.0, 0.0).astype(BF16)

    meta = jnp.broadcast_to(meta_tokens[None].astype(x.dtype), (batch, N_META, D_MODEL))
    h = jnp.concatenate([jnp.zeros((batch, PAD, D_MODEL), x.dtype), meta, x], axis=1).reshape(t, D_MODEL)

    for i in range(depth):
        z_mla, z_rw, z_sb, z_gla, z_gate, sb_values = _in_proj(h, vec(norm_mix)[i], w_all, i)
        q, k, v = _mla_prep(z_mla, vec(mla_q_norm)[i], vec(mla_kv_norm)[i], wq, wkv, i, cos, sin, lp)
        y_mla = _mla_attn(q, k, v, lp)
        rw = {"rw_mu": vec(rw_mu)[i], "rw_w0": vec(rw_w0)[i], "rw_w2": rw_w2p[i], "rw_a0": vec(rw_a0)[i],
              "rw_a2": rw_a2p[i], "rw_g2": rw_g2_b[i], "rw_k_k": vec(rw_k_k)[i], "rw_k_a": vec(rw_k_a)[i],
              "rw_r_k": vec(rw_r_k)[i]}
        *chunk_terms, g, bonus = _rw_prep(z_rw, rw, lp)
        y_rw = _rw_scan(*chunk_terms, batch)
        y_sb = _sb_attn(z_sb, sb_values, later_mat, lp)
        o_gla = _gla(z_gla, gla_a2p[i], vec(gla_a_b)[i], batch)
        mp = {"rw_ln_w": vec(rw_ln_w)[i], "rw_ln_b": vec(rw_ln_b)[i], "gla_norm": vec(gla_norm)[i],
              "gate_b": gate_b[i], "w_branch": w_branch_b, "w_out": w_out_b, "layer": i}
        h = _merge(h, y_mla, y_rw, bonus, g, y_sb, o_gla, z_gla, z_gate, mp, lp)
        h = _ffn(h, vec(norm_ffn)[i], w_ffn_in_b, ffn_conv_w[i], vec(ffn_conv_b)[i], w_ffn_out_b, i, lp)
    return _final_norm(h, norm_final.reshape(1, D_MODEL), batch, seq)
```

```python
import functools

import jax
import jax.numpy as jnp
from jax import lax
from jax.experimental import pallas as pl
from jax.experimental.pallas import tpu as pltpu

F32 = jnp.float32
BF16 = jnp.bfloat16

D_MODEL = 1024
DEPTH = 4
N_META = 16
BLOCK = 128
PAD = (-N_META) % BLOCK
EPS = 1e-6
NEG_INF = -1e30
LOG2_E = 1.4426950408889634

N_HEADS = 4
MLA_NOPE = 64
MLA_ROPE = 32
MLA_Q_RANK = 256
MLA_KV_RANK = 128
ROPE_THETA = 10000.0

RW_W = 256
RW_GN_EPS = 64e-5
RW_CHUNK = 64

SB_W = 256
SB_HEAD = 64
SB_SUFFIX_SPLIT = 256

GLA_DK = 32
GLA_QK = 128
GLA_W = 256
GLA_TAU = 16.0
GLA_CHUNK = 128

D_FF = 2816
FFN_COL_CHUNK = 1408

W_MLA, W_RW, W_SB, W_GLA, W_GATE = 640, 1024, 768, 896, 4096
IN_GROUP_WIDTHS = (W_MLA, W_RW, W_SB, W_GLA, W_GATE)
IN_GROUP_DTYPES = (BF16, F32, BF16, F32, BF16)

ROW_TILE = 384
ATTN_TILE = 384
FINAL_ROWS = 1024
VMEM_LIMIT = 56 * 1024 * 1024


def _bf(x):
    return x.astype(BF16)


def _dot(a, b):
    return jnp.dot(_bf(a), _bf(b), preferred_element_type=F32)


def _dot_nt(a, b):
    return lax.dot_general(_bf(a), _bf(b), (((1,), (1,)), ((), ())), preferred_element_type=F32)


def _dot_tn(a, b):
    return lax.dot_general(_bf(a), _bf(b), (((0,), (0,)), ((), ())), preferred_element_type=F32)


def _split_hi_lo(x):
    hi = _bf(x)
    lo = _bf(x - hi.astype(F32))
    return hi, lo


def _dot_exact_rhs(x, m):
    hi, lo = _split_hi_lo(x)
    return jnp.dot(hi, m, preferred_element_type=F32) + jnp.dot(lo, m, preferred_element_type=F32)


def _dot_exact_lhs(m, x):
    hi, lo = _split_hi_lo(x)
    return jnp.dot(m, hi, preferred_element_type=F32) + jnp.dot(m, lo, preferred_element_type=F32)


def _iota(shape, dim):
    return lax.broadcasted_iota(jnp.int32, shape, dim)


def _div_pow2(x, d):
    assert d & (d - 1) == 0
    return lax.shift_right_logical(x, d.bit_length() - 1)


def _same_segment(shape, row_seg, col_seg):
    return _div_pow2(_iota(shape, 0), row_seg) == _div_pow2(_iota(shape, 1), col_seg)


def _segment_matrix(n, seg, value):
    return jnp.where(_same_segment((n, n), seg, seg), value, 0.0).astype(BF16)


def _head_lane_mask(width, head_width, h):
    lane = _iota((1, width), 1)
    return (lane >= h * head_width) & (lane < (h + 1) * head_width)


def _stack_heads(x, head_width):
    w = x.shape[1]
    return jnp.concatenate(
        [jnp.where(_head_lane_mask(w, head_width, h), x, 0.0) for h in range(N_HEADS)], axis=0)


def _unstack_heads(xs, c):
    return xs[0:c] + xs[c:2 * c] + xs[2 * c:3 * c] + xs[3 * c:4 * c]


def _rms(x, eps):
    return x * lax.rsqrt(jnp.mean(x * x, axis=-1, keepdims=True) + eps)


def _const_spec(shape):
    nd = len(shape)
    return pl.BlockSpec(shape, lambda *_: (0,) * nd)


def _layer_spec(stacked, layer):
    nd = stacked.ndim - 1
    return pl.BlockSpec((None,) + stacked.shape[1:], lambda *_: (layer,) + (0,) * nd)


def _params(sem, vmem=VMEM_LIMIT):
    return pltpu.CompilerParams(dimension_semantics=sem, vmem_limit_bytes=vmem)


def _embed_kernel(x_ref, meta_ref, o_ref):
    first_real = PAD + N_META

    @pl.when(pl.program_id(1) == 0)
    def _():
        o_ref[0:PAD, :] = jnp.zeros((PAD, D_MODEL), o_ref.dtype)
        o_ref[PAD:first_real, :] = meta_ref[...]
        o_ref[first_real:, :] = x_ref[0:o_ref.shape[0] - first_real, :]

    @pl.when(pl.program_id(1) > 0)
    def _():
        o_ref[...] = x_ref[...]


def _embed(x2, meta, batch):
    seq = x2.shape[0] // batch
    first_real = PAD + N_META
    lp = first_real + seq
    tm = ROW_TILE
    window = lambda b, i: (pl.multiple_of(b * seq + jnp.maximum(i * tm - first_real, 0), BLOCK), 0)
    return pl.pallas_call(
        _embed_kernel,
        grid=(batch, lp // tm),
        in_specs=[pl.BlockSpec((pl.Element(tm), pl.Element(D_MODEL)), window), _const_spec(meta.shape)],
        out_specs=pl.BlockSpec((tm, D_MODEL), lambda b, i: (b * (lp // tm) + i, 0)),
        out_shape=jax.ShapeDtypeStruct((batch * lp, D_MODEL), x2.dtype),
        compiler_params=_params(("parallel", "parallel")),
        name="embed",
    )(x2, meta)


def _in_proj_kernel(h_ref, g_ref, w_ref, *out_refs):
    *group_refs, sb_values_ref = out_refs
    n = _bf(_rms(h_ref[...], EPS) * g_ref[...])
    off = 0
    for o_ref, width in zip(group_refs, IN_GROUP_WIDTHS):
        for c0 in range(0, width, 1024):
            c1 = min(c0 + 1024, width)
            o_ref[:, c0:c1] = jnp.dot(
                n, w_ref[:, off + c0:off + c1], preferred_element_type=F32).astype(o_ref.dtype)
        off += width
    sb_values_ref[0] = _head_stacked_values(group_refs[2][:, 2 * SB_W:3 * SB_W])


def _in_proj(h, g, w_all, layer):
    t = h.shape[0]
    tm = ROW_TILE
    return pl.pallas_call(
        _in_proj_kernel,
        grid=(t // tm,),
        in_specs=[pl.BlockSpec((tm, D_MODEL), lambda i: (i, 0)),
                  _const_spec((1, D_MODEL)),
                  _layer_spec(w_all, layer)],
        out_specs=[pl.BlockSpec((tm, w), lambda i: (i, 0)) for w in IN_GROUP_WIDTHS]
        + [pl.BlockSpec((1, N_HEADS * tm, SB_W), lambda i: (i, 0, 0))],
        out_shape=[jax.ShapeDtypeStruct((t, w), dt) for w, dt in zip(IN_GROUP_WIDTHS, IN_GROUP_DTYPES)]
        + [jax.ShapeDtypeStruct((t // tm, N_HEADS * tm, SB_W), BF16)],
        compiler_params=_params(("parallel",)),
        name="in_proj",
    )(h, g, w_all)


def _mla_prep_kernel(z_ref, qg_ref, kvg_ref, wq_ref, wkv_ref, cos_ref, sin_ref, q_ref, k_ref, vt_ref):
    z = z_ref[...].astype(F32)
    nq = _rms(z[:, 0:256], EPS) * qg_ref[...]
    nkv = _rms(z[:, 256:384], EPS) * kvg_ref[...]
    cos = cos_ref[...]
    sin = sin_ref[...]
    ql = _dot(nq, wq_ref[...])
    scale = (MLA_NOPE + MLA_ROPE) ** -0.5
    q = jnp.concatenate([ql[:, 0:128], ql[:, 128:256] * cos + ql[:, 512:640] * sin,
                         ql[:, 256:384], ql[:, 384:512] * cos + ql[:, 640:768] * sin], axis=1)
    q_ref[...] = _bf(q * scale)
    kvl = _dot(nkv, wkv_ref[...])
    k_rope = z[:, 384:512] * cos + z[:, 512:640] * sin
    k_rope = jnp.where(_iota((1, 128), 1) < 2 * MLA_ROPE, k_rope, 0.0)
    k_ref[...] = _bf(jnp.concatenate([kvl[:, 0:128], k_rope, kvl[:, 128:256], k_rope], axis=1))
    vt = _bf(kvl[:, 256:512].T)
    row_head = _div_pow2(_iota((256, 1), 0), 64)
    vt_ref[0] = jnp.concatenate([jnp.where(row_head == h, vt, jnp.zeros((), BF16)) for h in range(N_HEADS)], axis=1)


def _mla_prep(z_mla, qg, kvg, wq, wkv, layer, cos, sin, lp):
    t = z_mla.shape[0]
    tm = ROW_TILE
    nb = lp // tm
    row = lambda b, i: (b * nb + i, 0)
    return pl.pallas_call(
        _mla_prep_kernel,
        grid=(t // lp, nb),
        in_specs=[pl.BlockSpec((tm, W_MLA), row),
                  _const_spec((1, MLA_Q_RANK)), _const_spec((1, MLA_KV_RANK)),
                  _layer_spec(wq, layer), _layer_spec(wkv, layer),
                  pl.BlockSpec((tm, 128), lambda b, i: (i, 0)),
                  pl.BlockSpec((tm, 128), lambda b, i: (i, 0))],
        out_specs=[pl.BlockSpec((tm, 512), row), pl.BlockSpec((tm, 512), row),
                   pl.BlockSpec((1, 256, N_HEADS * tm), lambda b, i: (b * nb + i, 0, 0))],
        out_shape=[jax.ShapeDtypeStruct((t, 512), BF16), jax.ShapeDtypeStruct((t, 512), BF16),
                   jax.ShapeDtypeStruct((t // tm, 256, N_HEADS * tm), BF16)],
        compiler_params=_params(("parallel", "parallel")),
        name="mla_prep",
    )(z_mla, qg, kvg, wq, wkv, cos, sin)


def _per_head_lanes(cols):
    lane = _iota((1, 256), 1)
    return jnp.where(lane < 64, cols[0], jnp.where(lane < 128, cols[1], jnp.where(lane < 192, cols[2], cols[3])))


def _head_stacked_values(vb):
    zero = jnp.zeros((), vb.dtype)
    return jnp.concatenate([jnp.where(_head_lane_mask(256, 64, h), vb, zero) for h in range(N_HEADS)], axis=0)


def _per_head_rows(rows, n):
    return jnp.concatenate([jnp.broadcast_to(r, (64, n)) for r in rows], axis=0)


def _mla_attn_kernel(q_ref, k_ref, vt_ref, o_ref, qh_ref, m_ref, l_ref, acc_ref, p_ref, alpha_ref, *, tile):
    i = pl.program_id(1)
    q_pos = i * tile + _iota((1, tile), 1)
    lane = _iota((1, 256), 1)
    for h in range(N_HEADS):
        half, slot = h // 2, h % 2
        head_lanes = (((lane >= slot * MLA_NOPE) & (lane < (slot + 1) * MLA_NOPE))
                      | ((lane >= 128 + slot * MLA_ROPE) & (lane < 128 + (slot + 1) * MLA_ROPE)))
        qh_ref[half, slot * tile:(slot + 1) * tile, :] = jnp.where(
            head_lanes, q_ref[:, half * 256:(half + 1) * 256], jnp.zeros((), BF16))
    m_ref[...] = jnp.full(m_ref.shape, NEG_INF, F32)
    l_ref[...] = jnp.zeros(l_ref.shape, F32)
    acc_ref[...] = jnp.zeros(acc_ref.shape, F32)

    def add_values(j, slot):
        pv = jnp.dot(vt_ref[j], p_ref[slot], preferred_element_type=F32)
        acc_ref[...] = _per_head_rows([alpha_ref[slot, h] for h in range(N_HEADS)], tile) * acc_ref[...] + pv

    def block(j, masked, prev, slot):
        start = pl.multiple_of(j * tile, tile)
        heads = range(N_HEADS)
        s_pair = [_dot_nt(k_ref[pl.ds(start, tile), half * 256:(half + 1) * 256], qh_ref[half]) for half in (0, 1)]
        s = [s_pair[h // 2][:, (h % 2) * tile:(h % 2 + 1) * tile] for h in heads]
        if prev is not None:
            add_values(prev, 1 - slot)
        for h in heads:
            s_h = s[h]
            if masked:
                k_pos = start + _iota((tile, 1), 0)
                s_h = jnp.where((k_pos <= q_pos) & (k_pos >= PAD), s_h, NEG_INF)
            m_old = m_ref[h]
            m_new = jnp.maximum(m_old, jnp.max(s_h, axis=0, keepdims=True))
            p = jnp.exp(s_h - m_new)
            alpha = jnp.exp(m_old - m_new)
            l_ref[h] = alpha * l_ref[h] + jnp.sum(p, axis=0, keepdims=True)
            m_ref[h] = m_new
            p_ref[slot, h * tile:(h + 1) * tile, :] = _bf(p)
            alpha_ref[slot, h] = alpha

    block(i, True, None, 0)

    @pl.when(i > 0)
    def _():
        block(0, True, i, 1)

    interior_blocks = jnp.maximum(i - 1, 0)

    def interior_pair(t, carry):
        j = 1 + 2 * t
        block(j, False, j - 1, 0)
        block(j + 1, False, j, 1)
        return carry

    lax.fori_loop(0, lax.shift_right_logical(interior_blocks, 1), interior_pair, 0)
    odd = (interior_blocks & 1) == 1

    @pl.when(odd)
    def _():
        block(i - 1, False, i - 2, 0)

    last = jnp.maximum(i - 1, 0)
    last_in_slot0 = (i == 0) | odd

    @pl.when(last_in_slot0)
    def _():
        add_values(last, 0)

    @pl.when(jnp.logical_not(last_in_slot0))
    def _():
        add_values(last, 1)
    o_ref[...] = _bf((acc_ref[...] / _per_head_rows([l_ref[h] for h in range(N_HEADS)], tile)).T)


def _mla_attn(q, k, vt, lp):
    t = q.shape[0]
    tile = ATTN_TILE
    nb = lp // tile
    return pl.pallas_call(
        functools.partial(_mla_attn_kernel, tile=tile),
        grid=(t // lp, nb),
        in_specs=[pl.BlockSpec((tile, 512), lambda b, i: (b * nb + i, 0)),
                  pl.BlockSpec((lp, 512), lambda b, i: (b, 0)),
                  pl.BlockSpec((nb, 256, N_HEADS * tile), lambda b, i: (b, 0, 0))],
        out_specs=pl.BlockSpec((tile, 256), lambda b, i: (b * nb + i, 0)),
        out_shape=jax.ShapeDtypeStruct((t, 256), BF16),
        scratch_shapes=[pltpu.VMEM((2, 2 * tile, 256), BF16), pltpu.VMEM((N_HEADS, 1, tile), F32),
                        pltpu.VMEM((N_HEADS, 1, tile), F32), pltpu.VMEM((256, tile), F32),
                        pltpu.VMEM((2, N_HEADS * tile, tile), BF16), pltpu.VMEM((2, N_HEADS, 1, tile), F32)],
        compiler_params=_params(("parallel", "arbitrary")),
        name="mla_attn",
    )(q, k, vt)


def _sb_attn_kernel(q_ref, k_ref, v_ref, later_ref, o_ref, qh_ref, c_ref, acc_ref, wts_ref, *, tile):
    i = pl.program_id(1)
    row = i * tile + _iota((tile, 1), 0)
    for h in range(N_HEADS):
        qh_ref[h] = jnp.where(_head_lane_mask(SB_W, SB_HEAD, h), q_ref[...], jnp.zeros((), BF16))
    c_ref[...] = jnp.zeros(c_ref.shape, F32)
    acc_ref[...] = jnp.zeros(acc_ref.shape, F32)

    def add_values(j, slot):
        acc_ref[...] += jnp.dot(wts_ref[slot], v_ref[j], preferred_element_type=F32)

    def block(j, masked, prev, slot):
        start = pl.multiple_of(j * tile, tile)
        heads = range(N_HEADS)
        cut = SB_SUFFIX_SPLIT
        z = [_dot_nt(qh_ref[h], k_ref[pl.ds(start, tile), :]) for h in heads]
        if prev is not None:
            add_values(prev, 1 - slot)
        log_take = [jnp.minimum(z[h], 0.0) - jnp.log2(1.0 + jnp.exp2(-jnp.abs(z[h]))) for h in heads]
        log_keep = [log_take[h] - z[h] for h in heads]
        if masked:
            mask = (start + _iota((1, tile), 1)) < row
            log_keep = [jnp.where(mask, log_keep[h], 0.0) for h in heads]
        keep16 = [_bf(log_keep[h]) for h in heads]
        later_head = [jnp.dot(keep16[h][:, :cut], later_ref[...], preferred_element_type=F32) for h in heads]
        later_tail = [jnp.dot(keep16[h][:, cut:], later_ref[:tile - cut, :tile - cut], preferred_element_type=F32)
                      for h in heads]
        for h in heads:
            head_sum = jnp.sum(log_keep[h][:, :cut], axis=-1, keepdims=True)
            tail_sum = jnp.sum(log_keep[h][:, cut:], axis=-1, keepdims=True)
            c = c_ref[h]
            later = jnp.concatenate([later_head[h] + tail_sum, later_tail[h]], axis=1)
            w = jnp.exp2(log_take[h] + later + c)
            if masked:
                w = jnp.where(mask, w, 0.0)
            c_ref[h] = c + (head_sum + tail_sum)
            wts_ref[slot, :, h * tile:(h + 1) * tile] = _bf(w)

    block(i, True, None, 0)

    def below_pair(t, carry):
        j = i - 1 - 2 * t
        block(j, False, j + 1, 1)
        block(j - 1, False, j, 0)
        return carry

    lax.fori_loop(0, lax.shift_right_logical(i, 1), below_pair, 0)
    odd = (i & 1) == 1

    @pl.when(odd)
    def _():
        block(0, False, 1, 1)
        add_values(0, 1)

    @pl.when(jnp.logical_not(odd))
    def _():
        add_values(0, 0)
    o_ref[...] = _bf(acc_ref[...])


def _sb_attn(z_sb, sb_values, later_mat, lp):
    t = z_sb.shape[0]
    tile = ATTN_TILE
    nb = lp // tile
    return pl.pallas_call(
        functools.partial(_sb_attn_kernel, tile=tile),
        grid=(t // lp, nb),
        in_specs=[pl.BlockSpec((tile, SB_W), lambda b, i: (b * nb + i, 0)),
                  pl.BlockSpec((lp, SB_W), lambda b, i: (b, 1)),
                  pl.BlockSpec((nb, N_HEADS * tile, SB_W), lambda b, i: (b, 0, 0)),
                  _const_spec((SB_SUFFIX_SPLIT, SB_SUFFIX_SPLIT))],
        out_specs=pl.BlockSpec((tile, SB_W), lambda b, i: (b * nb + i, 0)),
        out_shape=jax.ShapeDtypeStruct((t, SB_W), BF16),
        scratch_shapes=[pltpu.VMEM((N_HEADS, tile, SB_W), BF16), pltpu.VMEM((N_HEADS, tile, 1), F32),
                        pltpu.VMEM((tile, SB_W), F32), pltpu.VMEM((2, tile, N_HEADS * tile), BF16)],
        compiler_params=_params(("parallel", "arbitrary")),
        name="sb_attn",
    )(z_sb, z_sb, sb_values, later_mat)


def _compact_eye(c):
    return jnp.where(_iota((c, N_HEADS * c), 0) == (_iota((c, N_HEADS * c), 1) & (c - 1)), 1.0, 0.0)


def _rw_prep_kernel(z_ref, zprev_ref, mu_ref, w0_ref, w2_ref, a0_ref, a2_ref, g2_ref, kk_ref, ka_ref, rk_ref,
                    w_o, rt_o, arb_o, kbar_o, bbar_o, v_o, uv_o, yv_o, dec_o, g_o, bonus_o):
    i = pl.program_id(1)
    z = z_ref[...]
    tm = z.shape[0]
    prev = jnp.where(i == 0, 0.0, zprev_ref[7:8, :])
    shifted = jnp.where(_iota((tm, 1), 0) == 0, prev, pltpu.roll(z, 1, 0))
    zz = z + (shifted - z) * mu_ref[...]
    r, k, v = zz[:, 0:256], zz[:, 256:512], zz[:, 512:768]
    lora_in = zz[:, 768:896]
    u = w0_ref[...] + _dot(jnp.tanh(lora_in), w2_ref[...])
    w = jnp.minimum(u, 0.0) - jnp.log1p(jnp.exp(-jnp.abs(u))) - 0.5
    a = jax.nn.sigmoid(a0_ref[...] + _dot(lora_in, a2_ref[...]))
    seg = _segment_matrix(RW_W, 64, 1.0)
    kx = k * kk_ref[...]
    kap = kx / jnp.maximum(jnp.sqrt(_dot_exact_rhs(kx * kx, seg)), 1e-12)
    kmod = k * (1.0 + (a - 1.0) * ka_ref[...])
    beta = kap * a
    lw = -jnp.exp(w)
    g_o[...] = _dot(jax.nn.sigmoid(zz[:, 896:1024]), g2_ref[...])
    bonus_o[...] = _dot_exact_rhs(r * kmod * rk_ref[...], seg) * v

    c = RW_CHUNK
    same_chunk = _same_segment((tm, tm), c, c)
    lw_hi, lw_lo = _split_hi_lo(lw)
    cum = jnp.where(same_chunk & (_iota((tm, tm), 0) >= _iota((tm, tm), 1)), 1.0, 0.0).astype(BF16)
    tot = jnp.where(same_chunk, 1.0, 0.0).astype(BF16)
    b = jnp.dot(cum, lw_hi, preferred_element_type=F32) + jnp.dot(cum, lw_lo, preferred_element_type=F32)
    b_end = jnp.dot(tot, lw_hi, preferred_element_type=F32) + jnp.dot(tot, lw_lo, preferred_element_type=F32)
    grow = jnp.exp(-b)
    to_end = jnp.exp(b_end - b)
    kap_t = kap * jnp.exp(b - lw)
    r_t = r * jnp.exp(b)
    beta_g = beta * grow
    k_g = kmod * grow
    rt_o[...] = _bf(r_t)
    kbar_o[...] = _bf(kmod * to_end)
    bbar_o[...] = _bf(beta * to_end)
    v_o[...] = _bf(v)
    decay_end = jnp.exp(b_end)

    lane_pos = _iota((c, N_HEADS * c), 1) & (c - 1)
    strictly_earlier = _iota((c, N_HEADS * c), 0) > lane_pos
    not_later = _iota((c, N_HEADS * c), 0) >= lane_pos
    chunks = [slice(n * c, (n + 1) * c) for n in range(tm // c)]
    pair = [_dot_nt(jnp.concatenate([kap_t[rows], r_t[rows]], axis=0),
                    jnp.concatenate([_stack_heads(beta_g[rows], 64), _stack_heads(k_g[rows], 64)], axis=0))
            for rows in chunks]
    a_kb = [jnp.where(strictly_earlier, p[0:c, 0:4 * c], 0.0) for p in pair]
    a_kk = [jnp.where(strictly_earlier, p[0:c, 4 * c:8 * c], 0.0) for p in pair]
    a_rk = [jnp.where(not_later, p[c:2 * c, 4 * c:8 * c], 0.0) for p in pair]
    for rows, p in zip(chunks, pair):
        arb_o[rows, :] = _bf(jnp.where(not_later, p[c:2 * c, 0:4 * c], 0.0))
    inv = [_compact_eye(c) - a for a in a_kb]
    power = [_dot(a, _stack_heads(a, c)) for a in a_kb]
    span = 2
    while span < c:
        inv = [t + _dot(t, _stack_heads(p, c)) for t, p in zip(inv, power)]
        span *= 2
        if span < c:
            power = [_dot(p, _stack_heads(p, c)) for p in power]
    v_st = [_stack_heads(v[rows], 64) for rows in chunks]
    kk_v = [_dot(a, vs) for a, vs in zip(a_kk, v_st)]
    for n, rows in enumerate(chunks):
        w_o[rows, :] = _bf(_dot(inv[n], _stack_heads(kap_t[rows], 64)))
        uv_o[rows, :] = _dot(inv[n], _stack_heads(kk_v[n], 64))
        yv_o[rows, :] = _dot(a_rk[n], v_st[n])
        dec_o[n * 8:(n + 1) * 8, :] = decay_end[n * c:n * c + 8]


def _rw_prep(z_rw, p, lp):
    t = z_rw.shape[0]
    tm = ROW_TILE
    nb = lp // tm
    row = lambda b, i: (b * nb + i, 0)
    prev = lambda b, i: (jnp.maximum(b * (lp // 8) + i * (tm // 8) - 1, 0), 0)
    consts = [p["rw_mu"], p["rw_w0"], p["rw_w2"], p["rw_a0"], p["rw_a2"], p["rw_g2"],
              p["rw_k_k"], p["rw_k_a"], p["rw_r_k"]]
    return pl.pallas_call(
        _rw_prep_kernel,
        grid=(t // lp, nb),
        in_specs=[pl.BlockSpec((tm, W_RW), row), pl.BlockSpec((8, W_RW), prev)]
        + [_const_spec(c.shape) for c in consts],
        out_specs=[pl.BlockSpec((tm, RW_W), row)] * 8
        + [pl.BlockSpec((tm // RW_CHUNK * 8, RW_W), row)] + [pl.BlockSpec((tm, RW_W), row)] * 2,
        out_shape=[jax.ShapeDtypeStruct((t, RW_W), BF16)] * 6 + [jax.ShapeDtypeStruct((t, RW_W), F32)] * 2
        + [jax.ShapeDtypeStruct((t // RW_CHUNK * 8, RW_W), F32)] + [jax.ShapeDtypeStruct((t, RW_W), F32)] * 2,
        compiler_params=_params(("parallel", "parallel")),
        name="rw_prep",
    )(z_rw, z_rw, *consts)


def _rw_scan_kernel(w_ref, rt_ref, arb_ref, kbar_ref, bbar_ref, v_ref, uv_ref, yv_ref, dec_ref, y_ref, ht_ref):
    c = RW_CHUNK

    @pl.when(pl.program_id(0) == 0)
    def _():
        ht_ref[...] = jnp.zeros_like(ht_ref)

    same_head = _same_segment((RW_W, RW_W), 64, 64)
    seqs = range(ht_ref.shape[0])
    ht = [ht_ref[s] for s in seqs]
    from_state = [_dot_nt(jnp.concatenate([w_ref[s], rt_ref[s]], axis=0), ht[s]) for s in seqs]
    u = [from_state[s][0:c] + uv_ref[s] for s in seqs]
    from_u = [_dot(arb_ref[s], _stack_heads(u[s], 64)) for s in seqs]
    upd = [_dot_tn(jnp.concatenate([v_ref[s], _bf(u[s])], axis=0),
                   jnp.concatenate([kbar_ref[s], -bbar_ref[s]], axis=0)) for s in seqs]
    for s in seqs:
        y_ref[s] = from_state[s][c:2 * c] + yv_ref[s] - from_u[s]
        ht_ref[s] = ht[s] * dec_ref[s, 0:1, :] + jnp.where(same_head, upd[s], 0.0)


def _rw_scan(w, rt, arb, kbar, bbar, v, uv, yv, dec, batch):
    t = w.shape[0]
    lp = t // batch
    c = RW_CHUNK
    as_seq = lambda a: a.reshape(batch, a.shape[0] // batch, RW_W)
    spec = pl.BlockSpec((batch, c, RW_W), lambda i: (0, i, 0))
    y = pl.pallas_call(
        _rw_scan_kernel,
        grid=(lp // c,),
        in_specs=[spec] * 8 + [pl.BlockSpec((batch, 8, RW_W), lambda i: (0, i, 0))],
        out_specs=spec,
        out_shape=jax.ShapeDtypeStruct((batch, lp, RW_W), F32),
        scratch_shapes=[pltpu.VMEM((batch, RW_W, RW_W), F32)],
        compiler_params=_params(("arbitrary",)),
        name="rw_scan",
    )(*map(as_seq, (w, rt, arb, kbar, bbar, v, uv, yv, dec)))
    return y.reshape(t, RW_W)


def _gla_kernel(q_ref, k_ref, v_ref, al_ref, a2_ref, ab_ref, o_ref, st_ref):
    c = GLA_CHUNK
    n = N_HEADS * c
    seqs = range(st_ref.shape[0])

    @pl.when(pl.program_id(0) == 0)
    def _():
        st_ref[...] = jnp.zeros_like(st_ref)

    tri = jnp.where(_iota((c, c), 0) >= _iota((c, c), 1), 1.0, 0.0).astype(BF16)
    causal = (_iota((n, c), 0) & (c - 1)) >= _iota((n, c), 1)
    same_head = _same_segment((GLA_W, GLA_QK), 64, GLA_DK)
    x = [_dot(al_ref[s], a2_ref[...]) + ab_ref[...] for s in seqs]
    log_a = [(jnp.minimum(x[s], 0.0) - jnp.log1p(jnp.exp(-jnp.abs(x[s])))) * (1.0 / GLA_TAU) for s in seqs]
    b = [_dot_exact_lhs(tri, log_a[s]) for s in seqs]
    q = [q_ref[s] * (GLA_DK ** -0.5) for s in seqs]
    st = [st_ref[s] for s in seqs]
    inter = [_dot_nt(q[s] * jnp.exp(b[s]), st[s]) for s in seqs]
    scores = []
    for s in seqs:
        b_mid = b[s][c // 2 - 1:c // 2, :]
        sc = _dot_nt(_stack_heads(q[s] * jnp.exp(b[s] - b_mid), GLA_DK), k_ref[s] * jnp.exp(b_mid - b[s]))
        scores.append(jnp.where(causal, sc, 0.0))
    per_head = [_dot(scores[s], v_ref[s]) for s in seqs]
    upd = [_dot_tn(v_ref[s], k_ref[s] * jnp.exp(b[s][c - 1:c, :] - b[s])) for s in seqs]
    for s in seqs:
        intra = jnp.zeros((c, GLA_W), F32)
        for h in range(N_HEADS):
            intra = intra + jnp.where(_head_lane_mask(GLA_W, 64, h), per_head[s][h * c:(h + 1) * c], 0.0)
        o_ref[s] = inter[s] + intra
        st_ref[s] = st[s] * jnp.exp(b[s][c - 1:c, :]) + jnp.where(same_head, upd[s], 0.0)


def _gla(z_gla, a2, ab, batch):
    t = z_gla.shape[0]
    lp = t // batch
    c = GLA_CHUNK
    z3 = z_gla.reshape(batch, lp, W_GLA)
    o = pl.pallas_call(
        _gla_kernel,
        grid=(lp // c,),
        in_specs=[pl.BlockSpec((batch, c, 128), lambda i: (0, i, 0)),
                  pl.BlockSpec((batch, c, 128), lambda i: (0, i, 1)),
                  pl.BlockSpec((batch, c, 256), lambda i: (0, i, 1)),
                  pl.BlockSpec((batch, c, 128), lambda i: (0, i, 6)),
                  _const_spec(a2.shape), _const_spec(ab.shape)],
        out_specs=pl.BlockSpec((batch, c, GLA_W), lambda i: (0, i, 0)),
        out_shape=jax.ShapeDtypeStruct((batch, lp, GLA_W), F32),
        scratch_shapes=[pltpu.VMEM((batch, GLA_W, GLA_QK), F32)],
        compiler_params=_params(("arbitrary",)),
        name="gla",
    )(z3, z3, z3, z3, a2, ab)
    return o.reshape(t, GLA_W)


def _merge_kernel(h_ref, ymla_ref, yrw_ref, bonus_ref, g_ref, ysb_ref, ogla_ref, rgla_ref, zg_ref,
                  lnw_ref, lnb_ref, gn_ref, gb_ref, wb_ref, wo_ref, o_ref, *, tiles_per_seq):
    tm = h_ref.shape[0]
    avg = _segment_matrix(256, 64, 1.0 / 64)

    y = yrw_ref[...]
    d = y - _dot_exact_rhs(y, avg)
    var = _dot_exact_rhs(d * d, avg)
    y_rw = (d * lax.rsqrt(var + RW_GN_EPS) * lnw_ref[...] + lnb_ref[...] + bonus_ref[...]) * g_ref[...]

    o = ogla_ref[...]
    r = rgla_ref[...]
    y_gla = o * lax.rsqrt(_dot_exact_rhs(o * o, avg) + EPS) * gn_ref[...] * (r * jax.nn.sigmoid(r))

    acc = jnp.zeros((tm, D_MODEL), F32)
    for m, y_m in enumerate((ymla_ref[...], y_rw, ysb_ref[...], y_gla)):
        gate = jax.nn.sigmoid(zg_ref[:, m * D_MODEL:(m + 1) * D_MODEL].astype(F32) + gb_ref[m:m + 1, :])
        acc = acc + gate * _dot(y_m, wb_ref[m])
    delta = _dot(acc, wo_ref[...])
    row = (pl.program_id(0) % tiles_per_seq) * tm + _iota((tm, 1), 0)
    o_ref[...] = h_ref[...] + jnp.where(row >= PAD, delta, 0.0)


def _merge(h, y_mla, y_rw, bonus, g, y_sb, o_gla, z_gla, z_gate, p, lp):
    t = h.shape[0]
    tm = ROW_TILE
    row = lambda i: (i, 0)
    w256 = pl.BlockSpec((tm, 256), row)
    consts = [p["rw_ln_w"], p["rw_ln_b"], p["gla_norm"], p["gate_b"]]
    return pl.pallas_call(
        functools.partial(_merge_kernel, tiles_per_seq=lp // tm),
        grid=(t // tm,),
        in_specs=[pl.BlockSpec((tm, D_MODEL), row), w256, w256, w256, w256, w256, w256,
                  pl.BlockSpec((tm, 256), lambda i: (i, 2)),
                  pl.BlockSpec((tm, W_GATE), row)]
        + [_const_spec(c.shape) for c in consts]
        + [_layer_spec(p["w_branch"], p["layer"]), _layer_spec(p["w_out"], p["layer"])],
        out_specs=pl.BlockSpec((tm, D_MODEL), row),
        out_shape=jax.ShapeDtypeStruct((t, D_MODEL), F32),
        compiler_params=_params(("parallel",)),
        name="merge",
    )(h, y_mla, y_rw, bonus, g, y_sb, o_gla, z_gla, z_gate, *consts, p["w_branch"], p["w_out"])


def _ffn_kernel(h_ref, g_ref, win_ref, cw_ref, cb_ref, wout_ref, o_ref, tail_ref):
    @pl.when(pl.program_id(1) == 0)
    def _():
        tail_ref[...] = jnp.zeros_like(tail_ref)

    x = h_ref[...]
    tm = x.shape[0]
    n = _bf(_rms(x, EPS) * g_ref[...])
    rowi = _iota((tm, 1), 0)
    acc = jnp.zeros((tm, D_MODEL), F32)
    for c0 in range(0, D_FF, FFN_COL_CHUNK):
        c1 = c0 + FFN_COL_CHUNK
        a = jnp.dot(n, win_ref[:, c0:c1], preferred_element_type=F32)
        u = jnp.dot(n, win_ref[:, D_FF + c0:D_FF + c1], preferred_element_type=F32)
        prev1 = tail_ref[7:8, c0:c1]
        prev2 = tail_ref[6:7, c0:c1]
        a1 = jnp.where(rowi == 0, prev1, pltpu.roll(a, 1, 0))
        a2 = jnp.where(rowi == 0, prev2, jnp.where(rowi == 1, prev1, pltpu.roll(a, 2, 0)))
        tail_ref[:, c0:c1] = a[tm - 8:tm, :]
        conv = cb_ref[:, c0:c1] + cw_ref[0:1, c0:c1] * a2 + cw_ref[1:2, c0:c1] * a1 + cw_ref[2:3, c0:c1] * a
        acc = acc + _dot(conv * jax.nn.sigmoid(conv) * u, wout_ref[c0:c1, :])
    o_ref[...] = x + acc


def _ffn(h, g, w_in, conv_w, conv_b, w_out, layer, lp):
    t = h.shape[0]
    tm = ROW_TILE
    nb = lp // tm
    row = lambda b, i: (b * nb + i, 0)
    return pl.pallas_call(
        _ffn_kernel,
        grid=(t // lp, nb),
        in_specs=[pl.BlockSpec((tm, D_MODEL), row), _const_spec((1, D_MODEL)), _layer_spec(w_in, layer),
                  _const_spec(conv_w.shape), _const_spec(conv_b.shape), _layer_spec(w_out, layer)],
        out_specs=pl.BlockSpec((tm, D_MODEL), row),
        out_shape=jax.ShapeDtypeStruct((t, D_MODEL), F32),
        scratch_shapes=[pltpu.VMEM((8, D_FF), F32)],
        compiler_params=_params(("arbitrary", "arbitrary")),
        name="conv_ffn",
    )(h, g, w_in, conv_w, conv_b, w_out)


def _final_norm_kernel(h_ref, g_ref, o_ref):
    o_ref[0] = _rms(h_ref[...], EPS) * g_ref[...]


def _final_norm(h, g, batch, seq):
    lp = h.shape[0] // batch
    rows = FINAL_ROWS
    assert seq % rows == 0
    first = lambda bi, i: (pl.multiple_of(bi * lp + (PAD + N_META) + i * rows, BLOCK), 0)
    return pl.pallas_call(
        _final_norm_kernel,
        grid=(batch, seq // rows),
        in_specs=[pl.BlockSpec((pl.Element(rows), pl.Element(D_MODEL)), first), _const_spec((1, D_MODEL))],
        out_specs=pl.BlockSpec((1, rows, D_MODEL), lambda bi, i: (bi, i, 0)),
        out_shape=jax.ShapeDtypeStruct((batch, seq, D_MODEL), F32),
        compiler_params=_params(("parallel", "parallel")),
        name="final_norm",
    )(h, g)


def _rope_swap(w):
    half = w.shape[-1] // 2
    return jnp.concatenate([-w[..., half:], w[..., :half]], axis=-1)


def _layout_params(w_in, mla_w_uq, mla_w_ukv, rw_w2, rw_a2, gla_a2):
    kr = w_in[..., 384:416]
    w_mla = jnp.concatenate([w_in[..., 0:384], jnp.tile(kr, (1, 1, 4)), jnp.tile(_rope_swap(kr), (1, 1, 4))], axis=-1)
    zg = w_in[..., 2208:2992]
    w_gla = jnp.concatenate([zg[..., 0:512], zg[..., 528:784], zg[..., 512:528],
                             jnp.zeros(zg.shape[:-1] + (W_GLA - 784,), zg.dtype)], axis=-1)
    w_all = jnp.concatenate([w_mla, w_in[..., 416:1440], w_in[..., 1440:2208], w_gla, w_in[..., 2992:]], axis=-1)
    col = jnp.arange(w_all.shape[-1])
    sb_q_cols = (col >= W_MLA + W_RW) & (col < W_MLA + W_RW + SB_W)
    w_all = _bf(w_all * jnp.where(sb_q_cols, SB_HEAD ** -0.5 * LOG2_E, 1.0).astype(w_all.dtype))

    depth = w_in.shape[0]
    wuq = mla_w_uq.reshape(depth, MLA_Q_RANK, N_HEADS, MLA_NOPE + MLA_ROPE)
    nope, rope = wuq[..., :MLA_NOPE], wuq[..., MLA_NOPE:]
    rope_sw = _rope_swap(rope)
    zeros64 = jnp.zeros((depth, MLA_Q_RANK, 64), w_in.dtype)
    pair = lambda x, a, b: jnp.concatenate([x[:, :, a], x[:, :, b]], axis=-1)
    rope_pair = lambda x, a, b: jnp.concatenate([x[:, :, a], x[:, :, b], zeros64], axis=-1)
    wq = _bf(jnp.concatenate([pair(nope, 0, 1), rope_pair(rope, 0, 1), pair(nope, 2, 3), rope_pair(rope, 2, 3),
                              rope_pair(rope_sw, 0, 1), rope_pair(rope_sw, 2, 3)], axis=-1))
    wukv = mla_w_ukv.reshape(depth, MLA_KV_RANK, N_HEADS, 128)
    wkv = _bf(jnp.concatenate([wukv[..., :64].reshape(depth, MLA_KV_RANK, 256),
                               wukv[..., 64:].reshape(depth, MLA_KV_RANK, 256)], axis=-1))

    z64 = jnp.zeros_like(rw_w2)
    w2 = _bf(jnp.concatenate([rw_w2, z64], axis=1))
    a2 = _bf(jnp.concatenate([z64, rw_a2], axis=1))
    gla_a2p = _bf(jnp.concatenate([gla_a2, jnp.zeros((depth, 128 - gla_a2.shape[1], GLA_QK), gla_a2.dtype)], axis=1))
    return w_all, wq, wkv, w2, a2, gla_a2p


def _rope_tables(lp):
    half = MLA_ROPE // 2
    freqs = ROPE_THETA ** (-jnp.arange(half, dtype=F32) / half)
    pos = (jnp.arange(lp) - PAD).astype(F32)
    ang = pos[:, None] * freqs[None, :]
    return jnp.tile(jnp.cos(ang), (1, 128 // half)), jnp.tile(jnp.sin(ang), (1, 128 // half))


def kernel(x, meta_tokens, norm_mix, w_in, mla_q_norm, mla_w_uq, mla_kv_norm, mla_w_ukv, rw_mu, rw_w0, rw_w2, rw_a0, rw_a2, rw_g2, rw_k_k, rw_k_a, rw_r_k, rw_ln_w, rw_ln_b, gla_a2, gla_a_b, gla_norm, gate_b, w_branch, w_out, norm_ffn, w_ffn_in, ffn_conv_w, ffn_conv_b, w_ffn_out, norm_final):
    batch, seq, _ = x.shape
    depth = w_in.shape[0]
    lp = PAD + N_META + seq
    t = batch * lp
    assert lp % ROW_TILE == 0 and lp % ATTN_TILE == 0 and lp % GLA_CHUNK == 0 and lp % RW_CHUNK == 0

    w_all, wq, wkv, rw_w2p, rw_a2p, gla_a2p = _layout_params(w_in, mla_w_uq, mla_w_ukv, rw_w2, rw_a2, gla_a2)
    w_branch_b, w_out_b, w_ffn_in_b, w_ffn_out_b, rw_g2_b = map(_bf, (w_branch, w_out, w_ffn_in, w_ffn_out, rw_g2))
    vec = lambda a: a.reshape(depth, 1, -1)
    cos, sin = _rope_tables(lp)
    idx = jnp.arange(SB_SUFFIX_SPLIT)
    later_mat = jnp.where(idx[:, None] > idx[None, :], 1.0, 0.0).astype(BF16)

    h = _embed(x.reshape(batch * seq, D_MODEL), meta_tokens.astype(x.dtype), batch)

    for i in range(depth):
        z_mla, z_rw, z_sb, z_gla, z_gate, sb_values = _in_proj(h, vec(norm_mix)[i], w_all, i)
        q, k, v = _mla_prep(z_mla, vec(mla_q_norm)[i], vec(mla_kv_norm)[i], wq, wkv, i, cos, sin, lp)
        y_mla = _mla_attn(q, k, v, lp)
        rw = {"rw_mu": vec(rw_mu)[i], "rw_w0": vec(rw_w0)[i], "rw_w2": rw_w2p[i], "rw_a0": vec(rw_a0)[i],
              "rw_a2": rw_a2p[i], "rw_g2": rw_g2_b[i], "rw_k_k": vec(rw_k_k)[i], "rw_k_a": vec(rw_k_a)[i],
              "rw_r_k": vec(rw_r_k)[i]}
        *chunk_terms, g, bonus = _rw_prep(z_rw, rw, lp)
        y_rw = _rw_scan(*chunk_terms, batch)
        y_sb = _sb_attn(z_sb, sb_values, later_mat, lp)
        o_gla = _gla(z_gla, gla_a2p[i], vec(gla_a_b)[i], batch)
        mp = {"rw_ln_w": vec(rw_ln_w)[i], "rw_ln_b": vec(rw_ln_b)[i], "gla_norm": vec(gla_norm)[i],
              "gate_b": gate_b[i], "w_branch": w_branch_b, "w_out": w_out_b, "layer": i}
        h = _merge(h, y_mla, y_rw, bonus, g, y_sb, o_gla, z_gla, z_gate, mp, lp)
        h = _ffn(h, vec(norm_ffn)[i], w_ffn_in_b, ffn_conv_w[i], vec(ffn_conv_b)[i], w_ffn_out_b, i, lp)
    return _final_norm(h, norm_final.reshape(1, D_MODEL), batch, seq)
```

```python
import functools

import jax
import jax.numpy as jnp
from jax import lax
from jax.experimental import pallas as pl
from jax.experimental.pallas import tpu as pltpu

F32 = jnp.float32
BF16 = jnp.bfloat16

D_MODEL = 1024
DEPTH = 4
N_META = 16
BLOCK = 128
PAD = (-N_META) % BLOCK
EPS = 1e-6
NEG_INF = -1e30
LOG2_E = 1.4426950408889634

N_HEADS = 4
MLA_NOPE = 64
MLA_ROPE = 32
MLA_Q_RANK = 256
MLA_KV_RANK = 128
ROPE_THETA = 10000.0

RW_W = 256
RW_GN_EPS = 64e-5
RW_CHUNK = 64

SB_W = 256
SB_HEAD = 64
SB_SUFFIX_SPLIT = 256

GLA_DK = 32
GLA_QK = 128
GLA_W = 256
GLA_TAU = 16.0
GLA_CHUNK = 128

D_FF = 2816
FFN_COL_CHUNK = 1408

W_MLA, W_RW, W_SB, W_GLA, W_GATE = 640, 1024, 768, 896, 4096
IN_GROUP_WIDTHS = (W_MLA, W_RW, W_SB, W_GLA)
IN_GROUP_DTYPES = (BF16, F32, BF16, F32)

ROW_TILE = 384
ATTN_TILE = 384
FINAL_ROWS = 1024
VMEM_LIMIT = 56 * 1024 * 1024


def _bf(x):
    return x.astype(BF16)


def _dot(a, b):
    return jnp.dot(_bf(a), _bf(b), preferred_element_type=F32)


def _dot_nt(a, b):
    return lax.dot_general(_bf(a), _bf(b), (((1,), (1,)), ((), ())), preferred_element_type=F32)


def _dot_tn(a, b):
    return lax.dot_general(_bf(a), _bf(b), (((0,), (0,)), ((), ())), preferred_element_type=F32)


def _split_hi_lo(x):
    hi = _bf(x)
    lo = _bf(x - hi.astype(F32))
    return hi, lo


def _dot_exact_rhs(x, m):
    hi, lo = _split_hi_lo(x)
    return jnp.dot(hi, m, preferred_element_type=F32) + jnp.dot(lo, m, preferred_element_type=F32)


def _dot_exact_lhs(m, x):
    hi, lo = _split_hi_lo(x)
    return jnp.dot(m, hi, preferred_element_type=F32) + jnp.dot(m, lo, preferred_element_type=F32)


def _iota(shape, dim):
    return lax.broadcasted_iota(jnp.int32, shape, dim)


def _div_pow2(x, d):
    assert d & (d - 1) == 0
    return lax.shift_right_logical(x, d.bit_length() - 1)


def _same_segment(shape, row_seg, col_seg):
    return _div_pow2(_iota(shape, 0), row_seg) == _div_pow2(_iota(shape, 1), col_seg)


def _segment_matrix(n, seg, value):
    return jnp.where(_same_segment((n, n), seg, seg), value, 0.0).astype(BF16)


def _head_lane_mask(width, head_width, h):
    lane = _iota((1, width), 1)
    return (lane >= h * head_width) & (lane < (h + 1) * head_width)


def _stack_heads(x, head_width):
    w = x.shape[1]
    return jnp.concatenate(
        [jnp.where(_head_lane_mask(w, head_width, h), x, 0.0) for h in range(N_HEADS)], axis=0)


def _unstack_heads(xs, c):
    return xs[0:c] + xs[c:2 * c] + xs[2 * c:3 * c] + xs[3 * c:4 * c]


def _rms(x, eps):
    return x * lax.rsqrt(jnp.mean(x * x, axis=-1, keepdims=True) + eps)


def _const_spec(shape):
    nd = len(shape)
    return pl.BlockSpec(shape, lambda *_: (0,) * nd)


def _layer_spec(stacked, layer):
    nd = stacked.ndim - 1
    return pl.BlockSpec((None,) + stacked.shape[1:], lambda *_: (layer,) + (0,) * nd)


def _params(sem, vmem=VMEM_LIMIT):
    return pltpu.CompilerParams(dimension_semantics=sem, vmem_limit_bytes=vmem)


def _embed_kernel(x_ref, meta_ref, o_ref):
    first_real = PAD + N_META

    @pl.when(pl.program_id(1) == 0)
    def _():
        o_ref[0:PAD, :] = jnp.zeros((PAD, D_MODEL), o_ref.dtype)
        o_ref[PAD:first_real, :] = meta_ref[...]
        o_ref[first_real:, :] = x_ref[0:o_ref.shape[0] - first_real, :]

    @pl.when(pl.program_id(1) > 0)
    def _():
        o_ref[...] = x_ref[...]


def _embed(x2, meta, batch):
    seq = x2.shape[0] // batch
    first_real = PAD + N_META
    lp = first_real + seq
    tm = ROW_TILE
    window = lambda b, i: (pl.multiple_of(b * seq + jnp.maximum(i * tm - first_real, 0), BLOCK), 0)
    return pl.pallas_call(
        _embed_kernel,
        grid=(batch, lp // tm),
        in_specs=[pl.BlockSpec((pl.Element(tm), pl.Element(D_MODEL)), window), _const_spec(meta.shape)],
        out_specs=pl.BlockSpec((tm, D_MODEL), lambda b, i: (b * (lp // tm) + i, 0)),
        out_shape=jax.ShapeDtypeStruct((batch * lp, D_MODEL), x2.dtype),
        compiler_params=_params(("parallel", "parallel")),
        name="embed",
    )(x2, meta)


def _in_proj_kernel(h_ref, g_ref, w_ref, *out_refs):
    *group_refs, sb_values_ref = out_refs
    n = _bf(_rms(h_ref[...], EPS) * g_ref[...])
    off = 0
    for o_ref, width in zip(group_refs, IN_GROUP_WIDTHS):
        for c0 in range(0, width, 1024):
            c1 = min(c0 + 1024, width)
            o_ref[:, c0:c1] = jnp.dot(
                n, w_ref[:, off + c0:off + c1], preferred_element_type=F32).astype(o_ref.dtype)
        off += width
    sb_values_ref[0] = _head_stacked_values(group_refs[2][:, 2 * SB_W:3 * SB_W])


def _in_proj(h, g, w_all, layer):
    t = h.shape[0]
    tm = ROW_TILE
    return pl.pallas_call(
        _in_proj_kernel,
        grid=(t // tm,),
        in_specs=[pl.BlockSpec((tm, D_MODEL), lambda i: (i, 0)),
                  _const_spec((1, D_MODEL)),
                  _layer_spec(w_all, layer)],
        out_specs=[pl.BlockSpec((tm, w), lambda i: (i, 0)) for w in IN_GROUP_WIDTHS]
        + [pl.BlockSpec((1, N_HEADS * tm, SB_W), lambda i: (i, 0, 0))],
        out_shape=[jax.ShapeDtypeStruct((t, w), dt) for w, dt in zip(IN_GROUP_WIDTHS, IN_GROUP_DTYPES)]
        + [jax.ShapeDtypeStruct((t // tm, N_HEADS * tm, SB_W), BF16)],
        compiler_params=_params(("parallel",)),
        name="in_proj",
    )(h, g, w_all)


def _mla_prep_kernel(z_ref, qg_ref, kvg_ref, wq_ref, wkv_ref, cos_ref, sin_ref, q_ref, k_ref, vt_ref):
    z = z_ref[...].astype(F32)
    nq = _rms(z[:, 0:256], EPS) * qg_ref[...]
    nkv = _rms(z[:, 256:384], EPS) * kvg_ref[...]
    cos = cos_ref[...]
    sin = sin_ref[...]
    ql = _dot(nq, wq_ref[...])
    scale = (MLA_NOPE + MLA_ROPE) ** -0.5
    q = jnp.concatenate([ql[:, 0:128], ql[:, 128:256] * cos + ql[:, 512:640] * sin,
                         ql[:, 256:384], ql[:, 384:512] * cos + ql[:, 640:768] * sin], axis=1)
    q_ref[...] = _bf(q * scale)
    kvl = _dot(nkv, wkv_ref[...])
    k_rope = z[:, 384:512] * cos + z[:, 512:640] * sin
    k_rope = jnp.where(_iota((1, 128), 1) < 2 * MLA_ROPE, k_rope, 0.0)
    k_ref[...] = _bf(jnp.concatenate([kvl[:, 0:128], k_rope, kvl[:, 128:256], k_rope], axis=1))
    vt = _bf(kvl[:, 256:512].T)
    row_head = _div_pow2(_iota((256, 1), 0), 64)
    vt_ref[0] = jnp.concatenate([jnp.where(row_head == h, vt, jnp.zeros((), BF16)) for h in range(N_HEADS)], axis=1)


def _mla_prep(z_mla, qg, kvg, wq, wkv, layer, cos, sin, lp):
    t = z_mla.shape[0]
    tm = ROW_TILE
    nb = lp // tm
    row = lambda b, i: (b * nb + i, 0)
    return pl.pallas_call(
        _mla_prep_kernel,
        grid=(t // lp, nb),
        in_specs=[pl.BlockSpec((tm, W_MLA), row),
                  _const_spec((1, MLA_Q_RANK)), _const_spec((1, MLA_KV_RANK)),
                  _layer_spec(wq, layer), _layer_spec(wkv, layer),
                  pl.BlockSpec((tm, 128), lambda b, i: (i, 0)),
                  pl.BlockSpec((tm, 128), lambda b, i: (i, 0))],
        out_specs=[pl.BlockSpec((tm, 512), row), pl.BlockSpec((tm, 512), row),
                   pl.BlockSpec((1, 256, N_HEADS * tm), lambda b, i: (b * nb + i, 0, 0))],
        out_shape=[jax.ShapeDtypeStruct((t, 512), BF16), jax.ShapeDtypeStruct((t, 512), BF16),
                   jax.ShapeDtypeStruct((t // tm, 256, N_HEADS * tm), BF16)],
        compiler_params=_params(("parallel", "parallel")),
        name="mla_prep",
    )(z_mla, qg, kvg, wq, wkv, cos, sin)


def _per_head_lanes(cols):
    lane = _iota((1, 256), 1)
    return jnp.where(lane < 64, cols[0], jnp.where(lane < 128, cols[1], jnp.where(lane < 192, cols[2], cols[3])))


def _head_stacked_values(vb):
    zero = jnp.zeros((), vb.dtype)
    return jnp.concatenate([jnp.where(_head_lane_mask(256, 64, h), vb, zero) for h in range(N_HEADS)], axis=0)


def _per_head_rows(rows, n):
    return jnp.concatenate([jnp.broadcast_to(r, (64, n)) for r in rows], axis=0)


def _mla_attn_kernel(q_ref, k_ref, vt_ref, o_ref, qh_ref, m_ref, l_ref, acc_ref, p_ref, alpha_ref, *, tile):
    i = pl.program_id(1)
    q_pos = i * tile + _iota((1, tile), 1)
    lane = _iota((1, 256), 1)
    for h in range(N_HEADS):
        half, slot = h // 2, h % 2
        head_lanes = (((lane >= slot * MLA_NOPE) & (lane < (slot + 1) * MLA_NOPE))
                      | ((lane >= 128 + slot * MLA_ROPE) & (lane < 128 + (slot + 1) * MLA_ROPE)))
        qh_ref[half, slot * tile:(slot + 1) * tile, :] = jnp.where(
            head_lanes, q_ref[:, half * 256:(half + 1) * 256], jnp.zeros((), BF16))
    m_ref[...] = jnp.full(m_ref.shape, NEG_INF, F32)
    l_ref[...] = jnp.zeros(l_ref.shape, F32)
    acc_ref[...] = jnp.zeros(acc_ref.shape, F32)

    def add_values(j, slot):
        pv = jnp.dot(vt_ref[j], p_ref[slot], preferred_element_type=F32)
        acc_ref[...] = _per_head_rows([alpha_ref[slot, h] for h in range(N_HEADS)], tile) * acc_ref[...] + pv

    def block(j, masked, prev, slot):
        start = pl.multiple_of(j * tile, tile)
        heads = range(N_HEADS)
        s_pair = [_dot_nt(k_ref[pl.ds(start, tile), half * 256:(half + 1) * 256], qh_ref[half]) for half in (0, 1)]
        s = [s_pair[h // 2][:, (h % 2) * tile:(h % 2 + 1) * tile] for h in heads]
        if prev is not None:
            add_values(prev, 1 - slot)
        for h in heads:
            s_h = s[h]
            if masked:
                k_pos = start + _iota((tile, 1), 0)
                s_h = jnp.where((k_pos <= q_pos) & (k_pos >= PAD), s_h, NEG_INF)
            m_old = m_ref[h]
            m_new = jnp.maximum(m_old, jnp.max(s_h, axis=0, keepdims=True))
            p = jnp.exp(s_h - m_new)
            alpha = jnp.exp(m_old - m_new)
            l_ref[h] = alpha * l_ref[h] + jnp.sum(p, axis=0, keepdims=True)
            m_ref[h] = m_new
            p_ref[slot, h * tile:(h + 1) * tile, :] = _bf(p)
            alpha_ref[slot, h] = alpha

    block(i, True, None, 0)

    @pl.when(i > 0)
    def _():
        block(0, True, i, 1)

    interior_blocks = jnp.maximum(i - 1, 0)

    def interior_pair(t, carry):
        j = 1 + 2 * t
        block(j, False, j - 1, 0)
        block(j + 1, False, j, 1)
        return carry

    lax.fori_loop(0, lax.shift_right_logical(interior_blocks, 1), interior_pair, 0)
    odd = (interior_blocks & 1) == 1

    @pl.when(odd)
    def _():
        block(i - 1, False, i - 2, 0)

    last = jnp.maximum(i - 1, 0)
    last_in_slot0 = (i == 0) | odd

    @pl.when(last_in_slot0)
    def _():
        add_values(last, 0)

    @pl.when(jnp.logical_not(last_in_slot0))
    def _():
        add_values(last, 1)
    o_ref[...] = _bf((acc_ref[...] / _per_head_rows([l_ref[h] for h in range(N_HEADS)], tile)).T)


def _mla_attn(q, k, vt, lp):
    t = q.shape[0]
    tile = ATTN_TILE
    nb = lp // tile
    return pl.pallas_call(
        functools.partial(_mla_attn_kernel, tile=tile),
        grid=(t // lp, nb),
        in_specs=[pl.BlockSpec((tile, 512), lambda b, i: (b * nb + i, 0)),
                  pl.BlockSpec((lp, 512), lambda b, i: (b, 0)),
                  pl.BlockSpec((nb, 256, N_HEADS * tile), lambda b, i: (b, 0, 0))],
        out_specs=pl.BlockSpec((tile, 256), lambda b, i: (b * nb + i, 0)),
        out_shape=jax.ShapeDtypeStruct((t, 256), BF16),
        scratch_shapes=[pltpu.VMEM((2, 2 * tile, 256), BF16), pltpu.VMEM((N_HEADS, 1, tile), F32),
                        pltpu.VMEM((N_HEADS, 1, tile), F32), pltpu.VMEM((256, tile), F32),
                        pltpu.VMEM((2, N_HEADS * tile, tile), BF16), pltpu.VMEM((2, N_HEADS, 1, tile), F32)],
        compiler_params=_params(("parallel", "arbitrary")),
        name="mla_attn",
    )(q, k, vt)


def _sb_attn_kernel(q_ref, k_ref, v_ref, later_ref, o_ref, qh_ref, c_ref, acc_ref, wts_ref, *, tile):
    i = pl.program_id(1)
    row = i * tile + _iota((tile, 1), 0)
    for h in range(N_HEADS):
        qh_ref[h] = jnp.where(_head_lane_mask(SB_W, SB_HEAD, h), q_ref[...], jnp.zeros((), BF16))
    c_ref[...] = jnp.zeros(c_ref.shape, F32)
    acc_ref[...] = jnp.zeros(acc_ref.shape, F32)

    def add_values(j, slot):
        acc_ref[...] += jnp.dot(wts_ref[slot], v_ref[j], preferred_element_type=F32)

    def block(j, masked, prev, slot):
        start = pl.multiple_of(j * tile, tile)
        heads = range(N_HEADS)
        cut = SB_SUFFIX_SPLIT
        z = [_dot_nt(qh_ref[h], k_ref[pl.ds(start, tile), :]) for h in heads]
        if prev is not None:
            add_values(prev, 1 - slot)
        log_take = [jnp.minimum(z[h], 0.0) - jnp.log2(1.0 + jnp.exp2(-jnp.abs(z[h]))) for h in heads]
        log_keep = [log_take[h] - z[h] for h in heads]
        if masked:
            mask = (start + _iota((1, tile), 1)) < row
            log_keep = [jnp.where(mask, log_keep[h], 0.0) for h in heads]
        keep16 = [_bf(log_keep[h]) for h in heads]
        later_head = [jnp.dot(keep16[h][:, :cut], later_ref[...], preferred_element_type=F32) for h in heads]
        later_tail = [jnp.dot(keep16[h][:, cut:], later_ref[:tile - cut, :tile - cut], preferred_element_type=F32)
                      for h in heads]
        for h in heads:
            head_sum = jnp.sum(log_keep[h][:, :cut], axis=-1, keepdims=True)
            tail_sum = jnp.sum(log_keep[h][:, cut:], axis=-1, keepdims=True)
            c = c_ref[h]
            later = jnp.concatenate([later_head[h] + tail_sum, later_tail[h]], axis=1)
            w = jnp.exp2(log_take[h] + later + c)
            if masked:
                w = jnp.where(mask, w, 0.0)
            c_ref[h] = c + (head_sum + tail_sum)
            wts_ref[slot, :, h * tile:(h + 1) * tile] = _bf(w)

    block(i, True, None, 0)

    def below_pair(t, carry):
        j = i - 1 - 2 * t
        block(j, False, j + 1, 1)
        block(j - 1, False, j, 0)
        return carry

    lax.fori_loop(0, lax.shift_right_logical(i, 1), below_pair, 0)
    odd = (i & 1) == 1

    @pl.when(odd)
    def _():
        block(0, False, 1, 1)
        add_values(0, 1)

    @pl.when(jnp.logical_not(odd))
    def _():
        add_values(0, 0)
    o_ref[...] = _bf(acc_ref[...])


def _sb_attn(z_sb, sb_values, later_mat, lp):
    t = z_sb.shape[0]
    tile = ATTN_TILE
    nb = lp // tile
    return pl.pallas_call(
        functools.partial(_sb_attn_kernel, tile=tile),
        grid=(t // lp, nb),
        in_specs=[pl.BlockSpec((tile, SB_W), lambda b, i: (b * nb + i, 0)),
                  pl.BlockSpec((lp, SB_W), lambda b, i: (b, 1)),
                  pl.BlockSpec((nb, N_HEADS * tile, SB_W), lambda b, i: (b, 0, 0)),
                  _const_spec((SB_SUFFIX_SPLIT, SB_SUFFIX_SPLIT))],
        out_specs=pl.BlockSpec((tile, SB_W), lambda b, i: (b * nb + i, 0)),
        out_shape=jax.ShapeDtypeStruct((t, SB_W), BF16),
        scratch_shapes=[pltpu.VMEM((N_HEADS, tile, SB_W), BF16), pltpu.VMEM((N_HEADS, tile, 1), F32),
                        pltpu.VMEM((tile, SB_W), F32), pltpu.VMEM((2, tile, N_HEADS * tile), BF16)],
        compiler_params=_params(("parallel", "arbitrary")),
        name="sb_attn",
    )(z_sb, z_sb, sb_values, later_mat)


def _compact_eye(c):
    return jnp.where(_iota((c, N_HEADS * c), 0) == (_iota((c, N_HEADS * c), 1) & (c - 1)), 1.0, 0.0)


def _rw_prep_kernel(z_ref, zprev_ref, mu_ref, w0_ref, w2_ref, a0_ref, a2_ref, g2_ref, kk_ref, ka_ref, rk_ref,
                    w_o, rt_o, arb_o, kbar_o, bbar_o, v_o, uv_o, yv_o, dec_o, g_o, bonus_o):
    i = pl.program_id(1)
    z = z_ref[...]
    tm = z.shape[0]
    prev = jnp.where(i == 0, 0.0, zprev_ref[7:8, :])
    shifted = jnp.where(_iota((tm, 1), 0) == 0, prev, pltpu.roll(z, 1, 0))
    zz = z + (shifted - z) * mu_ref[...]
    r, k, v = zz[:, 0:256], zz[:, 256:512], zz[:, 512:768]
    lora_in = zz[:, 768:896]
    u = w0_ref[...] + _dot(jnp.tanh(lora_in), w2_ref[...])
    w = jnp.minimum(u, 0.0) - jnp.log1p(jnp.exp(-jnp.abs(u))) - 0.5
    a = jax.nn.sigmoid(a0_ref[...] + _dot(lora_in, a2_ref[...]))
    seg = _segment_matrix(RW_W, 64, 1.0)
    kx = k * kk_ref[...]
    kap = kx / jnp.maximum(jnp.sqrt(_dot_exact_rhs(kx * kx, seg)), 1e-12)
    kmod = k * (1.0 + (a - 1.0) * ka_ref[...])
    beta = kap * a
    lw = -jnp.exp(w)
    g_o[...] = _dot(jax.nn.sigmoid(zz[:, 896:1024]), g2_ref[...])
    bonus_o[...] = _dot_exact_rhs(r * kmod * rk_ref[...], seg) * v

    c = RW_CHUNK
    same_chunk = _same_segment((tm, tm), c, c)
    lw_hi, lw_lo = _split_hi_lo(lw)
    cum = jnp.where(same_chunk & (_iota((tm, tm), 0) >= _iota((tm, tm), 1)), 1.0, 0.0).astype(BF16)
    tot = jnp.where(same_chunk, 1.0, 0.0).astype(BF16)
    b = jnp.dot(cum, lw_hi, preferred_element_type=F32) + jnp.dot(cum, lw_lo, preferred_element_type=F32)
    b_end = jnp.dot(tot, lw_hi, preferred_element_type=F32) + jnp.dot(tot, lw_lo, preferred_element_type=F32)
    grow = jnp.exp(-b)
    to_end = jnp.exp(b_end - b)
    kap_t = kap * jnp.exp(b - lw)
    r_t = r * jnp.exp(b)
    beta_g = beta * grow
    k_g = kmod * grow
    rt_o[...] = _bf(r_t)
    kbar_o[...] = _bf(kmod * to_end)
    bbar_o[...] = _bf(beta * to_end)
    v_o[...] = _bf(v)
    decay_end = jnp.exp(b_end)

    lane_pos = _iota((c, N_HEADS * c), 1) & (c - 1)
    strictly_earlier = _iota((c, N_HEADS * c), 0) > lane_pos
    not_later = _iota((c, N_HEADS * c), 0) >= lane_pos
    chunks = [slice(n * c, (n + 1) * c) for n in range(tm // c)]
    pair = [_dot_nt(jnp.concatenate([kap_t[rows], r_t[rows]], axis=0),
                    jnp.concatenate([_stack_heads(beta_g[rows], 64), _stack_heads(k_g[rows], 64)], axis=0))
            for rows in chunks]
    a_kb = [jnp.where(strictly_earlier, p[0:c, 0:4 * c], 0.0) for p in pair]
    a_kk = [jnp.where(strictly_earlier, p[0:c, 4 * c:8 * c], 0.0) for p in pair]
    a_rk = [jnp.where(not_later, p[c:2 * c, 4 * c:8 * c], 0.0) for p in pair]
    for rows, p in zip(chunks, pair):
        arb_o[rows, :] = _bf(jnp.where(not_later, p[c:2 * c, 0:4 * c], 0.0))
    inv = [_compact_eye(c) - a for a in a_kb]
    power = [_dot(a, _stack_heads(a, c)) for a in a_kb]
    span = 2
    while span < c:
        inv = [t + _dot(t, _stack_heads(p, c)) for t, p in zip(inv, power)]
        span *= 2
        if span < c:
            power = [_dot(p, _stack_heads(p, c)) for p in power]
    v_st = [_stack_heads(v[rows], 64) for rows in chunks]
    kk_v = [_dot(a, vs) for a, vs in zip(a_kk, v_st)]
    for n, rows in enumerate(chunks):
        w_o[rows, :] = _bf(_dot(inv[n], _stack_heads(kap_t[rows], 64)))
        uv_o[rows, :] = _dot(inv[n], _stack_heads(kk_v[n], 64))
        yv_o[rows, :] = _dot(a_rk[n], v_st[n])
        dec_o[n * 8:(n + 1) * 8, :] = decay_end[n * c:n * c + 8]


def _rw_prep(z_rw, p, lp):
    t = z_rw.shape[0]
    tm = ROW_TILE
    nb = lp // tm
    row = lambda b, i: (b * nb + i, 0)
    prev = lambda b, i: (jnp.maximum(b * (lp // 8) + i * (tm // 8) - 1, 0), 0)
    consts = [p["rw_mu"], p["rw_w0"], p["rw_w2"], p["rw_a0"], p["rw_a2"], p["rw_g2"],
              p["rw_k_k"], p["rw_k_a"], p["rw_r_k"]]
    return pl.pallas_call(
        _rw_prep_kernel,
        grid=(t // lp, nb),
        in_specs=[pl.BlockSpec((tm, W_RW), row), pl.BlockSpec((8, W_RW), prev)]
        + [_const_spec(c.shape) for c in consts],
        out_specs=[pl.BlockSpec((tm, RW_W), row)] * 8
        + [pl.BlockSpec((tm // RW_CHUNK * 8, RW_W), row)] + [pl.BlockSpec((tm, RW_W), row)] * 2,
        out_shape=[jax.ShapeDtypeStruct((t, RW_W), BF16)] * 6 + [jax.ShapeDtypeStruct((t, RW_W), F32)] * 2
        + [jax.ShapeDtypeStruct((t // RW_CHUNK * 8, RW_W), F32)] + [jax.ShapeDtypeStruct((t, RW_W), F32)] * 2,
        compiler_params=_params(("parallel", "parallel")),
        name="rw_prep",
    )(z_rw, z_rw, *consts)


def _rw_scan_kernel(w_ref, rt_ref, arb_ref, kbar_ref, bbar_ref, v_ref, uv_ref, yv_ref, dec_ref, y_ref, ht_ref):
    c = RW_CHUNK

    @pl.when(pl.program_id(0) == 0)
    def _():
        ht_ref[...] = jnp.zeros_like(ht_ref)

    same_head = _same_segment((RW_W, RW_W), 64, 64)
    seqs = range(ht_ref.shape[0])
    ht = [ht_ref[s] for s in seqs]
    from_state = [_dot_nt(jnp.concatenate([w_ref[s], rt_ref[s]], axis=0), ht[s]) for s in seqs]
    u = [from_state[s][0:c] + uv_ref[s] for s in seqs]
    from_u = [_dot(arb_ref[s], _stack_heads(u[s], 64)) for s in seqs]
    upd = [_dot_tn(jnp.concatenate([v_ref[s], _bf(u[s])], axis=0),
                   jnp.concatenate([kbar_ref[s], -bbar_ref[s]], axis=0)) for s in seqs]
    for s in seqs:
        y_ref[s] = from_state[s][c:2 * c] + yv_ref[s] - from_u[s]
        ht_ref[s] = ht[s] * dec_ref[s, 0:1, :] + jnp.where(same_head, upd[s], 0.0)


def _rw_scan(w, rt, arb, kbar, bbar, v, uv, yv, dec, batch):
    t = w.shape[0]
    lp = t // batch
    c = RW_CHUNK
    as_seq = lambda a: a.reshape(batch, a.shape[0] // batch, RW_W)
    spec = pl.BlockSpec((batch, c, RW_W), lambda i: (0, i, 0))
    y = pl.pallas_call(
        _rw_scan_kernel,
        grid=(lp // c,),
        in_specs=[spec] * 8 + [pl.BlockSpec((batch, 8, RW_W), lambda i: (0, i, 0))],
        out_specs=spec,
        out_shape=jax.ShapeDtypeStruct((batch, lp, RW_W), F32),
        scratch_shapes=[pltpu.VMEM((batch, RW_W, RW_W), F32)],
        compiler_params=_params(("arbitrary",)),
        name="rw_scan",
    )(*map(as_seq, (w, rt, arb, kbar, bbar, v, uv, yv, dec)))
    return y.reshape(t, RW_W)


def _gla_kernel(q_ref, k_ref, v_ref, al_ref, a2_ref, ab_ref, o_ref, st_ref):
    c = GLA_CHUNK
    n = N_HEADS * c
    seqs = range(st_ref.shape[0])

    @pl.when(pl.program_id(0) == 0)
    def _():
        st_ref[...] = jnp.zeros_like(st_ref)

    tri = jnp.where(_iota((c, c), 0) >= _iota((c, c), 1), 1.0, 0.0).astype(BF16)
    causal = (_iota((n, c), 0) & (c - 1)) >= _iota((n, c), 1)
    same_head = _same_segment((GLA_W, GLA_QK), 64, GLA_DK)
    x = [_dot(al_ref[s], a2_ref[...]) + ab_ref[...] for s in seqs]
    log_a = [(jnp.minimum(x[s], 0.0) - jnp.log1p(jnp.exp(-jnp.abs(x[s])))) * (1.0 / GLA_TAU) for s in seqs]
    b = [_dot_exact_lhs(tri, log_a[s]) for s in seqs]
    q = [q_ref[s] * (GLA_DK ** -0.5) for s in seqs]
    st = [st_ref[s] for s in seqs]
    inter = [_dot_nt(q[s] * jnp.exp(b[s]), st[s]) for s in seqs]
    scores = []
    for s in seqs:
        b_mid = b[s][c // 2 - 1:c // 2, :]
        sc = _dot_nt(_stack_heads(q[s] * jnp.exp(b[s] - b_mid), GLA_DK), k_ref[s] * jnp.exp(b_mid - b[s]))
        scores.append(jnp.where(causal, sc, 0.0))
    per_head = [_dot(scores[s], v_ref[s]) for s in seqs]
    upd = [_dot_tn(v_ref[s], k_ref[s] * jnp.exp(b[s][c - 1:c, :] - b[s])) for s in seqs]
    for s in seqs:
        intra = jnp.zeros((c, GLA_W), F32)
        for h in range(N_HEADS):
            intra = intra + jnp.where(_head_lane_mask(GLA_W, 64, h), per_head[s][h * c:(h + 1) * c], 0.0)
        o_ref[s] = inter[s] + intra
        st_ref[s] = st[s] * jnp.exp(b[s][c - 1:c, :]) + jnp.where(same_head, upd[s], 0.0)


def _gla(z_gla, a2, ab, batch):
    t = z_gla.shape[0]
    lp = t // batch
    c = GLA_CHUNK
    z3 = z_gla.reshape(batch, lp, W_GLA)
    o = pl.pallas_call(
        _gla_kernel,
        grid=(lp // c,),
        in_specs=[pl.BlockSpec((batch, c, 128), lambda i: (0, i, 0)),
                  pl.BlockSpec((batch, c, 128), lambda i: (0, i, 1)),
                  pl.BlockSpec((batch, c, 256), lambda i: (0, i, 1)),
                  pl.BlockSpec((batch, c, 128), lambda i: (0, i, 6)),
                  _const_spec(a2.shape), _const_spec(ab.shape)],
        out_specs=pl.BlockSpec((batch, c, GLA_W), lambda i: (0, i, 0)),
        out_shape=jax.ShapeDtypeStruct((batch, lp, GLA_W), F32),
        scratch_shapes=[pltpu.VMEM((batch, GLA_W, GLA_QK), F32)],
        compiler_params=_params(("arbitrary",)),
        name="gla",
    )(z3, z3, z3, z3, a2, ab)
    return o.reshape(t, GLA_W)


def _merge_kernel(h_ref, ymla_ref, yrw_ref, bonus_ref, g_ref, ysb_ref, ogla_ref, rgla_ref,
                  nmix_ref, lnw_ref, lnb_ref, gn_ref, gb_ref, wg_ref, wb_ref, wo_ref, o_ref, *, tiles_per_seq):
    tm = h_ref.shape[0]
    avg = _segment_matrix(256, 64, 1.0 / 64)
    h = h_ref[...]
    n = _bf(_rms(h, EPS) * nmix_ref[...])

    y = yrw_ref[...]
    d = y - _dot_exact_rhs(y, avg)
    var = _dot_exact_rhs(d * d, avg)
    y_rw = (d * lax.rsqrt(var + RW_GN_EPS) * lnw_ref[...] + lnb_ref[...] + bonus_ref[...]) * g_ref[...]

    o = ogla_ref[...]
    r = rgla_ref[...]
    y_gla = o * lax.rsqrt(_dot_exact_rhs(o * o, avg) + EPS) * gn_ref[...] * (r * jax.nn.sigmoid(r))

    acc = jnp.zeros((tm, D_MODEL), F32)
    for m, y_m in enumerate((ymla_ref[...], y_rw, ysb_ref[...], y_gla)):
        logits = jnp.dot(n, wg_ref[:, m * D_MODEL:(m + 1) * D_MODEL], preferred_element_type=F32)
        gate = jax.nn.sigmoid(logits + gb_ref[m:m + 1, :])
        acc = acc + gate * _dot(y_m, wb_ref[m])
    delta = _dot(acc, wo_ref[...])
    row = (pl.program_id(0) % tiles_per_seq) * tm + _iota((tm, 1), 0)
    o_ref[...] = h + jnp.where(row >= PAD, delta, 0.0)


def _merge(h, y_mla, y_rw, bonus, g, y_sb, o_gla, z_gla, p, lp):
    t = h.shape[0]
    tm = ROW_TILE
    row = lambda i: (i, 0)
    w256 = pl.BlockSpec((tm, 256), row)
    consts = [p["norm_mix"], p["rw_ln_w"], p["rw_ln_b"], p["gla_norm"], p["gate_b"]]
    stacked = [p["w_gate"], p["w_branch"], p["w_out"]]
    return pl.pallas_call(
        functools.partial(_merge_kernel, tiles_per_seq=lp // tm),
        grid=(t // tm,),
        in_specs=[pl.BlockSpec((tm, D_MODEL), row), w256, w256, w256, w256, w256, w256,
                  pl.BlockSpec((tm, 256), lambda i: (i, 2))]
        + [_const_spec(c.shape) for c in consts]
        + [_layer_spec(w, p["layer"]) for w in stacked],
        out_specs=pl.BlockSpec((tm, D_MODEL), row),
        out_shape=jax.ShapeDtypeStruct((t, D_MODEL), F32),
        compiler_params=_params(("parallel",)),
        name="merge",
    )(h, y_mla, y_rw, bonus, g, y_sb, o_gla, z_gla, *consts, *stacked)


def _ffn_kernel(h_ref, g_ref, win_ref, cw_ref, cb_ref, wout_ref, o_ref, tail_ref):
    @pl.when(pl.program_id(1) == 0)
    def _():
        tail_ref[...] = jnp.zeros_like(tail_ref)

    x = h_ref[...]
    tm = x.shape[0]
    n = _bf(_rms(x, EPS) * g_ref[...])
    rowi = _iota((tm, 1), 0)
    acc = jnp.zeros((tm, D_MODEL), F32)
    for c0 in range(0, D_FF, FFN_COL_CHUNK):
        c1 = c0 + FFN_COL_CHUNK
        a = jnp.dot(n, win_ref[:, c0:c1], preferred_element_type=F32)
        u = jnp.dot(n, win_ref[:, D_FF + c0:D_FF + c1], preferred_element_type=F32)
        prev1 = tail_ref[7:8, c0:c1]
        prev2 = tail_ref[6:7, c0:c1]
        a1 = jnp.where(rowi == 0, prev1, pltpu.roll(a, 1, 0))
        a2 = jnp.where(rowi == 0, prev2, jnp.where(rowi == 1, prev1, pltpu.roll(a, 2, 0)))
        tail_ref[:, c0:c1] = a[tm - 8:tm, :]
        conv = cb_ref[:, c0:c1] + cw_ref[0:1, c0:c1] * a2 + cw_ref[1:2, c0:c1] * a1 + cw_ref[2:3, c0:c1] * a
        acc = acc + _dot(conv * jax.nn.sigmoid(conv) * u, wout_ref[c0:c1, :])
    o_ref[...] = x + acc


def _ffn(h, g, w_in, conv_w, conv_b, w_out, layer, lp):
    t = h.shape[0]
    tm = ROW_TILE
    nb = lp // tm
    row = lambda b, i: (b * nb + i, 0)
    return pl.pallas_call(
        _ffn_kernel,
        grid=(t // lp, nb),
        in_specs=[pl.BlockSpec((tm, D_MODEL), row), _const_spec((1, D_MODEL)), _layer_spec(w_in, layer),
                  _const_spec(conv_w.shape), _const_spec(conv_b.shape), _layer_spec(w_out, layer)],
        out_specs=pl.BlockSpec((tm, D_MODEL), row),
        out_shape=jax.ShapeDtypeStruct((t, D_MODEL), F32),
        scratch_shapes=[pltpu.VMEM((8, D_FF), F32)],
        compiler_params=_params(("arbitrary", "arbitrary")),
        name="conv_ffn",
    )(h, g, w_in, conv_w, conv_b, w_out)


def _final_norm_kernel(h_ref, g_ref, o_ref):
    o_ref[0] = _rms(h_ref[...], EPS) * g_ref[...]


def _final_norm(h, g, batch, seq):
    lp = h.shape[0] // batch
    rows = FINAL_ROWS
    assert seq % rows == 0
    first = lambda bi, i: (pl.multiple_of(bi * lp + (PAD + N_META) + i * rows, BLOCK), 0)
    return pl.pallas_call(
        _final_norm_kernel,
        grid=(batch, seq // rows),
        in_specs=[pl.BlockSpec((pl.Element(rows), pl.Element(D_MODEL)), first), _const_spec((1, D_MODEL))],
        out_specs=pl.BlockSpec((1, rows, D_MODEL), lambda bi, i: (bi, i, 0)),
        out_shape=jax.ShapeDtypeStruct((batch, seq, D_MODEL), F32),
        compiler_params=_params(("parallel", "parallel")),
        name="final_norm",
    )(h, g)


def _rope_swap(w):
    half = w.shape[-1] // 2
    return jnp.concatenate([-w[..., half:], w[..., :half]], axis=-1)


def _in_weight_layout_kernel(w_ref, o_ref, gate_ref):
    def put(dst, val):
        o_ref[0, :, dst:dst + val.shape[1]] = _bf(val)

    put(0, w_ref[0, :, 0:384])
    kr = w_ref[0, :, 384:416]
    put(384, jnp.concatenate([kr] * 4, axis=1))
    put(512, jnp.concatenate([_rope_swap(kr)] * 4, axis=1))
    put(W_MLA, w_ref[0, :, 416:1440])
    sb = W_MLA + W_RW
    put(sb, w_ref[0, :, 1440:1440 + SB_W] * (SB_HEAD ** -0.5 * LOG2_E))
    put(sb + SB_W, w_ref[0, :, 1440 + SB_W:2208])
    gla = sb + W_SB
    put(gla, w_ref[0, :, 2208:2720])
    put(gla + 512, w_ref[0, :, 2736:2992])
    lora = w_ref[0, :, 2720:2736]
    put(gla + 768, jnp.concatenate([lora, jnp.zeros((lora.shape[0], W_GLA - 784), lora.dtype)], axis=1))
    gate_ref[0] = _bf(w_ref[0, :, 2992:2992 + W_GATE])


def _in_weight_layout(w_in):
    depth, d, n = w_in.shape
    rows = 128
    total = sum(IN_GROUP_WIDTHS)
    return pl.pallas_call(
        _in_weight_layout_kernel,
        grid=(depth, d // rows),
        in_specs=[pl.BlockSpec((1, rows, n), lambda l, i: (l, i, 0))],
        out_specs=[pl.BlockSpec((1, rows, total), lambda l, i: (l, i, 0)),
                   pl.BlockSpec((1, rows, W_GATE), lambda l, i: (l, i, 0))],
        out_shape=[jax.ShapeDtypeStruct((depth, d, total), BF16), jax.ShapeDtypeStruct((depth, d, W_GATE), BF16)],
        compiler_params=_params(("parallel", "parallel")),
        name="in_weight_layout",
    )(w_in)


def _layout_params(w_in, mla_w_uq, mla_w_ukv, rw_w2, rw_a2, gla_a2):
    w_all, w_gate = _in_weight_layout(w_in)

    depth = w_in.shape[0]
    wuq = mla_w_uq.reshape(depth, MLA_Q_RANK, N_HEADS, MLA_NOPE + MLA_ROPE)
    nope, rope = wuq[..., :MLA_NOPE], wuq[..., MLA_NOPE:]
    rope_sw = _rope_swap(rope)
    zeros64 = jnp.zeros((depth, MLA_Q_RANK, 64), w_in.dtype)
    pair = lambda x, a, b: jnp.concatenate([x[:, :, a], x[:, :, b]], axis=-1)
    rope_pair = lambda x, a, b: jnp.concatenate([x[:, :, a], x[:, :, b], zeros64], axis=-1)
    wq = _bf(jnp.concatenate([pair(nope, 0, 1), rope_pair(rope, 0, 1), pair(nope, 2, 3), rope_pair(rope, 2, 3),
                              rope_pair(rope_sw, 0, 1), rope_pair(rope_sw, 2, 3)], axis=-1))
    wukv = mla_w_ukv.reshape(depth, MLA_KV_RANK, N_HEADS, 128)
    wkv = _bf(jnp.concatenate([wukv[..., :64].reshape(depth, MLA_KV_RANK, 256),
                               wukv[..., 64:].reshape(depth, MLA_KV_RANK, 256)], axis=-1))

    z64 = jnp.zeros_like(rw_w2)
    w2 = _bf(jnp.concatenate([rw_w2, z64], axis=1))
    a2 = _bf(jnp.concatenate([z64, rw_a2], axis=1))
    gla_a2p = _bf(jnp.concatenate([gla_a2, jnp.zeros((depth, 128 - gla_a2.shape[1], GLA_QK), gla_a2.dtype)], axis=1))
    return w_all, w_gate, wq, wkv, w2, a2, gla_a2p


def _rope_tables(lp):
    half = MLA_ROPE // 2
    freqs = ROPE_THETA ** (-jnp.arange(half, dtype=F32) / half)
    pos = (jnp.arange(lp) - PAD).astype(F32)
    ang = pos[:, None] * freqs[None, :]
    return jnp.tile(jnp.cos(ang), (1, 128 // half)), jnp.tile(jnp.sin(ang), (1, 128 // half))


def kernel(x, meta_tokens, norm_mix, w_in, mla_q_norm, mla_w_uq, mla_kv_norm, mla_w_ukv, rw_mu, rw_w0, rw_w2, rw_a0, rw_a2, rw_g2, rw_k_k, rw_k_a, rw_r_k, rw_ln_w, rw_ln_b, gla_a2, gla_a_b, gla_norm, gate_b, w_branch, w_out, norm_ffn, w_ffn_in, ffn_conv_w, ffn_conv_b, w_ffn_out, norm_final):
    batch, seq, _ = x.shape
    depth = w_in.shape[0]
    lp = PAD + N_META + seq
    t = batch * lp
    assert lp % ROW_TILE == 0 and lp % ATTN_TILE == 0 and lp % GLA_CHUNK == 0 and lp % RW_CHUNK == 0

    w_all, w_gate, wq, wkv, rw_w2p, rw_a2p, gla_a2p = _layout_params(
        w_in, mla_w_uq, mla_w_ukv, rw_w2, rw_a2, gla_a2)
    w_branch_b, w_out_b, w_ffn_in_b, w_ffn_out_b, rw_g2_b = map(_bf, (w_branch, w_out, w_ffn_in, w_ffn_out, rw_g2))
    vec = lambda a: a.reshape(depth, 1, -1)
    cos, sin = _rope_tables(lp)
    idx = jnp.arange(SB_SUFFIX_SPLIT)
    later_mat = jnp.where(idx[:, None] > idx[None, :], 1.0, 0.0).astype(BF16)

    h = _embed(x.reshape(batch * seq, D_MODEL), meta_tokens.astype(x.dtype), batch)

    for i in range(depth):
        z_mla, z_rw, z_sb, z_gla, sb_values = _in_proj(h, vec(norm_mix)[i], w_all, i)
        q, k, v = _mla_prep(z_mla, vec(mla_q_norm)[i], vec(mla_kv_norm)[i], wq, wkv, i, cos, sin, lp)
        y_mla = _mla_attn(q, k, v, lp)
        rw = {"rw_mu": vec(rw_mu)[i], "rw_w0": vec(rw_w0)[i], "rw_w2": rw_w2p[i], "rw_a0": vec(rw_a0)[i],
              "rw_a2": rw_a2p[i], "rw_g2": rw_g2_b[i], "rw_k_k": vec(rw_k_k)[i], "rw_k_a": vec(rw_k_a)[i],
              "rw_r_k": vec(rw_r_k)[i]}
        *chunk_terms, g, bonus = _rw_prep(z_rw, rw, lp)
        y_rw = _rw_scan(*chunk_terms, batch)
        y_sb = _sb_attn(z_sb, sb_values, later_mat, lp)
        o_gla = _gla(z_gla, gla_a2p[i], vec(gla_a_b)[i], batch)
        mp = {"rw_ln_w": vec(rw_ln_w)[i], "rw_ln_b": vec(rw_ln_b)[i], "gla_norm": vec(gla_norm)[i],
              "gate_b": gate_b[i], "norm_mix": vec(norm_mix)[i], "w_gate": w_gate, "w_branch": w_branch_b,
              "w_out": w_out_b, "layer": i}
        h = _merge(h, y_mla, y_rw, bonus, g, y_sb, o_gla, z_gla, mp, lp)
        h = _ffn(h, vec(norm_ffn)[i], w_ffn_in_b, ffn_conv_w[i], vec(ffn_conv_b)[i], w_ffn_out_b, i, lp)
    return _final_norm(h, norm_final.reshape(1, D_MODEL), batch, seq)
```

```python
import functools

import jax
import jax.numpy as jnp
from jax import lax
from jax.experimental import pallas as pl
from jax.experimental.pallas import tpu as pltpu

F32 = jnp.float32
BF16 = jnp.bfloat16

D_MODEL = 1024
DEPTH = 4
N_META = 16
BLOCK = 128
PAD = (-N_META) % BLOCK
EPS = 1e-6
NEG_INF = -1e30
LOG2_E = 1.4426950408889634

N_HEADS = 4
MLA_NOPE = 64
MLA_ROPE = 32
MLA_Q_RANK = 256
MLA_KV_RANK = 128
ROPE_THETA = 10000.0

RW_W = 256
RW_GN_EPS = 64e-5
RW_CHUNK = 64

SB_W = 256
SB_HEAD = 64
SB_SUFFIX_SPLIT = 256

GLA_DK = 32
GLA_QK = 128
GLA_W = 256
GLA_TAU = 16.0
GLA_CHUNK = 128

D_FF = 2816
FFN_COL_CHUNK = 1408

W_MLA, W_RW, W_SB, W_GLA, W_GATE = 640, 1024, 768, 896, 4096
IN_GROUP_WIDTHS = (W_MLA, W_RW, W_SB, W_GLA)
IN_GROUP_DTYPES = (BF16, F32, BF16, F32)

ROW_TILE = 384
ATTN_TILE = 384
FINAL_ROWS = 1024
VMEM_LIMIT = 56 * 1024 * 1024


def _bf(x):
    return x.astype(BF16)


def _dot(a, b):
    return jnp.dot(_bf(a), _bf(b), preferred_element_type=F32)


def _dot_nt(a, b):
    return lax.dot_general(_bf(a), _bf(b), (((1,), (1,)), ((), ())), preferred_element_type=F32)


def _dot_tn(a, b):
    return lax.dot_general(_bf(a), _bf(b), (((0,), (0,)), ((), ())), preferred_element_type=F32)


def _split_hi_lo(x):
    hi = _bf(x)
    lo = _bf(x - hi.astype(F32))
    return hi, lo


def _dot_exact_rhs(x, m):
    hi, lo = _split_hi_lo(x)
    return jnp.dot(hi, m, preferred_element_type=F32) + jnp.dot(lo, m, preferred_element_type=F32)


def _dot_exact_lhs(m, x):
    hi, lo = _split_hi_lo(x)
    return jnp.dot(m, hi, preferred_element_type=F32) + jnp.dot(m, lo, preferred_element_type=F32)


def _iota(shape, dim):
    return lax.broadcasted_iota(jnp.int32, shape, dim)


def _div_pow2(x, d):
    assert d & (d - 1) == 0
    return lax.shift_right_logical(x, d.bit_length() - 1)


def _same_segment(shape, row_seg, col_seg):
    return _div_pow2(_iota(shape, 0), row_seg) == _div_pow2(_iota(shape, 1), col_seg)


def _segment_matrix(n, seg, value):
    return jnp.where(_same_segment((n, n), seg, seg), value, 0.0).astype(BF16)


def _head_lane_mask(width, head_width, h):
    lane = _iota((1, width), 1)
    return (lane >= h * head_width) & (lane < (h + 1) * head_width)


def _stack_heads(x, head_width):
    w = x.shape[1]
    return jnp.concatenate(
        [jnp.where(_head_lane_mask(w, head_width, h), x, 0.0) for h in range(N_HEADS)], axis=0)


def _unstack_heads(xs, c):
    return xs[0:c] + xs[c:2 * c] + xs[2 * c:3 * c] + xs[3 * c:4 * c]


def _rms(x, eps):
    return x * lax.rsqrt(jnp.mean(x * x, axis=-1, keepdims=True) + eps)


def _const_spec(shape):
    nd = len(shape)
    return pl.BlockSpec(shape, lambda *_: (0,) * nd)


def _layer_spec(stacked, layer):
    nd = stacked.ndim - 1
    return pl.BlockSpec((None,) + stacked.shape[1:], lambda *_: (layer,) + (0,) * nd)


def _params(sem, vmem=VMEM_LIMIT):
    return pltpu.CompilerParams(dimension_semantics=sem, vmem_limit_bytes=vmem)


def _embed_kernel(x_ref, meta_ref, o_ref):
    first_real = PAD + N_META

    @pl.when(pl.program_id(1) == 0)
    def _():
        o_ref[0:PAD, :] = jnp.zeros((PAD, D_MODEL), o_ref.dtype)
        o_ref[PAD:first_real, :] = meta_ref[...]
        o_ref[first_real:, :] = x_ref[0:o_ref.shape[0] - first_real, :]

    @pl.when(pl.program_id(1) > 0)
    def _():
        o_ref[...] = x_ref[...]


def _embed(x2, meta, batch):
    seq = x2.shape[0] // batch
    first_real = PAD + N_META
    lp = first_real + seq
    tm = ROW_TILE
    window = lambda b, i: (pl.multiple_of(b * seq + jnp.maximum(i * tm - first_real, 0), BLOCK), 0)
    return pl.pallas_call(
        _embed_kernel,
        grid=(batch, lp // tm),
        in_specs=[pl.BlockSpec((pl.Element(tm), pl.Element(D_MODEL)), window), _const_spec(meta.shape)],
        out_specs=pl.BlockSpec((tm, D_MODEL), lambda b, i: (b * (lp // tm) + i, 0)),
        out_shape=jax.ShapeDtypeStruct((batch * lp, D_MODEL), x2.dtype),
        compiler_params=_params(("parallel", "parallel")),
        name="embed",
    )(x2, meta)


def _in_proj_kernel(h_ref, g_ref, w_ref, *out_refs):
    *group_refs, sb_values_ref = out_refs
    n = _bf(_rms(h_ref[...], EPS) * g_ref[...])
    off = 0
    for o_ref, width in zip(group_refs, IN_GROUP_WIDTHS):
        for c0 in range(0, width, 1024):
            c1 = min(c0 + 1024, width)
            o_ref[:, c0:c1] = _dot_nt(n, w_ref[off + c0:off + c1, :]).astype(o_ref.dtype)
        off += width
    sb_values_ref[0] = _head_stacked_values(group_refs[2][:, 2 * SB_W:3 * SB_W])


def _in_proj(h, g, w_all, layer):
    t = h.shape[0]
    tm = ROW_TILE
    return pl.pallas_call(
        _in_proj_kernel,
        grid=(t // tm,),
        in_specs=[pl.BlockSpec((tm, D_MODEL), lambda i: (i, 0)),
                  _const_spec((1, D_MODEL)),
                  _layer_spec(w_all, layer)],
        out_specs=[pl.BlockSpec((tm, w), lambda i: (i, 0)) for w in IN_GROUP_WIDTHS]
        + [pl.BlockSpec((1, N_HEADS * tm, SB_W), lambda i: (i, 0, 0))],
        out_shape=[jax.ShapeDtypeStruct((t, w), dt) for w, dt in zip(IN_GROUP_WIDTHS, IN_GROUP_DTYPES)]
        + [jax.ShapeDtypeStruct((t // tm, N_HEADS * tm, SB_W), BF16)],
        compiler_params=_params(("parallel",)),
        name="in_proj",
    )(h, g, w_all)


def _mla_prep_kernel(z_ref, qg_ref, kvg_ref, wq_ref, wkv_ref, cos_ref, sin_ref, q_ref, k_ref, vt_ref):
    z = z_ref[...].astype(F32)
    nq = _rms(z[:, 0:256], EPS) * qg_ref[...]
    nkv = _rms(z[:, 256:384], EPS) * kvg_ref[...]
    cos = cos_ref[...]
    sin = sin_ref[...]
    ql = _dot(nq, wq_ref[...])
    scale = (MLA_NOPE + MLA_ROPE) ** -0.5
    q = jnp.concatenate([ql[:, 0:128], ql[:, 128:256] * cos + ql[:, 512:640] * sin,
                         ql[:, 256:384], ql[:, 384:512] * cos + ql[:, 640:768] * sin], axis=1)
    q_ref[...] = _bf(q * scale)
    kvl = _dot(nkv, wkv_ref[...])
    k_rope = z[:, 384:512] * cos + z[:, 512:640] * sin
    k_rope = jnp.where(_iota((1, 128), 1) < 2 * MLA_ROPE, k_rope, 0.0)
    k_ref[...] = _bf(jnp.concatenate([kvl[:, 0:128], k_rope, kvl[:, 128:256], k_rope], axis=1))
    vt = _bf(kvl[:, 256:512].T)
    row_head = _div_pow2(_iota((256, 1), 0), 64)
    vt_ref[0] = jnp.concatenate([jnp.where(row_head == h, vt, jnp.zeros((), BF16)) for h in range(N_HEADS)], axis=1)


def _mla_prep(z_mla, qg, kvg, wq, wkv, layer, cos, sin, lp):
    t = z_mla.shape[0]
    tm = ROW_TILE
    nb = lp // tm
    row = lambda b, i: (b * nb + i, 0)
    return pl.pallas_call(
        _mla_prep_kernel,
        grid=(t // lp, nb),
        in_specs=[pl.BlockSpec((tm, W_MLA), row),
                  _const_spec((1, MLA_Q_RANK)), _const_spec((1, MLA_KV_RANK)),
                  _layer_spec(wq, layer), _layer_spec(wkv, layer),
                  pl.BlockSpec((tm, 128), lambda b, i: (i, 0)),
                  pl.BlockSpec((tm, 128), lambda b, i: (i, 0))],
        out_specs=[pl.BlockSpec((tm, 512), row), pl.BlockSpec((tm, 512), row),
                   pl.BlockSpec((1, 256, N_HEADS * tm), lambda b, i: (b * nb + i, 0, 0))],
        out_shape=[jax.ShapeDtypeStruct((t, 512), BF16), jax.ShapeDtypeStruct((t, 512), BF16),
                   jax.ShapeDtypeStruct((t // tm, 256, N_HEADS * tm), BF16)],
        compiler_params=_params(("parallel", "parallel")),
        name="mla_prep",
    )(z_mla, qg, kvg, wq, wkv, cos, sin)


def _per_head_lanes(cols):
    lane = _iota((1, 256), 1)
    return jnp.where(lane < 64, cols[0], jnp.where(lane < 128, cols[1], jnp.where(lane < 192, cols[2], cols[3])))


def _head_stacked_values(vb):
    zero = jnp.zeros((), vb.dtype)
    return jnp.concatenate([jnp.where(_head_lane_mask(256, 64, h), vb, zero) for h in range(N_HEADS)], axis=0)


def _per_head_rows(rows, n):
    return jnp.concatenate([jnp.broadcast_to(r, (64, n)) for r in rows], axis=0)


def _mla_attn_kernel(q_ref, k_ref, vt_ref, o_ref, qh_ref, m_ref, l_ref, acc_ref, p_ref, alpha_ref, *, tile):
    i = pl.program_id(1)
    q_pos = i * tile + _iota((1, tile), 1)
    lane = _iota((1, 256), 1)
    for h in range(N_HEADS):
        half, slot = h // 2, h % 2
        head_lanes = (((lane >= slot * MLA_NOPE) & (lane < (slot + 1) * MLA_NOPE))
                      | ((lane >= 128 + slot * MLA_ROPE) & (lane < 128 + (slot + 1) * MLA_ROPE)))
        qh_ref[half, slot * tile:(slot + 1) * tile, :] = jnp.where(
            head_lanes, q_ref[:, half * 256:(half + 1) * 256], jnp.zeros((), BF16))
    m_ref[...] = jnp.full(m_ref.shape, NEG_INF, F32)
    l_ref[...] = jnp.zeros(l_ref.shape, F32)
    acc_ref[...] = jnp.zeros(acc_ref.shape, F32)

    def add_values(j, slot):
        pv = jnp.dot(vt_ref[j], p_ref[slot], preferred_element_type=F32)
        acc_ref[...] = _per_head_rows([alpha_ref[slot, h] for h in range(N_HEADS)], tile) * acc_ref[...] + pv

    def block(j, masked, prev, slot):
        start = pl.multiple_of(j * tile, tile)
        heads = range(N_HEADS)
        s_pair = [_dot_nt(k_ref[pl.ds(start, tile), half * 256:(half + 1) * 256], qh_ref[half]) for half in (0, 1)]
        s = [s_pair[h // 2][:, (h % 2) * tile:(h % 2 + 1) * tile] for h in heads]
        if prev is not None:
            add_values(prev, 1 - slot)
        for h in heads:
            s_h = s[h]
            if masked:
                k_pos = start + _iota((tile, 1), 0)
                s_h = jnp.where((k_pos <= q_pos) & (k_pos >= PAD), s_h, NEG_INF)
            m_old = m_ref[h]
            m_new = jnp.maximum(m_old, jnp.max(s_h, axis=0, keepdims=True))
            p = jnp.exp(s_h - m_new)
            alpha = jnp.exp(m_old - m_new)
            l_ref[h] = alpha * l_ref[h] + jnp.sum(p, axis=0, keepdims=True)
            m_ref[h] = m_new
            p_ref[slot, h * tile:(h + 1) * tile, :] = _bf(p)
            alpha_ref[slot, h] = alpha

    block(i, True, None, 0)

    @pl.when(i > 0)
    def _():
        block(0, True, i, 1)

    interior_blocks = jnp.maximum(i - 1, 0)

    def interior_pair(t, carry):
        j = 1 + 2 * t
        block(j, False, j - 1, 0)
        block(j + 1, False, j, 1)
        return carry

    lax.fori_loop(0, lax.shift_right_logical(interior_blocks, 1), interior_pair, 0)
    odd = (interior_blocks & 1) == 1

    @pl.when(odd)
    def _():
        block(i - 1, False, i - 2, 0)

    last = jnp.maximum(i - 1, 0)
    last_in_slot0 = (i == 0) | odd

    @pl.when(last_in_slot0)
    def _():
        add_values(last, 0)

    @pl.when(jnp.logical_not(last_in_slot0))
    def _():
        add_values(last, 1)
    o_ref[...] = _bf((acc_ref[...] / _per_head_rows([l_ref[h] for h in range(N_HEADS)], tile)).T)


def _mla_attn(q, k, vt, lp):
    t = q.shape[0]
    tile = ATTN_TILE
    nb = lp // tile
    return pl.pallas_call(
        functools.partial(_mla_attn_kernel, tile=tile),
        grid=(t // lp, nb),
        in_specs=[pl.BlockSpec((tile, 512), lambda b, i: (b * nb + i, 0)),
                  pl.BlockSpec((lp, 512), lambda b, i: (b, 0)),
                  pl.BlockSpec((nb, 256, N_HEADS * tile), lambda b, i: (b, 0, 0))],
        out_specs=pl.BlockSpec((tile, 256), lambda b, i: (b * nb + i, 0)),
        out_shape=jax.ShapeDtypeStruct((t, 256), BF16),
        scratch_shapes=[pltpu.VMEM((2, 2 * tile, 256), BF16), pltpu.VMEM((N_HEADS, 1, tile), F32),
                        pltpu.VMEM((N_HEADS, 1, tile), F32), pltpu.VMEM((256, tile), F32),
                        pltpu.VMEM((2, N_HEADS * tile, tile), BF16), pltpu.VMEM((2, N_HEADS, 1, tile), F32)],
        compiler_params=_params(("parallel", "arbitrary")),
        name="mla_attn",
    )(q, k, vt)


def _sb_attn_kernel(q_ref, k_ref, v_ref, later_ref, o_ref, qh_ref, c_ref, acc_ref, wts_ref, *, tile):
    i = pl.program_id(1)
    row = i * tile + _iota((tile, 1), 0)
    for h in range(N_HEADS):
        qh_ref[h] = jnp.where(_head_lane_mask(SB_W, SB_HEAD, h), q_ref[...], jnp.zeros((), BF16))
    c_ref[...] = jnp.zeros(c_ref.shape, F32)
    acc_ref[...] = jnp.zeros(acc_ref.shape, F32)

    def add_values(j, slot):
        acc_ref[...] += jnp.dot(wts_ref[slot], v_ref[j], preferred_element_type=F32)

    def block(j, masked, prev, slot):
        start = pl.multiple_of(j * tile, tile)
        heads = range(N_HEADS)
        cut = SB_SUFFIX_SPLIT
        z = [_dot_nt(qh_ref[h], k_ref[pl.ds(start, tile), :]) for h in heads]
        if prev is not None:
            add_values(prev, 1 - slot)
        log_take = [jnp.minimum(z[h], 0.0) - jnp.log2(1.0 + jnp.exp2(-jnp.abs(z[h]))) for h in heads]
        log_keep = [log_take[h] - z[h] for h in heads]
        if masked:
            mask = (start + _iota((1, tile), 1)) < row
            log_keep = [jnp.where(mask, log_keep[h], 0.0) for h in heads]
        keep16 = [_bf(log_keep[h]) for h in heads]
        later_head = [jnp.dot(keep16[h][:, :cut], later_ref[...], preferred_element_type=F32) for h in heads]
        later_tail = [jnp.dot(keep16[h][:, cut:], later_ref[:tile - cut, :tile - cut], preferred_element_type=F32)
                      for h in heads]
        for h in heads:
            head_sum = jnp.sum(log_keep[h][:, :cut], axis=-1, keepdims=True)
            tail_sum = jnp.sum(log_keep[h][:, cut:], axis=-1, keepdims=True)
            c = c_ref[h]
            later = jnp.concatenate([later_head[h] + tail_sum, later_tail[h]], axis=1)
            w = jnp.exp2(log_take[h] + later + c)
            if masked:
                w = jnp.where(mask, w, 0.0)
            c_ref[h] = c + (head_sum + tail_sum)
            wts_ref[slot, :, h * tile:(h + 1) * tile] = _bf(w)

    block(i, True, None, 0)

    def below_pair(t, carry):
        j = i - 1 - 2 * t
        block(j, False, j + 1, 1)
        block(j - 1, False, j, 0)
        return carry

    lax.fori_loop(0, lax.shift_right_logical(i, 1), below_pair, 0)
    odd = (i & 1) == 1

    @pl.when(odd)
    def _():
        block(0, False, 1, 1)
        add_values(0, 1)

    @pl.when(jnp.logical_not(odd))
    def _():
        add_values(0, 0)
    o_ref[...] = _bf(acc_ref[...])


def _sb_attn(z_sb, sb_values, later_mat, lp):
    t = z_sb.shape[0]
    tile = ATTN_TILE
    nb = lp // tile
    return pl.pallas_call(
        functools.partial(_sb_attn_kernel, tile=tile),
        grid=(t // lp, nb),
        in_specs=[pl.BlockSpec((tile, SB_W), lambda b, i: (b * nb + i, 0)),
                  pl.BlockSpec((lp, SB_W), lambda b, i: (b, 1)),
                  pl.BlockSpec((nb, N_HEADS * tile, SB_W), lambda b, i: (b, 0, 0)),
                  _const_spec((SB_SUFFIX_SPLIT, SB_SUFFIX_SPLIT))],
        out_specs=pl.BlockSpec((tile, SB_W), lambda b, i: (b * nb + i, 0)),
        out_shape=jax.ShapeDtypeStruct((t, SB_W), BF16),
        scratch_shapes=[pltpu.VMEM((N_HEADS, tile, SB_W), BF16), pltpu.VMEM((N_HEADS, tile, 1), F32),
                        pltpu.VMEM((tile, SB_W), F32), pltpu.VMEM((2, tile, N_HEADS * tile), BF16)],
        compiler_params=_params(("parallel", "arbitrary")),
        name="sb_attn",
    )(z_sb, z_sb, sb_values, later_mat)


def _compact_eye(c):
    return jnp.where(_iota((c, N_HEADS * c), 0) == (_iota((c, N_HEADS * c), 1) & (c - 1)), 1.0, 0.0)


def _rw_prep_kernel(z_ref, zprev_ref, mu_ref, w0_ref, w2_ref, a0_ref, a2_ref, g2_ref, kk_ref, ka_ref, rk_ref,
                    w_o, rt_o, arb_o, kbar_o, bbar_o, v_o, uv_o, yv_o, dec_o, g_o, bonus_o):
    i = pl.program_id(1)
    z = z_ref[...]
    tm = z.shape[0]
    prev = jnp.where(i == 0, 0.0, zprev_ref[7:8, :])
    shifted = jnp.where(_iota((tm, 1), 0) == 0, prev, pltpu.roll(z, 1, 0))
    zz = z + (shifted - z) * mu_ref[...]
    r, k, v = zz[:, 0:256], zz[:, 256:512], zz[:, 512:768]
    lora_in = zz[:, 768:896]
    u = w0_ref[...] + _dot(jnp.tanh(lora_in), w2_ref[...])
    w = jnp.minimum(u, 0.0) - jnp.log1p(jnp.exp(-jnp.abs(u))) - 0.5
    a = jax.nn.sigmoid(a0_ref[...] + _dot(lora_in, a2_ref[...]))
    seg = _segment_matrix(RW_W, 64, 1.0)
    kx = k * kk_ref[...]
    kap = kx / jnp.maximum(jnp.sqrt(_dot_exact_rhs(kx * kx, seg)), 1e-12)
    kmod = k * (1.0 + (a - 1.0) * ka_ref[...])
    beta = kap * a
    lw = -jnp.exp(w)
    g_o[...] = _dot(jax.nn.sigmoid(zz[:, 896:1024]), g2_ref[...])
    bonus_o[...] = _dot_exact_rhs(r * kmod * rk_ref[...], seg) * v

    c = RW_CHUNK
    same_chunk = _same_segment((tm, tm), c, c)
    lw_hi, lw_lo = _split_hi_lo(lw)
    cum = jnp.where(same_chunk & (_iota((tm, tm), 0) >= _iota((tm, tm), 1)), 1.0, 0.0).astype(BF16)
    tot = jnp.where(same_chunk, 1.0, 0.0).astype(BF16)
    b = jnp.dot(cum, lw_hi, preferred_element_type=F32) + jnp.dot(cum, lw_lo, preferred_element_type=F32)
    b_end = jnp.dot(tot, lw_hi, preferred_element_type=F32) + jnp.dot(tot, lw_lo, preferred_element_type=F32)
    grow = jnp.exp(-b)
    to_end = jnp.exp(b_end - b)
    kap_t = kap * jnp.exp(b - lw)
    r_t = r * jnp.exp(b)
    beta_g = beta * grow
    k_g = kmod * grow
    rt_o[...] = _bf(r_t)
    kbar_o[...] = _bf(kmod * to_end)
    bbar_o[...] = _bf(beta * to_end)
    v_o[...] = _bf(v)
    decay_end = jnp.exp(b_end)

    lane_pos = _iota((c, N_HEADS * c), 1) & (c - 1)
    strictly_earlier = _iota((c, N_HEADS * c), 0) > lane_pos
    not_later = _iota((c, N_HEADS * c), 0) >= lane_pos
    chunks = [slice(n * c, (n + 1) * c) for n in range(tm // c)]
    pair = [_dot_nt(jnp.concatenate([kap_t[rows], r_t[rows]], axis=0),
                    jnp.concatenate([_stack_heads(beta_g[rows], 64), _stack_heads(k_g[rows], 64)], axis=0))
            for rows in chunks]
    a_kb = [jnp.where(strictly_earlier, p[0:c, 0:4 * c], 0.0) for p in pair]
    a_kk = [jnp.where(strictly_earlier, p[0:c, 4 * c:8 * c], 0.0) for p in pair]
    a_rk = [jnp.where(not_later, p[c:2 * c, 4 * c:8 * c], 0.0) for p in pair]
    for rows, p in zip(chunks, pair):
        arb_o[rows, :] = _bf(jnp.where(not_later, p[c:2 * c, 0:4 * c], 0.0))
    inv = [_compact_eye(c) - a for a in a_kb]
    power = [_dot(a, _stack_heads(a, c)) for a in a_kb]
    span = 2
    while span < c:
        inv = [t + _dot(t, _stack_heads(p, c)) for t, p in zip(inv, power)]
        span *= 2
        if span < c:
            power = [_dot(p, _stack_heads(p, c)) for p in power]
    v_st = [_stack_heads(v[rows], 64) for rows in chunks]
    kk_v = [_dot(a, vs) for a, vs in zip(a_kk, v_st)]
    for n, rows in enumerate(chunks):
        w_o[rows, :] = _bf(_dot(inv[n], _stack_heads(kap_t[rows], 64)))
        uv_o[rows, :] = _dot(inv[n], _stack_heads(kk_v[n], 64))
        yv_o[rows, :] = _dot(a_rk[n], v_st[n])
        dec_o[n * 8:(n + 1) * 8, :] = decay_end[n * c:n * c + 8]


def _rw_prep(z_rw, p, lp):
    t = z_rw.shape[0]
    tm = ROW_TILE
    nb = lp // tm
    row = lambda b, i: (b * nb + i, 0)
    prev = lambda b, i: (jnp.maximum(b * (lp // 8) + i * (tm // 8) - 1, 0), 0)
    consts = [p["rw_mu"], p["rw_w0"], p["rw_w2"], p["rw_a0"], p["rw_a2"], p["rw_g2"],
              p["rw_k_k"], p["rw_k_a"], p["rw_r_k"]]
    return pl.pallas_call(
        _rw_prep_kernel,
        grid=(t // lp, nb),
        in_specs=[pl.BlockSpec((tm, W_RW), row), pl.BlockSpec((8, W_RW), prev)]
        + [_const_spec(c.shape) for c in consts],
        out_specs=[pl.BlockSpec((tm, RW_W), row)] * 8
        + [pl.BlockSpec((tm // RW_CHUNK * 8, RW_W), row)] + [pl.BlockSpec((tm, RW_W), row)] * 2,
        out_shape=[jax.ShapeDtypeStruct((t, RW_W), BF16)] * 6 + [jax.ShapeDtypeStruct((t, RW_W), F32)] * 2
        + [jax.ShapeDtypeStruct((t // RW_CHUNK * 8, RW_W), F32)] + [jax.ShapeDtypeStruct((t, RW_W), F32)] * 2,
        compiler_params=_params(("parallel", "parallel")),
        name="rw_prep",
    )(z_rw, z_rw, *consts)


def _rw_scan_kernel(w_ref, rt_ref, arb_ref, kbar_ref, bbar_ref, v_ref, uv_ref, yv_ref, dec_ref, y_ref, ht_ref):
    c = RW_CHUNK

    @pl.when(pl.program_id(0) == 0)
    def _():
        ht_ref[...] = jnp.zeros_like(ht_ref)

    same_head = _same_segment((RW_W, RW_W), 64, 64)
    seqs = range(ht_ref.shape[0])
    ht = [ht_ref[s] for s in seqs]
    from_state = [_dot_nt(jnp.concatenate([w_ref[s], rt_ref[s]], axis=0), ht[s]) for s in seqs]
    u = [from_state[s][0:c] + uv_ref[s] for s in seqs]
    from_u = [_dot(arb_ref[s], _stack_heads(u[s], 64)) for s in seqs]
    upd = [_dot_tn(jnp.concatenate([v_ref[s], _bf(u[s])], axis=0),
                   jnp.concatenate([kbar_ref[s], -bbar_ref[s]], axis=0)) for s in seqs]
    for s in seqs:
        y_ref[s] = from_state[s][c:2 * c] + yv_ref[s] - from_u[s]
        ht_ref[s] = ht[s] * dec_ref[s, 0:1, :] + jnp.where(same_head, upd[s], 0.0)


def _rw_scan(w, rt, arb, kbar, bbar, v, uv, yv, dec, batch):
    t = w.shape[0]
    lp = t // batch
    c = RW_CHUNK
    as_seq = lambda a: a.reshape(batch, a.shape[0] // batch, RW_W)
    spec = pl.BlockSpec((batch, c, RW_W), lambda i: (0, i, 0))
    y = pl.pallas_call(
        _rw_scan_kernel,
        grid=(lp // c,),
        in_specs=[spec] * 8 + [pl.BlockSpec((batch, 8, RW_W), lambda i: (0, i, 0))],
        out_specs=spec,
        out_shape=jax.ShapeDtypeStruct((batch, lp, RW_W), F32),
        scratch_shapes=[pltpu.VMEM((batch, RW_W, RW_W), F32)],
        compiler_params=_params(("arbitrary",)),
        name="rw_scan",
    )(*map(as_seq, (w, rt, arb, kbar, bbar, v, uv, yv, dec)))
    return y.reshape(t, RW_W)


def _gla_kernel(q_ref, k_ref, v_ref, al_ref, a2_ref, ab_ref, o_ref, st_ref):
    c = GLA_CHUNK
    n = N_HEADS * c
    seqs = range(st_ref.shape[0])

    @pl.when(pl.program_id(0) == 0)
    def _():
        st_ref[...] = jnp.zeros_like(st_ref)

    tri = jnp.where(_iota((c, c), 0) >= _iota((c, c), 1), 1.0, 0.0).astype(BF16)
    causal = (_iota((n, c), 0) & (c - 1)) >= _iota((n, c), 1)
    same_head = _same_segment((GLA_W, GLA_QK), 64, GLA_DK)
    x = [_dot(al_ref[s], a2_ref[...]) + ab_ref[...] for s in seqs]
    log_a = [(jnp.minimum(x[s], 0.0) - jnp.log1p(jnp.exp(-jnp.abs(x[s])))) * (1.0 / GLA_TAU) for s in seqs]
    b = [_dot_exact_lhs(tri, log_a[s]) for s in seqs]
    q = [q_ref[s] * (GLA_DK ** -0.5) for s in seqs]
    st = [st_ref[s] for s in seqs]
    inter = [_dot_nt(q[s] * jnp.exp(b[s]), st[s]) for s in seqs]
    scores = []
    for s in seqs:
        b_mid = b[s][c // 2 - 1:c // 2, :]
        sc = _dot_nt(_stack_heads(q[s] * jnp.exp(b[s] - b_mid), GLA_DK), k_ref[s] * jnp.exp(b_mid - b[s]))
        scores.append(jnp.where(causal, sc, 0.0))
    per_head = [_dot(scores[s], v_ref[s]) for s in seqs]
    upd = [_dot_tn(v_ref[s], k_ref[s] * jnp.exp(b[s][c - 1:c, :] - b[s])) for s in seqs]
    for s in seqs:
        intra = jnp.zeros((c, GLA_W), F32)
        for h in range(N_HEADS):
            intra = intra + jnp.where(_head_lane_mask(GLA_W, 64, h), per_head[s][h * c:(h + 1) * c], 0.0)
        o_ref[s] = inter[s] + intra
        st_ref[s] = st[s] * jnp.exp(b[s][c - 1:c, :]) + jnp.where(same_head, upd[s], 0.0)


def _gla(z_gla, a2, ab, batch):
    t = z_gla.shape[0]
    lp = t // batch
    c = GLA_CHUNK
    z3 = z_gla.reshape(batch, lp, W_GLA)
    o = pl.pallas_call(
        _gla_kernel,
        grid=(lp // c,),
        in_specs=[pl.BlockSpec((batch, c, 128), lambda i: (0, i, 0)),
                  pl.BlockSpec((batch, c, 128), lambda i: (0, i, 1)),
                  pl.BlockSpec((batch, c, 256), lambda i: (0, i, 1)),
                  pl.BlockSpec((batch, c, 128), lambda i: (0, i, 6)),
                  _const_spec(a2.shape), _const_spec(ab.shape)],
        out_specs=pl.BlockSpec((batch, c, GLA_W), lambda i: (0, i, 0)),
        out_shape=jax.ShapeDtypeStruct((batch, lp, GLA_W), F32),
        scratch_shapes=[pltpu.VMEM((batch, GLA_W, GLA_QK), F32)],
        compiler_params=_params(("arbitrary",)),
        name="gla",
    )(z3, z3, z3, z3, a2, ab)
    return o.reshape(t, GLA_W)


def _merge_kernel(h_ref, ymla_ref, yrw_ref, bonus_ref, g_ref, ysb_ref, ogla_ref, rgla_ref,
                  nmix_ref, lnw_ref, lnb_ref, gn_ref, gb_ref, wg_ref, wb_ref, wo_ref, o_ref, *, tiles_per_seq):
    tm = h_ref.shape[0]
    avg = _segment_matrix(256, 64, 1.0 / 64)
    h = h_ref[...]
    n = _bf(_rms(h, EPS) * nmix_ref[...])

    y = yrw_ref[...]
    d = y - _dot_exact_rhs(y, avg)
    var = _dot_exact_rhs(d * d, avg)
    y_rw = (d * lax.rsqrt(var + RW_GN_EPS) * lnw_ref[...] + lnb_ref[...] + bonus_ref[...]) * g_ref[...]

    o = ogla_ref[...]
    r = rgla_ref[...]
    y_gla = o * lax.rsqrt(_dot_exact_rhs(o * o, avg) + EPS) * gn_ref[...] * (r * jax.nn.sigmoid(r))

    acc = jnp.zeros((tm, D_MODEL), F32)
    for m, y_m in enumerate((ymla_ref[...], y_rw, ysb_ref[...], y_gla)):
        logits = _dot_nt(n, wg_ref[m * D_MODEL:(m + 1) * D_MODEL, :])
        gate = jax.nn.sigmoid(logits + gb_ref[m:m + 1, :])
        acc = acc + gate * _dot(y_m, wb_ref[m])
    delta = _dot(acc, wo_ref[...])
    row = (pl.program_id(0) % tiles_per_seq) * tm + _iota((tm, 1), 0)
    o_ref[...] = h + jnp.where(row >= PAD, delta, 0.0)


def _merge(h, y_mla, y_rw, bonus, g, y_sb, o_gla, z_gla, p, lp):
    t = h.shape[0]
    tm = ROW_TILE
    row = lambda i: (i, 0)
    w256 = pl.BlockSpec((tm, 256), row)
    consts = [p["norm_mix"], p["rw_ln_w"], p["rw_ln_b"], p["gla_norm"], p["gate_b"]]
    stacked = [p["w_gate"], p["w_branch"], p["w_out"]]
    return pl.pallas_call(
        functools.partial(_merge_kernel, tiles_per_seq=lp // tm),
        grid=(t // tm,),
        in_specs=[pl.BlockSpec((tm, D_MODEL), row), w256, w256, w256, w256, w256, w256,
                  pl.BlockSpec((tm, 256), lambda i: (i, 2))]
        + [_const_spec(c.shape) for c in consts]
        + [_layer_spec(w, p["layer"]) for w in stacked],
        out_specs=pl.BlockSpec((tm, D_MODEL), row),
        out_shape=jax.ShapeDtypeStruct((t, D_MODEL), F32),
        compiler_params=_params(("parallel",)),
        name="merge",
    )(h, y_mla, y_rw, bonus, g, y_sb, o_gla, z_gla, *consts, *stacked)


def _ffn_kernel(h_ref, g_ref, win_ref, cw_ref, cb_ref, wout_ref, o_ref, tail_ref):
    @pl.when(pl.program_id(1) == 0)
    def _():
        tail_ref[...] = jnp.zeros_like(tail_ref)

    x = h_ref[...]
    tm = x.shape[0]
    n = _bf(_rms(x, EPS) * g_ref[...])
    rowi = _iota((tm, 1), 0)
    acc = jnp.zeros((tm, D_MODEL), F32)
    for c0 in range(0, D_FF, FFN_COL_CHUNK):
        c1 = c0 + FFN_COL_CHUNK
        a = jnp.dot(n, win_ref[:, c0:c1], preferred_element_type=F32)
        u = jnp.dot(n, win_ref[:, D_FF + c0:D_FF + c1], preferred_element_type=F32)
        prev1 = tail_ref[7:8, c0:c1]
        prev2 = tail_ref[6:7, c0:c1]
        a1 = jnp.where(rowi == 0, prev1, pltpu.roll(a, 1, 0))
        a2 = jnp.where(rowi == 0, prev2, jnp.where(rowi == 1, prev1, pltpu.roll(a, 2, 0)))
        tail_ref[:, c0:c1] = a[tm - 8:tm, :]
        conv = cb_ref[:, c0:c1] + cw_ref[0:1, c0:c1] * a2 + cw_ref[1:2, c0:c1] * a1 + cw_ref[2:3, c0:c1] * a
        acc = acc + _dot(conv * jax.nn.sigmoid(conv) * u, wout_ref[c0:c1, :])
    o_ref[...] = x + acc


def _ffn(h, g, w_in, conv_w, conv_b, w_out, layer, lp):
    t = h.shape[0]
    tm = ROW_TILE
    nb = lp // tm
    row = lambda b, i: (b * nb + i, 0)
    return pl.pallas_call(
        _ffn_kernel,
        grid=(t // lp, nb),
        in_specs=[pl.BlockSpec((tm, D_MODEL), row), _const_spec((1, D_MODEL)), _layer_spec(w_in, layer),
                  _const_spec(conv_w.shape), _const_spec(conv_b.shape), _layer_spec(w_out, layer)],
        out_specs=pl.BlockSpec((tm, D_MODEL), row),
        out_shape=jax.ShapeDtypeStruct((t, D_MODEL), F32),
        scratch_shapes=[pltpu.VMEM((8, D_FF), F32)],
        compiler_params=_params(("arbitrary", "arbitrary")),
        name="conv_ffn",
    )(h, g, w_in, conv_w, conv_b, w_out)


def _final_norm_kernel(h_ref, g_ref, o_ref):
    o_ref[0] = _rms(h_ref[...], EPS) * g_ref[...]


def _final_norm(h, g, batch, seq):
    lp = h.shape[0] // batch
    rows = FINAL_ROWS
    assert seq % rows == 0
    first = lambda bi, i: (pl.multiple_of(bi * lp + (PAD + N_META) + i * rows, BLOCK), 0)
    return pl.pallas_call(
        _final_norm_kernel,
        grid=(batch, seq // rows),
        in_specs=[pl.BlockSpec((pl.Element(rows), pl.Element(D_MODEL)), first), _const_spec((1, D_MODEL))],
        out_specs=pl.BlockSpec((1, rows, D_MODEL), lambda bi, i: (bi, i, 0)),
        out_shape=jax.ShapeDtypeStruct((batch, seq, D_MODEL), F32),
        compiler_params=_params(("parallel", "parallel")),
        name="final_norm",
    )(h, g)


def _rope_swap(w):
    half = w.shape[-1] // 2
    return jnp.concatenate([-w[..., half:], w[..., :half]], axis=-1)


def _in_weight_layout_kernel(w_ref, o_ref, gate_ref):
    def put(dst, val):
        o_ref[0, dst:dst + val.shape[0], :] = _bf(val)

    put(0, w_ref[0, 0:384, :])
    half = MLA_ROPE // 2
    kr = w_ref[0, 384:384 + MLA_ROPE, :]
    kr_swapped = jnp.concatenate([-kr[half:], kr[:half]], axis=0)
    put(384, jnp.concatenate([kr] * 4, axis=0))
    put(512, jnp.concatenate([kr_swapped] * 4, axis=0))
    put(W_MLA, w_ref[0, 416:1440, :])
    sb = W_MLA + W_RW
    put(sb, w_ref[0, 1440:1440 + SB_W, :] * (SB_HEAD ** -0.5 * LOG2_E))
    put(sb + SB_W, w_ref[0, 1440 + SB_W:2208, :])
    gla = sb + W_SB
    put(gla, w_ref[0, 2208:2720, :])
    put(gla + 512, w_ref[0, 2736:2992, :])
    lora = w_ref[0, 2720:2736, :]
    put(gla + 768, jnp.concatenate([lora, jnp.zeros((W_GLA - 784, lora.shape[1]), lora.dtype)], axis=0))
    gate_ref[0] = _bf(w_ref[0, 2992:2992 + W_GATE, :])


def _in_weight_layout(w_in):
    wt = jnp.swapaxes(w_in, 1, 2)
    depth, n, d = wt.shape
    cols = 256
    total = sum(IN_GROUP_WIDTHS)
    return pl.pallas_call(
        _in_weight_layout_kernel,
        grid=(depth, d // cols),
        in_specs=[pl.BlockSpec((1, n, cols), lambda l, i: (l, 0, i))],
        out_specs=[pl.BlockSpec((1, total, cols), lambda l, i: (l, 0, i)),
                   pl.BlockSpec((1, W_GATE, cols), lambda l, i: (l, 0, i))],
        out_shape=[jax.ShapeDtypeStruct((depth, total, d), BF16), jax.ShapeDtypeStruct((depth, W_GATE, d), BF16)],
        compiler_params=_params(("parallel", "parallel")),
        name="in_weight_layout",
    )(wt)


def _layout_params(w_in, mla_w_uq, mla_w_ukv, rw_w2, rw_a2, gla_a2):
    w_all, w_gate = _in_weight_layout(w_in)

    depth = w_in.shape[0]
    wuq = mla_w_uq.reshape(depth, MLA_Q_RANK, N_HEADS, MLA_NOPE + MLA_ROPE)
    nope, rope = wuq[..., :MLA_NOPE], wuq[..., MLA_NOPE:]
    rope_sw = _rope_swap(rope)
    zeros64 = jnp.zeros((depth, MLA_Q_RANK, 64), w_in.dtype)
    pair = lambda x, a, b: jnp.concatenate([x[:, :, a], x[:, :, b]], axis=-1)
    rope_pair = lambda x, a, b: jnp.concatenate([x[:, :, a], x[:, :, b], zeros64], axis=-1)
    wq = _bf(jnp.concatenate([pair(nope, 0, 1), rope_pair(rope, 0, 1), pair(nope, 2, 3), rope_pair(rope, 2, 3),
                              rope_pair(rope_sw, 0, 1), rope_pair(rope_sw, 2, 3)], axis=-1))
    wukv = mla_w_ukv.reshape(depth, MLA_KV_RANK, N_HEADS, 128)
    wkv = _bf(jnp.concatenate([wukv[..., :64].reshape(depth, MLA_KV_RANK, 256),
                               wukv[..., 64:].reshape(depth, MLA_KV_RANK, 256)], axis=-1))

    z64 = jnp.zeros_like(rw_w2)
    w2 = _bf(jnp.concatenate([rw_w2, z64], axis=1))
    a2 = _bf(jnp.concatenate([z64, rw_a2], axis=1))
    gla_a2p = _bf(jnp.concatenate([gla_a2, jnp.zeros((depth, 128 - gla_a2.shape[1], GLA_QK), gla_a2.dtype)], axis=1))
    return w_all, w_gate, wq, wkv, w2, a2, gla_a2p


def _rope_tables(lp):
    half = MLA_ROPE // 2
    freqs = ROPE_THETA ** (-jnp.arange(half, dtype=F32) / half)
    pos = (jnp.arange(lp) - PAD).astype(F32)
    ang = pos[:, None] * freqs[None, :]
    return jnp.tile(jnp.cos(ang), (1, 128 // half)), jnp.tile(jnp.sin(ang), (1, 128 // half))


def kernel(x, meta_tokens, norm_mix, w_in, mla_q_norm, mla_w_uq, mla_kv_norm, mla_w_ukv, rw_mu, rw_w0, rw_w2, rw_a0, rw_a2, rw_g2, rw_k_k, rw_k_a, rw_r_k, rw_ln_w, rw_ln_b, gla_a2, gla_a_b, gla_norm, gate_b, w_branch, w_out, norm_ffn, w_ffn_in, ffn_conv_w, ffn_conv_b, w_ffn_out, norm_final):
    batch, seq, _ = x.shape
    depth = w_in.shape[0]
    lp = PAD + N_META + seq
    t = batch * lp
    assert lp % ROW_TILE == 0 and lp % ATTN_TILE == 0 and lp % GLA_CHUNK == 0 and lp % RW_CHUNK == 0

    w_all, w_gate, wq, wkv, rw_w2p, rw_a2p, gla_a2p = _layout_params(
        w_in, mla_w_uq, mla_w_ukv, rw_w2, rw_a2, gla_a2)
    w_branch_b, w_out_b, w_ffn_in_b, w_ffn_out_b, rw_g2_b = map(_bf, (w_branch, w_out, w_ffn_in, w_ffn_out, rw_g2))
    vec = lambda a: a.reshape(depth, 1, -1)
    cos, sin = _rope_tables(lp)
    idx = jnp.arange(SB_SUFFIX_SPLIT)
    later_mat = jnp.where(idx[:, None] > idx[None, :], 1.0, 0.0).astype(BF16)

    h = _embed(x.reshape(batch * seq, D_MODEL), meta_tokens.astype(x.dtype), batch)

    for i in range(depth):
        z_mla, z_rw, z_sb, z_gla, sb_values = _in_proj(h, vec(norm_mix)[i], w_all, i)
        q, k, v = _mla_prep(z_mla, vec(mla_q_norm)[i], vec(mla_kv_norm)[i], wq, wkv, i, cos, sin, lp)
        y_mla = _mla_attn(q, k, v, lp)
        rw = {"rw_mu": vec(rw_mu)[i], "rw_w0": vec(rw_w0)[i], "rw_w2": rw_w2p[i], "rw_a0": vec(rw_a0)[i],
              "rw_a2": rw_a2p[i], "rw_g2": rw_g2_b[i], "rw_k_k": vec(rw_k_k)[i], "rw_k_a": vec(rw_k_a)[i],
              "rw_r_k": vec(rw_r_k)[i]}
        *chunk_terms, g, bonus = _rw_prep(z_rw, rw, lp)
        y_rw = _rw_scan(*chunk_terms, batch)
        y_sb = _sb_attn(z_sb, sb_values, later_mat, lp)
        o_gla = _gla(z_gla, gla_a2p[i], vec(gla_a_b)[i], batch)
        mp = {"rw_ln_w": vec(rw_ln_w)[i], "rw_ln_b": vec(rw_ln_b)[i], "gla_norm": vec(gla_norm)[i],
              "gate_b": gate_b[i], "norm_mix": vec(norm_mix)[i], "w_gate": w_gate, "w_branch": w_branch_b,
              "w_out": w_out_b, "layer": i}
        h = _merge(h, y_mla, y_rw, bonus, g, y_sb, o_gla, z_gla, mp, lp)
        h = _ffn(h, vec(norm_ffn)[i], w_ffn_in_b, ffn_conv_w[i], vec(ffn_conv_b)[i], w_ffn_out_b, i, lp)
    return _final_norm(h, norm_final.reshape(1, D_MODEL), batch, seq)
```

```python
import functools

import jax
import jax.numpy as jnp
from jax import lax
from jax.experimental import pallas as pl
from jax.experimental.pallas import tpu as pltpu

F32 = jnp.float32
BF16 = jnp.bfloat16

D_MODEL = 1024
DEPTH = 4
N_META = 16
BLOCK = 128
PAD = (-N_META) % BLOCK
EPS = 1e-6
NEG_INF = -1e30
LOG2_E = 1.4426950408889634

N_HEADS = 4
MLA_NOPE = 64
MLA_ROPE = 32
MLA_Q_RANK = 256
MLA_KV_RANK = 128
ROPE_THETA = 10000.0

RW_W = 256
RW_GN_EPS = 64e-5
RW_CHUNK = 64

SB_W = 256
SB_HEAD = 64
SB_SUFFIX_SPLIT = 256

GLA_DK = 32
GLA_QK = 128
GLA_W = 256
GLA_TAU = 16.0
GLA_CHUNK = 128
GLA_SUB = 16

D_FF = 2816
FFN_COL_CHUNK = 1408

W_MLA, W_RW, W_SB, W_GLA, W_GATE = 640, 1024, 768, 896, 4096
IN_GROUP_WIDTHS = (W_MLA, W_RW, W_SB, W_GLA)
IN_GROUP_DTYPES = (BF16, F32, BF16, F32)

ROW_TILE = 384
ATTN_TILE = 384
FINAL_ROWS = 1024
VMEM_LIMIT = 56 * 1024 * 1024


def _bf(x):
    return x.astype(BF16)


def _dot(a, b):
    return jnp.dot(_bf(a), _bf(b), preferred_element_type=F32)


def _dot_nt(a, b):
    return lax.dot_general(_bf(a), _bf(b), (((1,), (1,)), ((), ())), preferred_element_type=F32)


def _dot_tn(a, b):
    return lax.dot_general(_bf(a), _bf(b), (((0,), (0,)), ((), ())), preferred_element_type=F32)


def _split_hi_lo(x):
    hi = _bf(x)
    lo = _bf(x - hi.astype(F32))
    return hi, lo


def _dot_exact_rhs(x, m):
    hi, lo = _split_hi_lo(x)
    return jnp.dot(hi, m, preferred_element_type=F32) + jnp.dot(lo, m, preferred_element_type=F32)


def _dot_exact_lhs(m, x):
    hi, lo = _split_hi_lo(x)
    return jnp.dot(m, hi, preferred_element_type=F32) + jnp.dot(m, lo, preferred_element_type=F32)


def _iota(shape, dim):
    return lax.broadcasted_iota(jnp.int32, shape, dim)


def _div_pow2(x, d):
    assert d & (d - 1) == 0
    return lax.shift_right_logical(x, d.bit_length() - 1)


def _same_segment(shape, row_seg, col_seg):
    return _div_pow2(_iota(shape, 0), row_seg) == _div_pow2(_iota(shape, 1), col_seg)


def _segment_matrix(n, seg, value):
    return jnp.where(_same_segment((n, n), seg, seg), value, 0.0).astype(BF16)


def _head_lane_mask(width, head_width, h):
    lane = _iota((1, width), 1)
    return (lane >= h * head_width) & (lane < (h + 1) * head_width)


def _stack_heads(x, head_width):
    w = x.shape[1]
    return jnp.concatenate(
        [jnp.where(_head_lane_mask(w, head_width, h), x, 0.0) for h in range(N_HEADS)], axis=0)


def _rms(x, eps):
    return x * lax.rsqrt(jnp.mean(x * x, axis=-1, keepdims=True) + eps)


def _const_spec(shape):
    nd = len(shape)
    return pl.BlockSpec(shape, lambda *_: (0,) * nd)


def _layer_spec(stacked, layer):
    nd = stacked.ndim - 1
    return pl.BlockSpec((None,) + stacked.shape[1:], lambda *_: (layer,) + (0,) * nd)


def _params(sem, vmem=VMEM_LIMIT):
    return pltpu.CompilerParams(dimension_semantics=sem, vmem_limit_bytes=vmem)


def _embed_kernel(x_ref, meta_ref, o_ref):
    first_real = PAD + N_META

    @pl.when(pl.program_id(1) == 0)
    def _():
        o_ref[0:PAD, :] = jnp.zeros((PAD, D_MODEL), o_ref.dtype)
        o_ref[PAD:first_real, :] = meta_ref[...]
        o_ref[first_real:, :] = x_ref[0:o_ref.shape[0] - first_real, :]

    @pl.when(pl.program_id(1) > 0)
    def _():
        o_ref[...] = x_ref[...]


def _embed(x2, meta, batch):
    seq = x2.shape[0] // batch
    first_real = PAD + N_META
    lp = first_real + seq
    tm = ROW_TILE
    window = lambda b, i: (pl.multiple_of(b * seq + jnp.maximum(i * tm - first_real, 0), BLOCK), 0)
    return pl.pallas_call(
        _embed_kernel,
        grid=(batch, lp // tm),
        in_specs=[pl.BlockSpec((pl.Element(tm), pl.Element(D_MODEL)), window), _const_spec(meta.shape)],
        out_specs=pl.BlockSpec((tm, D_MODEL), lambda b, i: (b * (lp // tm) + i, 0)),
        out_shape=jax.ShapeDtypeStruct((batch * lp, D_MODEL), x2.dtype),
        compiler_params=_params(("parallel", "parallel")),
        name="embed",
    )(x2, meta)


def _in_proj_kernel(h_ref, g_ref, w_ref, *out_refs):
    *group_refs, sb_values_ref = out_refs
    n = _bf(_rms(h_ref[...], EPS) * g_ref[...])
    off = 0
    for o_ref, width in zip(group_refs, IN_GROUP_WIDTHS):
        for c0 in range(0, width, 1024):
            c1 = min(c0 + 1024, width)
            o_ref[:, c0:c1] = _dot_nt(n, w_ref[off + c0:off + c1, :]).astype(o_ref.dtype)
        off += width
    sb_values_ref[0] = _head_stacked_values(group_refs[2][:, 2 * SB_W:3 * SB_W])


def _in_proj(h, g, w_all, layer):
    t = h.shape[0]
    tm = ROW_TILE
    return pl.pallas_call(
        _in_proj_kernel,
        grid=(t // tm,),
        in_specs=[pl.BlockSpec((tm, D_MODEL), lambda i: (i, 0)),
                  _const_spec((1, D_MODEL)),
                  _layer_spec(w_all, layer)],
        out_specs=[pl.BlockSpec((tm, w), lambda i: (i, 0)) for w in IN_GROUP_WIDTHS]
        + [pl.BlockSpec((1, N_HEADS * tm, SB_W), lambda i: (i, 0, 0))],
        out_shape=[jax.ShapeDtypeStruct((t, w), dt) for w, dt in zip(IN_GROUP_WIDTHS, IN_GROUP_DTYPES)]
        + [jax.ShapeDtypeStruct((t // tm, N_HEADS * tm, SB_W), BF16)],
        compiler_params=_params(("parallel",)),
        name="in_proj",
    )(h, g, w_all)


def _mla_prep_kernel(z_ref, qg_ref, kvg_ref, wq_ref, wkv_ref, cos_ref, sin_ref, q_ref, k_ref, vt_ref):
    z = z_ref[...].astype(F32)
    nq = _rms(z[:, 0:256], EPS) * qg_ref[...]
    nkv = _rms(z[:, 256:384], EPS) * kvg_ref[...]
    cos = cos_ref[...]
    sin = sin_ref[...]
    ql = _dot(nq, wq_ref[...])
    scale = (MLA_NOPE + MLA_ROPE) ** -0.5
    q = jnp.concatenate([ql[:, 0:128], ql[:, 128:256] * cos + ql[:, 512:640] * sin,
                         ql[:, 256:384], ql[:, 384:512] * cos + ql[:, 640:768] * sin], axis=1)
    q_ref[...] = _bf(q * scale)
    kvl = _dot(nkv, wkv_ref[...])
    k_rope = z[:, 384:512] * cos + z[:, 512:640] * sin
    k_rope = jnp.where(_iota((1, 128), 1) < 2 * MLA_ROPE, k_rope, 0.0)
    k_ref[...] = _bf(jnp.concatenate([kvl[:, 0:128], k_rope, kvl[:, 128:256], k_rope], axis=1))
    vt = _bf(kvl[:, 256:512].T)
    row_head = _div_pow2(_iota((256, 1), 0), 64)
    vt_ref[0] = jnp.concatenate([jnp.where(row_head == h, vt, jnp.zeros((), BF16)) for h in range(N_HEADS)], axis=1)


def _mla_prep(z_mla, qg, kvg, wq, wkv, layer, cos, sin, lp):
    t = z_mla.shape[0]
    tm = ROW_TILE
    nb = lp // tm
    row = lambda b, i: (b * nb + i, 0)
    return pl.pallas_call(
        _mla_prep_kernel,
        grid=(t // lp, nb),
        in_specs=[pl.BlockSpec((tm, W_MLA), row),
                  _const_spec((1, MLA_Q_RANK)), _const_spec((1, MLA_KV_RANK)),
                  _layer_spec(wq, layer), _layer_spec(wkv, layer),
                  pl.BlockSpec((tm, 128), lambda b, i: (i, 0)),
                  pl.BlockSpec((tm, 128), lambda b, i: (i, 0))],
        out_specs=[pl.BlockSpec((tm, 512), row), pl.BlockSpec((tm, 512), row),
                   pl.BlockSpec((1, 256, N_HEADS * tm), lambda b, i: (b * nb + i, 0, 0))],
        out_shape=[jax.ShapeDtypeStruct((t, 512), BF16), jax.ShapeDtypeStruct((t, 512), BF16),
                   jax.ShapeDtypeStruct((t // tm, 256, N_HEADS * tm), BF16)],
        compiler_params=_params(("parallel", "parallel")),
        name="mla_prep",
    )(z_mla, qg, kvg, wq, wkv, cos, sin)


def _head_stacked_values(vb):
    zero = jnp.zeros((), vb.dtype)
    return jnp.concatenate([jnp.where(_head_lane_mask(256, 64, h), vb, zero) for h in range(N_HEADS)], axis=0)


def _per_head_rows(rows, n):
    return jnp.concatenate([jnp.broadcast_to(r, (64, n)) for r in rows], axis=0)


def _mla_attn_kernel(q_ref, k_ref, vt_ref, o_ref, qh_ref, m_ref, l_ref, acc_ref, p_ref, alpha_ref, *, tile):
    i = pl.program_id(1)
    q_pos = i * tile + _iota((1, tile), 1)
    lane = _iota((1, 256), 1)
    for h in range(N_HEADS):
        half, slot = h // 2, h % 2
        head_lanes = (((lane >= slot * MLA_NOPE) & (lane < (slot + 1) * MLA_NOPE))
                      | ((lane >= 128 + slot * MLA_ROPE) & (lane < 128 + (slot + 1) * MLA_ROPE)))
        qh_ref[half, slot * tile:(slot + 1) * tile, :] = jnp.where(
            head_lanes, q_ref[:, half * 256:(half + 1) * 256], jnp.zeros((), BF16))
    m_ref[...] = jnp.full(m_ref.shape, NEG_INF, F32)
    l_ref[...] = jnp.zeros(l_ref.shape, F32)
    acc_ref[...] = jnp.zeros(acc_ref.shape, F32)

    def add_values(j, slot):
        pv = jnp.dot(vt_ref[j], p_ref[slot], preferred_element_type=F32)
        acc_ref[...] = _per_head_rows([alpha_ref[slot, h] for h in range(N_HEADS)], tile) * acc_ref[...] + pv

    def block(j, masked, prev, slot):
        start = pl.multiple_of(j * tile, tile)
        heads = range(N_HEADS)
        s_pair = [_dot_nt(k_ref[pl.ds(start, tile), half * 256:(half + 1) * 256], qh_ref[half]) for half in (0, 1)]
        s = [s_pair[h // 2][:, (h % 2) * tile:(h % 2 + 1) * tile] for h in heads]
        if prev is not None:
            add_values(prev, 1 - slot)
        for h in heads:
            s_h = s[h]
            if masked:
                k_pos = start + _iota((tile, 1), 0)
                s_h = jnp.where((k_pos <= q_pos) & (k_pos >= PAD), s_h, NEG_INF)
            m_old = m_ref[h]
            m_new = jnp.maximum(m_old, jnp.max(s_h, axis=0, keepdims=True))
            p = jnp.exp(s_h - m_new)
            alpha = jnp.exp(m_old - m_new)
            l_ref[h] = alpha * l_ref[h] + jnp.sum(p, axis=0, keepdims=True)
            m_ref[h] = m_new
            p_ref[slot, h * tile:(h + 1) * tile, :] = _bf(p)
            alpha_ref[slot, h] = alpha

    block(i, True, None, 0)

    @pl.when(i > 0)
    def _():
        block(0, True, i, 1)

    interior_blocks = jnp.maximum(i - 1, 0)

    def interior_pair(t, carry):
        j = 1 + 2 * t
        block(j, False, j - 1, 0)
        block(j + 1, False, j, 1)
        return carry

    lax.fori_loop(0, lax.shift_right_logical(interior_blocks, 1), interior_pair, 0)
    odd = (interior_blocks & 1) == 1

    @pl.when(odd)
    def _():
        block(i - 1, False, i - 2, 0)

    last = jnp.maximum(i - 1, 0)
    last_in_slot0 = (i == 0) | odd

    @pl.when(last_in_slot0)
    def _():
        add_values(last, 0)

    @pl.when(jnp.logical_not(last_in_slot0))
    def _():
        add_values(last, 1)
    o_ref[...] = _bf((acc_ref[...] / _per_head_rows([l_ref[h] for h in range(N_HEADS)], tile)).T)


def _mla_attn(q, k, vt, lp):
    t = q.shape[0]
    tile = ATTN_TILE
    nb = lp // tile
    return pl.pallas_call(
        functools.partial(_mla_attn_kernel, tile=tile),
        grid=(t // lp, nb),
        in_specs=[pl.BlockSpec((tile, 512), lambda b, i: (b * nb + i, 0)),
                  pl.BlockSpec((lp, 512), lambda b, i: (b, 0)),
                  pl.BlockSpec((nb, 256, N_HEADS * tile), lambda b, i: (b, 0, 0))],
        out_specs=pl.BlockSpec((tile, 256), lambda b, i: (b * nb + i, 0)),
        out_shape=jax.ShapeDtypeStruct((t, 256), BF16),
        scratch_shapes=[pltpu.VMEM((2, 2 * tile, 256), BF16), pltpu.VMEM((N_HEADS, 1, tile), F32),
                        pltpu.VMEM((N_HEADS, 1, tile), F32), pltpu.VMEM((256, tile), F32),
                        pltpu.VMEM((2, N_HEADS * tile, tile), BF16), pltpu.VMEM((2, N_HEADS, 1, tile), F32)],
        compiler_params=_params(("parallel", "arbitrary")),
        name="mla_attn",
    )(q, k, vt)


def _sb_attn_kernel(q_ref, k_ref, v_ref, later_ref, o_ref, qh_ref, c_ref, acc_ref, wts_ref, *, tile):
    i = pl.program_id(1)
    row = i * tile + _iota((tile, 1), 0)
    for h in range(N_HEADS):
        qh_ref[h] = jnp.where(_head_lane_mask(SB_W, SB_HEAD, h), q_ref[...], jnp.zeros((), BF16))
    c_ref[...] = jnp.zeros(c_ref.shape, F32)
    acc_ref[...] = jnp.zeros(acc_ref.shape, F32)

    def add_values(j, slot):
        acc_ref[...] += jnp.dot(wts_ref[slot], v_ref[j], preferred_element_type=F32)

    def block(j, masked, prev, slot):
        start = pl.multiple_of(j * tile, tile)
        heads = range(N_HEADS)
        cut = SB_SUFFIX_SPLIT
        z = [_dot_nt(qh_ref[h], k_ref[pl.ds(start, tile), :]) for h in heads]
        if prev is not None:
            add_values(prev, 1 - slot)
        log_take = [jnp.minimum(z[h], 0.0) - jnp.log2(1.0 + jnp.exp2(-jnp.abs(z[h]))) for h in heads]
        log_keep = [log_take[h] - z[h] for h in heads]
        if masked:
            mask = (start + _iota((1, tile), 1)) < row
            log_keep = [jnp.where(mask, log_keep[h], 0.0) for h in heads]
        keep16 = [_bf(log_keep[h]) for h in heads]
        later_head = [jnp.dot(keep16[h][:, :cut], later_ref[...], preferred_element_type=F32) for h in heads]
        later_tail = [jnp.dot(keep16[h][:, cut:], later_ref[:tile - cut, :tile - cut], preferred_element_type=F32)
                      for h in heads]
        for h in heads:
            head_sum = jnp.sum(log_keep[h][:, :cut], axis=-1, keepdims=True)
            tail_sum = jnp.sum(log_keep[h][:, cut:], axis=-1, keepdims=True)
            c = c_ref[h]
            later = jnp.concatenate([later_head[h] + tail_sum, later_tail[h]], axis=1)
            w = jnp.exp2(log_take[h] + later + c)
            if masked:
                w = jnp.where(mask, w, 0.0)
            c_ref[h] = c + (head_sum + tail_sum)
            wts_ref[slot, :, h * tile:(h + 1) * tile] = _bf(w)

    block(i, True, None, 0)

    def below_pair(t, carry):
        j = i - 1 - 2 * t
        block(j, False, j + 1, 1)
        block(j - 1, False, j, 0)
        return carry

    lax.fori_loop(0, lax.shift_right_logical(i, 1), below_pair, 0)
    odd = (i & 1) == 1

    @pl.when(odd)
    def _():
        block(0, False, 1, 1)
        add_values(0, 1)

    @pl.when(jnp.logical_not(odd))
    def _():
        add_values(0, 0)
    o_ref[...] = _bf(acc_ref[...])


def _sb_attn(z_sb, sb_values, later_mat, lp):
    t = z_sb.shape[0]
    tile = ATTN_TILE
    nb = lp // tile
    return pl.pallas_call(
        functools.partial(_sb_attn_kernel, tile=tile),
        grid=(t // lp, nb),
        in_specs=[pl.BlockSpec((tile, SB_W), lambda b, i: (b * nb + i, 0)),
                  pl.BlockSpec((lp, SB_W), lambda b, i: (b, 1)),
                  pl.BlockSpec((nb, N_HEADS * tile, SB_W), lambda b, i: (b, 0, 0)),
                  _const_spec((SB_SUFFIX_SPLIT, SB_SUFFIX_SPLIT))],
        out_specs=pl.BlockSpec((tile, SB_W), lambda b, i: (b * nb + i, 0)),
        out_shape=jax.ShapeDtypeStruct((t, SB_W), BF16),
        scratch_shapes=[pltpu.VMEM((N_HEADS, tile, SB_W), BF16), pltpu.VMEM((N_HEADS, tile, 1), F32),
                        pltpu.VMEM((tile, SB_W), F32), pltpu.VMEM((2, tile, N_HEADS * tile), BF16)],
        compiler_params=_params(("parallel", "arbitrary")),
        name="sb_attn",
    )(z_sb, z_sb, sb_values, later_mat)


def _compact_eye(c):
    return jnp.where(_iota((c, N_HEADS * c), 0) == (_iota((c, N_HEADS * c), 1) & (c - 1)), 1.0, 0.0)


def _rw_prep_kernel(z_ref, zprev_ref, mu_ref, w0_ref, w2_ref, a0_ref, a2_ref, g2_ref, kk_ref, ka_ref, rk_ref,
                    w_o, rt_o, arb_o, kbar_o, bbar_o, v_o, uv_o, yv_o, dec_o, g_o, bonus_o):
    i = pl.program_id(1)
    z = z_ref[...]
    tm = z.shape[0]
    prev = jnp.where(i == 0, 0.0, zprev_ref[7:8, :])
    shifted = jnp.where(_iota((tm, 1), 0) == 0, prev, pltpu.roll(z, 1, 0))
    zz = z + (shifted - z) * mu_ref[...]
    r, k, v = zz[:, 0:256], zz[:, 256:512], zz[:, 512:768]
    lora_in = zz[:, 768:896]
    u = w0_ref[...] + _dot(jnp.tanh(lora_in), w2_ref[...])
    w = jnp.minimum(u, 0.0) - jnp.log1p(jnp.exp(-jnp.abs(u))) - 0.5
    a = jax.nn.sigmoid(a0_ref[...] + _dot(lora_in, a2_ref[...]))
    seg = _segment_matrix(RW_W, 64, 1.0)
    kx = k * kk_ref[...]
    kap = kx / jnp.maximum(jnp.sqrt(_dot_exact_rhs(kx * kx, seg)), 1e-12)
    kmod = k * (1.0 + (a - 1.0) * ka_ref[...])
    beta = kap * a
    lw = -jnp.exp(w)
    g_o[...] = _dot(jax.nn.sigmoid(zz[:, 896:1024]), g2_ref[...])
    bonus_o[...] = _dot_exact_rhs(r * kmod * rk_ref[...], seg) * v

    c = RW_CHUNK
    same_chunk = _same_segment((tm, tm), c, c)
    lw_hi, lw_lo = _split_hi_lo(lw)
    cum = jnp.where(same_chunk & (_iota((tm, tm), 0) >= _iota((tm, tm), 1)), 1.0, 0.0).astype(BF16)
    b = jnp.dot(cum, lw_hi, preferred_element_type=F32) + jnp.dot(cum, lw_lo, preferred_element_type=F32)
    b_end = jnp.concatenate(
        [jnp.broadcast_to(b[n * c + c - 1:(n + 1) * c, :], (c, RW_W)) for n in range(tm // c)], axis=0)
    grow = jnp.exp(-b)
    to_end = jnp.exp(b_end - b)
    kap_t = kap * jnp.exp(b - lw)
    r_t = r * jnp.exp(b)
    beta_g = beta * grow
    k_g = kmod * grow
    rt_o[...] = _bf(r_t)
    kbar_o[...] = _bf(kmod * to_end)
    bbar_o[...] = _bf(beta * to_end)
    v_o[...] = _bf(v)
    decay_end = jnp.exp(b_end)

    lane_pos = _iota((c, N_HEADS * c), 1) & (c - 1)
    strictly_earlier = _iota((c, N_HEADS * c), 0) > lane_pos
    not_later = _iota((c, N_HEADS * c), 0) >= lane_pos
    chunks = [slice(n * c, (n + 1) * c) for n in range(tm // c)]
    pair = [_dot_nt(jnp.concatenate([kap_t[rows], r_t[rows]], axis=0),
                    jnp.concatenate([_stack_heads(beta_g[rows], 64), _stack_heads(k_g[rows], 64)], axis=0))
            for rows in chunks]
    a_kb = [jnp.where(strictly_earlier, p[0:c, 0:4 * c], 0.0) for p in pair]
    a_kk = [jnp.where(strictly_earlier, p[0:c, 4 * c:8 * c], 0.0) for p in pair]
    a_rk = [jnp.where(not_later, p[c:2 * c, 4 * c:8 * c], 0.0) for p in pair]
    for rows, p in zip(chunks, pair):
        arb_o[rows, :] = _bf(jnp.where(not_later, p[c:2 * c, 0:4 * c], 0.0))
    inv = [_compact_eye(c) - a for a in a_kb]
    power = [_dot(a, _stack_heads(a, c)) for a in a_kb]
    span = 2
    while span < c:
        inv = [t + _dot(t, _stack_heads(p, c)) for t, p in zip(inv, power)]
        span *= 2
        if span < c:
            power = [_dot(p, _stack_heads(p, c)) for p in power]
    v_st = [_stack_heads(v[rows], 64) for rows in chunks]
    kk_v = [_dot(a, vs) for a, vs in zip(a_kk, v_st)]
    for n, rows in enumerate(chunks):
        w_o[rows, :] = _bf(_dot(inv[n], _stack_heads(kap_t[rows], 64)))
        uv_o[rows, :] = _dot(inv[n], _stack_heads(kk_v[n], 64))
        yv_o[rows, :] = _dot(a_rk[n], v_st[n])
        dec_o[n * 8:(n + 1) * 8, :] = decay_end[n * c:n * c + 8]


def _rw_prep(z_rw, p, lp):
    t = z_rw.shape[0]
    tm = ROW_TILE
    nb = lp // tm
    row = lambda b, i: (b * nb + i, 0)
    prev = lambda b, i: (jnp.maximum(b * (lp // 8) + i * (tm // 8) - 1, 0), 0)
    consts = [p["rw_mu"], p["rw_w0"], p["rw_w2"], p["rw_a0"], p["rw_a2"], p["rw_g2"],
              p["rw_k_k"], p["rw_k_a"], p["rw_r_k"]]
    return pl.pallas_call(
        _rw_prep_kernel,
        grid=(t // lp, nb),
        in_specs=[pl.BlockSpec((tm, W_RW), row), pl.BlockSpec((8, W_RW), prev)]
        + [_const_spec(c.shape) for c in consts],
        out_specs=[pl.BlockSpec((tm, RW_W), row)] * 8
        + [pl.BlockSpec((tm // RW_CHUNK * 8, RW_W), row)] + [pl.BlockSpec((tm, RW_W), row)] * 2,
        out_shape=[jax.ShapeDtypeStruct((t, RW_W), BF16)] * 6 + [jax.ShapeDtypeStruct((t, RW_W), F32)] * 2
        + [jax.ShapeDtypeStruct((t // RW_CHUNK * 8, RW_W), F32)] + [jax.ShapeDtypeStruct((t, RW_W), F32)] * 2,
        compiler_params=_params(("parallel", "parallel")),
        name="rw_prep",
    )(z_rw, z_rw, *consts)


def _rw_scan_kernel(w_ref, rt_ref, arb_ref, kbar_ref, bbar_ref, v_ref, uv_ref, yv_ref, dec_ref, y_ref, ht_ref):
    c = RW_CHUNK

    @pl.when(pl.program_id(0) == 0)
    def _():
        ht_ref[...] = jnp.zeros_like(ht_ref)

    same_head = _same_segment((RW_W, RW_W), 64, 64)
    seqs = range(ht_ref.shape[0])
    ht = [ht_ref[s] for s in seqs]
    from_state = [_dot_nt(jnp.concatenate([w_ref[s], rt_ref[s]], axis=0), ht[s]) for s in seqs]
    u = [from_state[s][0:c] + uv_ref[s] for s in seqs]
    from_u = [_dot(arb_ref[s], _stack_heads(u[s], 64)) for s in seqs]
    upd = [_dot_tn(jnp.concatenate([v_ref[s], _bf(u[s])], axis=0),
                   jnp.concatenate([kbar_ref[s], -bbar_ref[s]], axis=0)) for s in seqs]
    for s in seqs:
        y_ref[s] = from_state[s][c:2 * c] + yv_ref[s] - from_u[s]
        ht_ref[s] = ht[s] * dec_ref[s, 0:1, :] + jnp.where(same_head, upd[s], 0.0)


def _rw_scan(w, rt, arb, kbar, bbar, v, uv, yv, dec, batch):
    t = w.shape[0]
    lp = t // batch
    c = RW_CHUNK
    as_seq = lambda a: a.reshape(batch, a.shape[0] // batch, RW_W)
    spec = pl.BlockSpec((batch, c, RW_W), lambda i: (0, i, 0))
    y = pl.pallas_call(
        _rw_scan_kernel,
        grid=(lp // c,),
        in_specs=[spec] * 8 + [pl.BlockSpec((batch, 8, RW_W), lambda i: (0, i, 0))],
        out_specs=spec,
        out_shape=jax.ShapeDtypeStruct((batch, lp, RW_W), F32),
        scratch_shapes=[pltpu.VMEM((batch, RW_W, RW_W), F32)],
        compiler_params=_params(("arbitrary",)),
        name="rw_scan",
    )(*map(as_seq, (w, rt, arb, kbar, bbar, v, uv, yv, dec)))
    return y.reshape(t, RW_W)


def _gla_kernel(q_ref, k_ref, v_ref, al_ref, a2_ref, ab_ref, o_ref, st_ref):
    c = GLA_CHUNK
    seqs = range(st_ref.shape[0])

    @pl.when(pl.program_id(0) == 0)
    def _():
        st_ref[...] = jnp.zeros_like(st_ref)

    sub = GLA_SUB
    nsub = c // sub
    row_i = _iota((c, c), 0)
    col_i = _iota((c, c), 1)
    tri = jnp.where(row_i >= col_i, 1.0, 0.0).astype(BF16)
    tri_sub = jnp.where((row_i >= col_i) & _same_segment((c, c), sub, sub), 1.0, 0.0).astype(BF16)
    key_pos = _iota((c, 1), 0)
    query_in_sub = _iota((N_HEADS * sub, c), 0) & (sub - 1)
    same_head = _same_segment((GLA_W, GLA_QK), 64, GLA_DK)

    x = [_dot(al_ref[s], a2_ref[...]) + ab_ref[...] for s in seqs]
    log_a = [(jnp.minimum(x[s], 0.0) - jnp.log1p(jnp.exp(-jnp.abs(x[s])))) * (1.0 / GLA_TAU) for s in seqs]
    b = [_dot_exact_lhs(tri, log_a[s]) for s in seqs]
    b_sub = [_dot_exact_lhs(tri_sub, log_a[s]) for s in seqs]
    q = [q_ref[s] * (GLA_DK ** -0.5) for s in seqs]
    st = [st_ref[s] for s in seqs]
    inter = [_dot_nt(q[s] * jnp.exp(b[s]), st[s]) for s in seqs]

    scores = []
    for s in seqs:
        q_sub = q[s] * jnp.exp(b_sub[s])
        beta = b[s] - b_sub[s]
        k = k_ref[s]
        blocks = []
        for blk in range(nsub):
            lo, hi = blk * sub, (blk + 1) * sub
            expo = jnp.where(key_pos < hi, beta[lo:lo + 1, :] - b[s], NEG_INF)
            sc = _dot_nt(_stack_heads(q_sub[lo:hi], GLA_DK), k * jnp.exp(expo))
            blocks.append(jnp.where(_iota((N_HEADS * sub, c), 1) <= lo + query_in_sub, sc, 0.0))
        scores.append(jnp.concatenate(blocks, axis=0))
    per_head = [_dot(scores[s], v_ref[s]) for s in seqs]
    upd = [_dot_tn(v_ref[s], k_ref[s] * jnp.exp(b[s][c - 1:c, :] - b[s])) for s in seqs]
    for s in seqs:
        pieces = []
        for blk in range(nsub):
            piece = jnp.zeros((sub, GLA_W), F32)
            for h in range(N_HEADS):
                r0 = (blk * N_HEADS + h) * sub
                piece = piece + jnp.where(_head_lane_mask(GLA_W, 64, h), per_head[s][r0:r0 + sub], 0.0)
            pieces.append(piece)
        o_ref[s] = inter[s] + jnp.concatenate(pieces, axis=0)
        st_ref[s] = st[s] * jnp.exp(b[s][c - 1:c, :]) + jnp.where(same_head, upd[s], 0.0)


def _gla(z_gla, a2, ab, batch):
    t = z_gla.shape[0]
    lp = t // batch
    c = GLA_CHUNK
    z3 = z_gla.reshape(batch, lp, W_GLA)
    o = pl.pallas_call(
        _gla_kernel,
        grid=(lp // c,),
        in_specs=[pl.BlockSpec((batch, c, 128), lambda i: (0, i, 0)),
                  pl.BlockSpec((batch, c, 128), lambda i: (0, i, 1)),
                  pl.BlockSpec((batch, c, 256), lambda i: (0, i, 1)),
                  pl.BlockSpec((batch, c, 128), lambda i: (0, i, 6)),
                  _const_spec(a2.shape), _const_spec(ab.shape)],
        out_specs=pl.BlockSpec((batch, c, GLA_W), lambda i: (0, i, 0)),
        out_shape=jax.ShapeDtypeStruct((batch, lp, GLA_W), F32),
        scratch_shapes=[pltpu.VMEM((batch, GLA_W, GLA_QK), F32)],
        compiler_params=_params(("arbitrary",)),
        name="gla",
    )(z3, z3, z3, z3, a2, ab)
    return o.reshape(t, GLA_W)


def _merge_kernel(h_ref, ymla_ref, yrw_ref, bonus_ref, g_ref, ysb_ref, ogla_ref, rgla_ref,
                  nmix_ref, lnw_ref, lnb_ref, gn_ref, gb_ref, wg_ref, wb_ref, wo_ref, o_ref, *, tiles_per_seq):
    tm = h_ref.shape[0]
    avg = _segment_matrix(256, 64, 1.0 / 64)
    h = h_ref[...]
    n = _bf(_rms(h, EPS) * nmix_ref[...])

    y = yrw_ref[...]
    d = y - _dot_exact_rhs(y, avg)
    var = _dot_exact_rhs(d * d, avg)
    y_rw = (d * lax.rsqrt(var + RW_GN_EPS) * lnw_ref[...] + lnb_ref[...] + bonus_ref[...]) * g_ref[...]

    o = ogla_ref[...]
    r = rgla_ref[...]
    y_gla = o * lax.rsqrt(_dot_exact_rhs(o * o, avg) + EPS) * gn_ref[...] * (r * jax.nn.sigmoid(r))

    acc = jnp.zeros((tm, D_MODEL), F32)
    for m, y_m in enumerate((ymla_ref[...], y_rw, ysb_ref[...], y_gla)):
        logits = _dot_nt(n, wg_ref[m * D_MODEL:(m + 1) * D_MODEL, :])
        gate = jax.nn.sigmoid(logits + gb_ref[m:m + 1, :])
        acc = acc + gate * _dot(y_m, wb_ref[m])
    delta = _dot(acc, wo_ref[...])
    row = (pl.program_id(0) % tiles_per_seq) * tm + _iota((tm, 1), 0)
    o_ref[...] = h + jnp.where(row >= PAD, delta, 0.0)


def _merge(h, y_mla, y_rw, bonus, g, y_sb, o_gla, z_gla, p, lp):
    t = h.shape[0]
    tm = ROW_TILE
    row = lambda i: (i, 0)
    w256 = pl.BlockSpec((tm, 256), row)
    consts = [p["norm_mix"], p["rw_ln_w"], p["rw_ln_b"], p["gla_norm"], p["gate_b"]]
    stacked = [p["w_gate"], p["w_branch"], p["w_out"]]
    return pl.pallas_call(
        functools.partial(_merge_kernel, tiles_per_seq=lp // tm),
        grid=(t // tm,),
        in_specs=[pl.BlockSpec((tm, D_MODEL), row), w256, w256, w256, w256, w256, w256,
                  pl.BlockSpec((tm, 256), lambda i: (i, 2))]
        + [_const_spec(c.shape) for c in consts]
        + [_layer_spec(w, p["layer"]) for w in stacked],
        out_specs=pl.BlockSpec((tm, D_MODEL), row),
        out_shape=jax.ShapeDtypeStruct((t, D_MODEL), F32),
        compiler_params=_params(("parallel",)),
        name="merge",
    )(h, y_mla, y_rw, bonus, g, y_sb, o_gla, z_gla, *consts, *stacked)


def _ffn_kernel(h_ref, g_ref, win_ref, cw_ref, cb_ref, wout_ref, o_ref, tail_ref):
    @pl.when(pl.program_id(1) == 0)
    def _():
        tail_ref[...] = jnp.zeros_like(tail_ref)

    x = h_ref[...]
    tm = x.shape[0]
    n = _bf(_rms(x, EPS) * g_ref[...])
    rowi = _iota((tm, 1), 0)
    acc = jnp.zeros((tm, D_MODEL), F32)
    for c0 in range(0, D_FF, FFN_COL_CHUNK):
        c1 = c0 + FFN_COL_CHUNK
        a = jnp.dot(n, win_ref[:, c0:c1], preferred_element_type=F32)
        u = jnp.dot(n, win_ref[:, D_FF + c0:D_FF + c1], preferred_element_type=F32)
        prev1 = tail_ref[7:8, c0:c1]
        prev2 = tail_ref[6:7, c0:c1]
        a1 = jnp.where(rowi == 0, prev1, pltpu.roll(a, 1, 0))
        a2 = jnp.where(rowi == 0, prev2, jnp.where(rowi == 1, prev1, pltpu.roll(a, 2, 0)))
        tail_ref[:, c0:c1] = a[tm - 8:tm, :]
        conv = cb_ref[:, c0:c1] + cw_ref[0:1, c0:c1] * a2 + cw_ref[1:2, c0:c1] * a1 + cw_ref[2:3, c0:c1] * a
        acc = acc + _dot(conv * jax.nn.sigmoid(conv) * u, wout_ref[c0:c1, :])
    o_ref[...] = x + acc


def _ffn(h, g, w_in, conv_w, conv_b, w_out, layer, lp):
    t = h.shape[0]
    tm = ROW_TILE
    nb = lp // tm
    row = lambda b, i: (b * nb + i, 0)
    return pl.pallas_call(
        _ffn_kernel,
        grid=(t // lp, nb),
        in_specs=[pl.BlockSpec((tm, D_MODEL), row), _const_spec((1, D_MODEL)), _layer_spec(w_in, layer),
                  _const_spec(conv_w.shape), _const_spec(conv_b.shape), _layer_spec(w_out, layer)],
        out_specs=pl.BlockSpec((tm, D_MODEL), row),
        out_shape=jax.ShapeDtypeStruct((t, D_MODEL), F32),
        scratch_shapes=[pltpu.VMEM((8, D_FF), F32)],
        compiler_params=_params(("arbitrary", "arbitrary")),
        name="conv_ffn",
    )(h, g, w_in, conv_w, conv_b, w_out)


def _final_norm_kernel(h_ref, g_ref, o_ref):
    o_ref[0] = _rms(h_ref[...], EPS) * g_ref[...]


def _final_norm(h, g, batch, seq):
    lp = h.shape[0] // batch
    rows = FINAL_ROWS
    assert seq % rows == 0
    first = lambda bi, i: (pl.multiple_of(bi * lp + (PAD + N_META) + i * rows, BLOCK), 0)
    return pl.pallas_call(
        _final_norm_kernel,
        grid=(batch, seq // rows),
        in_specs=[pl.BlockSpec((pl.Element(rows), pl.Element(D_MODEL)), first), _const_spec((1, D_MODEL))],
        out_specs=pl.BlockSpec((1, rows, D_MODEL), lambda bi, i: (bi, i, 0)),
        out_shape=jax.ShapeDtypeStruct((batch, seq, D_MODEL), F32),
        compiler_params=_params(("parallel", "parallel")),
        name="final_norm",
    )(h, g)


def _rope_swap(w):
    half = w.shape[-1] // 2
    return jnp.concatenate([-w[..., half:], w[..., :half]], axis=-1)


def _in_weight_layout_kernel(w_ref, o_ref, gate_ref):
    def put(dst, val):
        o_ref[0, dst:dst + val.shape[0], :] = _bf(val)

    put(0, w_ref[0, 0:384, :])
    half = MLA_ROPE // 2
    kr = w_ref[0, 384:384 + MLA_ROPE, :]
    kr_swapped = jnp.concatenate([-kr[half:], kr[:half]], axis=0)
    put(384, jnp.concatenate([kr] * 4, axis=0))
    put(512, jnp.concatenate([kr_swapped] * 4, axis=0))
    put(W_MLA, w_ref[0, 416:1440, :])
    sb = W_MLA + W_RW
    put(sb, w_ref[0, 1440:1440 + SB_W, :] * (SB_HEAD ** -0.5 * LOG2_E))
    put(sb + SB_W, w_ref[0, 1440 + SB_W:2208, :])
    gla = sb + W_SB
    put(gla, w_ref[0, 2208:2720, :])
    put(gla + 512, w_ref[0, 2736:2992, :])
    lora = w_ref[0, 2720:2736, :]
    put(gla + 768, jnp.concatenate([lora, jnp.zeros((W_GLA - 784, lora.shape[1]), lora.dtype)], axis=0))
    gate_ref[0] = _bf(w_ref[0, 2992:2992 + W_GATE, :])


def _in_weight_layout(w_in):
    wt = jnp.swapaxes(w_in, 1, 2)
    depth, n, d = wt.shape
    cols = 256
    total = sum(IN_GROUP_WIDTHS)
    return pl.pallas_call(
        _in_weight_layout_kernel,
        grid=(depth, d // cols),
        in_specs=[pl.BlockSpec((1, n, cols), lambda l, i: (l, 0, i))],
        out_specs=[pl.BlockSpec((1, total, cols), lambda l, i: (l, 0, i)),
                   pl.BlockSpec((1, W_GATE, cols), lambda l, i: (l, 0, i))],
        out_shape=[jax.ShapeDtypeStruct((depth, total, d), BF16), jax.ShapeDtypeStruct((depth, W_GATE, d), BF16)],
        compiler_params=_params(("parallel", "parallel")),
        name="in_weight_layout",
    )(wt)


def _layout_params(w_in, mla_w_uq, mla_w_ukv, rw_w2, rw_a2, gla_a2):
    w_all, w_gate = _in_weight_layout(w_in)

    depth = w_in.shape[0]
    wuq = mla_w_uq.reshape(depth, MLA_Q_RANK, N_HEADS, MLA_NOPE + MLA_ROPE)
    nope, rope = wuq[..., :MLA_NOPE], wuq[..., MLA_NOPE:]
    rope_sw = _rope_swap(rope)
    zeros64 = jnp.zeros((depth, MLA_Q_RANK, 64), w_in.dtype)
    pair = lambda x, a, b: jnp.concatenate([x[:, :, a], x[:, :, b]], axis=-1)
    rope_pair = lambda x, a, b: jnp.concatenate([x[:, :, a], x[:, :, b], zeros64], axis=-1)
    wq = _bf(jnp.concatenate([pair(nope, 0, 1), rope_pair(rope, 0, 1), pair(nope, 2, 3), rope_pair(rope, 2, 3),
                              rope_pair(rope_sw, 0, 1), rope_pair(rope_sw, 2, 3)], axis=-1))
    wukv = mla_w_ukv.reshape(depth, MLA_KV_RANK, N_HEADS, 128)
    wkv = _bf(jnp.concatenate([wukv[..., :64].reshape(depth, MLA_KV_RANK, 256),
                               wukv[..., 64:].reshape(depth, MLA_KV_RANK, 256)], axis=-1))

    z64 = jnp.zeros_like(rw_w2)
    w2 = _bf(jnp.concatenate([rw_w2, z64], axis=1))
    a2 = _bf(jnp.concatenate([z64, rw_a2], axis=1))
    gla_a2p = _bf(jnp.concatenate([gla_a2, jnp.zeros((depth, 128 - gla_a2.shape[1], GLA_QK), gla_a2.dtype)], axis=1))
    return w_all, w_gate, wq, wkv, w2, a2, gla_a2p


def _rope_tables(lp):
    half = MLA_ROPE // 2
    freqs = ROPE_THETA ** (-jnp.arange(half, dtype=F32) / half)
    pos = (jnp.arange(lp) - PAD).astype(F32)
    ang = pos[:, None] * freqs[None, :]
    return jnp.tile(jnp.cos(ang), (1, 128 // half)), jnp.tile(jnp.sin(ang), (1, 128 // half))


def kernel(x, meta_tokens, norm_mix, w_in, mla_q_norm, mla_w_uq, mla_kv_norm, mla_w_ukv, rw_mu, rw_w0, rw_w2, rw_a0, rw_a2, rw_g2, rw_k_k, rw_k_a, rw_r_k, rw_ln_w, rw_ln_b, gla_a2, gla_a_b, gla_norm, gate_b, w_branch, w_out, norm_ffn, w_ffn_in, ffn_conv_w, ffn_conv_b, w_ffn_out, norm_final):
    batch, seq, _ = x.shape
    depth = w_in.shape[0]
    lp = PAD + N_META + seq
    t = batch * lp
    assert lp % ROW_TILE == 0 and lp % ATTN_TILE == 0 and lp % GLA_CHUNK == 0 and lp % RW_CHUNK == 0

    w_all, w_gate, wq, wkv, rw_w2p, rw_a2p, gla_a2p = _layout_params(
        w_in, mla_w_uq, mla_w_ukv, rw_w2, rw_a2, gla_a2)
    w_branch_b, w_out_b, w_ffn_in_b, w_ffn_out_b, rw_g2_b = map(_bf, (w_branch, w_out, w_ffn_in, w_ffn_out, rw_g2))
    vec = lambda a: a.reshape(depth, 1, -1)
    cos, sin = _rope_tables(lp)
    idx = jnp.arange(SB_SUFFIX_SPLIT)
    later_mat = jnp.where(idx[:, None] > idx[None, :], 1.0, 0.0).astype(BF16)

    h = _embed(x.reshape(batch * seq, D_MODEL), meta_tokens.astype(x.dtype), batch)

    for i in range(depth):
        z_mla, z_rw, z_sb, z_gla, sb_values = _in_proj(h, vec(norm_mix)[i], w_all, i)
        q, k, v = _mla_prep(z_mla, vec(mla_q_norm)[i], vec(mla_kv_norm)[i], wq, wkv, i, cos, sin, lp)
        y_mla = _mla_attn(q, k, v, lp)
        rw = {"rw_mu": vec(rw_mu)[i], "rw_w0": vec(rw_w0)[i], "rw_w2": rw_w2p[i], "rw_a0": vec(rw_a0)[i],
              "rw_a2": rw_a2p[i], "rw_g2": rw_g2_b[i], "rw_k_k": vec(rw_k_k)[i], "rw_k_a": vec(rw_k_a)[i],
              "rw_r_k": vec(rw_r_k)[i]}
        *chunk_terms, g, bonus = _rw_prep(z_rw, rw, lp)
        y_rw = _rw_scan(*chunk_terms, batch)
        y_sb = _sb_attn(z_sb, sb_values, later_mat, lp)
        o_gla = _gla(z_gla, gla_a2p[i], vec(gla_a_b)[i], batch)
        mp = {"rw_ln_w": vec(rw_ln_w)[i], "rw_ln_b": vec(rw_ln_b)[i], "gla_norm": vec(gla_norm)[i],
              "gate_b": gate_b[i], "norm_mix": vec(norm_mix)[i], "w_gate": w_gate, "w_branch": w_branch_b,
              "w_out": w_out_b, "layer": i}
        h = _merge(h, y_mla, y_rw, bonus, g, y_sb, o_gla, z_gla, mp, lp)
        h = _ffn(h, vec(norm_ffn)[i], w_ffn_in_b, ffn_conv_w[i], vec(ffn_conv_b)[i], w_ffn_out_b, i, lp)
    return _final_norm(h, norm_final.reshape(1, D_MODEL), batch, seq)
```

```python
import functools

import jax
import jax.numpy as jnp
from jax import lax
from jax.experimental import pallas as pl
from jax.experimental.pallas import tpu as pltpu

F32 = jnp.float32
BF16 = jnp.bfloat16

D_MODEL = 1024
DEPTH = 4
N_META = 16
BLOCK = 128
PAD = (-N_META) % BLOCK
EPS = 1e-6
NEG_INF = -1e30
LOG2_E = 1.4426950408889634

N_HEADS = 4
MLA_NOPE = 64
MLA_ROPE = 32
MLA_Q_RANK = 256
MLA_KV_RANK = 128
ROPE_THETA = 10000.0

RW_W = 256
RW_GN_EPS = 64e-5
RW_CHUNK = 64

SB_W = 256
SB_HEAD = 64
SB_SUFFIX_SPLIT = 256

GLA_DK = 32
GLA_QK = 128
GLA_W = 256
GLA_TAU = 16.0
GLA_CHUNK = 128
GLA_SUB = 16

D_FF = 2816
FFN_COL_CHUNKS = (1536, 1280)

W_MLA, W_RW, W_SB, W_GLA, W_GATE = 640, 1024, 768, 896, 4096
IN_GROUP_WIDTHS = (W_MLA, W_RW, W_SB, W_GLA)
IN_GROUP_DTYPES = (BF16, F32, BF16, F32)
IN_ROW_RW, IN_ROW_SB, IN_ROW_MLA, IN_ROW_GLA = 0, W_RW, W_RW + W_SB, W_RW + W_SB + W_MLA

ROW_TILE = 384
ATTN_TILE = 384
FINAL_ROWS = 1024
VMEM_LIMIT = 56 * 1024 * 1024


def _bf(x):
    return x.astype(BF16)


def _dot(a, b):
    return jnp.dot(_bf(a), _bf(b), preferred_element_type=F32)


def _dot_nt(a, b):
    return lax.dot_general(_bf(a), _bf(b), (((1,), (1,)), ((), ())), preferred_element_type=F32)


def _dot_tn(a, b):
    return lax.dot_general(_bf(a), _bf(b), (((0,), (0,)), ((), ())), preferred_element_type=F32)


def _split_hi_lo(x):
    hi = _bf(x)
    lo = _bf(x - hi.astype(F32))
    return hi, lo


def _dot_exact_rhs(x, m):
    hi, lo = _split_hi_lo(x)
    return jnp.dot(hi, m, preferred_element_type=F32) + jnp.dot(lo, m, preferred_element_type=F32)


def _dot_exact_lhs(m, x):
    hi, lo = _split_hi_lo(x)
    return jnp.dot(m, hi, preferred_element_type=F32) + jnp.dot(m, lo, preferred_element_type=F32)


def _iota(shape, dim):
    return lax.broadcasted_iota(jnp.int32, shape, dim)


def _div_pow2(x, d):
    assert d & (d - 1) == 0
    return lax.shift_right_logical(x, d.bit_length() - 1)


def _same_segment(shape, row_seg, col_seg):
    return _div_pow2(_iota(shape, 0), row_seg) == _div_pow2(_iota(shape, 1), col_seg)


def _segment_matrix(n, seg, value):
    return jnp.where(_same_segment((n, n), seg, seg), value, 0.0).astype(BF16)


def _head_lane_mask(width, head_width, h):
    lane = _iota((1, width), 1)
    return (lane >= h * head_width) & (lane < (h + 1) * head_width)


def _stack_heads(x, head_width):
    w = x.shape[1]
    return jnp.concatenate(
        [jnp.where(_head_lane_mask(w, head_width, h), x, 0.0) for h in range(N_HEADS)], axis=0)


def _rms(x, eps):
    return x * lax.rsqrt(jnp.mean(x * x, axis=-1, keepdims=True) + eps)


def _const_spec(shape):
    nd = len(shape)
    return pl.BlockSpec(shape, lambda *_: (0,) * nd)


def _layer_spec(stacked, layer):
    nd = stacked.ndim - 1
    return pl.BlockSpec((None,) + stacked.shape[1:], lambda *_: (layer,) + (0,) * nd)


def _params(sem, vmem=VMEM_LIMIT):
    return pltpu.CompilerParams(dimension_semantics=sem, vmem_limit_bytes=vmem)


def _embed_kernel(x_ref, meta_ref, o_ref):
    first_real = PAD + N_META

    @pl.when(pl.program_id(1) == 0)
    def _():
        o_ref[0:PAD, :] = jnp.zeros((PAD, D_MODEL), o_ref.dtype)
        o_ref[PAD:first_real, :] = meta_ref[...]
        o_ref[first_real:, :] = x_ref[0:o_ref.shape[0] - first_real, :]

    @pl.when(pl.program_id(1) > 0)
    def _():
        o_ref[...] = x_ref[...]


def _embed(x2, meta, batch):
    seq = x2.shape[0] // batch
    first_real = PAD + N_META
    lp = first_real + seq
    tm = ROW_TILE
    window = lambda b, i: (pl.multiple_of(b * seq + jnp.maximum(i * tm - first_real, 0), BLOCK), 0)
    return pl.pallas_call(
        _embed_kernel,
        grid=(batch, lp // tm),
        in_specs=[pl.BlockSpec((pl.Element(tm), pl.Element(D_MODEL)), window), _const_spec(meta.shape)],
        out_specs=pl.BlockSpec((tm, D_MODEL), lambda b, i: (b * (lp // tm) + i, 0)),
        out_shape=jax.ShapeDtypeStruct((batch * lp, D_MODEL), x2.dtype),
        compiler_params=_params(("parallel", "parallel")),
        name="embed",
    )(x2, meta)


def _in_proj_kernel(h_ref, g_ref, w_ref, *out_refs):
    mla_ref, rw_ref, sb_ref, gla_ref, sb_values_ref = out_refs
    n = _bf(_rms(h_ref[...], EPS) * g_ref[...])
    rw_ref[...] = _dot_nt(n, w_ref[IN_ROW_RW:IN_ROW_SB, :])
    sb_ref[...] = _bf(_dot_nt(n, w_ref[IN_ROW_SB:IN_ROW_MLA, :]))
    both = _dot_nt(n, w_ref[IN_ROW_MLA:IN_ROW_GLA + W_GLA, :])
    mla_ref[...] = _bf(both[:, :W_MLA])
    gla_ref[...] = both[:, W_MLA:]
    sb_values_ref[0] = _head_stacked_values(sb_ref[:, 2 * SB_W:3 * SB_W])


def _in_proj(h, g, w_all, layer):
    t = h.shape[0]
    tm = ROW_TILE
    return pl.pallas_call(
        _in_proj_kernel,
        grid=(t // tm,),
        in_specs=[pl.BlockSpec((tm, D_MODEL), lambda i: (i, 0)),
                  _const_spec((1, D_MODEL)),
                  _layer_spec(w_all, layer)],
        out_specs=[pl.BlockSpec((tm, w), lambda i: (i, 0)) for w in IN_GROUP_WIDTHS]
        + [pl.BlockSpec((1, N_HEADS * tm, SB_W), lambda i: (i, 0, 0))],
        out_shape=[jax.ShapeDtypeStruct((t, w), dt) for w, dt in zip(IN_GROUP_WIDTHS, IN_GROUP_DTYPES)]
        + [jax.ShapeDtypeStruct((t // tm, N_HEADS * tm, SB_W), BF16)],
        compiler_params=_params(("parallel",)),
        name="in_proj",
    )(h, g, w_all)


def _mla_prep_kernel(z_ref, qg_ref, kvg_ref, wq_ref, wkv_ref, cos_ref, sin_ref, q_ref, k_ref, vt_ref):
    z = z_ref[...].astype(F32)
    nq = _rms(z[:, 0:256], EPS) * qg_ref[...]
    nkv = _rms(z[:, 256:384], EPS) * kvg_ref[...]
    cos = cos_ref[...]
    sin = sin_ref[...]
    ql = _dot(nq, wq_ref[...])
    scale = (MLA_NOPE + MLA_ROPE) ** -0.5
    q = jnp.concatenate([ql[:, 0:128], ql[:, 128:256] * cos + ql[:, 512:640] * sin,
                         ql[:, 256:384], ql[:, 384:512] * cos + ql[:, 640:768] * sin], axis=1)
    q_ref[...] = _bf(q * scale)
    kvl = _dot(nkv, wkv_ref[...])
    k_rope = z[:, 384:512] * cos + z[:, 512:640] * sin
    k_rope = jnp.where(_iota((1, 128), 1) < 2 * MLA_ROPE, k_rope, 0.0)
    k_ref[...] = _bf(jnp.concatenate([kvl[:, 0:128], k_rope, kvl[:, 128:256], k_rope], axis=1))
    vt = _bf(kvl[:, 256:512].T)
    row_head = _div_pow2(_iota((256, 1), 0), 64)
    vt_ref[0] = jnp.concatenate([jnp.where(row_head == h, vt, jnp.zeros((), BF16)) for h in range(N_HEADS)], axis=1)


def _mla_prep(z_mla, qg, kvg, wq, wkv, layer, cos, sin, lp):
    t = z_mla.shape[0]
    tm = ROW_TILE
    nb = lp // tm
    row = lambda b, i: (b * nb + i, 0)
    return pl.pallas_call(
        _mla_prep_kernel,
        grid=(t // lp, nb),
        in_specs=[pl.BlockSpec((tm, W_MLA), row),
                  _const_spec((1, MLA_Q_RANK)), _const_spec((1, MLA_KV_RANK)),
                  _layer_spec(wq, layer), _layer_spec(wkv, layer),
                  pl.BlockSpec((tm, 128), lambda b, i: (i, 0)),
                  pl.BlockSpec((tm, 128), lambda b, i: (i, 0))],
        out_specs=[pl.BlockSpec((tm, 512), row), pl.BlockSpec((tm, 512), row),
                   pl.BlockSpec((1, 256, N_HEADS * tm), lambda b, i: (b * nb + i, 0, 0))],
        out_shape=[jax.ShapeDtypeStruct((t, 512), BF16), jax.ShapeDtypeStruct((t, 512), BF16),
                   jax.ShapeDtypeStruct((t // tm, 256, N_HEADS * tm), BF16)],
        compiler_params=_params(("parallel", "parallel")),
        name="mla_prep",
    )(z_mla, qg, kvg, wq, wkv, cos, sin)


def _head_stacked_values(vb):
    zero = jnp.zeros((), vb.dtype)
    return jnp.concatenate([jnp.where(_head_lane_mask(256, 64, h), vb, zero) for h in range(N_HEADS)], axis=0)


def _per_head_rows(rows, n):
    return jnp.concatenate([jnp.broadcast_to(r, (64, n)) for r in rows], axis=0)


def _mla_attn_kernel(q_ref, k_ref, vt_ref, o_ref, qh_ref, m_ref, l_ref, acc_ref, p_ref, alpha_ref, *, tile):
    i = pl.program_id(1)
    q_pos = i * tile + _iota((1, tile), 1)
    lane = _iota((1, 256), 1)
    for h in range(N_HEADS):
        half, slot = h // 2, h % 2
        head_lanes = (((lane >= slot * MLA_NOPE) & (lane < (slot + 1) * MLA_NOPE))
                      | ((lane >= 128 + slot * MLA_ROPE) & (lane < 128 + (slot + 1) * MLA_ROPE)))
        qh_ref[half, slot * tile:(slot + 1) * tile, :] = jnp.where(
            head_lanes, q_ref[:, half * 256:(half + 1) * 256], jnp.zeros((), BF16))
    m_ref[...] = jnp.full(m_ref.shape, NEG_INF, F32)
    l_ref[...] = jnp.zeros(l_ref.shape, F32)
    acc_ref[...] = jnp.zeros(acc_ref.shape, F32)

    def add_values(j, slot):
        pv = jnp.dot(vt_ref[j], p_ref[slot], preferred_element_type=F32)
        acc_ref[...] = _per_head_rows([alpha_ref[slot, h] for h in range(N_HEADS)], tile) * acc_ref[...] + pv

    def block(j, masked, prev, slot):
        start = pl.multiple_of(j * tile, tile)
        heads = range(N_HEADS)
        s_pair = [_dot_nt(k_ref[pl.ds(start, tile), half * 256:(half + 1) * 256], qh_ref[half]) for half in (0, 1)]
        s = [s_pair[h // 2][:, (h % 2) * tile:(h % 2 + 1) * tile] for h in heads]
        if prev is not None:
            add_values(prev, 1 - slot)
        for h in heads:
            s_h = s[h]
            if masked:
                k_pos = start + _iota((tile, 1), 0)
                s_h = jnp.where((k_pos <= q_pos) & (k_pos >= PAD), s_h, NEG_INF)
            m_old = m_ref[h]
            m_new = jnp.maximum(m_old, jnp.max(s_h, axis=0, keepdims=True))
            p = jnp.exp(s_h - m_new)
            alpha = jnp.exp(m_old - m_new)
            l_ref[h] = alpha * l_ref[h] + jnp.sum(p, axis=0, keepdims=True)
            m_ref[h] = m_new
            p_ref[slot, h * tile:(h + 1) * tile, :] = _bf(p)
            alpha_ref[slot, h] = alpha

    block(i, True, None, 0)

    @pl.when(i > 0)
    def _():
        block(0, True, i, 1)

    interior_blocks = jnp.maximum(i - 1, 0)

    def interior_pair(t, carry):
        j = 1 + 2 * t
        block(j, False, j - 1, 0)
        block(j + 1, False, j, 1)
        return carry

    lax.fori_loop(0, lax.shift_right_logical(interior_blocks, 1), interior_pair, 0)
    odd = (interior_blocks & 1) == 1

    @pl.when(odd)
    def _():
        block(i - 1, False, i - 2, 0)

    last = jnp.maximum(i - 1, 0)
    last_in_slot0 = (i == 0) | odd

    @pl.when(last_in_slot0)
    def _():
        add_values(last, 0)

    @pl.when(jnp.logical_not(last_in_slot0))
    def _():
        add_values(last, 1)
    o_ref[...] = _bf((acc_ref[...] / _per_head_rows([l_ref[h] for h in range(N_HEADS)], tile)).T)


def _mla_attn(q, k, vt, lp):
    t = q.shape[0]
    tile = ATTN_TILE
    nb = lp // tile
    return pl.pallas_call(
        functools.partial(_mla_attn_kernel, tile=tile),
        grid=(t // lp, nb),
        in_specs=[pl.BlockSpec((tile, 512), lambda b, i: (b * nb + i, 0)),
                  pl.BlockSpec((lp, 512), lambda b, i: (b, 0)),
                  pl.BlockSpec((nb, 256, N_HEADS * tile), lambda b, i: (b, 0, 0))],
        out_specs=pl.BlockSpec((tile, 256), lambda b, i: (b * nb + i, 0)),
        out_shape=jax.ShapeDtypeStruct((t, 256), BF16),
        scratch_shapes=[pltpu.VMEM((2, 2 * tile, 256), BF16), pltpu.VMEM((N_HEADS, 1, tile), F32),
                        pltpu.VMEM((N_HEADS, 1, tile), F32), pltpu.VMEM((256, tile), F32),
                        pltpu.VMEM((2, N_HEADS * tile, tile), BF16), pltpu.VMEM((2, N_HEADS, 1, tile), F32)],
        compiler_params=_params(("parallel", "arbitrary")),
        name="mla_attn",
    )(q, k, vt)


def _sb_attn_kernel(q_ref, k_ref, v_ref, later_ref, o_ref, qh_ref, c_ref, acc_ref, wts_ref, *, tile):
    i = pl.program_id(1)
    row = i * tile + _iota((tile, 1), 0)
    for h in range(N_HEADS):
        qh_ref[h] = jnp.where(_head_lane_mask(SB_W, SB_HEAD, h), q_ref[...], jnp.zeros((), BF16))
    c_ref[...] = jnp.zeros(c_ref.shape, F32)
    acc_ref[...] = jnp.zeros(acc_ref.shape, F32)

    def add_values(j, slot):
        acc_ref[...] += jnp.dot(wts_ref[slot], v_ref[j], preferred_element_type=F32)

    def block(j, masked, prev, slot):
        start = pl.multiple_of(j * tile, tile)
        heads = range(N_HEADS)
        cut = SB_SUFFIX_SPLIT
        z = [_dot_nt(qh_ref[h], k_ref[pl.ds(start, tile), :]) for h in heads]
        if prev is not None:
            add_values(prev, 1 - slot)
        log_take = [jnp.minimum(z[h], 0.0) - jnp.log2(1.0 + jnp.exp2(-jnp.abs(z[h]))) for h in heads]
        log_keep = [log_take[h] - z[h] for h in heads]
        if masked:
            mask = (start + _iota((1, tile), 1)) < row
            log_keep = [jnp.where(mask, log_keep[h], 0.0) for h in heads]
        keep16 = [_bf(log_keep[h]) for h in heads]
        later_head = [jnp.dot(keep16[h][:, :cut], later_ref[...], preferred_element_type=F32) for h in heads]
        later_tail = [jnp.dot(keep16[h][:, cut:], later_ref[:tile - cut, :tile - cut], preferred_element_type=F32)
                      for h in heads]
        for h in heads:
            head_sum = jnp.sum(log_keep[h][:, :cut], axis=-1, keepdims=True)
            tail_sum = jnp.sum(log_keep[h][:, cut:], axis=-1, keepdims=True)
            c = c_ref[h]
            later = jnp.concatenate([later_head[h] + tail_sum, later_tail[h]], axis=1)
            w = jnp.exp2(log_take[h] + later + c)
            if masked:
                w = jnp.where(mask, w, 0.0)
            c_ref[h] = c + (head_sum + tail_sum)
            wts_ref[slot, :, h * tile:(h + 1) * tile] = _bf(w)

    block(i, True, None, 0)

    def below_pair(t, carry):
        j = i - 1 - 2 * t
        block(j, False, j + 1, 1)
        block(j - 1, False, j, 0)
        return carry

    lax.fori_loop(0, lax.shift_right_logical(i, 1), below_pair, 0)
    odd = (i & 1) == 1

    @pl.when(odd)
    def _():
        block(0, False, 1, 1)
        add_values(0, 1)

    @pl.when(jnp.logical_not(odd))
    def _():
        add_values(0, 0)
    o_ref[...] = _bf(acc_ref[...])


def _sb_attn(z_sb, sb_values, later_mat, lp):
    t = z_sb.shape[0]
    tile = ATTN_TILE
    nb = lp // tile
    return pl.pallas_call(
        functools.partial(_sb_attn_kernel, tile=tile),
        grid=(t // lp, nb),
        in_specs=[pl.BlockSpec((tile, SB_W), lambda b, i: (b * nb + i, 0)),
                  pl.BlockSpec((lp, SB_W), lambda b, i: (b, 1)),
                  pl.BlockSpec((nb, N_HEADS * tile, SB_W), lambda b, i: (b, 0, 0)),
                  _const_spec((SB_SUFFIX_SPLIT, SB_SUFFIX_SPLIT))],
        out_specs=pl.BlockSpec((tile, SB_W), lambda b, i: (b * nb + i, 0)),
        out_shape=jax.ShapeDtypeStruct((t, SB_W), BF16),
        scratch_shapes=[pltpu.VMEM((N_HEADS, tile, SB_W), BF16), pltpu.VMEM((N_HEADS, tile, 1), F32),
                        pltpu.VMEM((tile, SB_W), F32), pltpu.VMEM((2, tile, N_HEADS * tile), BF16)],
        compiler_params=_params(("parallel", "arbitrary")),
        name="sb_attn",
    )(z_sb, z_sb, sb_values, later_mat)


def _compact_eye(c):
    return jnp.where(_iota((c, N_HEADS * c), 0) == (_iota((c, N_HEADS * c), 1) & (c - 1)), 1.0, 0.0)


def _rw_prep_kernel(z_ref, zprev_ref, mu_ref, w0_ref, w2_ref, a0_ref, a2_ref, g2_ref, kk_ref, ka_ref, rk_ref,
                    w_o, rt_o, arb_o, kbar_o, bbar_o, v_o, uv_o, yv_o, dec_o, g_o, bonus_o):
    i = pl.program_id(1)
    z = z_ref[...]
    tm = z.shape[0]
    prev = jnp.where(i == 0, 0.0, zprev_ref[7:8, :])
    shifted = jnp.where(_iota((tm, 1), 0) == 0, prev, pltpu.roll(z, 1, 0))
    zz = z + (shifted - z) * mu_ref[...]
    r, k, v = zz[:, 0:256], zz[:, 256:512], zz[:, 512:768]
    lora_in = zz[:, 768:896]
    u = w0_ref[...] + _dot(jnp.tanh(lora_in), w2_ref[...])
    w = jnp.minimum(u, 0.0) - jnp.log1p(jnp.exp(-jnp.abs(u))) - 0.5
    a = jax.nn.sigmoid(a0_ref[...] + _dot(lora_in, a2_ref[...]))
    seg = _segment_matrix(RW_W, 64, 1.0)
    kx = k * kk_ref[...]
    kap = kx / jnp.maximum(jnp.sqrt(_dot_exact_rhs(kx * kx, seg)), 1e-12)
    kmod = k * (1.0 + (a - 1.0) * ka_ref[...])
    beta = kap * a
    lw = -jnp.exp(w)
    g_o[...] = _dot(jax.nn.sigmoid(zz[:, 896:1024]), g2_ref[...])
    bonus_o[...] = _dot_exact_rhs(r * kmod * rk_ref[...], seg) * v

    c = RW_CHUNK
    same_chunk = _same_segment((tm, tm), c, c)
    lw_hi, lw_lo = _split_hi_lo(lw)
    cum = jnp.where(same_chunk & (_iota((tm, tm), 0) >= _iota((tm, tm), 1)), 1.0, 0.0).astype(BF16)
    b = jnp.dot(cum, lw_hi, preferred_element_type=F32) + jnp.dot(cum, lw_lo, preferred_element_type=F32)
    b_end = jnp.concatenate(
        [jnp.broadcast_to(b[n * c + c - 1:(n + 1) * c, :], (c, RW_W)) for n in range(tm // c)], axis=0)
    grow = jnp.exp(-b)
    to_end = jnp.exp(b_end - b)
    kap_t = kap * jnp.exp(b - lw)
    r_t = r * jnp.exp(b)
    beta_g = beta * grow
    k_g = kmod * grow
    rt_o[...] = _bf(r_t)
    kbar_o[...] = _bf(kmod * to_end)
    bbar_o[...] = _bf(beta * to_end)
    v_o[...] = _bf(v)
    decay_end = jnp.exp(b_end)

    lane_pos = _iota((c, N_HEADS * c), 1) & (c - 1)
    strictly_earlier = _iota((c, N_HEADS * c), 0) > lane_pos
    not_later = _iota((c, N_HEADS * c), 0) >= lane_pos
    chunks = [slice(n * c, (n + 1) * c) for n in range(tm // c)]
    pair = [_dot_nt(jnp.concatenate([kap_t[rows], r_t[rows]], axis=0),
                    jnp.concatenate([_stack_heads(beta_g[rows], 64), _stack_heads(k_g[rows], 64)], axis=0))
            for rows in chunks]
    a_kb = [jnp.where(strictly_earlier, p[0:c, 0:4 * c], 0.0) for p in pair]
    a_kk = [jnp.where(strictly_earlier, p[0:c, 4 * c:8 * c], 0.0) for p in pair]
    a_rk = [jnp.where(not_later, p[c:2 * c, 4 * c:8 * c], 0.0) for p in pair]
    for rows, p in zip(chunks, pair):
        arb_o[rows, :] = _bf(jnp.where(not_later, p[c:2 * c, 0:4 * c], 0.0))
    inv = [_compact_eye(c) - a for a in a_kb]
    power = [_dot(a, _stack_heads(a, c)) for a in a_kb]
    span = 2
    while span < c:
        inv = [t + _dot(t, _stack_heads(p, c)) for t, p in zip(inv, power)]
        span *= 2
        if span < c:
            power = [_dot(p, _stack_heads(p, c)) for p in power]
    v_st = [_stack_heads(v[rows], 64) for rows in chunks]
    kk_v = [_dot(a, vs) for a, vs in zip(a_kk, v_st)]
    for n, rows in enumerate(chunks):
        w_o[rows, :] = _bf(_dot(inv[n], _stack_heads(kap_t[rows], 64)))
        uv_o[rows, :] = _dot(inv[n], _stack_heads(kk_v[n], 64))
        yv_o[rows, :] = _dot(a_rk[n], v_st[n])
        dec_o[n * 8:(n + 1) * 8, :] = decay_end[n * c:n * c + 8]


def _rw_prep(z_rw, p, lp):
    t = z_rw.shape[0]
    tm = ROW_TILE
    nb = lp // tm
    row = lambda b, i: (b * nb + i, 0)
    prev = lambda b, i: (jnp.maximum(b * (lp // 8) + i * (tm // 8) - 1, 0), 0)
    consts = [p["rw_mu"], p["rw_w0"], p["rw_w2"], p["rw_a0"], p["rw_a2"], p["rw_g2"],
              p["rw_k_k"], p["rw_k_a"], p["rw_r_k"]]
    return pl.pallas_call(
        _rw_prep_kernel,
        grid=(t // lp, nb),
        in_specs=[pl.BlockSpec((tm, W_RW), row), pl.BlockSpec((8, W_RW), prev)]
        + [_const_spec(c.shape) for c in consts],
        out_specs=[pl.BlockSpec((tm, RW_W), row)] * 8
        + [pl.BlockSpec((tm // RW_CHUNK * 8, RW_W), row)] + [pl.BlockSpec((tm, RW_W), row)] * 2,
        out_shape=[jax.ShapeDtypeStruct((t, RW_W), BF16)] * 6 + [jax.ShapeDtypeStruct((t, RW_W), F32)] * 2
        + [jax.ShapeDtypeStruct((t // RW_CHUNK * 8, RW_W), F32)] + [jax.ShapeDtypeStruct((t, RW_W), F32)] * 2,
        compiler_params=_params(("parallel", "parallel")),
        name="rw_prep",
    )(z_rw, z_rw, *consts)


def _rw_scan_kernel(w_ref, rt_ref, arb_ref, kbar_ref, bbar_ref, v_ref, uv_ref, yv_ref, dec_ref, y_ref, ht_ref):
    c = RW_CHUNK

    @pl.when(pl.program_id(0) == 0)
    def _():
        ht_ref[...] = jnp.zeros_like(ht_ref)

    same_head = _same_segment((RW_W, RW_W), 64, 64)
    seqs = range(ht_ref.shape[0])
    ht = [ht_ref[s] for s in seqs]
    from_state = [_dot_nt(jnp.concatenate([w_ref[s], rt_ref[s]], axis=0), ht[s]) for s in seqs]
    u = [from_state[s][0:c] + uv_ref[s] for s in seqs]
    from_u = [_dot(arb_ref[s], _stack_heads(u[s], 64)) for s in seqs]
    upd = [_dot_tn(jnp.concatenate([v_ref[s], _bf(u[s])], axis=0),
                   jnp.concatenate([kbar_ref[s], -bbar_ref[s]], axis=0)) for s in seqs]
    for s in seqs:
        y_ref[s] = from_state[s][c:2 * c] + yv_ref[s] - from_u[s]
        ht_ref[s] = ht[s] * dec_ref[s, 0:1, :] + jnp.where(same_head, upd[s], 0.0)


def _rw_scan(w, rt, arb, kbar, bbar, v, uv, yv, dec, batch):
    t = w.shape[0]
    lp = t // batch
    c = RW_CHUNK
    as_seq = lambda a: a.reshape(batch, a.shape[0] // batch, RW_W)
    spec = pl.BlockSpec((batch, c, RW_W), lambda i: (0, i, 0))
    y = pl.pallas_call(
        _rw_scan_kernel,
        grid=(lp // c,),
        in_specs=[spec] * 8 + [pl.BlockSpec((batch, 8, RW_W), lambda i: (0, i, 0))],
        out_specs=spec,
        out_shape=jax.ShapeDtypeStruct((batch, lp, RW_W), F32),
        scratch_shapes=[pltpu.VMEM((batch, RW_W, RW_W), F32)],
        compiler_params=_params(("arbitrary",)),
        name="rw_scan",
    )(*map(as_seq, (w, rt, arb, kbar, bbar, v, uv, yv, dec)))
    return y.reshape(t, RW_W)


def _gla_kernel(q_ref, k_ref, v_ref, al_ref, a2_ref, ab_ref, o_ref, st_ref):
    c = GLA_CHUNK
    seqs = range(st_ref.shape[0])

    @pl.when(pl.program_id(0) == 0)
    def _():
        st_ref[...] = jnp.zeros_like(st_ref)

    sub = GLA_SUB
    nsub = c // sub
    row_i = _iota((c, c), 0)
    col_i = _iota((c, c), 1)
    tri = jnp.where(row_i >= col_i, 1.0, 0.0).astype(BF16)
    tri_sub = jnp.where((row_i >= col_i) & _same_segment((c, c), sub, sub), 1.0, 0.0).astype(BF16)
    key_pos = _iota((c, 1), 0)
    query_in_sub = _iota((N_HEADS * sub, c), 0) & (sub - 1)
    same_head = _same_segment((GLA_W, GLA_QK), 64, GLA_DK)

    x = [_dot(al_ref[s], a2_ref[...]) + ab_ref[...] for s in seqs]
    log_a = [(jnp.minimum(x[s], 0.0) - jnp.log1p(jnp.exp(-jnp.abs(x[s])))) * (1.0 / GLA_TAU) for s in seqs]
    b = [_dot_exact_lhs(tri, log_a[s]) for s in seqs]
    b_sub = [_dot_exact_lhs(tri_sub, log_a[s]) for s in seqs]
    q = [q_ref[s] * (GLA_DK ** -0.5) for s in seqs]
    st = [st_ref[s] for s in seqs]
    inter = [_dot_nt(q[s] * jnp.exp(b[s]), st[s]) for s in seqs]

    scores = []
    for s in seqs:
        q_sub = q[s] * jnp.exp(b_sub[s])
        beta = b[s] - b_sub[s]
        k = k_ref[s]
        blocks = []
        for blk in range(nsub):
            lo, hi = blk * sub, (blk + 1) * sub
            expo = jnp.where(key_pos < hi, beta[lo:lo + 1, :] - b[s], NEG_INF)
            sc = _dot_nt(_stack_heads(q_sub[lo:hi], GLA_DK), k * jnp.exp(expo))
            blocks.append(jnp.where(_iota((N_HEADS * sub, c), 1) <= lo + query_in_sub, sc, 0.0))
        scores.append(jnp.concatenate(blocks, axis=0))
    per_head = [_dot(scores[s], v_ref[s]) for s in seqs]
    upd = [_dot_tn(v_ref[s], k_ref[s] * jnp.exp(b[s][c - 1:c, :] - b[s])) for s in seqs]
    for s in seqs:
        pieces = []
        for blk in range(nsub):
            piece = jnp.zeros((sub, GLA_W), F32)
            for h in range(N_HEADS):
                r0 = (blk * N_HEADS + h) * sub
                piece = piece + jnp.where(_head_lane_mask(GLA_W, 64, h), per_head[s][r0:r0 + sub], 0.0)
            pieces.append(piece)
        o_ref[s] = inter[s] + jnp.concatenate(pieces, axis=0)
        st_ref[s] = st[s] * jnp.exp(b[s][c - 1:c, :]) + jnp.where(same_head, upd[s], 0.0)


def _gla(z_gla, a2, ab, batch):
    t = z_gla.shape[0]
    lp = t // batch
    c = GLA_CHUNK
    z3 = z_gla.reshape(batch, lp, W_GLA)
    o = pl.pallas_call(
        _gla_kernel,
        grid=(lp // c,),
        in_specs=[pl.BlockSpec((batch, c, 128), lambda i: (0, i, 0)),
                  pl.BlockSpec((batch, c, 128), lambda i: (0, i, 1)),
                  pl.BlockSpec((batch, c, 256), lambda i: (0, i, 1)),
                  pl.BlockSpec((batch, c, 128), lambda i: (0, i, 6)),
                  _const_spec(a2.shape), _const_spec(ab.shape)],
        out_specs=pl.BlockSpec((batch, c, GLA_W), lambda i: (0, i, 0)),
        out_shape=jax.ShapeDtypeStruct((batch, lp, GLA_W), F32),
        scratch_shapes=[pltpu.VMEM((batch, GLA_W, GLA_QK), F32)],
        compiler_params=_params(("arbitrary",)),
        name="gla",
    )(z3, z3, z3, z3, a2, ab)
    return o.reshape(t, GLA_W)


def _merge_kernel(h_ref, ymla_ref, yrw_ref, bonus_ref, g_ref, ysb_ref, ogla_ref, rgla_ref,
                  nmix_ref, lnw_ref, lnb_ref, gn_ref, gb_ref, wg_ref, wb_ref, wo_ref, o_ref, *, tiles_per_seq):
    tm = h_ref.shape[0]
    avg = _segment_matrix(256, 64, 1.0 / 64)
    h = h_ref[...]
    n = _bf(_rms(h, EPS) * nmix_ref[...])

    y = yrw_ref[...]
    d = y - _dot_exact_rhs(y, avg)
    var = _dot_exact_rhs(d * d, avg)
    y_rw = (d * lax.rsqrt(var + RW_GN_EPS) * lnw_ref[...] + lnb_ref[...] + bonus_ref[...]) * g_ref[...]

    o = ogla_ref[...]
    r = rgla_ref[...]
    y_gla = o * lax.rsqrt(_dot_exact_rhs(o * o, avg) + EPS) * gn_ref[...] * (r * jax.nn.sigmoid(r))

    acc = jnp.zeros((tm, D_MODEL), F32)
    for m, y_m in enumerate((ymla_ref[...], y_rw, ysb_ref[...], y_gla)):
        logits = _dot_nt(n, wg_ref[m * D_MODEL:(m + 1) * D_MODEL, :])
        gate = jax.nn.sigmoid(logits + gb_ref[m:m + 1, :])
        acc = acc + gate * _dot(y_m, wb_ref[m])
    delta = _dot(acc, wo_ref[...])
    row = (pl.program_id(0) % tiles_per_seq) * tm + _iota((tm, 1), 0)
    o_ref[...] = h + jnp.where(row >= PAD, delta, 0.0)


def _merge(h, y_mla, y_rw, bonus, g, y_sb, o_gla, z_gla, p, lp):
    t = h.shape[0]
    tm = ROW_TILE
    row = lambda i: (i, 0)
    w256 = pl.BlockSpec((tm, 256), row)
    consts = [p["norm_mix"], p["rw_ln_w"], p["rw_ln_b"], p["gla_norm"], p["gate_b"]]
    stacked = [p["w_gate"], p["w_branch"], p["w_out"]]
    return pl.pallas_call(
        functools.partial(_merge_kernel, tiles_per_seq=lp // tm),
        grid=(t // tm,),
        in_specs=[pl.BlockSpec((tm, D_MODEL), row), w256, w256, w256, w256, w256, w256,
                  pl.BlockSpec((tm, 256), lambda i: (i, 2))]
        + [_const_spec(c.shape) for c in consts]
        + [_layer_spec(w, p["layer"]) for w in stacked],
        out_specs=pl.BlockSpec((tm, D_MODEL), row),
        out_shape=jax.ShapeDtypeStruct((t, D_MODEL), F32),
        compiler_params=_params(("parallel",)),
        name="merge",
    )(h, y_mla, y_rw, bonus, g, y_sb, o_gla, z_gla, *consts, *stacked)


def _ffn_kernel(h_ref, g_ref, win_ref, cw_ref, cb_ref, wout_ref, o_ref, tail_ref):
    @pl.when(pl.program_id(1) == 0)
    def _():
        tail_ref[...] = jnp.zeros_like(tail_ref)

    x = h_ref[...]
    tm = x.shape[0]
    n = _bf(_rms(x, EPS) * g_ref[...])
    rowi = _iota((tm, 1), 0)
    acc = jnp.zeros((tm, D_MODEL), F32)
    assert sum(FFN_COL_CHUNKS) == D_FF
    for c0, c1 in zip((0, FFN_COL_CHUNKS[0]), (FFN_COL_CHUNKS[0], D_FF)):
        a = jnp.dot(n, win_ref[:, c0:c1], preferred_element_type=F32)
        u = jnp.dot(n, win_ref[:, D_FF + c0:D_FF + c1], preferred_element_type=F32)
        prev1 = tail_ref[7:8, c0:c1]
        prev2 = tail_ref[6:7, c0:c1]
        a1 = jnp.where(rowi == 0, prev1, pltpu.roll(a, 1, 0))
        a2 = jnp.where(rowi == 0, prev2, jnp.where(rowi == 1, prev1, pltpu.roll(a, 2, 0)))
        tail_ref[:, c0:c1] = a[tm - 8:tm, :]
        conv = cb_ref[:, c0:c1] + cw_ref[0:1, c0:c1] * a2 + cw_ref[1:2, c0:c1] * a1 + cw_ref[2:3, c0:c1] * a
        acc = acc + _dot(conv * jax.nn.sigmoid(conv) * u, wout_ref[c0:c1, :])
    o_ref[...] = x + acc


def _ffn(h, g, w_in, conv_w, conv_b, w_out, layer, lp):
    t = h.shape[0]
    tm = ROW_TILE
    nb = lp // tm
    row = lambda b, i: (b * nb + i, 0)
    return pl.pallas_call(
        _ffn_kernel,
        grid=(t // lp, nb),
        in_specs=[pl.BlockSpec((tm, D_MODEL), row), _const_spec((1, D_MODEL)), _layer_spec(w_in, layer),
                  _const_spec(conv_w.shape), _const_spec(conv_b.shape), _layer_spec(w_out, layer)],
        out_specs=pl.BlockSpec((tm, D_MODEL), row),
        out_shape=jax.ShapeDtypeStruct((t, D_MODEL), F32),
        scratch_shapes=[pltpu.VMEM((8, D_FF), F32)],
        compiler_params=_params(("arbitrary", "arbitrary")),
        name="conv_ffn",
    )(h, g, w_in, conv_w, conv_b, w_out)


def _final_norm_kernel(h_ref, g_ref, o_ref):
    o_ref[0] = _rms(h_ref[...], EPS) * g_ref[...]


def _final_norm(h, g, batch, seq):
    lp = h.shape[0] // batch
    rows = FINAL_ROWS
    assert seq % rows == 0
    first = lambda bi, i: (pl.multiple_of(bi * lp + (PAD + N_META) + i * rows, BLOCK), 0)
    return pl.pallas_call(
        _final_norm_kernel,
        grid=(batch, seq // rows),
        in_specs=[pl.BlockSpec((pl.Element(rows), pl.Element(D_MODEL)), first), _const_spec((1, D_MODEL))],
        out_specs=pl.BlockSpec((1, rows, D_MODEL), lambda bi, i: (bi, i, 0)),
        out_shape=jax.ShapeDtypeStruct((batch, seq, D_MODEL), F32),
        compiler_params=_params(("parallel", "parallel")),
        name="final_norm",
    )(h, g)


def _rope_swap(w):
    half = w.shape[-1] // 2
    return jnp.concatenate([-w[..., half:], w[..., :half]], axis=-1)


def _in_weight_layout_kernel(w_ref, o_ref, gate_ref):
    def put(dst, val):
        o_ref[0, dst:dst + val.shape[0], :] = _bf(val)

    mla, sb, gla = IN_ROW_MLA, IN_ROW_SB, IN_ROW_GLA
    put(mla, w_ref[0, 0:384, :])
    half = MLA_ROPE // 2
    kr = w_ref[0, 384:384 + MLA_ROPE, :]
    kr_swapped = jnp.concatenate([-kr[half:], kr[:half]], axis=0)
    put(mla + 384, jnp.concatenate([kr] * 4, axis=0))
    put(mla + 512, jnp.concatenate([kr_swapped] * 4, axis=0))
    put(IN_ROW_RW, w_ref[0, 416:1440, :])
    put(sb, w_ref[0, 1440:1440 + SB_W, :] * (SB_HEAD ** -0.5 * LOG2_E))
    put(sb + SB_W, w_ref[0, 1440 + SB_W:2208, :])
    put(gla, w_ref[0, 2208:2720, :])
    put(gla + 512, w_ref[0, 2736:2992, :])
    lora = w_ref[0, 2720:2736, :]
    put(gla + 768, jnp.concatenate([lora, jnp.zeros((W_GLA - 784, lora.shape[1]), lora.dtype)], axis=0))
    gate_ref[0] = _bf(w_ref[0, 2992:2992 + W_GATE, :])


def _in_weight_layout(w_in):
    wt = jnp.swapaxes(w_in, 1, 2)
    depth, n, d = wt.shape
    cols = 256
    total = sum(IN_GROUP_WIDTHS)
    return pl.pallas_call(
        _in_weight_layout_kernel,
        grid=(depth, d // cols),
        in_specs=[pl.BlockSpec((1, n, cols), lambda l, i: (l, 0, i))],
        out_specs=[pl.BlockSpec((1, total, cols), lambda l, i: (l, 0, i)),
                   pl.BlockSpec((1, W_GATE, cols), lambda l, i: (l, 0, i))],
        out_shape=[jax.ShapeDtypeStruct((depth, total, d), BF16), jax.ShapeDtypeStruct((depth, W_GATE, d), BF16)],
        compiler_params=_params(("parallel", "parallel")),
        name="in_weight_layout",
    )(wt)


def _layout_params(w_in, mla_w_uq, mla_w_ukv, rw_w2, rw_a2, gla_a2):
    w_all, w_gate = _in_weight_layout(w_in)

    depth = w_in.shape[0]
    wuq = mla_w_uq.reshape(depth, MLA_Q_RANK, N_HEADS, MLA_NOPE + MLA_ROPE)
    nope, rope = wuq[..., :MLA_NOPE], wuq[..., MLA_NOPE:]
    rope_sw = _rope_swap(rope)
    zeros64 = jnp.zeros((depth, MLA_Q_RANK, 64), w_in.dtype)
    pair = lambda x, a, b: jnp.concatenate([x[:, :, a], x[:, :, b]], axis=-1)
    rope_pair = lambda x, a, b: jnp.concatenate([x[:, :, a], x[:, :, b], zeros64], axis=-1)
    wq = _bf(jnp.concatenate([pair(nope, 0, 1), rope_pair(rope, 0, 1), pair(nope, 2, 3), rope_pair(rope, 2, 3),
                              rope_pair(rope_sw, 0, 1), rope_pair(rope_sw, 2, 3)], axis=-1))
    wukv = mla_w_ukv.reshape(depth, MLA_KV_RANK, N_HEADS, 128)
    wkv = _bf(jnp.concatenate([wukv[..., :64].reshape(depth, MLA_KV_RANK, 256),
                               wukv[..., 64:].reshape(depth, MLA_KV_RANK, 256)], axis=-1))

    z64 = jnp.zeros_like(rw_w2)
    w2 = _bf(jnp.concatenate([rw_w2, z64], axis=1))
    a2 = _bf(jnp.concatenate([z64, rw_a2], axis=1))
    gla_a2p = _bf(jnp.concatenate([gla_a2, jnp.zeros((depth, 128 - gla_a2.shape[1], GLA_QK), gla_a2.dtype)], axis=1))
    return w_all, w_gate, wq, wkv, w2, a2, gla_a2p


def _rope_tables(lp):
    half = MLA_ROPE // 2
    freqs = ROPE_THETA ** (-jnp.arange(half, dtype=F32) / half)
    pos = (jnp.arange(lp) - PAD).astype(F32)
    ang = pos[:, None] * freqs[None, :]
    return jnp.tile(jnp.cos(ang), (1, 128 // half)), jnp.tile(jnp.sin(ang), (1, 128 // half))


def kernel(x, meta_tokens, norm_mix, w_in, mla_q_norm, mla_w_uq, mla_kv_norm, mla_w_ukv, rw_mu, rw_w0, rw_w2, rw_a0, rw_a2, rw_g2, rw_k_k, rw_k_a, rw_r_k, rw_ln_w, rw_ln_b, gla_a2, gla_a_b, gla_norm, gate_b, w_branch, w_out, norm_ffn, w_ffn_in, ffn_conv_w, ffn_conv_b, w_ffn_out, norm_final):
    batch, seq, _ = x.shape
    depth = w_in.shape[0]
    lp = PAD + N_META + seq
    t = batch * lp
    assert lp % ROW_TILE == 0 and lp % ATTN_TILE == 0 and lp % GLA_CHUNK == 0 and lp % RW_CHUNK == 0

    w_all, w_gate, wq, wkv, rw_w2p, rw_a2p, gla_a2p = _layout_params(
        w_in, mla_w_uq, mla_w_ukv, rw_w2, rw_a2, gla_a2)
    w_branch_b, w_out_b, w_ffn_in_b, w_ffn_out_b, rw_g2_b = map(_bf, (w_branch, w_out, w_ffn_in, w_ffn_out, rw_g2))
    vec = lambda a: a.reshape(depth, 1, -1)
    cos, sin = _rope_tables(lp)
    idx = jnp.arange(SB_SUFFIX_SPLIT)
    later_mat = jnp.where(idx[:, None] > idx[None, :], 1.0, 0.0).astype(BF16)

    h = _embed(x.reshape(batch * seq, D_MODEL), meta_tokens.astype(x.dtype), batch)

    for i in range(depth):
        z_mla, z_rw, z_sb, z_gla, sb_values = _in_proj(h, vec(norm_mix)[i], w_all, i)
        q, k, v = _mla_prep(z_mla, vec(mla_q_norm)[i], vec(mla_kv_norm)[i], wq, wkv, i, cos, sin, lp)
        y_mla = _mla_attn(q, k, v, lp)
        rw = {"rw_mu": vec(rw_mu)[i], "rw_w0": vec(rw_w0)[i], "rw_w2": rw_w2p[i], "rw_a0": vec(rw_a0)[i],
              "rw_a2": rw_a2p[i], "rw_g2": rw_g2_b[i], "rw_k_k": vec(rw_k_k)[i], "rw_k_a": vec(rw_k_a)[i],
              "rw_r_k": vec(rw_r_k)[i]}
        *chunk_terms, g, bonus = _rw_prep(z_rw, rw, lp)
        y_rw = _rw_scan(*chunk_terms, batch)
        y_sb = _sb_attn(z_sb, sb_values, later_mat, lp)
        o_gla = _gla(z_gla, gla_a2p[i], vec(gla_a_b)[i], batch)
        mp = {"rw_ln_w": vec(rw_ln_w)[i], "rw_ln_b": vec(rw_ln_b)[i], "gla_norm": vec(gla_norm)[i],
              "gate_b": gate_b[i], "norm_mix": vec(norm_mix)[i], "w_gate": w_gate, "w_branch": w_branch_b,
              "w_out": w_out_b, "layer": i}
        h = _merge(h, y_mla, y_rw, bonus, g, y_sb, o_gla, z_gla, mp, lp)
        h = _ffn(h, vec(norm_ffn)[i], w_ffn_in_b, ffn_conv_w[i], vec(ffn_conv_b)[i], w_ffn_out_b, i, lp)
    return _final_norm(h, norm_final.reshape(1, D_MODEL), batch, seq)
```

```python
import functools

import jax
import jax.numpy as jnp
from jax import lax
from jax.experimental import pallas as pl
from jax.experimental.pallas import tpu as pltpu

F32 = jnp.float32
BF16 = jnp.bfloat16

D_MODEL = 1024
DEPTH = 4
N_META = 16
BLOCK = 128
PAD = (-N_META) % BLOCK
EPS = 1e-6
NEG_INF = -1e30
LOG2_E = 1.4426950408889634

N_HEADS = 4
MLA_NOPE = 64
MLA_ROPE = 32
MLA_Q_RANK = 256
MLA_KV_RANK = 128
ROPE_THETA = 10000.0

RW_W = 256
RW_GN_EPS = 64e-5
RW_CHUNK = 64

SB_W = 256
SB_HEAD = 64
SB_SUFFIX_SPLIT = 256

GLA_DK = 32
GLA_QK = 128
GLA_W = 256
GLA_TAU = 16.0
GLA_CHUNK = 128
GLA_SUB = 16

D_FF = 2816
FFN_COL_CHUNKS = (1536, 1280)

W_MLA, W_RW, W_SB, W_GLA, W_GATE = 640, 1024, 768, 896, 4096
IN_GROUP_WIDTHS = (W_MLA, W_RW, W_SB, W_GLA)
IN_GROUP_DTYPES = (BF16, F32, BF16, F32)
IN_ROW_RW, IN_ROW_SB, IN_ROW_MLA, IN_ROW_GLA = 0, W_RW, W_RW + W_SB, W_RW + W_SB + W_MLA

ROW_TILE = 384
ATTN_TILE = 384
FINAL_ROWS = 1024
VMEM_LIMIT = 56 * 1024 * 1024


def _bf(x):
    return x.astype(BF16)


def _dot(a, b):
    return jnp.dot(_bf(a), _bf(b), preferred_element_type=F32)


def _dot_nt(a, b):
    return lax.dot_general(_bf(a), _bf(b), (((1,), (1,)), ((), ())), preferred_element_type=F32)


def _dot_tn(a, b):
    return lax.dot_general(_bf(a), _bf(b), (((0,), (0,)), ((), ())), preferred_element_type=F32)


def _split_hi_lo(x):
    hi = _bf(x)
    lo = _bf(x - hi.astype(F32))
    return hi, lo


def _dot_exact_rhs(x, m):
    hi, lo = _split_hi_lo(x)
    return jnp.dot(hi, m, preferred_element_type=F32) + jnp.dot(lo, m, preferred_element_type=F32)


def _dot_exact_lhs(m, x):
    hi, lo = _split_hi_lo(x)
    return jnp.dot(m, hi, preferred_element_type=F32) + jnp.dot(m, lo, preferred_element_type=F32)


def _iota(shape, dim):
    return lax.broadcasted_iota(jnp.int32, shape, dim)


def _div_pow2(x, d):
    assert d & (d - 1) == 0
    return lax.shift_right_logical(x, d.bit_length() - 1)


def _same_segment(shape, row_seg, col_seg):
    return _div_pow2(_iota(shape, 0), row_seg) == _div_pow2(_iota(shape, 1), col_seg)


def _segment_matrix(n, seg, value):
    return jnp.where(_same_segment((n, n), seg, seg), value, 0.0).astype(BF16)


def _head_lane_mask(width, head_width, h):
    lane = _iota((1, width), 1)
    return (lane >= h * head_width) & (lane < (h + 1) * head_width)


def _stack_heads(x, head_width):
    w = x.shape[1]
    return jnp.concatenate(
        [jnp.where(_head_lane_mask(w, head_width, h), x, 0.0) for h in range(N_HEADS)], axis=0)


def _rms(x, eps):
    return x * lax.rsqrt(jnp.mean(x * x, axis=-1, keepdims=True) + eps)


def _const_spec(shape):
    nd = len(shape)
    return pl.BlockSpec(shape, lambda *_: (0,) * nd)


def _layer_spec(stacked, layer):
    nd = stacked.ndim - 1
    return pl.BlockSpec((None,) + stacked.shape[1:], lambda *_: (layer,) + (0,) * nd)


def _params(sem, vmem=VMEM_LIMIT):
    return pltpu.CompilerParams(dimension_semantics=sem, vmem_limit_bytes=vmem)


def _embed_kernel(x_ref, meta_ref, o_ref):
    first_real = PAD + N_META

    @pl.when(pl.program_id(1) == 0)
    def _():
        o_ref[0:PAD, :] = jnp.zeros((PAD, D_MODEL), o_ref.dtype)
        o_ref[PAD:first_real, :] = meta_ref[...]
        o_ref[first_real:, :] = x_ref[0:o_ref.shape[0] - first_real, :]

    @pl.when(pl.program_id(1) > 0)
    def _():
        o_ref[...] = x_ref[...]


def _embed(x2, meta, batch):
    seq = x2.shape[0] // batch
    first_real = PAD + N_META
    lp = first_real + seq
    tm = ROW_TILE
    window = lambda b, i: (pl.multiple_of(b * seq + jnp.maximum(i * tm - first_real, 0), BLOCK), 0)
    return pl.pallas_call(
        _embed_kernel,
        grid=(batch, lp // tm),
        in_specs=[pl.BlockSpec((pl.Element(tm), pl.Element(D_MODEL)), window), _const_spec(meta.shape)],
        out_specs=pl.BlockSpec((tm, D_MODEL), lambda b, i: (b * (lp // tm) + i, 0)),
        out_shape=jax.ShapeDtypeStruct((batch * lp, D_MODEL), x2.dtype),
        compiler_params=_params(("parallel", "parallel")),
        name="embed",
    )(x2, meta)


def _in_proj_kernel(h_ref, g_ref, w_ref, *out_refs):
    mla_ref, rw_ref, sb_ref, gla_ref, sb_values_ref = out_refs
    n = _bf(_rms(h_ref[...], EPS) * g_ref[...])
    rw_ref[...] = _dot_nt(n, w_ref[IN_ROW_RW:IN_ROW_SB, :])
    sb_ref[...] = _bf(_dot_nt(n, w_ref[IN_ROW_SB:IN_ROW_MLA, :]))
    both = _dot_nt(n, w_ref[IN_ROW_MLA:IN_ROW_GLA + W_GLA, :])
    mla_ref[...] = _bf(both[:, :W_MLA])
    gla_ref[...] = both[:, W_MLA:]
    sb_values_ref[0] = _head_stacked_values(sb_ref[:, 2 * SB_W:3 * SB_W])


def _in_proj(h, g, w_all, layer):
    t = h.shape[0]
    tm = ROW_TILE
    return pl.pallas_call(
        _in_proj_kernel,
        grid=(t // tm,),
        in_specs=[pl.BlockSpec((tm, D_MODEL), lambda i: (i, 0)),
                  _const_spec((1, D_MODEL)),
                  _layer_spec(w_all, layer)],
        out_specs=[pl.BlockSpec((tm, w), lambda i: (i, 0)) for w in IN_GROUP_WIDTHS]
        + [pl.BlockSpec((1, N_HEADS * tm, SB_W), lambda i: (i, 0, 0))],
        out_shape=[jax.ShapeDtypeStruct((t, w), dt) for w, dt in zip(IN_GROUP_WIDTHS, IN_GROUP_DTYPES)]
        + [jax.ShapeDtypeStruct((t // tm, N_HEADS * tm, SB_W), BF16)],
        compiler_params=_params(("parallel",)),
        name="in_proj",
    )(h, g, w_all)


def _mla_prep_kernel(z_ref, qg_ref, kvg_ref, wq_ref, wkv_ref, cos_ref, sin_ref, q_ref, k_ref, vt_ref):
    z = z_ref[...].astype(F32)
    nq = _rms(z[:, 0:256], EPS) * qg_ref[...]
    nkv = _rms(z[:, 256:384], EPS) * kvg_ref[...]
    cos = cos_ref[...]
    sin = sin_ref[...]
    ql = _dot(nq, wq_ref[...])
    scale = (MLA_NOPE + MLA_ROPE) ** -0.5
    q = jnp.concatenate([ql[:, 0:128], ql[:, 128:256] * cos + ql[:, 512:640] * sin,
                         ql[:, 256:384], ql[:, 384:512] * cos + ql[:, 640:768] * sin], axis=1)
    q_ref[...] = _bf(q * scale)
    kvl = _dot(nkv, wkv_ref[...])
    k_rope = z[:, 384:512] * cos + z[:, 512:640] * sin
    k_rope = jnp.where(_iota((1, 128), 1) < 2 * MLA_ROPE, k_rope, 0.0)
    k_ref[...] = _bf(jnp.concatenate([kvl[:, 0:128], k_rope, kvl[:, 128:256], k_rope], axis=1))
    vt = _bf(kvl[:, 256:512].T)
    row_head = _div_pow2(_iota((256, 1), 0), 64)
    vt_ref[0] = jnp.concatenate([jnp.where(row_head == h, vt, jnp.zeros((), BF16)) for h in range(N_HEADS)], axis=1)


def _mla_prep(z_mla, qg, kvg, wq, wkv, layer, cos, sin, lp):
    t = z_mla.shape[0]
    tm = ROW_TILE
    nb = lp // tm
    row = lambda b, i: (b * nb + i, 0)
    return pl.pallas_call(
        _mla_prep_kernel,
        grid=(t // lp, nb),
        in_specs=[pl.BlockSpec((tm, W_MLA), row),
                  _const_spec((1, MLA_Q_RANK)), _const_spec((1, MLA_KV_RANK)),
                  _layer_spec(wq, layer), _layer_spec(wkv, layer),
                  pl.BlockSpec((tm, 128), lambda b, i: (i, 0)),
                  pl.BlockSpec((tm, 128), lambda b, i: (i, 0))],
        out_specs=[pl.BlockSpec((tm, 512), row), pl.BlockSpec((tm, 512), row),
                   pl.BlockSpec((1, 256, N_HEADS * tm), lambda b, i: (b * nb + i, 0, 0))],
        out_shape=[jax.ShapeDtypeStruct((t, 512), BF16), jax.ShapeDtypeStruct((t, 512), BF16),
                   jax.ShapeDtypeStruct((t // tm, 256, N_HEADS * tm), BF16)],
        compiler_params=_params(("parallel", "parallel")),
        name="mla_prep",
    )(z_mla, qg, kvg, wq, wkv, cos, sin)


def _head_stacked_values(vb):
    zero = jnp.zeros((), vb.dtype)
    return jnp.concatenate([jnp.where(_head_lane_mask(256, 64, h), vb, zero) for h in range(N_HEADS)], axis=0)


def _per_head_rows(rows, n):
    return jnp.concatenate([jnp.broadcast_to(r, (64, n)) for r in rows], axis=0)


def _mla_attn_kernel(q_ref, k_ref, vt_ref, o_ref, qh_ref, m_ref, l_ref, acc_ref, p_ref, alpha_ref, *, tile):
    i = pl.program_id(1)
    q_pos = i * tile + _iota((1, tile), 1)
    lane = _iota((1, 256), 1)
    for h in range(N_HEADS):
        half, slot = h // 2, h % 2
        head_lanes = (((lane >= slot * MLA_NOPE) & (lane < (slot + 1) * MLA_NOPE))
                      | ((lane >= 128 + slot * MLA_ROPE) & (lane < 128 + (slot + 1) * MLA_ROPE)))
        qh_ref[half, slot * tile:(slot + 1) * tile, :] = jnp.where(
            head_lanes, q_ref[:, half * 256:(half + 1) * 256], jnp.zeros((), BF16))
    m_ref[...] = jnp.full(m_ref.shape, NEG_INF, F32)
    l_ref[...] = jnp.zeros(l_ref.shape, F32)
    acc_ref[...] = jnp.zeros(acc_ref.shape, F32)

    def add_values(j, slot):
        pv = jnp.dot(vt_ref[j], p_ref[slot], preferred_element_type=F32)
        acc_ref[...] = _per_head_rows([alpha_ref[slot, h] for h in range(N_HEADS)], tile) * acc_ref[...] + pv

    def block(j, masked, prev, slot):
        start = pl.multiple_of(j * tile, tile)
        heads = range(N_HEADS)
        s_pair = [_dot_nt(k_ref[pl.ds(start, tile), half * 256:(half + 1) * 256], qh_ref[half]) for half in (0, 1)]
        s = [s_pair[h // 2][:, (h % 2) * tile:(h % 2 + 1) * tile] for h in heads]
        if prev is not None:
            add_values(prev, 1 - slot)
        for h in heads:
            s_h = s[h]
            if masked:
                k_pos = start + _iota((tile, 1), 0)
                s_h = jnp.where((k_pos <= q_pos) & (k_pos >= PAD), s_h, NEG_INF)
            m_old = m_ref[h]
            m_new = jnp.maximum(m_old, jnp.max(s_h, axis=0, keepdims=True))
            p = jnp.exp(s_h - m_new)
            alpha = jnp.exp(m_old - m_new)
            l_ref[h] = alpha * l_ref[h] + jnp.sum(p, axis=0, keepdims=True)
            m_ref[h] = m_new
            p_ref[slot, h * tile:(h + 1) * tile, :] = _bf(p)
            alpha_ref[slot, h] = alpha

    block(i, True, None, 0)

    @pl.when(i > 0)
    def _():
        block(0, True, i, 1)

    interior_blocks = jnp.maximum(i - 1, 0)

    def interior_pair(t, carry):
        j = 1 + 2 * t
        block(j, False, j - 1, 0)
        block(j + 1, False, j, 1)
        return carry

    lax.fori_loop(0, lax.shift_right_logical(interior_blocks, 1), interior_pair, 0)
    odd = (interior_blocks & 1) == 1

    @pl.when(odd)
    def _():
        block(i - 1, False, i - 2, 0)

    last = jnp.maximum(i - 1, 0)
    last_in_slot0 = (i == 0) | odd

    @pl.when(last_in_slot0)
    def _():
        add_values(last, 0)

    @pl.when(jnp.logical_not(last_in_slot0))
    def _():
        add_values(last, 1)
    o_ref[...] = _bf((acc_ref[...] / _per_head_rows([l_ref[h] for h in range(N_HEADS)], tile)).T)


def _mla_attn(q, k, vt, lp):
    t = q.shape[0]
    tile = ATTN_TILE
    nb = lp // tile
    return pl.pallas_call(
        functools.partial(_mla_attn_kernel, tile=tile),
        grid=(t // lp, nb),
        in_specs=[pl.BlockSpec((tile, 512), lambda b, i: (b * nb + i, 0)),
                  pl.BlockSpec((lp, 512), lambda b, i: (b, 0)),
                  pl.BlockSpec((nb, 256, N_HEADS * tile), lambda b, i: (b, 0, 0))],
        out_specs=pl.BlockSpec((tile, 256), lambda b, i: (b * nb + i, 0)),
        out_shape=jax.ShapeDtypeStruct((t, 256), BF16),
        scratch_shapes=[pltpu.VMEM((2, 2 * tile, 256), BF16), pltpu.VMEM((N_HEADS, 1, tile), F32),
                        pltpu.VMEM((N_HEADS, 1, tile), F32), pltpu.VMEM((256, tile), F32),
                        pltpu.VMEM((2, N_HEADS * tile, tile), BF16), pltpu.VMEM((2, N_HEADS, 1, tile), F32)],
        compiler_params=_params(("parallel", "arbitrary")),
        name="mla_attn",
    )(q, k, vt)


def _sb_attn_kernel(q_ref, k_ref, v_ref, later_ref, o_ref, qh_ref, c_ref, acc_ref, wts_ref, *, tile):
    i = pl.program_id(1)
    row = i * tile + _iota((tile, 1), 0)
    for h in range(N_HEADS):
        qh_ref[h] = jnp.where(_head_lane_mask(SB_W, SB_HEAD, h), q_ref[...], jnp.zeros((), BF16))
    c_ref[...] = jnp.zeros(c_ref.shape, F32)
    acc_ref[...] = jnp.zeros(acc_ref.shape, F32)

    def add_values(j, slot):
        acc_ref[...] += jnp.dot(wts_ref[slot], v_ref[j], preferred_element_type=F32)

    def block(j, masked, prev, slot):
        start = pl.multiple_of(j * tile, tile)
        heads = range(N_HEADS)
        cut = SB_SUFFIX_SPLIT
        z = [_dot_nt(qh_ref[h], k_ref[pl.ds(start, tile), :]) for h in heads]
        if prev is not None:
            add_values(prev, 1 - slot)
        log_take = [jnp.minimum(z[h], 0.0) - jnp.log2(1.0 + jnp.exp2(-jnp.abs(z[h]))) for h in heads]
        log_keep = [log_take[h] - z[h] for h in heads]
        if masked:
            mask = (start + _iota((1, tile), 1)) < row
            log_keep = [jnp.where(mask, log_keep[h], 0.0) for h in heads]
        keep16 = [_bf(log_keep[h]) for h in heads]
        later_head = [jnp.dot(keep16[h][:, :cut], later_ref[...], preferred_element_type=F32) for h in heads]
        later_tail = [jnp.dot(keep16[h][:, cut:], later_ref[:tile - cut, :tile - cut], preferred_element_type=F32)
                      for h in heads]
        for h in heads:
            head_sum = jnp.sum(log_keep[h][:, :cut], axis=-1, keepdims=True)
            tail_sum = jnp.sum(log_keep[h][:, cut:], axis=-1, keepdims=True)
            c = c_ref[h]
            later = jnp.concatenate([later_head[h] + tail_sum, later_tail[h]], axis=1)
            w = jnp.exp2(log_take[h] + later + c)
            if masked:
                w = jnp.where(mask, w, 0.0)
            c_ref[h] = c + (head_sum + tail_sum)
            wts_ref[slot, :, h * tile:(h + 1) * tile] = _bf(w)

    block(i, True, None, 0)

    def below_pair(t, carry):
        j = i - 1 - 2 * t
        block(j, False, j + 1, 1)
        block(j - 1, False, j, 0)
        return carry

    lax.fori_loop(0, lax.shift_right_logical(i, 1), below_pair, 0)
    odd = (i & 1) == 1

    @pl.when(odd)
    def _():
        block(0, False, 1, 1)
        add_values(0, 1)

    @pl.when(jnp.logical_not(odd))
    def _():
        add_values(0, 0)
    o_ref[...] = _bf(acc_ref[...])


def _sb_attn(z_sb, sb_values, later_mat, lp):
    t = z_sb.shape[0]
    tile = ATTN_TILE
    nb = lp // tile
    return pl.pallas_call(
        functools.partial(_sb_attn_kernel, tile=tile),
        grid=(t // lp, nb),
        in_specs=[pl.BlockSpec((tile, SB_W), lambda b, i: (b * nb + i, 0)),
                  pl.BlockSpec((lp, SB_W), lambda b, i: (b, 1)),
                  pl.BlockSpec((nb, N_HEADS * tile, SB_W), lambda b, i: (b, 0, 0)),
                  _const_spec((SB_SUFFIX_SPLIT, SB_SUFFIX_SPLIT))],
        out_specs=pl.BlockSpec((tile, SB_W), lambda b, i: (b * nb + i, 0)),
        out_shape=jax.ShapeDtypeStruct((t, SB_W), BF16),
        scratch_shapes=[pltpu.VMEM((N_HEADS, tile, SB_W), BF16), pltpu.VMEM((N_HEADS, tile, 1), F32),
                        pltpu.VMEM((tile, SB_W), F32), pltpu.VMEM((2, tile, N_HEADS * tile), BF16)],
        compiler_params=_params(("parallel", "arbitrary")),
        name="sb_attn",
    )(z_sb, z_sb, sb_values, later_mat)


def _compact_eye(c):
    return jnp.where(_iota((c, N_HEADS * c), 0) == (_iota((c, N_HEADS * c), 1) & (c - 1)), 1.0, 0.0)


def _rw_prep_kernel(z_ref, zprev_ref, mu_ref, w0_ref, w2_ref, a0_ref, a2_ref, g2_ref, kk_ref, ka_ref, rk_ref,
                    w_o, rt_o, arb_o, kbar_o, bbar_o, v_o, uv_o, yv_o, dec_o, g_o, bonus_o):
    i = pl.program_id(1)
    z = z_ref[...]
    tm = z.shape[0]
    prev = jnp.where(i == 0, 0.0, zprev_ref[7:8, :])
    shifted = jnp.where(_iota((tm, 1), 0) == 0, prev, pltpu.roll(z, 1, 0))
    zz = z + (shifted - z) * mu_ref[...]
    r, k, v = zz[:, 0:256], zz[:, 256:512], zz[:, 512:768]
    lora_in = zz[:, 768:896]
    u = w0_ref[...] + _dot(jnp.tanh(lora_in), w2_ref[...])
    w = jnp.minimum(u, 0.0) - jnp.log1p(jnp.exp(-jnp.abs(u))) - 0.5
    a = jax.nn.sigmoid(a0_ref[...] + _dot(lora_in, a2_ref[...]))
    seg = _segment_matrix(RW_W, 64, 1.0)
    kx = k * kk_ref[...]
    kap = kx / jnp.maximum(jnp.sqrt(_dot_exact_rhs(kx * kx, seg)), 1e-12)
    kmod = k * (1.0 + (a - 1.0) * ka_ref[...])
    beta = kap * a
    lw = -jnp.exp(w)
    g_o[...] = _dot(jax.nn.sigmoid(zz[:, 896:1024]), g2_ref[...])
    bonus_o[...] = _dot_exact_rhs(r * kmod * rk_ref[...], seg) * v

    c = RW_CHUNK
    same_chunk = _same_segment((tm, tm), c, c)
    lw_hi, lw_lo = _split_hi_lo(lw)
    cum = jnp.where(same_chunk & (_iota((tm, tm), 0) >= _iota((tm, tm), 1)), 1.0, 0.0).astype(BF16)
    b = jnp.dot(cum, lw_hi, preferred_element_type=F32) + jnp.dot(cum, lw_lo, preferred_element_type=F32)
    b_end = jnp.concatenate(
        [jnp.broadcast_to(b[n * c + c - 1:(n + 1) * c, :], (c, RW_W)) for n in range(tm // c)], axis=0)
    grow = jnp.exp(-b)
    to_end = jnp.exp(b_end - b)
    kap_t = kap * jnp.exp(b - lw)
    r_t = r * jnp.exp(b)
    beta_g = beta * grow
    k_g = kmod * grow
    rt_o[...] = _bf(r_t)
    kbar_o[...] = _bf(kmod * to_end)
    bbar_o[...] = _bf(beta * to_end)
    v_o[...] = _bf(v)
    decay_end = jnp.exp(b_end)

    lane_pos = _iota((c, N_HEADS * c), 1) & (c - 1)
    strictly_earlier = _iota((c, N_HEADS * c), 0) > lane_pos
    not_later = _iota((c, N_HEADS * c), 0) >= lane_pos
    chunks = [slice(n * c, (n + 1) * c) for n in range(tm // c)]
    pair = [_dot_nt(jnp.concatenate([kap_t[rows], r_t[rows]], axis=0),
                    jnp.concatenate([_stack_heads(beta_g[rows], 64), _stack_heads(k_g[rows], 64)], axis=0))
            for rows in chunks]
    a_kb = [jnp.where(strictly_earlier, p[0:c, 0:4 * c], 0.0) for p in pair]
    a_kk = [jnp.where(strictly_earlier, p[0:c, 4 * c:8 * c], 0.0) for p in pair]
    a_rk = [jnp.where(not_later, p[c:2 * c, 4 * c:8 * c], 0.0) for p in pair]
    for rows, p in zip(chunks, pair):
        arb_o[rows, :] = _bf(jnp.where(not_later, p[c:2 * c, 0:4 * c], 0.0))
    inv = [_compact_eye(c) - a for a in a_kb]
    power = [_dot(a, _stack_heads(a, c)) for a in a_kb]
    span = 2
    while span < c:
        inv = [t + _dot(t, _stack_heads(p, c)) for t, p in zip(inv, power)]
        span *= 2
        if span < c:
            power = [_dot(p, _stack_heads(p, c)) for p in power]
    v_st = [_stack_heads(v[rows], 64) for rows in chunks]
    kk_v = [_dot(a, vs) for a, vs in zip(a_kk, v_st)]
    for n, rows in enumerate(chunks):
        w_o[rows, :] = _bf(_dot(inv[n], _stack_heads(kap_t[rows], 64)))
        uv_o[rows, :] = _dot(inv[n], _stack_heads(kk_v[n], 64))
        yv_o[rows, :] = _dot(a_rk[n], v_st[n])
        dec_o[n * 8:(n + 1) * 8, :] = decay_end[n * c:n * c + 8]


def _rw_prep(z_rw, p, lp):
    t = z_rw.shape[0]
    tm = ROW_TILE
    nb = lp // tm
    row = lambda b, i: (b * nb + i, 0)
    prev = lambda b, i: (jnp.maximum(b * (lp // 8) + i * (tm // 8) - 1, 0), 0)
    consts = [p["rw_mu"], p["rw_w0"], p["rw_w2"], p["rw_a0"], p["rw_a2"], p["rw_g2"],
              p["rw_k_k"], p["rw_k_a"], p["rw_r_k"]]
    return pl.pallas_call(
        _rw_prep_kernel,
        grid=(t // lp, nb),
        in_specs=[pl.BlockSpec((tm, W_RW), row), pl.BlockSpec((8, W_RW), prev)]
        + [_const_spec(c.shape) for c in consts],
        out_specs=[pl.BlockSpec((tm, RW_W), row)] * 8
        + [pl.BlockSpec((tm // RW_CHUNK * 8, RW_W), row)] + [pl.BlockSpec((tm, RW_W), row)] * 2,
        out_shape=[jax.ShapeDtypeStruct((t, RW_W), BF16)] * 6 + [jax.ShapeDtypeStruct((t, RW_W), F32)] * 2
        + [jax.ShapeDtypeStruct((t // RW_CHUNK * 8, RW_W), F32)] + [jax.ShapeDtypeStruct((t, RW_W), F32)] * 2,
        compiler_params=_params(("parallel", "parallel")),
        name="rw_prep",
    )(z_rw, z_rw, *consts)


def _recurrent_kernel(w_ref, rt_ref, arb_ref, kbar_ref, bbar_ref, rv_ref, uv_ref, yv_ref, dec_ref,
                      q_ref, k_ref, v_ref, al_ref, a2_ref, ab_ref, y_ref, o_ref, ht_ref, st_ref):
    c = GLA_CHUNK
    rc = RW_CHUNK
    seqs = range(st_ref.shape[0])

    @pl.when(pl.program_id(0) == 0)
    def _():
        ht_ref[...] = jnp.zeros_like(ht_ref)
        st_ref[...] = jnp.zeros_like(st_ref)

    rw_same_head = _same_segment((RW_W, RW_W), 64, 64)

    def rw_from_state(n, ht):
        rows = slice(n * rc, (n + 1) * rc)
        return [_dot_nt(jnp.concatenate([w_ref[s, rows], rt_ref[s, rows]], axis=0), ht[s]) for s in seqs]

    def rw_finish(n, from_state, ht):
        rows = slice(n * rc, (n + 1) * rc)
        u = [from_state[s][0:rc] + uv_ref[s, rows] for s in seqs]
        from_u = [_dot(arb_ref[s, rows], _stack_heads(u[s], 64)) for s in seqs]
        upd = [_dot_tn(jnp.concatenate([rv_ref[s, rows], _bf(u[s])], axis=0),
                       jnp.concatenate([kbar_ref[s, rows], -bbar_ref[s, rows]], axis=0)) for s in seqs]
        for s in seqs:
            y_ref[s, rows] = from_state[s][rc:2 * rc] + yv_ref[s, rows] - from_u[s]
        return [ht[s] * dec_ref[s, n * 8:n * 8 + 1, :] + jnp.where(rw_same_head, upd[s], 0.0) for s in seqs]

    sub = GLA_SUB
    nsub = c // sub
    row_i = _iota((c, c), 0)
    col_i = _iota((c, c), 1)
    tri = jnp.where(row_i >= col_i, 1.0, 0.0).astype(BF16)
    tri_sub = jnp.where((row_i >= col_i) & _same_segment((c, c), sub, sub), 1.0, 0.0).astype(BF16)
    key_pos = _iota((c, 1), 0)
    query_in_sub = _iota((N_HEADS * sub, c), 0) & (sub - 1)
    gla_same_head = _same_segment((GLA_W, GLA_QK), 64, GLA_DK)

    ht = [ht_ref[s] for s in seqs]
    rw_state_0 = rw_from_state(0, ht)

    x = [_dot(al_ref[s], a2_ref[...]) + ab_ref[...] for s in seqs]
    log_a = [(jnp.minimum(x[s], 0.0) - jnp.log1p(jnp.exp(-jnp.abs(x[s])))) * (1.0 / GLA_TAU) for s in seqs]
    b = [_dot_exact_lhs(tri, log_a[s]) for s in seqs]
    b_sub = [_dot_exact_lhs(tri_sub, log_a[s]) for s in seqs]
    q = [q_ref[s] * (GLA_DK ** -0.5) for s in seqs]
    st = [st_ref[s] for s in seqs]
    inter = [_dot_nt(q[s] * jnp.exp(b[s]), st[s]) for s in seqs]

    ht = rw_finish(0, rw_state_0, ht)
    rw_state_1 = rw_from_state(1, ht)

    scores = []
    for s in seqs:
        q_sub = q[s] * jnp.exp(b_sub[s])
        beta = b[s] - b_sub[s]
        k = k_ref[s]
        blocks = []
        for blk in range(nsub):
            lo, hi = blk * sub, (blk + 1) * sub
            expo = jnp.where(key_pos < hi, beta[lo:lo + 1, :] - b[s], NEG_INF)
            sc = _dot_nt(_stack_heads(q_sub[lo:hi], GLA_DK), k * jnp.exp(expo))
            blocks.append(jnp.where(_iota((N_HEADS * sub, c), 1) <= lo + query_in_sub, sc, 0.0))
        scores.append(jnp.concatenate(blocks, axis=0))

    ht = rw_finish(1, rw_state_1, ht)
    for s in seqs:
        ht_ref[s] = ht[s]

    per_head = [_dot(scores[s], v_ref[s]) for s in seqs]
    upd = [_dot_tn(v_ref[s], k_ref[s] * jnp.exp(b[s][c - 1:c, :] - b[s])) for s in seqs]
    for s in seqs:
        pieces = []
        for blk in range(nsub):
            piece = jnp.zeros((sub, GLA_W), F32)
            for h in range(N_HEADS):
                r0 = (blk * N_HEADS + h) * sub
                piece = piece + jnp.where(_head_lane_mask(GLA_W, 64, h), per_head[s][r0:r0 + sub], 0.0)
            pieces.append(piece)
        o_ref[s] = inter[s] + jnp.concatenate(pieces, axis=0)
        st_ref[s] = st[s] * jnp.exp(b[s][c - 1:c, :]) + jnp.where(gla_same_head, upd[s], 0.0)


def _recurrent(rw_terms, z_gla, a2, ab, batch):
    t = z_gla.shape[0]
    lp = t // batch
    c = GLA_CHUNK
    assert c == 2 * RW_CHUNK
    as_seq = lambda a: a.reshape(batch, a.shape[0] // batch, a.shape[1])
    rows = pl.BlockSpec((batch, c, RW_W), lambda i: (0, i, 0))
    z3 = as_seq(z_gla)
    y, o = pl.pallas_call(
        _recurrent_kernel,
        grid=(lp // c,),
        in_specs=[rows] * 8 + [pl.BlockSpec((batch, 16, RW_W), lambda i: (0, i, 0)),
                               pl.BlockSpec((batch, c, 128), lambda i: (0, i, 0)),
                               pl.BlockSpec((batch, c, 128), lambda i: (0, i, 1)),
                               pl.BlockSpec((batch, c, 256), lambda i: (0, i, 1)),
                               pl.BlockSpec((batch, c, 128), lambda i: (0, i, 6)),
                               _const_spec(a2.shape), _const_spec(ab.shape)],
        out_specs=[rows, pl.BlockSpec((batch, c, GLA_W), lambda i: (0, i, 0))],
        out_shape=[jax.ShapeDtypeStruct((batch, lp, RW_W), F32), jax.ShapeDtypeStruct((batch, lp, GLA_W), F32)],
        scratch_shapes=[pltpu.VMEM((batch, RW_W, RW_W), F32), pltpu.VMEM((batch, GLA_W, GLA_QK), F32)],
        compiler_params=_params(("arbitrary",)),
        name="recurrent",
    )(*map(as_seq, rw_terms), z3, z3, z3, z3, a2, ab)
    return y.reshape(t, RW_W), o.reshape(t, GLA_W)


def _merge_kernel(h_ref, ymla_ref, yrw_ref, bonus_ref, g_ref, ysb_ref, ogla_ref, rgla_ref,
                  nmix_ref, lnw_ref, lnb_ref, gn_ref, gb_ref, wg_ref, wb_ref, wo_ref, o_ref, *, tiles_per_seq):
    tm = h_ref.shape[0]
    avg = _segment_matrix(256, 64, 1.0 / 64)
    h = h_ref[...]
    n = _bf(_rms(h, EPS) * nmix_ref[...])

    y = yrw_ref[...]
    d = y - _dot_exact_rhs(y, avg)
    var = _dot_exact_rhs(d * d, avg)
    y_rw = (d * lax.rsqrt(var + RW_GN_EPS) * lnw_ref[...] + lnb_ref[...] + bonus_ref[...]) * g_ref[...]

    o = ogla_ref[...]
    r = rgla_ref[...]
    y_gla = o * lax.rsqrt(_dot_exact_rhs(o * o, avg) + EPS) * gn_ref[...] * (r * jax.nn.sigmoid(r))

    acc = jnp.zeros((tm, D_MODEL), F32)
    for m, y_m in enumerate((ymla_ref[...], y_rw, ysb_ref[...], y_gla)):
        logits = _dot_nt(n, wg_ref[m * D_MODEL:(m + 1) * D_MODEL, :])
        gate = jax.nn.sigmoid(logits + gb_ref[m:m + 1, :])
        acc = acc + gate * _dot(y_m, wb_ref[m])
    delta = _dot(acc, wo_ref[...])
    row = (pl.program_id(0) % tiles_per_seq) * tm + _iota((tm, 1), 0)
    o_ref[...] = h + jnp.where(row >= PAD, delta, 0.0)


def _merge(h, y_mla, y_rw, bonus, g, y_sb, o_gla, z_gla, p, lp):
    t = h.shape[0]
    tm = ROW_TILE
    row = lambda i: (i, 0)
    w256 = pl.BlockSpec((tm, 256), row)
    consts = [p["norm_mix"], p["rw_ln_w"], p["rw_ln_b"], p["gla_norm"], p["gate_b"]]
    stacked = [p["w_gate"], p["w_branch"], p["w_out"]]
    return pl.pallas_call(
        functools.partial(_merge_kernel, tiles_per_seq=lp // tm),
        grid=(t // tm,),
        in_specs=[pl.BlockSpec((tm, D_MODEL), row), w256, w256, w256, w256, w256, w256,
                  pl.BlockSpec((tm, 256), lambda i: (i, 2))]
        + [_const_spec(c.shape) for c in consts]
        + [_layer_spec(w, p["layer"]) for w in stacked],
        out_specs=pl.BlockSpec((tm, D_MODEL), row),
        out_shape=jax.ShapeDtypeStruct((t, D_MODEL), F32),
        compiler_params=_params(("parallel",)),
        name="merge",
    )(h, y_mla, y_rw, bonus, g, y_sb, o_gla, z_gla, *consts, *stacked)


def _ffn_kernel(h_ref, g_ref, win_ref, cw_ref, cb_ref, wout_ref, o_ref, tail_ref):
    @pl.when(pl.program_id(1) == 0)
    def _():
        tail_ref[...] = jnp.zeros_like(tail_ref)

    x = h_ref[...]
    tm = x.shape[0]
    n = _bf(_rms(x, EPS) * g_ref[...])
    rowi = _iota((tm, 1), 0)
    acc = jnp.zeros((tm, D_MODEL), F32)
    assert sum(FFN_COL_CHUNKS) == D_FF
    for c0, c1 in zip((0, FFN_COL_CHUNKS[0]), (FFN_COL_CHUNKS[0], D_FF)):
        a = jnp.dot(n, win_ref[:, c0:c1], preferred_element_type=F32)
        u = jnp.dot(n, win_ref[:, D_FF + c0:D_FF + c1], preferred_element_type=F32)
        prev1 = tail_ref[7:8, c0:c1]
        prev2 = tail_ref[6:7, c0:c1]
        a1 = jnp.where(rowi == 0, prev1, pltpu.roll(a, 1, 0))
        a2 = jnp.where(rowi == 0, prev2, jnp.where(rowi == 1, prev1, pltpu.roll(a, 2, 0)))
        tail_ref[:, c0:c1] = a[tm - 8:tm, :]
        conv = cb_ref[:, c0:c1] + cw_ref[0:1, c0:c1] * a2 + cw_ref[1:2, c0:c1] * a1 + cw_ref[2:3, c0:c1] * a
        acc = acc + _dot(conv * jax.nn.sigmoid(conv) * u, wout_ref[c0:c1, :])
    o_ref[...] = x + acc


def _ffn(h, g, w_in, conv_w, conv_b, w_out, layer, lp):
    t = h.shape[0]
    tm = ROW_TILE
    nb = lp // tm
    row = lambda b, i: (b * nb + i, 0)
    return pl.pallas_call(
        _ffn_kernel,
        grid=(t // lp, nb),
        in_specs=[pl.BlockSpec((tm, D_MODEL), row), _const_spec((1, D_MODEL)), _layer_spec(w_in, layer),
                  _const_spec(conv_w.shape), _const_spec(conv_b.shape), _layer_spec(w_out, layer)],
        out_specs=pl.BlockSpec((tm, D_MODEL), row),
        out_shape=jax.ShapeDtypeStruct((t, D_MODEL), F32),
        scratch_shapes=[pltpu.VMEM((8, D_FF), F32)],
        compiler_params=_params(("arbitrary", "arbitrary")),
        name="conv_ffn",
    )(h, g, w_in, conv_w, conv_b, w_out)


def _final_norm_kernel(h_ref, g_ref, o_ref):
    o_ref[0] = _rms(h_ref[...], EPS) * g_ref[...]


def _final_norm(h, g, batch, seq):
    lp = h.shape[0] // batch
    rows = FINAL_ROWS
    assert seq % rows == 0
    first = lambda bi, i: (pl.multiple_of(bi * lp + (PAD + N_META) + i * rows, BLOCK), 0)
    return pl.pallas_call(
        _final_norm_kernel,
        grid=(batch, seq // rows),
        in_specs=[pl.BlockSpec((pl.Element(rows), pl.Element(D_MODEL)), first), _const_spec((1, D_MODEL))],
        out_specs=pl.BlockSpec((1, rows, D_MODEL), lambda bi, i: (bi, i, 0)),
        out_shape=jax.ShapeDtypeStruct((batch, seq, D_MODEL), F32),
        compiler_params=_params(("parallel", "parallel")),
        name="final_norm",
    )(h, g)


def _rope_swap(w):
    half = w.shape[-1] // 2
    return jnp.concatenate([-w[..., half:], w[..., :half]], axis=-1)


def _in_weight_layout_kernel(w_ref, o_ref, gate_ref):
    def put(dst, val):
        o_ref[0, dst:dst + val.shape[0], :] = _bf(val)

    mla, sb, gla = IN_ROW_MLA, IN_ROW_SB, IN_ROW_GLA
    put(mla, w_ref[0, 0:384, :])
    half = MLA_ROPE // 2
    kr = w_ref[0, 384:384 + MLA_ROPE, :]
    kr_swapped = jnp.concatenate([-kr[half:], kr[:half]], axis=0)
    put(mla + 384, jnp.concatenate([kr] * 4, axis=0))
    put(mla + 512, jnp.concatenate([kr_swapped] * 4, axis=0))
    put(IN_ROW_RW, w_ref[0, 416:1440, :])
    put(sb, w_ref[0, 1440:1440 + SB_W, :] * (SB_HEAD ** -0.5 * LOG2_E))
    put(sb + SB_W, w_ref[0, 1440 + SB_W:2208, :])
    put(gla, w_ref[0, 2208:2720, :])
    put(gla + 512, w_ref[0, 2736:2992, :])
    lora = w_ref[0, 2720:2736, :]
    put(gla + 768, jnp.concatenate([lora, jnp.zeros((W_GLA - 784, lora.shape[1]), lora.dtype)], axis=0))
    gate_ref[0] = _bf(w_ref[0, 2992:2992 + W_GATE, :])


def _in_weight_layout(w_in):
    wt = jnp.swapaxes(w_in, 1, 2)
    depth, n, d = wt.shape
    cols = 256
    total = sum(IN_GROUP_WIDTHS)
    return pl.pallas_call(
        _in_weight_layout_kernel,
        grid=(depth, d // cols),
        in_specs=[pl.BlockSpec((1, n, cols), lambda l, i: (l, 0, i))],
        out_specs=[pl.BlockSpec((1, total, cols), lambda l, i: (l, 0, i)),
                   pl.BlockSpec((1, W_GATE, cols), lambda l, i: (l, 0, i))],
        out_shape=[jax.ShapeDtypeStruct((depth, total, d), BF16), jax.ShapeDtypeStruct((depth, W_GATE, d), BF16)],
        compiler_params=_params(("parallel", "parallel")),
        name="in_weight_layout",
    )(wt)


def _layout_params(w_in, mla_w_uq, mla_w_ukv, rw_w2, rw_a2, gla_a2):
    w_all, w_gate = _in_weight_layout(w_in)

    depth = w_in.shape[0]
    wuq = mla_w_uq.reshape(depth, MLA_Q_RANK, N_HEADS, MLA_NOPE + MLA_ROPE)
    nope, rope = wuq[..., :MLA_NOPE], wuq[..., MLA_NOPE:]
    rope_sw = _rope_swap(rope)
    zeros64 = jnp.zeros((depth, MLA_Q_RANK, 64), w_in.dtype)
    pair = lambda x, a, b: jnp.concatenate([x[:, :, a], x[:, :, b]], axis=-1)
    rope_pair = lambda x, a, b: jnp.concatenate([x[:, :, a], x[:, :, b], zeros64], axis=-1)
    wq = _bf(jnp.concatenate([pair(nope, 0, 1), rope_pair(rope, 0, 1), pair(nope, 2, 3), rope_pair(rope, 2, 3),
                              rope_pair(rope_sw, 0, 1), rope_pair(rope_sw, 2, 3)], axis=-1))
    wukv = mla_w_ukv.reshape(depth, MLA_KV_RANK, N_HEADS, 128)
    wkv = _bf(jnp.concatenate([wukv[..., :64].reshape(depth, MLA_KV_RANK, 256),
                               wukv[..., 64:].reshape(depth, MLA_KV_RANK, 256)], axis=-1))

    z64 = jnp.zeros_like(rw_w2)
    w2 = _bf(jnp.concatenate([rw_w2, z64], axis=1))
    a2 = _bf(jnp.concatenate([z64, rw_a2], axis=1))
    gla_a2p = _bf(jnp.concatenate([gla_a2, jnp.zeros((depth, 128 - gla_a2.shape[1], GLA_QK), gla_a2.dtype)], axis=1))
    return w_all, w_gate, wq, wkv, w2, a2, gla_a2p


def _rope_tables(lp):
    half = MLA_ROPE // 2
    freqs = ROPE_THETA ** (-jnp.arange(half, dtype=F32) / half)
    pos = (jnp.arange(lp) - PAD).astype(F32)
    ang = pos[:, None] * freqs[None, :]
    return jnp.tile(jnp.cos(ang), (1, 128 // half)), jnp.tile(jnp.sin(ang), (1, 128 // half))


def kernel(x, meta_tokens, norm_mix, w_in, mla_q_norm, mla_w_uq, mla_kv_norm, mla_w_ukv, rw_mu, rw_w0, rw_w2, rw_a0, rw_a2, rw_g2, rw_k_k, rw_k_a, rw_r_k, rw_ln_w, rw_ln_b, gla_a2, gla_a_b, gla_norm, gate_b, w_branch, w_out, norm_ffn, w_ffn_in, ffn_conv_w, ffn_conv_b, w_ffn_out, norm_final):
    batch, seq, _ = x.shape
    depth = w_in.shape[0]
    lp = PAD + N_META + seq
    t = batch * lp
    assert lp % ROW_TILE == 0 and lp % ATTN_TILE == 0 and lp % GLA_CHUNK == 0 and lp % RW_CHUNK == 0

    w_all, w_gate, wq, wkv, rw_w2p, rw_a2p, gla_a2p = _layout_params(
        w_in, mla_w_uq, mla_w_ukv, rw_w2, rw_a2, gla_a2)
    w_branch_b, w_out_b, w_ffn_in_b, w_ffn_out_b, rw_g2_b = map(_bf, (w_branch, w_out, w_ffn_in, w_ffn_out, rw_g2))
    vec = lambda a: a.reshape(depth, 1, -1)
    cos, sin = _rope_tables(lp)
    idx = jnp.arange(SB_SUFFIX_SPLIT)
    later_mat = jnp.where(idx[:, None] > idx[None, :], 1.0, 0.0).astype(BF16)

    h = _embed(x.reshape(batch * seq, D_MODEL), meta_tokens.astype(x.dtype), batch)

    for i in range(depth):
        z_mla, z_rw, z_sb, z_gla, sb_values = _in_proj(h, vec(norm_mix)[i], w_all, i)
        q, k, v = _mla_prep(z_mla, vec(mla_q_norm)[i], vec(mla_kv_norm)[i], wq, wkv, i, cos, sin, lp)
        y_mla = _mla_attn(q, k, v, lp)
        rw = {"rw_mu": vec(rw_mu)[i], "rw_w0": vec(rw_w0)[i], "rw_w2": rw_w2p[i], "rw_a0": vec(rw_a0)[i],
              "rw_a2": rw_a2p[i], "rw_g2": rw_g2_b[i], "rw_k_k": vec(rw_k_k)[i], "rw_k_a": vec(rw_k_a)[i],
              "rw_r_k": vec(rw_r_k)[i]}
        *chunk_terms, g, bonus = _rw_prep(z_rw, rw, lp)
        y_rw, o_gla = _recurrent(chunk_terms, z_gla, gla_a2p[i], vec(gla_a_b)[i], batch)
        y_sb = _sb_attn(z_sb, sb_values, later_mat, lp)
        mp = {"rw_ln_w": vec(rw_ln_w)[i], "rw_ln_b": vec(rw_ln_b)[i], "gla_norm": vec(gla_norm)[i],
              "gate_b": gate_b[i], "norm_mix": vec(norm_mix)[i], "w_gate": w_gate, "w_branch": w_branch_b,
              "w_out": w_out_b, "layer": i}
        h = _merge(h, y_mla, y_rw, bonus, g, y_sb, o_gla, z_gla, mp, lp)
        h = _ffn(h, vec(norm_ffn)[i], w_ffn_in_b, ffn_conv_w[i], vec(ffn_conv_b)[i], w_ffn_out_b, i, lp)
    return _final_norm(h, norm_final.reshape(1, D_MODEL), batch, seq)
```

```python
import functools

import jax
import jax.numpy as jnp
from jax import lax
from jax.experimental import pallas as pl
from jax.experimental.pallas import tpu as pltpu

F32 = jnp.float32
BF16 = jnp.bfloat16

D_MODEL = 1024
DEPTH = 4
N_META = 16
BLOCK = 128
PAD = (-N_META) % BLOCK
EPS = 1e-6
NEG_INF = -1e30
LOG2_E = 1.4426950408889634

N_HEADS = 4
MLA_NOPE = 64
MLA_ROPE = 32
MLA_Q_RANK = 256
MLA_KV_RANK = 128
MLA_PAD_LANE = 192
ROPE_THETA = 10000.0

RW_W = 256
RW_GN_EPS = 64e-5
RW_CHUNK = 64

SB_W = 256
SB_HEAD = 64
SB_SUFFIX_SPLIT = 256

GLA_DK = 32
GLA_QK = 128
GLA_W = 256
GLA_TAU = 16.0
GLA_CHUNK = 128
GLA_SUB = 16

D_FF = 2816
FFN_COL_CHUNKS = (1536, 1280)

W_MLA, W_RW, W_SB, W_GLA, W_GATE = 640, 1024, 768, 896, 4096
IN_GROUP_WIDTHS = (W_MLA, W_RW, W_SB, W_GLA)
IN_GROUP_DTYPES = (BF16, F32, BF16, F32)
IN_ROW_RW, IN_ROW_SB, IN_ROW_MLA, IN_ROW_GLA = 0, W_RW, W_RW + W_SB, W_RW + W_SB + W_MLA

ROW_TILE = 384
ATTN_TILE = 384
FINAL_ROWS = 1024
VMEM_LIMIT = 56 * 1024 * 1024


def _bf(x):
    return x.astype(BF16)


def _dot(a, b):
    return jnp.dot(_bf(a), _bf(b), preferred_element_type=F32)


def _dot_nt(a, b):
    return lax.dot_general(_bf(a), _bf(b), (((1,), (1,)), ((), ())), preferred_element_type=F32)


def _dot_tn(a, b):
    return lax.dot_general(_bf(a), _bf(b), (((0,), (0,)), ((), ())), preferred_element_type=F32)


def _split_hi_lo(x):
    hi = _bf(x)
    lo = _bf(x - hi.astype(F32))
    return hi, lo


def _dot_exact_rhs(x, m):
    hi, lo = _split_hi_lo(x)
    return jnp.dot(hi, m, preferred_element_type=F32) + jnp.dot(lo, m, preferred_element_type=F32)


def _dot_exact_lhs(m, x):
    hi, lo = _split_hi_lo(x)
    return jnp.dot(m, hi, preferred_element_type=F32) + jnp.dot(m, lo, preferred_element_type=F32)


def _iota(shape, dim):
    return lax.broadcasted_iota(jnp.int32, shape, dim)


def _div_pow2(x, d):
    assert d & (d - 1) == 0
    return lax.shift_right_logical(x, d.bit_length() - 1)


def _same_segment(shape, row_seg, col_seg):
    return _div_pow2(_iota(shape, 0), row_seg) == _div_pow2(_iota(shape, 1), col_seg)


def _segment_matrix(n, seg, value):
    return jnp.where(_same_segment((n, n), seg, seg), value, 0.0).astype(BF16)


def _head_lane_mask(width, head_width, h):
    lane = _iota((1, width), 1)
    return (lane >= h * head_width) & (lane < (h + 1) * head_width)


def _stack_heads(x, head_width):
    w = x.shape[1]
    return jnp.concatenate(
        [jnp.where(_head_lane_mask(w, head_width, h), x, 0.0) for h in range(N_HEADS)], axis=0)


def _rms(x, eps):
    return x * lax.rsqrt(jnp.mean(x * x, axis=-1, keepdims=True) + eps)


def _const_spec(shape):
    nd = len(shape)
    return pl.BlockSpec(shape, lambda *_: (0,) * nd)


def _layer_spec(stacked, layer):
    nd = stacked.ndim - 1
    return pl.BlockSpec((None,) + stacked.shape[1:], lambda *_: (layer,) + (0,) * nd)


def _params(sem, vmem=VMEM_LIMIT):
    return pltpu.CompilerParams(dimension_semantics=sem, vmem_limit_bytes=vmem)


def _embed_kernel(x_ref, meta_ref, o_ref):
    first_real = PAD + N_META

    @pl.when(pl.program_id(1) == 0)
    def _():
        o_ref[0:PAD, :] = jnp.zeros((PAD, D_MODEL), o_ref.dtype)
        o_ref[PAD:first_real, :] = meta_ref[...]
        o_ref[first_real:, :] = x_ref[0:o_ref.shape[0] - first_real, :]

    @pl.when(pl.program_id(1) > 0)
    def _():
        o_ref[...] = x_ref[...]


def _embed(x2, meta, batch):
    seq = x2.shape[0] // batch
    first_real = PAD + N_META
    lp = first_real + seq
    tm = ROW_TILE
    window = lambda b, i: (pl.multiple_of(b * seq + jnp.maximum(i * tm - first_real, 0), BLOCK), 0)
    return pl.pallas_call(
        _embed_kernel,
        grid=(batch, lp // tm),
        in_specs=[pl.BlockSpec((pl.Element(tm), pl.Element(D_MODEL)), window), _const_spec(meta.shape)],
        out_specs=pl.BlockSpec((tm, D_MODEL), lambda b, i: (b * (lp // tm) + i, 0)),
        out_shape=jax.ShapeDtypeStruct((batch * lp, D_MODEL), x2.dtype),
        compiler_params=_params(("parallel", "parallel")),
        name="embed",
    )(x2, meta)


def _in_proj_kernel(h_ref, g_ref, w_ref, *out_refs):
    mla_ref, rw_ref, sb_ref, gla_ref, sb_values_ref = out_refs
    n = _bf(_rms(h_ref[...], EPS) * g_ref[...])
    rw_ref[...] = _dot_nt(n, w_ref[IN_ROW_RW:IN_ROW_SB, :])
    sb_ref[...] = _bf(_dot_nt(n, w_ref[IN_ROW_SB:IN_ROW_MLA, :]))
    both = _dot_nt(n, w_ref[IN_ROW_MLA:IN_ROW_GLA + W_GLA, :])
    mla_ref[...] = _bf(both[:, :W_MLA])
    gla_ref[...] = both[:, W_MLA:]
    sb_values_ref[0] = _head_stacked_values(sb_ref[:, 2 * SB_W:3 * SB_W])


def _in_proj(h, g, w_all, layer):
    t = h.shape[0]
    tm = ROW_TILE
    return pl.pallas_call(
        _in_proj_kernel,
        grid=(t // tm,),
        in_specs=[pl.BlockSpec((tm, D_MODEL), lambda i: (i, 0)),
                  _const_spec((1, D_MODEL)),
                  _layer_spec(w_all, layer)],
        out_specs=[pl.BlockSpec((tm, w), lambda i: (i, 0)) for w in IN_GROUP_WIDTHS]
        + [pl.BlockSpec((1, N_HEADS * tm, SB_W), lambda i: (i, 0, 0))],
        out_shape=[jax.ShapeDtypeStruct((t, w), dt) for w, dt in zip(IN_GROUP_WIDTHS, IN_GROUP_DTYPES)]
        + [jax.ShapeDtypeStruct((t // tm, N_HEADS * tm, SB_W), BF16)],
        compiler_params=_params(("parallel",)),
        name="in_proj",
    )(h, g, w_all)


def _mla_prep_kernel(z_ref, qg_ref, kvg_ref, wq_ref, wkv_ref, cos_ref, sin_ref, q_ref, k_ref, vt_ref):
    z = z_ref[...].astype(F32)
    nq = _rms(z[:, 0:256], EPS) * qg_ref[...]
    nkv = _rms(z[:, 256:384], EPS) * kvg_ref[...]
    cos = cos_ref[...]
    sin = sin_ref[...]
    ql = _dot(nq, wq_ref[...])
    scale = (MLA_NOPE + MLA_ROPE) ** -0.5
    q = jnp.concatenate([ql[:, 0:128], ql[:, 128:256] * cos + ql[:, 512:640] * sin,
                         ql[:, 256:384], ql[:, 384:512] * cos + ql[:, 640:768] * sin], axis=1)
    pad_lane = (_iota((1, 512), 1) & 255) == MLA_PAD_LANE
    q_ref[...] = _bf(jnp.where(pad_lane, 1.0, q * scale))
    kvl = _dot(nkv, wkv_ref[...])
    k_rope = z[:, 384:512] * cos + z[:, 512:640] * sin
    lane = _iota((1, 128), 1)
    is_pad_key = (pl.program_id(1) * z.shape[0] + _iota((z.shape[0], 1), 0)) < PAD
    k_rope = jnp.where(lane < 2 * MLA_ROPE, k_rope,
                       jnp.where((lane == MLA_PAD_LANE - 128) & is_pad_key, NEG_INF, 0.0))
    k_ref[...] = _bf(jnp.concatenate([kvl[:, 0:128], k_rope, kvl[:, 128:256], k_rope], axis=1))
    vt = _bf(kvl[:, 256:512].T)
    row_head = _div_pow2(_iota((256, 1), 0), 64)
    vt_ref[0] = jnp.concatenate([jnp.where(row_head == h, vt, jnp.zeros((), BF16)) for h in range(N_HEADS)], axis=1)


def _mla_prep(z_mla, qg, kvg, wq, wkv, layer, cos, sin, lp):
    t = z_mla.shape[0]
    tm = ROW_TILE
    nb = lp // tm
    row = lambda b, i: (b * nb + i, 0)
    return pl.pallas_call(
        _mla_prep_kernel,
        grid=(t // lp, nb),
        in_specs=[pl.BlockSpec((tm, W_MLA), row),
                  _const_spec((1, MLA_Q_RANK)), _const_spec((1, MLA_KV_RANK)),
                  _layer_spec(wq, layer), _layer_spec(wkv, layer),
                  pl.BlockSpec((tm, 128), lambda b, i: (i, 0)),
                  pl.BlockSpec((tm, 128), lambda b, i: (i, 0))],
        out_specs=[pl.BlockSpec((tm, 512), row), pl.BlockSpec((tm, 512), row),
                   pl.BlockSpec((1, 256, N_HEADS * tm), lambda b, i: (b * nb + i, 0, 0))],
        out_shape=[jax.ShapeDtypeStruct((t, 512), BF16), jax.ShapeDtypeStruct((t, 512), BF16),
                   jax.ShapeDtypeStruct((t // tm, 256, N_HEADS * tm), BF16)],
        compiler_params=_params(("parallel", "parallel")),
        name="mla_prep",
    )(z_mla, qg, kvg, wq, wkv, cos, sin)


def _head_stacked_values(vb):
    zero = jnp.zeros((), vb.dtype)
    return jnp.concatenate([jnp.where(_head_lane_mask(256, 64, h), vb, zero) for h in range(N_HEADS)], axis=0)


def _per_head_rows(rows, n):
    return jnp.concatenate([jnp.broadcast_to(r, (64, n)) for r in rows], axis=0)


def _mla_attn_kernel(q_ref, k_ref, vt_ref, o_ref, qh_ref, m_ref, l_ref, acc_ref, p_ref, alpha_ref, *, tile):
    i = pl.program_id(1)
    q_pos = i * tile + _iota((1, tile), 1)
    lane = _iota((1, 256), 1)
    for h in range(N_HEADS):
        half, slot = h // 2, h % 2
        head_lanes = (((lane >= slot * MLA_NOPE) & (lane < (slot + 1) * MLA_NOPE))
                      | ((lane >= 128 + slot * MLA_ROPE) & (lane < 128 + (slot + 1) * MLA_ROPE)))
        qh_ref[half, slot * tile:(slot + 1) * tile, :] = jnp.where(
            head_lanes, q_ref[:, half * 256:(half + 1) * 256], jnp.zeros((), BF16))
    m_ref[...] = jnp.full(m_ref.shape, NEG_INF, F32)
    l_ref[...] = jnp.zeros(l_ref.shape, F32)
    acc_ref[...] = jnp.zeros(acc_ref.shape, F32)

    def add_values(j, slot):
        pv = jnp.dot(vt_ref[j], p_ref[slot], preferred_element_type=F32)
        acc_ref[...] = _per_head_rows([alpha_ref[slot, h] for h in range(N_HEADS)], tile) * acc_ref[...] + pv

    def block(j, masked, prev, slot):
        start = pl.multiple_of(j * tile, tile)
        heads = range(N_HEADS)
        s_pair = [_dot_nt(k_ref[pl.ds(start, tile), half * 256:(half + 1) * 256], qh_ref[half]) for half in (0, 1)]
        s = [s_pair[h // 2][:, (h % 2) * tile:(h % 2 + 1) * tile] for h in heads]
        if prev is not None:
            add_values(prev, 1 - slot)
        for h in heads:
            s_h = s[h]
            if masked:
                k_pos = start + _iota((tile, 1), 0)
                s_h = jnp.where((k_pos <= q_pos) & (k_pos >= PAD), s_h, NEG_INF)
            m_old = m_ref[h]
            m_new = jnp.maximum(m_old, jnp.max(s_h, axis=0, keepdims=True))
            p = jnp.exp(s_h - m_new)
            alpha = jnp.exp(m_old - m_new)
            l_ref[h] = alpha * l_ref[h] + jnp.sum(p, axis=0, keepdims=True)
            m_ref[h] = m_new
            p_ref[slot, h * tile:(h + 1) * tile, :] = _bf(p)
            alpha_ref[slot, h] = alpha

    block(i, True, None, 0)

    @pl.when(i > 0)
    def _():
        block(0, True, i, 1)

    interior_blocks = jnp.maximum(i - 1, 0)

    def interior_pair(t, carry):
        j = 1 + 2 * t
        block(j, False, j - 1, 0)
        block(j + 1, False, j, 1)
        return carry

    lax.fori_loop(0, lax.shift_right_logical(interior_blocks, 1), interior_pair, 0)
    odd = (interior_blocks & 1) == 1

    @pl.when(odd)
    def _():
        block(i - 1, False, i - 2, 0)

    last = jnp.maximum(i - 1, 0)
    last_in_slot0 = (i == 0) | odd

    @pl.when(last_in_slot0)
    def _():
        add_values(last, 0)

    @pl.when(jnp.logical_not(last_in_slot0))
    def _():
        add_values(last, 1)
    o_ref[...] = _bf((acc_ref[...] / _per_head_rows([l_ref[h] for h in range(N_HEADS)], tile)).T)


def _mla_attn(q, k, vt, lp):
    t = q.shape[0]
    tile = ATTN_TILE
    nb = lp // tile
    return pl.pallas_call(
        functools.partial(_mla_attn_kernel, tile=tile),
        grid=(t // lp, nb),
        in_specs=[pl.BlockSpec((tile, 512), lambda b, i: (b * nb + i, 0)),
                  pl.BlockSpec((lp, 512), lambda b, i: (b, 0)),
                  pl.BlockSpec((nb, 256, N_HEADS * tile), lambda b, i: (b, 0, 0))],
        out_specs=pl.BlockSpec((tile, 256), lambda b, i: (b * nb + i, 0)),
        out_shape=jax.ShapeDtypeStruct((t, 256), BF16),
        scratch_shapes=[pltpu.VMEM((2, 2 * tile, 256), BF16), pltpu.VMEM((N_HEADS, 1, tile), F32),
                        pltpu.VMEM((N_HEADS, 1, tile), F32), pltpu.VMEM((256, tile), F32),
                        pltpu.VMEM((2, N_HEADS * tile, tile), BF16), pltpu.VMEM((2, N_HEADS, 1, tile), F32)],
        compiler_params=_params(("parallel", "arbitrary")),
        name="mla_attn",
    )(q, k, vt)


def _sb_attn_kernel(q_ref, k_ref, v_ref, later_ref, o_ref, qh_ref, c_ref, acc_ref, wts_ref, *, tile):
    i = pl.program_id(1)
    row = i * tile + _iota((tile, 1), 0)
    for h in range(N_HEADS):
        qh_ref[h] = jnp.where(_head_lane_mask(SB_W, SB_HEAD, h), q_ref[...], jnp.zeros((), BF16))
    c_ref[...] = jnp.zeros(c_ref.shape, F32)
    acc_ref[...] = jnp.zeros(acc_ref.shape, F32)

    def add_values(j, slot):
        acc_ref[...] += jnp.dot(wts_ref[slot], v_ref[j], preferred_element_type=F32)

    def block(j, masked, prev, slot):
        start = pl.multiple_of(j * tile, tile)
        heads = range(N_HEADS)
        cut = SB_SUFFIX_SPLIT
        z = [_dot_nt(qh_ref[h], k_ref[pl.ds(start, tile), :]) for h in heads]
        if prev is not None:
            add_values(prev, 1 - slot)
        log_take = [jnp.minimum(z[h], 0.0) - jnp.log2(1.0 + jnp.exp2(-jnp.abs(z[h]))) for h in heads]
        log_keep = [log_take[h] - z[h] for h in heads]
        if masked:
            mask = (start + _iota((1, tile), 1)) < row
            log_keep = [jnp.where(mask, log_keep[h], 0.0) for h in heads]
        keep16 = [_bf(log_keep[h]) for h in heads]
        later_head = [jnp.dot(keep16[h][:, :cut], later_ref[...], preferred_element_type=F32) for h in heads]
        later_tail = [jnp.dot(keep16[h][:, cut:], later_ref[:tile - cut, :tile - cut], preferred_element_type=F32)
                      for h in heads]
        for h in heads:
            head_sum = jnp.sum(log_keep[h][:, :cut], axis=-1, keepdims=True)
            tail_sum = jnp.sum(log_keep[h][:, cut:], axis=-1, keepdims=True)
            c = c_ref[h]
            later = jnp.concatenate([later_head[h] + tail_sum, later_tail[h]], axis=1)
            w = jnp.exp2(log_take[h] + later + c)
            if masked:
                w = jnp.where(mask, w, 0.0)
            c_ref[h] = c + (head_sum + tail_sum)
            wts_ref[slot, :, h * tile:(h + 1) * tile] = _bf(w)

    block(i, True, None, 0)

    def below_pair(t, carry):
        j = i - 1 - 2 * t
        block(j, False, j + 1, 1)
        block(j - 1, False, j, 0)
        return carry

    lax.fori_loop(0, lax.shift_right_logical(i, 1), below_pair, 0)
    odd = (i & 1) == 1

    @pl.when(odd)
    def _():
        block(0, False, 1, 1)
        add_values(0, 1)

    @pl.when(jnp.logical_not(odd))
    def _():
        add_values(0, 0)
    o_ref[...] = _bf(acc_ref[...])


def _sb_attn(z_sb, sb_values, later_mat, lp):
    t = z_sb.shape[0]
    tile = ATTN_TILE
    nb = lp // tile
    return pl.pallas_call(
        functools.partial(_sb_attn_kernel, tile=tile),
        grid=(t // lp, nb),
        in_specs=[pl.BlockSpec((tile, SB_W), lambda b, i: (b * nb + i, 0)),
                  pl.BlockSpec((lp, SB_W), lambda b, i: (b, 1)),
                  pl.BlockSpec((nb, N_HEADS * tile, SB_W), lambda b, i: (b, 0, 0)),
                  _const_spec((SB_SUFFIX_SPLIT, SB_SUFFIX_SPLIT))],
        out_specs=pl.BlockSpec((tile, SB_W), lambda b, i: (b * nb + i, 0)),
        out_shape=jax.ShapeDtypeStruct((t, SB_W), BF16),
        scratch_shapes=[pltpu.VMEM((N_HEADS, tile, SB_W), BF16), pltpu.VMEM((N_HEADS, tile, 1), F32),
                        pltpu.VMEM((tile, SB_W), F32), pltpu.VMEM((2, tile, N_HEADS * tile), BF16)],
        compiler_params=_params(("parallel", "arbitrary")),
        name="sb_attn",
    )(z_sb, z_sb, sb_values, later_mat)


def _attn_kernel(sq_ref, sk_ref, sv_ref, later_ref, mq_ref, mk_ref, mvt_ref, ysb_ref, ymla_ref,
                 sqh_ref, c_ref, sacc_ref, wts_ref, mqh_ref, m_ref, l_ref, macc_ref, p_ref, alpha_ref, *, tile):
    i = pl.program_id(1)
    heads = range(N_HEADS)
    q_row = i * tile + _iota((tile, 1), 0)
    q_pos = i * tile + _iota((1, tile), 1)
    for h in heads:
        sqh_ref[h] = jnp.where(_head_lane_mask(SB_W, SB_HEAD, h), sq_ref[...], jnp.zeros((), BF16))
    lane = _iota((1, 256), 1)
    for h in heads:
        half, slot = h // 2, h % 2
        head_lanes = (((lane >= slot * MLA_NOPE) & (lane < (slot + 1) * MLA_NOPE))
                      | ((lane >= 128 + slot * MLA_ROPE) & (lane < 128 + (slot + 1) * MLA_ROPE))
                      | (lane == MLA_PAD_LANE))
        mqh_ref[half, slot * tile:(slot + 1) * tile, :] = jnp.where(
            head_lanes, mq_ref[:, half * 256:(half + 1) * 256], jnp.zeros((), BF16))
    c_ref[...] = jnp.zeros(c_ref.shape, F32)
    sacc_ref[...] = jnp.zeros(sacc_ref.shape, F32)
    m_ref[...] = jnp.full(m_ref.shape, NEG_INF, F32)
    l_ref[...] = jnp.zeros(l_ref.shape, F32)
    macc_ref[...] = jnp.zeros(macc_ref.shape, F32)

    def add_values(j, slot):
        sacc_ref[...] += jnp.dot(wts_ref[slot], sv_ref[j], preferred_element_type=F32)
        pv = jnp.dot(mvt_ref[j], p_ref[slot], preferred_element_type=F32)
        macc_ref[...] = _per_head_rows([alpha_ref[slot, h] for h in heads], tile) * macc_ref[...] + pv

    def block(j, masked, prev, slot):
        start = pl.multiple_of(j * tile, tile)
        cut = SB_SUFFIX_SPLIT
        z = [_dot_nt(sqh_ref[h], sk_ref[pl.ds(start, tile), :]) for h in heads]
        s_pair = [_dot_nt(mk_ref[pl.ds(start, tile), half * 256:(half + 1) * 256], mqh_ref[half]) for half in (0, 1)]
        if prev is not None:
            add_values(prev, 1 - slot)

        log_take = [jnp.minimum(z[h], 0.0) - jnp.log2(1.0 + jnp.exp2(-jnp.abs(z[h]))) for h in heads]
        log_keep = [log_take[h] - z[h] for h in heads]
        if masked:
            mask = (start + _iota((1, tile), 1)) < q_row
            log_keep = [jnp.where(mask, log_keep[h], 0.0) for h in heads]
        keep16 = [_bf(log_keep[h]) for h in heads]
        later_head = [jnp.dot(keep16[h][:, :cut], later_ref[...], preferred_element_type=F32) for h in heads]
        later_tail = [jnp.dot(keep16[h][:, cut:], later_ref[:tile - cut, :tile - cut], preferred_element_type=F32)
                      for h in heads]

        for h in heads:
            s_h = s_pair[h // 2][:, (h % 2) * tile:(h % 2 + 1) * tile]
            if masked:
                s_h = jnp.where((start + _iota((tile, 1), 0)) <= q_pos, s_h, NEG_INF)
            m_old = m_ref[h]
            m_new = jnp.maximum(m_old, jnp.max(s_h, axis=0, keepdims=True))
            p = jnp.exp(s_h - m_new)
            alpha = jnp.exp(m_old - m_new)
            l_ref[h] = alpha * l_ref[h] + jnp.sum(p, axis=0, keepdims=True)
            m_ref[h] = m_new
            p_ref[slot, h * tile:(h + 1) * tile, :] = _bf(p)
            alpha_ref[slot, h] = alpha

        for h in heads:
            head_sum = jnp.sum(log_keep[h][:, :cut], axis=-1, keepdims=True)
            tail_sum = jnp.sum(log_keep[h][:, cut:], axis=-1, keepdims=True)
            c = c_ref[h]
            later = jnp.concatenate([later_head[h] + tail_sum, later_tail[h]], axis=1)
            w = jnp.exp2(log_take[h] + later + c)
            if masked:
                w = jnp.where(mask, w, 0.0)
            c_ref[h] = c + (head_sum + tail_sum)
            wts_ref[slot, :, h * tile:(h + 1) * tile] = _bf(w)

    block(i, True, None, 0)

    def below_pair(t, carry):
        j = i - 1 - 2 * t
        block(j, False, j + 1, 1)
        block(j - 1, False, j, 0)
        return carry

    lax.fori_loop(0, lax.shift_right_logical(i, 1), below_pair, 0)
    odd = (i & 1) == 1

    @pl.when(odd)
    def _():
        block(0, False, 1, 1)
        add_values(0, 1)

    @pl.when(jnp.logical_not(odd))
    def _():
        add_values(0, 0)
    ysb_ref[...] = _bf(sacc_ref[...])
    ymla_ref[...] = _bf((macc_ref[...] / _per_head_rows([l_ref[h] for h in heads], tile)).T)


def _attn(z_sb, sb_values, later_mat, q, k, vt, lp):
    t = z_sb.shape[0]
    tile = ATTN_TILE
    nb = lp // tile
    tiles = lambda w: pl.BlockSpec((tile, w), lambda b, i: (b * nb + i, 0))
    resident = lambda shape, index: pl.BlockSpec(shape, index, pipeline_mode=pl.Buffered(1))
    return pl.pallas_call(
        functools.partial(_attn_kernel, tile=tile),
        grid=(t // lp, nb),
        in_specs=[tiles(SB_W),
                  resident((lp, SB_W), lambda b, i: (b, 1)),
                  resident((nb, N_HEADS * tile, SB_W), lambda b, i: (b, 0, 0)),
                  _const_spec((SB_SUFFIX_SPLIT, SB_SUFFIX_SPLIT)),
                  tiles(512),
                  resident((lp, 512), lambda b, i: (b, 0)),
                  resident((nb, 256, N_HEADS * tile), lambda b, i: (b, 0, 0))],
        out_specs=[tiles(SB_W), tiles(256)],
        out_shape=[jax.ShapeDtypeStruct((t, SB_W), BF16), jax.ShapeDtypeStruct((t, 256), BF16)],
        scratch_shapes=[pltpu.VMEM((N_HEADS, tile, SB_W), BF16), pltpu.VMEM((N_HEADS, tile, 1), F32),
                        pltpu.VMEM((tile, SB_W), F32), pltpu.VMEM((2, tile, N_HEADS * tile), BF16),
                        pltpu.VMEM((2, 2 * tile, 256), BF16), pltpu.VMEM((N_HEADS, 1, tile), F32),
                        pltpu.VMEM((N_HEADS, 1, tile), F32), pltpu.VMEM((256, tile), F32),
                        pltpu.VMEM((2, N_HEADS * tile, tile), BF16), pltpu.VMEM((2, N_HEADS, 1, tile), F32)],
        compiler_params=_params(("parallel", "arbitrary")),
        name="attn",
    )(z_sb, z_sb, sb_values, later_mat, q, k, vt)


def _compact_eye(c):
    return jnp.where(_iota((c, N_HEADS * c), 0) == (_iota((c, N_HEADS * c), 1) & (c - 1)), 1.0, 0.0)


def _rw_prep_kernel(z_ref, zprev_ref, mu_ref, w0_ref, w2_ref, a0_ref, a2_ref, g2_ref, kk_ref, ka_ref, rk_ref,
                    w_o, rt_o, arb_o, kbar_o, bbar_o, v_o, uv_o, yv_o, dec_o, g_o, bonus_o):
    i = pl.program_id(1)
    z = z_ref[...]
    tm = z.shape[0]
    prev = jnp.where(i == 0, 0.0, zprev_ref[7:8, :])
    shifted = jnp.where(_iota((tm, 1), 0) == 0, prev, pltpu.roll(z, 1, 0))
    zz = z + (shifted - z) * mu_ref[...]
    r, k, v = zz[:, 0:256], zz[:, 256:512], zz[:, 512:768]
    lora_in = zz[:, 768:896]
    u = w0_ref[...] + _dot(jnp.tanh(lora_in), w2_ref[...])
    w = jnp.minimum(u, 0.0) - jnp.log1p(jnp.exp(-jnp.abs(u))) - 0.5
    a = jax.nn.sigmoid(a0_ref[...] + _dot(lora_in, a2_ref[...]))
    seg = _segment_matrix(RW_W, 64, 1.0)
    kx = k * kk_ref[...]
    kap = kx / jnp.maximum(jnp.sqrt(_dot_exact_rhs(kx * kx, seg)), 1e-12)
    kmod = k * (1.0 + (a - 1.0) * ka_ref[...])
    beta = kap * a
    lw = -jnp.exp(w)
    g_o[...] = _dot(jax.nn.sigmoid(zz[:, 896:1024]), g2_ref[...])
    bonus_o[...] = _dot_exact_rhs(r * kmod * rk_ref[...], seg) * v

    c = RW_CHUNK
    same_chunk = _same_segment((tm, tm), c, c)
    lw_hi, lw_lo = _split_hi_lo(lw)
    cum = jnp.where(same_chunk & (_iota((tm, tm), 0) >= _iota((tm, tm), 1)), 1.0, 0.0).astype(BF16)
    b = jnp.dot(cum, lw_hi, preferred_element_type=F32) + jnp.dot(cum, lw_lo, preferred_element_type=F32)
    b_end = jnp.concatenate(
        [jnp.broadcast_to(b[n * c + c - 1:(n + 1) * c, :], (c, RW_W)) for n in range(tm // c)], axis=0)
    grow = jnp.exp(-b)
    to_end = jnp.exp(b_end - b)
    kap_t = kap * jnp.exp(b - lw)
    r_t = r * jnp.exp(b)
    beta_g = beta * grow
    k_g = kmod * grow
    rt_o[...] = _bf(r_t)
    kbar_o[...] = _bf(kmod * to_end)
    bbar_o[...] = _bf(beta * to_end)
    v_o[...] = _bf(v)
    decay_end = jnp.exp(b_end)

    lane_pos = _iota((c, N_HEADS * c), 1) & (c - 1)
    strictly_earlier = _iota((c, N_HEADS * c), 0) > lane_pos
    not_later = _iota((c, N_HEADS * c), 0) >= lane_pos
    chunks = [slice(n * c, (n + 1) * c) for n in range(tm // c)]
    pair = [_dot_nt(jnp.concatenate([kap_t[rows], r_t[rows]], axis=0),
                    jnp.concatenate([_stack_heads(beta_g[rows], 64), _stack_heads(k_g[rows], 64)], axis=0))
            for rows in chunks]
    a_kb = [jnp.where(strictly_earlier, p[0:c, 0:4 * c], 0.0) for p in pair]
    a_kk = [jnp.where(strictly_earlier, p[0:c, 4 * c:8 * c], 0.0) for p in pair]
    a_rk = [jnp.where(not_later, p[c:2 * c, 4 * c:8 * c], 0.0) for p in pair]
    for rows, p in zip(chunks, pair):
        arb_o[rows, :] = _bf(jnp.where(not_later, p[c:2 * c, 0:4 * c], 0.0))
    inv = [_compact_eye(c) - a for a in a_kb]
    power = [_dot(a, _stack_heads(a, c)) for a in a_kb]
    span = 2
    while span < c:
        inv = [t + _dot(t, _stack_heads(p, c)) for t, p in zip(inv, power)]
        span *= 2
        if span < c:
            power = [_dot(p, _stack_heads(p, c)) for p in power]
    v_st = [_stack_heads(v[rows], 64) for rows in chunks]
    kk_v = [_dot(a, vs) for a, vs in zip(a_kk, v_st)]
    for n, rows in enumerate(chunks):
        w_o[rows, :] = _bf(_dot(inv[n], _stack_heads(kap_t[rows], 64)))
        uv_o[rows, :] = _dot(inv[n], _stack_heads(kk_v[n], 64))
        yv_o[rows, :] = _dot(a_rk[n], v_st[n])
        dec_o[n * 8:(n + 1) * 8, :] = decay_end[n * c:n * c + 8]


def _rw_prep(z_rw, p, lp):
    t = z_rw.shape[0]
    tm = ROW_TILE
    nb = lp // tm
    row = lambda b, i: (b * nb + i, 0)
    prev = lambda b, i: (jnp.maximum(b * (lp // 8) + i * (tm // 8) - 1, 0), 0)
    consts = [p["rw_mu"], p["rw_w0"], p["rw_w2"], p["rw_a0"], p["rw_a2"], p["rw_g2"],
              p["rw_k_k"], p["rw_k_a"], p["rw_r_k"]]
    return pl.pallas_call(
        _rw_prep_kernel,
        grid=(t // lp, nb),
        in_specs=[pl.BlockSpec((tm, W_RW), row), pl.BlockSpec((8, W_RW), prev)]
        + [_const_spec(c.shape) for c in consts],
        out_specs=[pl.BlockSpec((tm, RW_W), row)] * 8
        + [pl.BlockSpec((tm // RW_CHUNK * 8, RW_W), row)] + [pl.BlockSpec((tm, RW_W), row)] * 2,
        out_shape=[jax.ShapeDtypeStruct((t, RW_W), BF16)] * 6 + [jax.ShapeDtypeStruct((t, RW_W), F32)] * 2
        + [jax.ShapeDtypeStruct((t // RW_CHUNK * 8, RW_W), F32)] + [jax.ShapeDtypeStruct((t, RW_W), F32)] * 2,
        compiler_params=_params(("parallel", "parallel")),
        name="rw_prep",
    )(z_rw, z_rw, *consts)


def _recurrent_kernel(w_ref, rt_ref, arb_ref, kbar_ref, bbar_ref, rv_ref, uv_ref, yv_ref, dec_ref,
                      q_ref, k_ref, v_ref, al_ref, a2_ref, ab_ref, y_ref, o_ref, ht_ref, st_ref):
    c = GLA_CHUNK
    rc = RW_CHUNK
    seqs = range(st_ref.shape[0])

    @pl.when(pl.program_id(0) == 0)
    def _():
        ht_ref[...] = jnp.zeros_like(ht_ref)
        st_ref[...] = jnp.zeros_like(st_ref)

    rw_same_head = _same_segment((RW_W, RW_W), 64, 64)

    def rw_from_state(n, ht):
        rows = slice(n * rc, (n + 1) * rc)
        return [_dot_nt(jnp.concatenate([w_ref[s, rows], rt_ref[s, rows]], axis=0), ht[s]) for s in seqs]

    def rw_finish(n, from_state, ht):
        rows = slice(n * rc, (n + 1) * rc)
        u = [from_state[s][0:rc] + uv_ref[s, rows] for s in seqs]
        from_u = [_dot(arb_ref[s, rows], _stack_heads(u[s], 64)) for s in seqs]
        upd = [_dot_tn(jnp.concatenate([rv_ref[s, rows], _bf(u[s])], axis=0),
                       jnp.concatenate([kbar_ref[s, rows], -bbar_ref[s, rows]], axis=0)) for s in seqs]
        for s in seqs:
            y_ref[s, rows] = from_state[s][rc:2 * rc] + yv_ref[s, rows] - from_u[s]
        return [ht[s] * dec_ref[s, n * 8:n * 8 + 1, :] + jnp.where(rw_same_head, upd[s], 0.0) for s in seqs]

    sub = GLA_SUB
    nsub = c // sub
    row_i = _iota((c, c), 0)
    col_i = _iota((c, c), 1)
    tri = jnp.where(row_i >= col_i, 1.0, 0.0).astype(BF16)
    tri_sub = jnp.where((row_i >= col_i) & _same_segment((c, c), sub, sub), 1.0, 0.0).astype(BF16)
    key_pos = _iota((c, 1), 0)
    query_in_sub = _iota((N_HEADS * sub, c), 0) & (sub - 1)
    gla_same_head = _same_segment((GLA_W, GLA_QK), 64, GLA_DK)

    ht = [ht_ref[s] for s in seqs]
    rw_state_0 = rw_from_state(0, ht)

    x = [_dot(al_ref[s], a2_ref[...]) + ab_ref[...] for s in seqs]
    log_a = [(jnp.minimum(x[s], 0.0) - jnp.log1p(jnp.exp(-jnp.abs(x[s])))) * (1.0 / GLA_TAU) for s in seqs]
    b = [_dot_exact_lhs(tri, log_a[s]) for s in seqs]
    b_sub = [_dot_exact_lhs(tri_sub, log_a[s]) for s in seqs]
    q = [q_ref[s] * (GLA_DK ** -0.5) for s in seqs]
    st = [st_ref[s] for s in seqs]
    inter = [_dot_nt(q[s] * jnp.exp(b[s]), st[s]) for s in seqs]

    ht = rw_finish(0, rw_state_0, ht)
    rw_state_1 = rw_from_state(1, ht)

    scores = []
    for s in seqs:
        q_sub = q[s] * jnp.exp(b_sub[s])
        beta = b[s] - b_sub[s]
        k = k_ref[s]
        blocks = []
        for blk in range(nsub):
            lo, hi = blk * sub, (blk + 1) * sub
            expo = jnp.where(key_pos < hi, beta[lo:lo + 1, :] - b[s], NEG_INF)
            sc = _dot_nt(_stack_heads(q_sub[lo:hi], GLA_DK), k * jnp.exp(expo))
            blocks.append(jnp.where(_iota((N_HEADS * sub, c), 1) <= lo + query_in_sub, sc, 0.0))
        scores.append(jnp.concatenate(blocks, axis=0))

    ht = rw_finish(1, rw_state_1, ht)
    for s in seqs:
        ht_ref[s] = ht[s]

    per_head = [_dot(scores[s], v_ref[s]) for s in seqs]
    upd = [_dot_tn(v_ref[s], k_ref[s] * jnp.exp(b[s][c - 1:c, :] - b[s])) for s in seqs]
    for s in seqs:
        pieces = []
        for blk in range(nsub):
            piece = jnp.zeros((sub, GLA_W), F32)
            for h in range(N_HEADS):
                r0 = (blk * N_HEADS + h) * sub
                piece = piece + jnp.where(_head_lane_mask(GLA_W, 64, h), per_head[s][r0:r0 + sub], 0.0)
            pieces.append(piece)
        o_ref[s] = inter[s] + jnp.concatenate(pieces, axis=0)
        st_ref[s] = st[s] * jnp.exp(b[s][c - 1:c, :]) + jnp.where(gla_same_head, upd[s], 0.0)


def _recurrent(rw_terms, z_gla, a2, ab, batch):
    t = z_gla.shape[0]
    lp = t // batch
    c = GLA_CHUNK
    assert c == 2 * RW_CHUNK
    as_seq = lambda a: a.reshape(batch, a.shape[0] // batch, a.shape[1])
    rows = pl.BlockSpec((batch, c, RW_W), lambda i: (0, i, 0))
    z3 = as_seq(z_gla)
    y, o = pl.pallas_call(
        _recurrent_kernel,
        grid=(lp // c,),
        in_specs=[rows] * 8 + [pl.BlockSpec((batch, 16, RW_W), lambda i: (0, i, 0)),
                               pl.BlockSpec((batch, c, 128), lambda i: (0, i, 0)),
                               pl.BlockSpec((batch, c, 128), lambda i: (0, i, 1)),
                               pl.BlockSpec((batch, c, 256), lambda i: (0, i, 1)),
                               pl.BlockSpec((batch, c, 128), lambda i: (0, i, 6)),
                               _const_spec(a2.shape), _const_spec(ab.shape)],
        out_specs=[rows, pl.BlockSpec((batch, c, GLA_W), lambda i: (0, i, 0))],
        out_shape=[jax.ShapeDtypeStruct((batch, lp, RW_W), F32), jax.ShapeDtypeStruct((batch, lp, GLA_W), F32)],
        scratch_shapes=[pltpu.VMEM((batch, RW_W, RW_W), F32), pltpu.VMEM((batch, GLA_W, GLA_QK), F32)],
        compiler_params=_params(("arbitrary",)),
        name="recurrent",
    )(*map(as_seq, rw_terms), z3, z3, z3, z3, a2, ab)
    return y.reshape(t, RW_W), o.reshape(t, GLA_W)


def _merge_kernel(h_ref, ymla_ref, yrw_ref, bonus_ref, g_ref, ysb_ref, ogla_ref, rgla_ref,
                  nmix_ref, lnw_ref, lnb_ref, gn_ref, gb_ref, wg_ref, wb_ref, wo_ref, o_ref, *, tiles_per_seq):
    tm = h_ref.shape[0]
    avg = _segment_matrix(256, 64, 1.0 / 64)
    h = h_ref[...]
    n = _bf(_rms(h, EPS) * nmix_ref[...])

    y = yrw_ref[...]
    d = y - _dot_exact_rhs(y, avg)
    var = _dot_exact_rhs(d * d, avg)
    y_rw = (d * lax.rsqrt(var + RW_GN_EPS) * lnw_ref[...] + lnb_ref[...] + bonus_ref[...]) * g_ref[...]

    o = ogla_ref[...]
    r = rgla_ref[...]
    y_gla = o * lax.rsqrt(_dot_exact_rhs(o * o, avg) + EPS) * gn_ref[...] * (r * jax.nn.sigmoid(r))

    acc = jnp.zeros((tm, D_MODEL), F32)
    for m, y_m in enumerate((ymla_ref[...], y_rw, ysb_ref[...], y_gla)):
        logits = _dot_nt(n, wg_ref[m * D_MODEL:(m + 1) * D_MODEL, :])
        gate = jax.nn.sigmoid(logits + gb_ref[m:m + 1, :])
        acc = acc + gate * _dot(y_m, wb_ref[m])
    delta = _dot(acc, wo_ref[...])
    row = (pl.program_id(0) % tiles_per_seq) * tm + _iota((tm, 1), 0)
    o_ref[...] = h + jnp.where(row >= PAD, delta, 0.0)


def _merge(h, y_mla, y_rw, bonus, g, y_sb, o_gla, z_gla, p, lp):
    t = h.shape[0]
    tm = ROW_TILE
    row = lambda i: (i, 0)
    w256 = pl.BlockSpec((tm, 256), row)
    consts = [p["norm_mix"], p["rw_ln_w"], p["rw_ln_b"], p["gla_norm"], p["gate_b"]]
    stacked = [p["w_gate"], p["w_branch"], p["w_out"]]
    return pl.pallas_call(
        functools.partial(_merge_kernel, tiles_per_seq=lp // tm),
        grid=(t // tm,),
        in_specs=[pl.BlockSpec((tm, D_MODEL), row), w256, w256, w256, w256, w256, w256,
                  pl.BlockSpec((tm, 256), lambda i: (i, 2))]
        + [_const_spec(c.shape) for c in consts]
        + [_layer_spec(w, p["layer"]) for w in stacked],
        out_specs=pl.BlockSpec((tm, D_MODEL), row),
        out_shape=jax.ShapeDtypeStruct((t, D_MODEL), F32),
        compiler_params=_params(("parallel",)),
        name="merge",
    )(h, y_mla, y_rw, bonus, g, y_sb, o_gla, z_gla, *consts, *stacked)


def _ffn_kernel(h_ref, g_ref, win_ref, cw_ref, cb_ref, wout_ref, o_ref, tail_ref):
    @pl.when(pl.program_id(1) == 0)
    def _():
        tail_ref[...] = jnp.zeros_like(tail_ref)

    x = h_ref[...]
    tm = x.shape[0]
    n = _bf(_rms(x, EPS) * g_ref[...])
    rowi = _iota((tm, 1), 0)
    acc = jnp.zeros((tm, D_MODEL), F32)
    assert sum(FFN_COL_CHUNKS) == D_FF
    for c0, c1 in zip((0, FFN_COL_CHUNKS[0]), (FFN_COL_CHUNKS[0], D_FF)):
        a = jnp.dot(n, win_ref[:, c0:c1], preferred_element_type=F32)
        u = jnp.dot(n, win_ref[:, D_FF + c0:D_FF + c1], preferred_element_type=F32)
        prev1 = tail_ref[7:8, c0:c1]
        prev2 = tail_ref[6:7, c0:c1]
        a1 = jnp.where(rowi == 0, prev1, pltpu.roll(a, 1, 0))
        a2 = jnp.where(rowi == 0, prev2, jnp.where(rowi == 1, prev1, pltpu.roll(a, 2, 0)))
        tail_ref[:, c0:c1] = a[tm - 8:tm, :]
        conv = cb_ref[:, c0:c1] + cw_ref[0:1, c0:c1] * a2 + cw_ref[1:2, c0:c1] * a1 + cw_ref[2:3, c0:c1] * a
        acc = acc + _dot(conv * jax.nn.sigmoid(conv) * u, wout_ref[c0:c1, :])
    o_ref[...] = x + acc


def _ffn(h, g, w_in, conv_w, conv_b, w_out, layer, lp):
    t = h.shape[0]
    tm = ROW_TILE
    nb = lp // tm
    row = lambda b, i: (b * nb + i, 0)
    return pl.pallas_call(
        _ffn_kernel,
        grid=(t // lp, nb),
        in_specs=[pl.BlockSpec((tm, D_MODEL), row), _const_spec((1, D_MODEL)), _layer_spec(w_in, layer),
                  _const_spec(conv_w.shape), _const_spec(conv_b.shape), _layer_spec(w_out, layer)],
        out_specs=pl.BlockSpec((tm, D_MODEL), row),
        out_shape=jax.ShapeDtypeStruct((t, D_MODEL), F32),
        scratch_shapes=[pltpu.VMEM((8, D_FF), F32)],
        compiler_params=_params(("arbitrary", "arbitrary")),
        name="conv_ffn",
    )(h, g, w_in, conv_w, conv_b, w_out)


def _final_norm_kernel(h_ref, g_ref, o_ref):
    o_ref[0] = _rms(h_ref[...], EPS) * g_ref[...]


def _final_norm(h, g, batch, seq):
    lp = h.shape[0] // batch
    rows = FINAL_ROWS
    assert seq % rows == 0
    first = lambda bi, i: (pl.multiple_of(bi * lp + (PAD + N_META) + i * rows, BLOCK), 0)
    return pl.pallas_call(
        _final_norm_kernel,
        grid=(batch, seq // rows),
        in_specs=[pl.BlockSpec((pl.Element(rows), pl.Element(D_MODEL)), first), _const_spec((1, D_MODEL))],
        out_specs=pl.BlockSpec((1, rows, D_MODEL), lambda bi, i: (bi, i, 0)),
        out_shape=jax.ShapeDtypeStruct((batch, seq, D_MODEL), F32),
        compiler_params=_params(("parallel", "parallel")),
        name="final_norm",
    )(h, g)


def _rope_swap(w):
    half = w.shape[-1] // 2
    return jnp.concatenate([-w[..., half:], w[..., :half]], axis=-1)


def _in_weight_layout_kernel(w_ref, o_ref, gate_ref):
    def put(dst, val):
        o_ref[0, dst:dst + val.shape[0], :] = _bf(val)

    mla, sb, gla = IN_ROW_MLA, IN_ROW_SB, IN_ROW_GLA
    put(mla, w_ref[0, 0:384, :])
    half = MLA_ROPE // 2
    kr = w_ref[0, 384:384 + MLA_ROPE, :]
    kr_swapped = jnp.concatenate([-kr[half:], kr[:half]], axis=0)
    put(mla + 384, jnp.concatenate([kr] * 4, axis=0))
    put(mla + 512, jnp.concatenate([kr_swapped] * 4, axis=0))
    put(IN_ROW_RW, w_ref[0, 416:1440, :])
    put(sb, w_ref[0, 1440:1440 + SB_W, :] * (SB_HEAD ** -0.5 * LOG2_E))
    put(sb + SB_W, w_ref[0, 1440 + SB_W:2208, :])
    put(gla, w_ref[0, 2208:2720, :])
    put(gla + 512, w_ref[0, 2736:2992, :])
    lora = w_ref[0, 2720:2736, :]
    put(gla + 768, jnp.concatenate([lora, jnp.zeros((W_GLA - 784, lora.shape[1]), lora.dtype)], axis=0))
    gate_ref[0] = _bf(w_ref[0, 2992:2992 + W_GATE, :])


def _in_weight_layout(w_in):
    wt = jnp.swapaxes(w_in, 1, 2)
    depth, n, d = wt.shape
    cols = 256
    total = sum(IN_GROUP_WIDTHS)
    return pl.pallas_call(
        _in_weight_layout_kernel,
        grid=(depth, d // cols),
        in_specs=[pl.BlockSpec((1, n, cols), lambda l, i: (l, 0, i))],
        out_specs=[pl.BlockSpec((1, total, cols), lambda l, i: (l, 0, i)),
                   pl.BlockSpec((1, W_GATE, cols), lambda l, i: (l, 0, i))],
        out_shape=[jax.ShapeDtypeStruct((depth, total, d), BF16), jax.ShapeDtypeStruct((depth, W_GATE, d), BF16)],
        compiler_params=_params(("parallel", "parallel")),
        name="in_weight_layout",
    )(wt)


def _layout_params(w_in, mla_w_uq, mla_w_ukv, rw_w2, rw_a2, gla_a2):
    w_all, w_gate = _in_weight_layout(w_in)

    depth = w_in.shape[0]
    wuq = mla_w_uq.reshape(depth, MLA_Q_RANK, N_HEADS, MLA_NOPE + MLA_ROPE)
    nope, rope = wuq[..., :MLA_NOPE], wuq[..., MLA_NOPE:]
    rope_sw = _rope_swap(rope)
    zeros64 = jnp.zeros((depth, MLA_Q_RANK, 64), w_in.dtype)
    pair = lambda x, a, b: jnp.concatenate([x[:, :, a], x[:, :, b]], axis=-1)
    rope_pair = lambda x, a, b: jnp.concatenate([x[:, :, a], x[:, :, b], zeros64], axis=-1)
    wq = _bf(jnp.concatenate([pair(nope, 0, 1), rope_pair(rope, 0, 1), pair(nope, 2, 3), rope_pair(rope, 2, 3),
                              rope_pair(rope_sw, 0, 1), rope_pair(rope_sw, 2, 3)], axis=-1))
    wukv = mla_w_ukv.reshape(depth, MLA_KV_RANK, N_HEADS, 128)
    wkv = _bf(jnp.concatenate([wukv[..., :64].reshape(depth, MLA_KV_RANK, 256),
                               wukv[..., 64:].reshape(depth, MLA_KV_RANK, 256)], axis=-1))

    z64 = jnp.zeros_like(rw_w2)
    w2 = _bf(jnp.concatenate([rw_w2, z64], axis=1))
    a2 = _bf(jnp.concatenate([z64, rw_a2], axis=1))
    gla_a2p = _bf(jnp.concatenate([gla_a2, jnp.zeros((depth, 128 - gla_a2.shape[1], GLA_QK), gla_a2.dtype)], axis=1))
    return w_all, w_gate, wq, wkv, w2, a2, gla_a2p


def _rope_tables(lp):
    half = MLA_ROPE // 2
    freqs = ROPE_THETA ** (-jnp.arange(half, dtype=F32) / half)
    pos = (jnp.arange(lp) - PAD).astype(F32)
    ang = pos[:, None] * freqs[None, :]
    return jnp.tile(jnp.cos(ang), (1, 128 // half)), jnp.tile(jnp.sin(ang), (1, 128 // half))


def kernel(x, meta_tokens, norm_mix, w_in, mla_q_norm, mla_w_uq, mla_kv_norm, mla_w_ukv, rw_mu, rw_w0, rw_w2, rw_a0, rw_a2, rw_g2, rw_k_k, rw_k_a, rw_r_k, rw_ln_w, rw_ln_b, gla_a2, gla_a_b, gla_norm, gate_b, w_branch, w_out, norm_ffn, w_ffn_in, ffn_conv_w, ffn_conv_b, w_ffn_out, norm_final):
    batch, seq, _ = x.shape
    depth = w_in.shape[0]
    lp = PAD + N_META + seq
    t = batch * lp
    assert lp % ROW_TILE == 0 and lp % ATTN_TILE == 0 and lp % GLA_CHUNK == 0 and lp % RW_CHUNK == 0

    w_all, w_gate, wq, wkv, rw_w2p, rw_a2p, gla_a2p = _layout_params(
        w_in, mla_w_uq, mla_w_ukv, rw_w2, rw_a2, gla_a2)
    w_branch_b, w_out_b, w_ffn_in_b, w_ffn_out_b, rw_g2_b = map(_bf, (w_branch, w_out, w_ffn_in, w_ffn_out, rw_g2))
    vec = lambda a: a.reshape(depth, 1, -1)
    cos, sin = _rope_tables(lp)
    idx = jnp.arange(SB_SUFFIX_SPLIT)
    later_mat = jnp.where(idx[:, None] > idx[None, :], 1.0, 0.0).astype(BF16)

    h = _embed(x.reshape(batch * seq, D_MODEL), meta_tokens.astype(x.dtype), batch)

    for i in range(depth):
        z_mla, z_rw, z_sb, z_gla, sb_values = _in_proj(h, vec(norm_mix)[i], w_all, i)
        q, k, v = _mla_prep(z_mla, vec(mla_q_norm)[i], vec(mla_kv_norm)[i], wq, wkv, i, cos, sin, lp)
        y_sb, y_mla = _attn(z_sb, sb_values, later_mat, q, k, v, lp)
        rw = {"rw_mu": vec(rw_mu)[i], "rw_w0": vec(rw_w0)[i], "rw_w2": rw_w2p[i], "rw_a0": vec(rw_a0)[i],
              "rw_a2": rw_a2p[i], "rw_g2": rw_g2_b[i], "rw_k_k": vec(rw_k_k)[i], "rw_k_a": vec(rw_k_a)[i],
              "rw_r_k": vec(rw_r_k)[i]}
        *chunk_terms, g, bonus = _rw_prep(z_rw, rw, lp)
        y_rw, o_gla = _recurrent(chunk_terms, z_gla, gla_a2p[i], vec(gla_a_b)[i], batch)
        mp = {"rw_ln_w": vec(rw_ln_w)[i], "rw_ln_b": vec(rw_ln_b)[i], "gla_norm": vec(gla_norm)[i],
              "gate_b": gate_b[i], "norm_mix": vec(norm_mix)[i], "w_gate": w_gate, "w_branch": w_branch_b,
              "w_out": w_out_b, "layer": i}
        h = _merge(h, y_mla, y_rw, bonus, g, y_sb, o_gla, z_gla, mp, lp)
        h = _ffn(h, vec(norm_ffn)[i], w_ffn_in_b, ffn_conv_w[i], vec(ffn_conv_b)[i], w_ffn_out_b, i, lp)
    return _final_norm(h, norm_final.reshape(1, D_MODEL), batch, seq)
```

```python
import functools

import jax
import jax.numpy as jnp
from jax import lax
from jax.experimental import pallas as pl
from jax.experimental.pallas import tpu as pltpu

F32 = jnp.float32
BF16 = jnp.bfloat16

D_MODEL = 1024
DEPTH = 4
N_META = 16
BLOCK = 128
PAD = (-N_META) % BLOCK
EPS = 1e-6
NEG_INF = -1e30
LOG2_E = 1.4426950408889634

N_HEADS = 4
MLA_NOPE = 64
MLA_ROPE = 32
MLA_Q_RANK = 256
MLA_KV_RANK = 128
MLA_PAD_LANE = 192
ROPE_THETA = 10000.0

RW_W = 256
RW_GN_EPS = 64e-5
RW_CHUNK = 64

SB_W = 256
SB_HEAD = 64
SB_SUFFIX_SPLIT = 256

GLA_DK = 32
GLA_QK = 128
GLA_W = 256
GLA_TAU = 16.0
GLA_CHUNK = 128
GLA_SUB = 16

D_FF = 2816
FFN_COL_CHUNKS = (1536, 1280)

W_MLA, W_RW, W_SB, W_GLA, W_GATE = 640, 1024, 768, 896, 4096
IN_GROUP_WIDTHS = (W_MLA, W_RW, W_SB, W_GLA)
IN_GROUP_DTYPES = (BF16, F32, BF16, F32)
IN_ROW_RW, IN_ROW_SB, IN_ROW_MLA, IN_ROW_GLA = 0, W_RW, W_RW + W_SB, W_RW + W_SB + W_MLA

ROW_TILE = 384
ATTN_TILE = 384
FINAL_ROWS = 1024
VMEM_LIMIT = 56 * 1024 * 1024


def _bf(x):
    return x.astype(BF16)


def _dot(a, b):
    return jnp.dot(_bf(a), _bf(b), preferred_element_type=F32)


def _dot_nt(a, b):
    return lax.dot_general(_bf(a), _bf(b), (((1,), (1,)), ((), ())), preferred_element_type=F32)


def _dot_tn(a, b):
    return lax.dot_general(_bf(a), _bf(b), (((0,), (0,)), ((), ())), preferred_element_type=F32)


def _split_hi_lo(x):
    hi = _bf(x)
    lo = _bf(x - hi.astype(F32))
    return hi, lo


def _dot_exact_rhs(x, m):
    hi, lo = _split_hi_lo(x)
    return jnp.dot(hi, m, preferred_element_type=F32) + jnp.dot(lo, m, preferred_element_type=F32)


def _dot_exact_lhs(m, x):
    hi, lo = _split_hi_lo(x)
    return jnp.dot(m, hi, preferred_element_type=F32) + jnp.dot(m, lo, preferred_element_type=F32)


def _iota(shape, dim):
    return lax.broadcasted_iota(jnp.int32, shape, dim)


def _div_pow2(x, d):
    assert d & (d - 1) == 0
    return lax.shift_right_logical(x, d.bit_length() - 1)


def _same_segment(shape, row_seg, col_seg):
    return _div_pow2(_iota(shape, 0), row_seg) == _div_pow2(_iota(shape, 1), col_seg)


def _segment_matrix(n, seg, value):
    return jnp.where(_same_segment((n, n), seg, seg), value, 0.0).astype(BF16)


def _head_lane_mask(width, head_width, h):
    lane = _iota((1, width), 1)
    return (lane >= h * head_width) & (lane < (h + 1) * head_width)


def _stack_heads(x, head_width):
    w = x.shape[1]
    return jnp.concatenate(
        [jnp.where(_head_lane_mask(w, head_width, h), x, 0.0) for h in range(N_HEADS)], axis=0)


def _rms(x, eps):
    return x * lax.rsqrt(jnp.mean(x * x, axis=-1, keepdims=True) + eps)


def _const_spec(shape):
    nd = len(shape)
    return pl.BlockSpec(shape, lambda *_: (0,) * nd)


def _layer_spec(stacked, layer):
    nd = stacked.ndim - 1
    return pl.BlockSpec((None,) + stacked.shape[1:], lambda *_: (layer,) + (0,) * nd)


def _params(sem, vmem=VMEM_LIMIT):
    return pltpu.CompilerParams(dimension_semantics=sem, vmem_limit_bytes=vmem)


def _embed_kernel(x_ref, meta_ref, o_ref):
    first_real = PAD + N_META

    @pl.when(pl.program_id(1) == 0)
    def _():
        o_ref[0:PAD, :] = jnp.zeros((PAD, D_MODEL), o_ref.dtype)
        o_ref[PAD:first_real, :] = meta_ref[...]
        o_ref[first_real:, :] = x_ref[0:o_ref.shape[0] - first_real, :]

    @pl.when(pl.program_id(1) > 0)
    def _():
        o_ref[...] = x_ref[...]


def _embed(x2, meta, batch):
    seq = x2.shape[0] // batch
    first_real = PAD + N_META
    lp = first_real + seq
    tm = ROW_TILE
    window = lambda b, i: (pl.multiple_of(b * seq + jnp.maximum(i * tm - first_real, 0), BLOCK), 0)
    return pl.pallas_call(
        _embed_kernel,
        grid=(batch, lp // tm),
        in_specs=[pl.BlockSpec((pl.Element(tm), pl.Element(D_MODEL)), window), _const_spec(meta.shape)],
        out_specs=pl.BlockSpec((tm, D_MODEL), lambda b, i: (b * (lp // tm) + i, 0)),
        out_shape=jax.ShapeDtypeStruct((batch * lp, D_MODEL), x2.dtype),
        compiler_params=_params(("parallel", "parallel")),
        name="embed",
    )(x2, meta)


def _in_proj_kernel(h_ref, g_ref, w_ref, *out_refs):
    mla_ref, rw_ref, sb_ref, gla_ref, sb_values_ref = out_refs
    n = _bf(_rms(h_ref[...], EPS) * g_ref[...])
    rw_ref[...] = _dot_nt(n, w_ref[IN_ROW_RW:IN_ROW_SB, :])
    sb_ref[...] = _bf(_dot_nt(n, w_ref[IN_ROW_SB:IN_ROW_MLA, :]))
    both = _dot_nt(n, w_ref[IN_ROW_MLA:IN_ROW_GLA + W_GLA, :])
    mla_ref[...] = _bf(both[:, :W_MLA])
    gla_ref[...] = both[:, W_MLA:]
    sb_values_ref[0] = _head_stacked_values(sb_ref[:, 2 * SB_W:3 * SB_W])


def _in_proj(h, g, w_all, layer):
    t = h.shape[0]
    tm = ROW_TILE
    return pl.pallas_call(
        _in_proj_kernel,
        grid=(t // tm,),
        in_specs=[pl.BlockSpec((tm, D_MODEL), lambda i: (i, 0)),
                  _const_spec((1, D_MODEL)),
                  _layer_spec(w_all, layer)],
        out_specs=[pl.BlockSpec((tm, w), lambda i: (i, 0)) for w in IN_GROUP_WIDTHS]
        + [pl.BlockSpec((1, N_HEADS * tm, SB_W), lambda i: (i, 0, 0))],
        out_shape=[jax.ShapeDtypeStruct((t, w), dt) for w, dt in zip(IN_GROUP_WIDTHS, IN_GROUP_DTYPES)]
        + [jax.ShapeDtypeStruct((t // tm, N_HEADS * tm, SB_W), BF16)],
        compiler_params=_params(("parallel",)),
        name="in_proj",
    )(h, g, w_all)


def _mla_prep_kernel(z_ref, qg_ref, kvg_ref, wq_ref, wkv_ref, cos_ref, sin_ref, q_ref, k_ref, vt_ref):
    z = z_ref[...].astype(F32)
    nq = _rms(z[:, 0:256], EPS) * qg_ref[...]
    nkv = _rms(z[:, 256:384], EPS) * kvg_ref[...]
    cos = cos_ref[...]
    sin = sin_ref[...]
    ql = _dot(nq, wq_ref[...])
    scale = (MLA_NOPE + MLA_ROPE) ** -0.5
    q = jnp.concatenate([ql[:, 0:128], ql[:, 128:256] * cos + ql[:, 512:640] * sin,
                         ql[:, 256:384], ql[:, 384:512] * cos + ql[:, 640:768] * sin], axis=1)
    pad_lane = (_iota((1, 512), 1) & 255) == MLA_PAD_LANE
    q_ref[...] = _bf(jnp.where(pad_lane, 1.0, q * scale))
    kvl = _dot(nkv, wkv_ref[...])
    k_rope = z[:, 384:512] * cos + z[:, 512:640] * sin
    lane = _iota((1, 128), 1)
    is_pad_key = (pl.program_id(1) * z.shape[0] + _iota((z.shape[0], 1), 0)) < PAD
    k_rope = jnp.where(lane < 2 * MLA_ROPE, k_rope,
                       jnp.where((lane == MLA_PAD_LANE - 128) & is_pad_key, NEG_INF, 0.0))
    k_ref[...] = _bf(jnp.concatenate([kvl[:, 0:128], k_rope, kvl[:, 128:256], k_rope], axis=1))
    vt = _bf(kvl[:, 256:512].T)
    row_head = _div_pow2(_iota((256, 1), 0), 64)
    vt_ref[0] = jnp.concatenate([jnp.where(row_head == h, vt, jnp.zeros((), BF16)) for h in range(N_HEADS)], axis=1)


def _mla_prep(z_mla, qg, kvg, wq, wkv, layer, cos, sin, lp):
    t = z_mla.shape[0]
    tm = ROW_TILE
    nb = lp // tm
    row = lambda b, i: (b * nb + i, 0)
    return pl.pallas_call(
        _mla_prep_kernel,
        grid=(t // lp, nb),
        in_specs=[pl.BlockSpec((tm, W_MLA), row),
                  _const_spec((1, MLA_Q_RANK)), _const_spec((1, MLA_KV_RANK)),
                  _layer_spec(wq, layer), _layer_spec(wkv, layer),
                  pl.BlockSpec((tm, 128), lambda b, i: (i, 0)),
                  pl.BlockSpec((tm, 128), lambda b, i: (i, 0))],
        out_specs=[pl.BlockSpec((tm, 512), row), pl.BlockSpec((tm, 512), row),
                   pl.BlockSpec((1, 256, N_HEADS * tm), lambda b, i: (b * nb + i, 0, 0))],
        out_shape=[jax.ShapeDtypeStruct((t, 512), BF16), jax.ShapeDtypeStruct((t, 512), BF16),
                   jax.ShapeDtypeStruct((t // tm, 256, N_HEADS * tm), BF16)],
        compiler_params=_params(("parallel", "parallel")),
        name="mla_prep",
    )(z_mla, qg, kvg, wq, wkv, cos, sin)


def _head_stacked_values(vb):
    zero = jnp.zeros((), vb.dtype)
    return jnp.concatenate([jnp.where(_head_lane_mask(256, 64, h), vb, zero) for h in range(N_HEADS)], axis=0)


def _per_head_rows(rows, n):
    return jnp.concatenate([jnp.broadcast_to(r, (64, n)) for r in rows], axis=0)


def _attn_kernel(sq_ref, sk_ref, sv_ref, later_ref, mq_ref, mk_ref, mvt_ref, ysb_ref, ymla_ref,
                 sqh_ref, c_ref, sacc_ref, wts_ref, mqh_ref, m_ref, l_ref, macc_ref, p_ref, alpha_ref, *, tile):
    i = pl.program_id(1)
    heads = range(N_HEADS)
    q_row = i * tile + _iota((tile, 1), 0)
    q_pos = i * tile + _iota((1, tile), 1)
    for h in heads:
        sqh_ref[h] = jnp.where(_head_lane_mask(SB_W, SB_HEAD, h), sq_ref[...], jnp.zeros((), BF16))
    lane = _iota((1, 256), 1)
    for h in heads:
        half, slot = h // 2, h % 2
        head_lanes = (((lane >= slot * MLA_NOPE) & (lane < (slot + 1) * MLA_NOPE))
                      | ((lane >= 128 + slot * MLA_ROPE) & (lane < 128 + (slot + 1) * MLA_ROPE))
                      | (lane == MLA_PAD_LANE))
        mqh_ref[half, slot * tile:(slot + 1) * tile, :] = jnp.where(
            head_lanes, mq_ref[:, half * 256:(half + 1) * 256], jnp.zeros((), BF16))
    c_ref[...] = jnp.zeros(c_ref.shape, F32)
    sacc_ref[...] = jnp.zeros(sacc_ref.shape, F32)
    m_ref[...] = jnp.full(m_ref.shape, NEG_INF, F32)
    l_ref[...] = jnp.zeros(l_ref.shape, F32)
    macc_ref[...] = jnp.zeros(macc_ref.shape, F32)

    def add_values(j, slot):
        sacc_ref[...] += jnp.dot(wts_ref[slot], sv_ref[j], preferred_element_type=F32)
        pv = jnp.dot(mvt_ref[j], p_ref[slot], preferred_element_type=F32)
        macc_ref[...] = _per_head_rows([alpha_ref[slot, h] for h in heads], tile) * macc_ref[...] + pv

    def block(j, masked, prev, slot):
        start = pl.multiple_of(j * tile, tile)
        cut = SB_SUFFIX_SPLIT
        z = [_dot_nt(sqh_ref[h], sk_ref[pl.ds(start, tile), :]) for h in heads]
        s_pair = [_dot_nt(mk_ref[pl.ds(start, tile), half * 256:(half + 1) * 256], mqh_ref[half]) for half in (0, 1)]
        if prev is not None:
            add_values(prev, 1 - slot)

        for h in heads:
            s_h = s_pair[h // 2][:, (h % 2) * tile:(h % 2 + 1) * tile]
            if masked:
                s_h = jnp.where((start + _iota((tile, 1), 0)) <= q_pos, s_h, NEG_INF)
            m_old = m_ref[h]
            m_new = jnp.maximum(m_old, jnp.max(s_h, axis=0, keepdims=True))
            p = jnp.exp(s_h - m_new)
            alpha = jnp.exp(m_old - m_new)
            l_ref[h] = alpha * l_ref[h] + jnp.sum(p, axis=0, keepdims=True)
            m_ref[h] = m_new
            p_ref[slot, h * tile:(h + 1) * tile, :] = _bf(p)
            alpha_ref[slot, h] = alpha

        log_take = [jnp.minimum(z[h], 0.0) - jnp.log2(1.0 + jnp.exp2(-jnp.abs(z[h]))) for h in heads]
        log_keep = [log_take[h] - z[h] for h in heads]
        if masked:
            mask = (start + _iota((1, tile), 1)) < q_row
            log_keep = [jnp.where(mask, log_keep[h], 0.0) for h in heads]
        keep16 = [_bf(log_keep[h]) for h in heads]
        later_head = [jnp.dot(keep16[h][:, :cut], later_ref[...], preferred_element_type=F32) for h in heads]
        later_tail = [jnp.dot(keep16[h][:, cut:], later_ref[:tile - cut, :tile - cut], preferred_element_type=F32)
                      for h in heads]

        for h in heads:
            head_sum = jnp.sum(log_keep[h][:, :cut], axis=-1, keepdims=True)
            tail_sum = jnp.sum(log_keep[h][:, cut:], axis=-1, keepdims=True)
            c = c_ref[h]
            later = jnp.concatenate([later_head[h] + tail_sum, later_tail[h]], axis=1)
            w = jnp.exp2(log_take[h] + later + c)
            if masked:
                w = jnp.where(mask, w, 0.0)
            c_ref[h] = c + (head_sum + tail_sum)
            wts_ref[slot, :, h * tile:(h + 1) * tile] = _bf(w)

    block(i, True, None, 0)

    def below_pair(t, carry):
        j = i - 1 - 2 * t
        block(j, False, j + 1, 1)
        block(j - 1, False, j, 0)
        return carry

    lax.fori_loop(0, lax.shift_right_logical(i, 1), below_pair, 0)
    odd = (i & 1) == 1

    @pl.when(odd)
    def _():
        block(0, False, 1, 1)
        add_values(0, 1)

    @pl.when(jnp.logical_not(odd))
    def _():
        add_values(0, 0)
    ysb_ref[...] = _bf(sacc_ref[...])
    ymla_ref[...] = _bf((macc_ref[...] / _per_head_rows([l_ref[h] for h in heads], tile)).T)


def _attn(z_sb, sb_values, later_mat, q, k, vt, lp):
    t = z_sb.shape[0]
    tile = ATTN_TILE
    nb = lp // tile
    tiles = lambda w: pl.BlockSpec((tile, w), lambda b, i: (b * nb + i, 0))
    resident = lambda shape, index: pl.BlockSpec(shape, index, pipeline_mode=pl.Buffered(1))
    return pl.pallas_call(
        functools.partial(_attn_kernel, tile=tile),
        grid=(t // lp, nb),
        in_specs=[tiles(SB_W),
                  resident((lp, SB_W), lambda b, i: (b, 1)),
                  resident((nb, N_HEADS * tile, SB_W), lambda b, i: (b, 0, 0)),
                  _const_spec((SB_SUFFIX_SPLIT, SB_SUFFIX_SPLIT)),
                  tiles(512),
                  resident((lp, 512), lambda b, i: (b, 0)),
                  resident((nb, 256, N_HEADS * tile), lambda b, i: (b, 0, 0))],
        out_specs=[tiles(SB_W), tiles(256)],
        out_shape=[jax.ShapeDtypeStruct((t, SB_W), BF16), jax.ShapeDtypeStruct((t, 256), BF16)],
        scratch_shapes=[pltpu.VMEM((N_HEADS, tile, SB_W), BF16), pltpu.VMEM((N_HEADS, tile, 1), F32),
                        pltpu.VMEM((tile, SB_W), F32), pltpu.VMEM((2, tile, N_HEADS * tile), BF16),
                        pltpu.VMEM((2, 2 * tile, 256), BF16), pltpu.VMEM((N_HEADS, 1, tile), F32),
                        pltpu.VMEM((N_HEADS, 1, tile), F32), pltpu.VMEM((256, tile), F32),
                        pltpu.VMEM((2, N_HEADS * tile, tile), BF16), pltpu.VMEM((2, N_HEADS, 1, tile), F32)],
        compiler_params=_params(("parallel", "arbitrary")),
        name="attn",
    )(z_sb, z_sb, sb_values, later_mat, q, k, vt)


def _compact_eye(c):
    return jnp.where(_iota((c, N_HEADS * c), 0) == (_iota((c, N_HEADS * c), 1) & (c - 1)), 1.0, 0.0)


def _rw_prep_kernel(z_ref, zprev_ref, mu_ref, w0_ref, w2_ref, a0_ref, a2_ref, g2_ref, kk_ref, ka_ref, rk_ref,
                    w_o, rt_o, arb_o, kbar_o, bbar_o, v_o, uv_o, yv_o, dec_o, g_o, bonus_o):
    i = pl.program_id(1)
    z = z_ref[...]
    tm = z.shape[0]
    prev = jnp.where(i == 0, 0.0, zprev_ref[7:8, :])
    shifted = jnp.where(_iota((tm, 1), 0) == 0, prev, pltpu.roll(z, 1, 0))
    zz = z + (shifted - z) * mu_ref[...]
    r, k, v = zz[:, 0:256], zz[:, 256:512], zz[:, 512:768]
    lora_in = zz[:, 768:896]
    u = w0_ref[...] + _dot(jnp.tanh(lora_in), w2_ref[...])
    w = jnp.minimum(u, 0.0) - jnp.log1p(jnp.exp(-jnp.abs(u))) - 0.5
    a = jax.nn.sigmoid(a0_ref[...] + _dot(lora_in, a2_ref[...]))
    seg = _segment_matrix(RW_W, 64, 1.0)
    kx = k * kk_ref[...]
    kap = kx / jnp.maximum(jnp.sqrt(_dot_exact_rhs(kx * kx, seg)), 1e-12)
    kmod = k * (1.0 + (a - 1.0) * ka_ref[...])
    beta = kap * a
    lw = -jnp.exp(w)
    g_o[...] = _dot(jax.nn.sigmoid(zz[:, 896:1024]), g2_ref[...])
    bonus_o[...] = _dot_exact_rhs(r * kmod * rk_ref[...], seg) * v

    c = RW_CHUNK
    same_chunk = _same_segment((tm, tm), c, c)
    lw_hi, lw_lo = _split_hi_lo(lw)
    cum = jnp.where(same_chunk & (_iota((tm, tm), 0) >= _iota((tm, tm), 1)), 1.0, 0.0).astype(BF16)
    b = jnp.dot(cum, lw_hi, preferred_element_type=F32) + jnp.dot(cum, lw_lo, preferred_element_type=F32)
    b_end = jnp.concatenate(
        [jnp.broadcast_to(b[n * c + c - 1:(n + 1) * c, :], (c, RW_W)) for n in range(tm // c)], axis=0)
    grow = jnp.exp(-b)
    to_end = jnp.exp(b_end - b)
    kap_t = kap * jnp.exp(b - lw)
    r_t = r * jnp.exp(b)
    beta_g = beta * grow
    k_g = kmod * grow
    rt_o[...] = _bf(r_t)
    kbar_o[...] = _bf(kmod * to_end)
    bbar_o[...] = _bf(beta * to_end)
    v_o[...] = _bf(v)
    decay_end = jnp.exp(b_end)

    lane_pos = _iota((c, N_HEADS * c), 1) & (c - 1)
    strictly_earlier = _iota((c, N_HEADS * c), 0) > lane_pos
    not_later = _iota((c, N_HEADS * c), 0) >= lane_pos
    chunks = [slice(n * c, (n + 1) * c) for n in range(tm // c)]
    pair = [_dot_nt(jnp.concatenate([kap_t[rows], r_t[rows]], axis=0),
                    jnp.concatenate([_stack_heads(beta_g[rows], 64), _stack_heads(k_g[rows], 64)], axis=0))
            for rows in chunks]
    a_kb = [jnp.where(strictly_earlier, p[0:c, 0:4 * c], 0.0) for p in pair]
    a_kk = [jnp.where(strictly_earlier, p[0:c, 4 * c:8 * c], 0.0) for p in pair]
    a_rk = [jnp.where(not_later, p[c:2 * c, 4 * c:8 * c], 0.0) for p in pair]
    for rows, p in zip(chunks, pair):
        arb_o[rows, :] = _bf(jnp.where(not_later, p[c:2 * c, 0:4 * c], 0.0))
    inv = [_compact_eye(c) - a for a in a_kb]
    power = [_dot(a, _stack_heads(a, c)) for a in a_kb]
    span = 2
    while span < c:
        inv = [t + _dot(t, _stack_heads(p, c)) for t, p in zip(inv, power)]
        span *= 2
        if span < c:
            power = [_dot(p, _stack_heads(p, c)) for p in power]
    v_st = [_stack_heads(v[rows], 64) for rows in chunks]
    kk_v = [_dot(a, vs) for a, vs in zip(a_kk, v_st)]
    for n, rows in enumerate(chunks):
        w_o[rows, :] = _bf(_dot(inv[n], _stack_heads(kap_t[rows], 64)))
        uv_o[rows, :] = _dot(inv[n], _stack_heads(kk_v[n], 64))
        yv_o[rows, :] = _dot(a_rk[n], v_st[n])
        dec_o[n * 8:(n + 1) * 8, :] = decay_end[n * c:n * c + 8]


def _rw_prep(z_rw, p, lp):
    t = z_rw.shape[0]
    tm = ROW_TILE
    nb = lp // tm
    row = lambda b, i: (b * nb + i, 0)
    prev = lambda b, i: (jnp.maximum(b * (lp // 8) + i * (tm // 8) - 1, 0), 0)
    consts = [p["rw_mu"], p["rw_w0"], p["rw_w2"], p["rw_a0"], p["rw_a2"], p["rw_g2"],
              p["rw_k_k"], p["rw_k_a"], p["rw_r_k"]]
    return pl.pallas_call(
        _rw_prep_kernel,
        grid=(t // lp, nb),
        in_specs=[pl.BlockSpec((tm, W_RW), row), pl.BlockSpec((8, W_RW), prev)]
        + [_const_spec(c.shape) for c in consts],
        out_specs=[pl.BlockSpec((tm, RW_W), row)] * 8
        + [pl.BlockSpec((tm // RW_CHUNK * 8, RW_W), row)] + [pl.BlockSpec((tm, RW_W), row)] * 2,
        out_shape=[jax.ShapeDtypeStruct((t, RW_W), BF16)] * 6 + [jax.ShapeDtypeStruct((t, RW_W), F32)] * 2
        + [jax.ShapeDtypeStruct((t // RW_CHUNK * 8, RW_W), F32)] + [jax.ShapeDtypeStruct((t, RW_W), F32)] * 2,
        compiler_params=_params(("parallel", "parallel")),
        name="rw_prep",
    )(z_rw, z_rw, *consts)


def _recurrent_kernel(w_ref, rt_ref, arb_ref, kbar_ref, bbar_ref, rv_ref, uv_ref, yv_ref, dec_ref,
                      q_ref, k_ref, v_ref, al_ref, a2_ref, ab_ref, y_ref, o_ref, ht_ref, st_ref):
    c = GLA_CHUNK
    rc = RW_CHUNK
    seqs = range(st_ref.shape[0])

    @pl.when(pl.program_id(0) == 0)
    def _():
        ht_ref[...] = jnp.zeros_like(ht_ref)
        st_ref[...] = jnp.zeros_like(st_ref)

    rw_same_head = _same_segment((RW_W, RW_W), 64, 64)

    def rw_from_state(n, ht):
        rows = slice(n * rc, (n + 1) * rc)
        return [_dot_nt(jnp.concatenate([w_ref[s, rows], rt_ref[s, rows]], axis=0), ht[s]) for s in seqs]

    def rw_finish(n, from_state, ht):
        rows = slice(n * rc, (n + 1) * rc)
        u = [from_state[s][0:rc] + uv_ref[s, rows] for s in seqs]
        from_u = [_dot(arb_ref[s, rows], _stack_heads(u[s], 64)) for s in seqs]
        upd = [_dot_tn(jnp.concatenate([rv_ref[s, rows], _bf(u[s])], axis=0),
                       jnp.concatenate([kbar_ref[s, rows], -bbar_ref[s, rows]], axis=0)) for s in seqs]
        for s in seqs:
            y_ref[s, rows] = from_state[s][rc:2 * rc] + yv_ref[s, rows] - from_u[s]
        return [ht[s] * dec_ref[s, n * 8:n * 8 + 1, :] + jnp.where(rw_same_head, upd[s], 0.0) for s in seqs]

    sub = GLA_SUB
    nsub = c // sub
    row_i = _iota((c, c), 0)
    col_i = _iota((c, c), 1)
    tri = jnp.where(row_i >= col_i, 1.0, 0.0).astype(BF16)
    tri_sub = jnp.where((row_i >= col_i) & _same_segment((c, c), sub, sub), 1.0, 0.0).astype(BF16)
    key_pos = _iota((c, 1), 0)
    query_in_sub = _iota((N_HEADS * sub, c), 0) & (sub - 1)
    gla_same_head = _same_segment((GLA_W, GLA_QK), 64, GLA_DK)

    ht = [ht_ref[s] for s in seqs]
    rw_state_0 = rw_from_state(0, ht)

    x = [_dot(al_ref[s], a2_ref[...]) + ab_ref[...] for s in seqs]
    log_a = [(jnp.minimum(x[s], 0.0) - jnp.log1p(jnp.exp(-jnp.abs(x[s])))) * (1.0 / GLA_TAU) for s in seqs]
    b = [_dot_exact_lhs(tri, log_a[s]) for s in seqs]
    b_sub = [_dot_exact_lhs(tri_sub, log_a[s]) for s in seqs]
    q = [q_ref[s] * (GLA_DK ** -0.5) for s in seqs]
    st = [st_ref[s] for s in seqs]
    inter = [_dot_nt(q[s] * jnp.exp(b[s]), st[s]) for s in seqs]

    ht = rw_finish(0, rw_state_0, ht)
    rw_state_1 = rw_from_state(1, ht)

    scores = []
    for s in seqs:
        q_sub = q[s] * jnp.exp(b_sub[s])
        beta = b[s] - b_sub[s]
        k = k_ref[s]
        blocks = []
        for blk in range(nsub):
            lo, hi = blk * sub, (blk + 1) * sub
            expo = jnp.where(key_pos < hi, beta[lo:lo + 1, :] - b[s], NEG_INF)
            sc = _dot_nt(_stack_heads(q_sub[lo:hi], GLA_DK), k * jnp.exp(expo))
            blocks.append(jnp.where(_iota((N_HEADS * sub, c), 1) <= lo + query_in_sub, sc, 0.0))
        scores.append(jnp.concatenate(blocks, axis=0))

    ht = rw_finish(1, rw_state_1, ht)
    for s in seqs:
        ht_ref[s] = ht[s]

    per_head = [_dot(scores[s], v_ref[s]) for s in seqs]
    upd = [_dot_tn(v_ref[s], k_ref[s] * jnp.exp(b[s][c - 1:c, :] - b[s])) for s in seqs]
    for s in seqs:
        pieces = []
        for blk in range(nsub):
            piece = jnp.zeros((sub, GLA_W), F32)
            for h in range(N_HEADS):
                r0 = (blk * N_HEADS + h) * sub
                piece = piece + jnp.where(_head_lane_mask(GLA_W, 64, h), per_head[s][r0:r0 + sub], 0.0)
            pieces.append(piece)
        o_ref[s] = inter[s] + jnp.concatenate(pieces, axis=0)
        st_ref[s] = st[s] * jnp.exp(b[s][c - 1:c, :]) + jnp.where(gla_same_head, upd[s], 0.0)


def _recurrent(rw_terms, z_gla, a2, ab, batch):
    t = z_gla.shape[0]
    lp = t // batch
    c = GLA_CHUNK
    assert c == 2 * RW_CHUNK
    as_seq = lambda a: a.reshape(batch, a.shape[0] // batch, a.shape[1])
    rows = pl.BlockSpec((batch, c, RW_W), lambda i: (0, i, 0))
    z3 = as_seq(z_gla)
    y, o = pl.pallas_call(
        _recurrent_kernel,
        grid=(lp // c,),
        in_specs=[rows] * 8 + [pl.BlockSpec((batch, 16, RW_W), lambda i: (0, i, 0)),
                               pl.BlockSpec((batch, c, 128), lambda i: (0, i, 0)),
                               pl.BlockSpec((batch, c, 128), lambda i: (0, i, 1)),
                               pl.BlockSpec((batch, c, 256), lambda i: (0, i, 1)),
                               pl.BlockSpec((batch, c, 128), lambda i: (0, i, 6)),
                               _const_spec(a2.shape), _const_spec(ab.shape)],
        out_specs=[rows, pl.BlockSpec((batch, c, GLA_W), lambda i: (0, i, 0))],
        out_shape=[jax.ShapeDtypeStruct((batch, lp, RW_W), F32), jax.ShapeDtypeStruct((batch, lp, GLA_W), F32)],
        scratch_shapes=[pltpu.VMEM((batch, RW_W, RW_W), F32), pltpu.VMEM((batch, GLA_W, GLA_QK), F32)],
        compiler_params=_params(("arbitrary",)),
        name="recurrent",
    )(*map(as_seq, rw_terms), z3, z3, z3, z3, a2, ab)
    return y.reshape(t, RW_W), o.reshape(t, GLA_W)


def _merge_kernel(h_ref, ymla_ref, yrw_ref, bonus_ref, g_ref, ysb_ref, ogla_ref, rgla_ref,
                  nmix_ref, lnw_ref, lnb_ref, gn_ref, gb_ref, wg_ref, wb_ref, wo_ref, o_ref, *, tiles_per_seq):
    tm = h_ref.shape[0]
    avg = _segment_matrix(256, 64, 1.0 / 64)
    h = h_ref[...]
    n = _bf(_rms(h, EPS) * nmix_ref[...])

    y = yrw_ref[...]
    d = y - _dot_exact_rhs(y, avg)
    var = _dot_exact_rhs(d * d, avg)
    y_rw = (d * lax.rsqrt(var + RW_GN_EPS) * lnw_ref[...] + lnb_ref[...] + bonus_ref[...]) * g_ref[...]

    o = ogla_ref[...]
    r = rgla_ref[...]
    y_gla = o * lax.rsqrt(_dot_exact_rhs(o * o, avg) + EPS) * gn_ref[...] * (r * jax.nn.sigmoid(r))

    acc = jnp.zeros((tm, D_MODEL), F32)
    for m, y_m in enumerate((ymla_ref[...], y_rw, ysb_ref[...], y_gla)):
        logits = _dot_nt(n, wg_ref[m * D_MODEL:(m + 1) * D_MODEL, :])
        gate = jax.nn.sigmoid(logits + gb_ref[m:m + 1, :])
        acc = acc + gate * _dot(y_m, wb_ref[m])
    delta = _dot(acc, wo_ref[...])
    row = (pl.program_id(0) % tiles_per_seq) * tm + _iota((tm, 1), 0)
    o_ref[...] = h + jnp.where(row >= PAD, delta, 0.0)


def _merge(h, y_mla, y_rw, bonus, g, y_sb, o_gla, z_gla, p, lp):
    t = h.shape[0]
    tm = ROW_TILE
    row = lambda i: (i, 0)
    w256 = pl.BlockSpec((tm, 256), row)
    consts = [p["norm_mix"], p["rw_ln_w"], p["rw_ln_b"], p["gla_norm"], p["gate_b"]]
    stacked = [p["w_gate"], p["w_branch"], p["w_out"]]
    return pl.pallas_call(
        functools.partial(_merge_kernel, tiles_per_seq=lp // tm),
        grid=(t // tm,),
        in_specs=[pl.BlockSpec((tm, D_MODEL), row), w256, w256, w256, w256, w256, w256,
                  pl.BlockSpec((tm, 256), lambda i: (i, 2))]
        + [_const_spec(c.shape) for c in consts]
        + [_layer_spec(w, p["layer"]) for w in stacked],
        out_specs=pl.BlockSpec((tm, D_MODEL), row),
        out_shape=jax.ShapeDtypeStruct((t, D_MODEL), F32),
        compiler_params=_params(("parallel",)),
        name="merge",
    )(h, y_mla, y_rw, bonus, g, y_sb, o_gla, z_gla, *consts, *stacked)


def _ffn_kernel(h_ref, g_ref, win_ref, cw_ref, cb_ref, wout_ref, o_ref, tail_ref):
    @pl.when(pl.program_id(1) == 0)
    def _():
        tail_ref[...] = jnp.zeros_like(tail_ref)

    x = h_ref[...]
    tm = x.shape[0]
    n = _bf(_rms(x, EPS) * g_ref[...])
    rowi = _iota((tm, 1), 0)
    acc = jnp.zeros((tm, D_MODEL), F32)
    assert sum(FFN_COL_CHUNKS) == D_FF
    for c0, c1 in zip((0, FFN_COL_CHUNKS[0]), (FFN_COL_CHUNKS[0], D_FF)):
        a = jnp.dot(n, win_ref[:, c0:c1], preferred_element_type=F32)
        u = jnp.dot(n, win_ref[:, D_FF + c0:D_FF + c1], preferred_element_type=F32)
        prev1 = tail_ref[7:8, c0:c1]
        prev2 = tail_ref[6:7, c0:c1]
        a1 = jnp.where(rowi == 0, prev1, pltpu.roll(a, 1, 0))
        a2 = jnp.where(rowi == 0, prev2, jnp.where(rowi == 1, prev1, pltpu.roll(a, 2, 0)))
        tail_ref[:, c0:c1] = a[tm - 8:tm, :]
        conv = cb_ref[:, c0:c1] + cw_ref[0:1, c0:c1] * a2 + cw_ref[1:2, c0:c1] * a1 + cw_ref[2:3, c0:c1] * a
        acc = acc + _dot(conv * jax.nn.sigmoid(conv) * u, wout_ref[c0:c1, :])
    o_ref[...] = x + acc


def _ffn(h, g, w_in, conv_w, conv_b, w_out, layer, lp):
    t = h.shape[0]
    tm = ROW_TILE
    nb = lp // tm
    row = lambda b, i: (b * nb + i, 0)
    return pl.pallas_call(
        _ffn_kernel,
        grid=(t // lp, nb),
        in_specs=[pl.BlockSpec((tm, D_MODEL), row), _const_spec((1, D_MODEL)), _layer_spec(w_in, layer),
                  _const_spec(conv_w.shape), _const_spec(conv_b.shape), _layer_spec(w_out, layer)],
        out_specs=pl.BlockSpec((tm, D_MODEL), row),
        out_shape=jax.ShapeDtypeStruct((t, D_MODEL), F32),
        scratch_shapes=[pltpu.VMEM((8, D_FF), F32)],
        compiler_params=_params(("arbitrary", "arbitrary")),
        name="conv_ffn",
    )(h, g, w_in, conv_w, conv_b, w_out)


def _final_norm_kernel(h_ref, g_ref, o_ref):
    o_ref[0] = _rms(h_ref[...], EPS) * g_ref[...]


def _final_norm(h, g, batch, seq):
    lp = h.shape[0] // batch
    rows = FINAL_ROWS
    assert seq % rows == 0
    first = lambda bi, i: (pl.multiple_of(bi * lp + (PAD + N_META) + i * rows, BLOCK), 0)
    return pl.pallas_call(
        _final_norm_kernel,
        grid=(batch, seq // rows),
        in_specs=[pl.BlockSpec((pl.Element(rows), pl.Element(D_MODEL)), first), _const_spec((1, D_MODEL))],
        out_specs=pl.BlockSpec((1, rows, D_MODEL), lambda bi, i: (bi, i, 0)),
        out_shape=jax.ShapeDtypeStruct((batch, seq, D_MODEL), F32),
        compiler_params=_params(("parallel", "parallel")),
        name="final_norm",
    )(h, g)


def _rope_swap(w):
    half = w.shape[-1] // 2
    return jnp.concatenate([-w[..., half:], w[..., :half]], axis=-1)


def _in_weight_layout_kernel(w_ref, o_ref, gate_ref):
    def put(dst, val):
        o_ref[0, dst:dst + val.shape[0], :] = _bf(val)

    mla, sb, gla = IN_ROW_MLA, IN_ROW_SB, IN_ROW_GLA
    put(mla, w_ref[0, 0:384, :])
    half = MLA_ROPE // 2
    kr = w_ref[0, 384:384 + MLA_ROPE, :]
    kr_swapped = jnp.concatenate([-kr[half:], kr[:half]], axis=0)
    put(mla + 384, jnp.concatenate([kr] * 4, axis=0))
    put(mla + 512, jnp.concatenate([kr_swapped] * 4, axis=0))
    put(IN_ROW_RW, w_ref[0, 416:1440, :])
    put(sb, w_ref[0, 1440:1440 + SB_W, :] * (SB_HEAD ** -0.5 * LOG2_E))
    put(sb + SB_W, w_ref[0, 1440 + SB_W:2208, :])
    put(gla, w_ref[0, 2208:2720, :])
    put(gla + 512, w_ref[0, 2736:2992, :])
    lora = w_ref[0, 2720:2736, :]
    put(gla + 768, jnp.concatenate([lora, jnp.zeros((W_GLA - 784, lora.shape[1]), lora.dtype)], axis=0))
    gate_ref[0] = _bf(w_ref[0, 2992:2992 + W_GATE, :])


def _in_weight_layout(w_in):
    wt = jnp.swapaxes(w_in, 1, 2)
    depth, n, d = wt.shape
    cols = 256
    total = sum(IN_GROUP_WIDTHS)
    return pl.pallas_call(
        _in_weight_layout_kernel,
        grid=(depth, d // cols),
        in_specs=[pl.BlockSpec((1, n, cols), lambda l, i: (l, 0, i))],
        out_specs=[pl.BlockSpec((1, total, cols), lambda l, i: (l, 0, i)),
                   pl.BlockSpec((1, W_GATE, cols), lambda l, i: (l, 0, i))],
        out_shape=[jax.ShapeDtypeStruct((depth, total, d), BF16), jax.ShapeDtypeStruct((depth, W_GATE, d), BF16)],
        compiler_params=_params(("parallel", "parallel")),
        name="in_weight_layout",
    )(wt)


def _layout_params(w_in, mla_w_uq, mla_w_ukv, rw_w2, rw_a2, gla_a2):
    w_all, w_gate = _in_weight_layout(w_in)

    depth = w_in.shape[0]
    wuq = mla_w_uq.reshape(depth, MLA_Q_RANK, N_HEADS, MLA_NOPE + MLA_ROPE)
    nope, rope = wuq[..., :MLA_NOPE], wuq[..., MLA_NOPE:]
    rope_sw = _rope_swap(rope)
    zeros64 = jnp.zeros((depth, MLA_Q_RANK, 64), w_in.dtype)
    pair = lambda x, a, b: jnp.concatenate([x[:, :, a], x[:, :, b]], axis=-1)
    rope_pair = lambda x, a, b: jnp.concatenate([x[:, :, a], x[:, :, b], zeros64], axis=-1)
    wq = _bf(jnp.concatenate([pair(nope, 0, 1), rope_pair(rope, 0, 1), pair(nope, 2, 3), rope_pair(rope, 2, 3),
                              rope_pair(rope_sw, 0, 1), rope_pair(rope_sw, 2, 3)], axis=-1))
    wukv = mla_w_ukv.reshape(depth, MLA_KV_RANK, N_HEADS, 128)
    wkv = _bf(jnp.concatenate([wukv[..., :64].reshape(depth, MLA_KV_RANK, 256),
                               wukv[..., 64:].reshape(depth, MLA_KV_RANK, 256)], axis=-1))

    z64 = jnp.zeros_like(rw_w2)
    w2 = _bf(jnp.concatenate([rw_w2, z64], axis=1))
    a2 = _bf(jnp.concatenate([z64, rw_a2], axis=1))
    gla_a2p = _bf(jnp.concatenate([gla_a2, jnp.zeros((depth, 128 - gla_a2.shape[1], GLA_QK), gla_a2.dtype)], axis=1))
    return w_all, w_gate, wq, wkv, w2, a2, gla_a2p


def _rope_tables(lp):
    half = MLA_ROPE // 2
    freqs = ROPE_THETA ** (-jnp.arange(half, dtype=F32) / half)
    pos = (jnp.arange(lp) - PAD).astype(F32)
    ang = pos[:, None] * freqs[None, :]
    return jnp.tile(jnp.cos(ang), (1, 128 // half)), jnp.tile(jnp.sin(ang), (1, 128 // half))


def kernel(x, meta_tokens, norm_mix, w_in, mla_q_norm, mla_w_uq, mla_kv_norm, mla_w_ukv, rw_mu, rw_w0, rw_w2, rw_a0, rw_a2, rw_g2, rw_k_k, rw_k_a, rw_r_k, rw_ln_w, rw_ln_b, gla_a2, gla_a_b, gla_norm, gate_b, w_branch, w_out, norm_ffn, w_ffn_in, ffn_conv_w, ffn_conv_b, w_ffn_out, norm_final):
    batch, seq, _ = x.shape
    depth = w_in.shape[0]
    lp = PAD + N_META + seq
    t = batch * lp
    assert lp % ROW_TILE == 0 and lp % ATTN_TILE == 0 and lp % GLA_CHUNK == 0 and lp % RW_CHUNK == 0

    w_all, w_gate, wq, wkv, rw_w2p, rw_a2p, gla_a2p = _layout_params(
        w_in, mla_w_uq, mla_w_ukv, rw_w2, rw_a2, gla_a2)
    w_branch_b, w_out_b, w_ffn_in_b, w_ffn_out_b, rw_g2_b = map(_bf, (w_branch, w_out, w_ffn_in, w_ffn_out, rw_g2))
    vec = lambda a: a.reshape(depth, 1, -1)
    cos, sin = _rope_tables(lp)
    idx = jnp.arange(SB_SUFFIX_SPLIT)
    later_mat = jnp.where(idx[:, None] > idx[None, :], 1.0, 0.0).astype(BF16)

    h = _embed(x.reshape(batch * seq, D_MODEL), meta_tokens.astype(x.dtype), batch)

    for i in range(depth):
        z_mla, z_rw, z_sb, z_gla, sb_values = _in_proj(h, vec(norm_mix)[i], w_all, i)
        q, k, v = _mla_prep(z_mla, vec(mla_q_norm)[i], vec(mla_kv_norm)[i], wq, wkv, i, cos, sin, lp)
        y_sb, y_mla = _attn(z_sb, sb_values, later_mat, q, k, v, lp)
        rw = {"rw_mu": vec(rw_mu)[i], "rw_w0": vec(rw_w0)[i], "rw_w2": rw_w2p[i], "rw_a0": vec(rw_a0)[i],
              "rw_a2": rw_a2p[i], "rw_g2": rw_g2_b[i], "rw_k_k": vec(rw_k_k)[i], "rw_k_a": vec(rw_k_a)[i],
              "rw_r_k": vec(rw_r_k)[i]}
        *chunk_terms, g, bonus = _rw_prep(z_rw, rw, lp)
        y_rw, o_gla = _recurrent(chunk_terms, z_gla, gla_a2p[i], vec(gla_a_b)[i], batch)
        mp = {"rw_ln_w": vec(rw_ln_w)[i], "rw_ln_b": vec(rw_ln_b)[i], "gla_norm": vec(gla_norm)[i],
              "gate_b": gate_b[i], "norm_mix": vec(norm_mix)[i], "w_gate": w_gate, "w_branch": w_branch_b,
              "w_out": w_out_b, "layer": i}
        h = _merge(h, y_mla, y_rw, bonus, g, y_sb, o_gla, z_gla, mp, lp)
        h = _ffn(h, vec(norm_ffn)[i], w_ffn_in_b, ffn_conv_w[i], vec(ffn_conv_b)[i], w_ffn_out_b, i, lp)
    return _final_norm(h, norm_final.reshape(1, D_MODEL), batch, seq)
```

```python
import functools

import jax
import jax.numpy as jnp
from jax import lax
from jax.experimental import pallas as pl
from jax.experimental.pallas import tpu as pltpu

F32 = jnp.float32
BF16 = jnp.bfloat16

D_MODEL = 1024
DEPTH = 4
N_META = 16
BLOCK = 128
PAD = (-N_META) % BLOCK
EPS = 1e-6
NEG_INF = -1e30
LOG2_E = 1.4426950408889634

N_HEADS = 4
MLA_NOPE = 64
MLA_ROPE = 32
MLA_Q_RANK = 256
MLA_KV_RANK = 128
MLA_PAD_LANE = 192
ROPE_THETA = 10000.0

RW_W = 256
RW_GN_EPS = 64e-5
RW_CHUNK = 64

SB_W = 256
SB_HEAD = 64
SB_SUFFIX_SPLIT = 256

GLA_DK = 32
GLA_QK = 128
GLA_W = 256
GLA_TAU = 16.0
GLA_CHUNK = 128
GLA_SUB = 16

D_FF = 2816
FFN_COL_CHUNKS = (1536, 1280)

W_MLA, W_RW, W_SB, W_GLA, W_GATE = 640, 1024, 768, 896, 4096
IN_GROUP_WIDTHS = (W_MLA, W_RW, W_SB, W_GLA)
IN_ROW_RW, IN_ROW_SB, IN_ROW_MLA, IN_ROW_GLA = 0, W_RW, W_RW + W_SB, W_RW + W_SB + W_MLA

ROW_TILE = 384
ATTN_TILE = 384
FINAL_ROWS = 1024
VMEM_LIMIT = 56 * 1024 * 1024


def _bf(x):
    return x.astype(BF16)


def _dot(a, b):
    return jnp.dot(_bf(a), _bf(b), preferred_element_type=F32)


def _dot_nt(a, b):
    return lax.dot_general(_bf(a), _bf(b), (((1,), (1,)), ((), ())), preferred_element_type=F32)


def _dot_tn(a, b):
    return lax.dot_general(_bf(a), _bf(b), (((0,), (0,)), ((), ())), preferred_element_type=F32)


def _split_hi_lo(x):
    hi = _bf(x)
    lo = _bf(x - hi.astype(F32))
    return hi, lo


def _dot_exact_rhs(x, m):
    hi, lo = _split_hi_lo(x)
    return jnp.dot(hi, m, preferred_element_type=F32) + jnp.dot(lo, m, preferred_element_type=F32)


def _dot_exact_lhs(m, x):
    hi, lo = _split_hi_lo(x)
    return jnp.dot(m, hi, preferred_element_type=F32) + jnp.dot(m, lo, preferred_element_type=F32)


def _iota(shape, dim):
    return lax.broadcasted_iota(jnp.int32, shape, dim)


def _div_pow2(x, d):
    assert d & (d - 1) == 0
    return lax.shift_right_logical(x, d.bit_length() - 1)


def _same_segment(shape, row_seg, col_seg):
    return _div_pow2(_iota(shape, 0), row_seg) == _div_pow2(_iota(shape, 1), col_seg)


def _segment_matrix(n, seg, value):
    return jnp.where(_same_segment((n, n), seg, seg), value, 0.0).astype(BF16)


def _head_lane_mask(width, head_width, h):
    lane = _iota((1, width), 1)
    return (lane >= h * head_width) & (lane < (h + 1) * head_width)


def _stack_heads(x, head_width):
    w = x.shape[1]
    return jnp.concatenate(
        [jnp.where(_head_lane_mask(w, head_width, h), x, 0.0) for h in range(N_HEADS)], axis=0)


def _rms(x, eps):
    return x * lax.rsqrt(jnp.mean(x * x, axis=-1, keepdims=True) + eps)


def _const_spec(shape):
    nd = len(shape)
    return pl.BlockSpec(shape, lambda *_: (0,) * nd)


def _layer_spec(stacked, layer):
    nd = stacked.ndim - 1
    return pl.BlockSpec((None,) + stacked.shape[1:], lambda *_: (layer,) + (0,) * nd)


def _params(sem, vmem=VMEM_LIMIT):
    return pltpu.CompilerParams(dimension_semantics=sem, vmem_limit_bytes=vmem)


def _embed_kernel(x_ref, meta_ref, o_ref):
    first_real = PAD + N_META

    @pl.when(pl.program_id(1) == 0)
    def _():
        o_ref[0:PAD, :] = jnp.zeros((PAD, D_MODEL), o_ref.dtype)
        o_ref[PAD:first_real, :] = meta_ref[...]
        o_ref[first_real:, :] = x_ref[0:o_ref.shape[0] - first_real, :]

    @pl.when(pl.program_id(1) > 0)
    def _():
        o_ref[...] = x_ref[...]


def _embed(x2, meta, batch):
    seq = x2.shape[0] // batch
    first_real = PAD + N_META
    lp = first_real + seq
    tm = ROW_TILE
    window = lambda b, i: (pl.multiple_of(b * seq + jnp.maximum(i * tm - first_real, 0), BLOCK), 0)
    return pl.pallas_call(
        _embed_kernel,
        grid=(batch, lp // tm),
        in_specs=[pl.BlockSpec((pl.Element(tm), pl.Element(D_MODEL)), window), _const_spec(meta.shape)],
        out_specs=pl.BlockSpec((tm, D_MODEL), lambda b, i: (b * (lp // tm) + i, 0)),
        out_shape=jax.ShapeDtypeStruct((batch * lp, D_MODEL), x2.dtype),
        compiler_params=_params(("parallel", "parallel")),
        name="embed",
    )(x2, meta)


def _mla_operands(z, qg, kvg, wq, wkv, cos, sin, seq_tile):
    nq = _rms(z[:, 0:256], EPS) * qg
    nkv = _rms(z[:, 256:384], EPS) * kvg
    ql = _dot(nq, wq)
    scale = (MLA_NOPE + MLA_ROPE) ** -0.5
    q = jnp.concatenate([ql[:, 0:128], ql[:, 128:256] * cos + ql[:, 512:640] * sin,
                         ql[:, 256:384], ql[:, 384:512] * cos + ql[:, 640:768] * sin], axis=1)
    pad_lane = (_iota((1, 512), 1) & 255) == MLA_PAD_LANE
    q = _bf(jnp.where(pad_lane, 1.0, q * scale))
    kvl = _dot(nkv, wkv)
    k_rope = z[:, 384:512] * cos + z[:, 512:640] * sin
    lane = _iota((1, 128), 1)
    is_pad_key = (seq_tile * z.shape[0] + _iota((z.shape[0], 1), 0)) < PAD
    k_rope = jnp.where(lane < 2 * MLA_ROPE, k_rope,
                       jnp.where((lane == MLA_PAD_LANE - 128) & is_pad_key, NEG_INF, 0.0))
    k = _bf(jnp.concatenate([kvl[:, 0:128], k_rope, kvl[:, 128:256], k_rope], axis=1))
    vt = _bf(kvl[:, 256:512].T)
    row_head = _div_pow2(_iota((256, 1), 0), 64)
    vt_heads = jnp.concatenate([jnp.where(row_head == h, vt, jnp.zeros((), BF16)) for h in range(N_HEADS)], axis=1)
    return q, k, vt_heads


def _in_proj_kernel(h_ref, g_ref, w_ref, qg_ref, kvg_ref, wq_ref, wkv_ref, cos_ref, sin_ref,
                    rw_ref, sb_ref, gla_ref, sb_values_ref, q_ref, k_ref, vt_ref, *, tiles_per_seq):
    n = _bf(_rms(h_ref[...], EPS) * g_ref[...])
    rw_ref[...] = _dot_nt(n, w_ref[IN_ROW_RW:IN_ROW_SB, :])
    sb_ref[...] = _bf(_dot_nt(n, w_ref[IN_ROW_SB:IN_ROW_MLA, :]))
    both = _dot_nt(n, w_ref[IN_ROW_MLA:IN_ROW_GLA + W_GLA, :])
    gla_ref[...] = both[:, W_MLA:]
    sb_values_ref[0] = _head_stacked_values(sb_ref[:, 2 * SB_W:3 * SB_W])
    q, k, vt_heads = _mla_operands(both[:, :W_MLA], qg_ref[...], kvg_ref[...], wq_ref[...], wkv_ref[...],
                                   cos_ref[...], sin_ref[...], pl.program_id(0) % tiles_per_seq)
    q_ref[...] = q
    k_ref[...] = k
    vt_ref[0] = vt_heads


def _in_proj(h, g, w_all, qg, kvg, wq, wkv, layer, cos, sin, lp):
    t = h.shape[0]
    tm = ROW_TILE
    nb = lp // tm
    rows = lambda w: pl.BlockSpec((tm, w), lambda i: (i, 0))
    table = pl.BlockSpec((tm, 128), lambda i: (i % nb, 0))
    return pl.pallas_call(
        functools.partial(_in_proj_kernel, tiles_per_seq=nb),
        grid=(t // tm,),
        in_specs=[rows(D_MODEL), _const_spec((1, D_MODEL)), _layer_spec(w_all, layer),
                  _const_spec((1, MLA_Q_RANK)), _const_spec((1, MLA_KV_RANK)),
                  _layer_spec(wq, layer), _layer_spec(wkv, layer), table, table],
        out_specs=[rows(W_RW), rows(W_SB), rows(W_GLA),
                   pl.BlockSpec((1, N_HEADS * tm, SB_W), lambda i: (i, 0, 0)),
                   rows(512), rows(512), pl.BlockSpec((1, 256, N_HEADS * tm), lambda i: (i, 0, 0))],
        out_shape=[jax.ShapeDtypeStruct((t, W_RW), F32), jax.ShapeDtypeStruct((t, W_SB), BF16),
                   jax.ShapeDtypeStruct((t, W_GLA), F32),
                   jax.ShapeDtypeStruct((t // tm, N_HEADS * tm, SB_W), BF16),
                   jax.ShapeDtypeStruct((t, 512), BF16), jax.ShapeDtypeStruct((t, 512), BF16),
                   jax.ShapeDtypeStruct((t // tm, 256, N_HEADS * tm), BF16)],
        compiler_params=_params(("parallel",)),
        name="in_proj",
    )(h, g, w_all, qg, kvg, wq, wkv, cos, sin)


def _head_stacked_values(vb):
    zero = jnp.zeros((), vb.dtype)
    return jnp.concatenate([jnp.where(_head_lane_mask(256, 64, h), vb, zero) for h in range(N_HEADS)], axis=0)


def _per_head_rows(rows, n):
    return jnp.concatenate([jnp.broadcast_to(r, (64, n)) for r in rows], axis=0)


def _attn_kernel(sq_ref, sk_ref, sv_ref, later_ref, mq_ref, mk_ref, mvt_ref, ysb_ref, ymla_ref,
                 sqh_ref, c_ref, sacc_ref, wts_ref, mqh_ref, m_ref, l_ref, macc_ref, p_ref, alpha_ref, *, tile):
    i = pl.program_id(1)
    heads = range(N_HEADS)
    q_row = i * tile + _iota((tile, 1), 0)
    q_pos = i * tile + _iota((1, tile), 1)
    for h in heads:
        sqh_ref[h] = jnp.where(_head_lane_mask(SB_W, SB_HEAD, h), sq_ref[...], jnp.zeros((), BF16))
    lane = _iota((1, 256), 1)
    for h in heads:
        half, slot = h // 2, h % 2
        head_lanes = (((lane >= slot * MLA_NOPE) & (lane < (slot + 1) * MLA_NOPE))
                      | ((lane >= 128 + slot * MLA_ROPE) & (lane < 128 + (slot + 1) * MLA_ROPE))
                      | (lane == MLA_PAD_LANE))
        mqh_ref[half, slot * tile:(slot + 1) * tile, :] = jnp.where(
            head_lanes, mq_ref[:, half * 256:(half + 1) * 256], jnp.zeros((), BF16))
    c_ref[...] = jnp.zeros(c_ref.shape, F32)
    sacc_ref[...] = jnp.zeros(sacc_ref.shape, F32)
    m_ref[...] = jnp.full(m_ref.shape, NEG_INF, F32)
    l_ref[...] = jnp.zeros(l_ref.shape, F32)
    macc_ref[...] = jnp.zeros(macc_ref.shape, F32)

    def add_values(j, slot):
        sacc_ref[...] += jnp.dot(wts_ref[slot], sv_ref[j], preferred_element_type=F32)
        pv = jnp.dot(mvt_ref[j], p_ref[slot], preferred_element_type=F32)
        macc_ref[...] = _per_head_rows([alpha_ref[slot, h] for h in heads], tile) * macc_ref[...] + pv

    def block(j, masked, prev, slot):
        start = pl.multiple_of(j * tile, tile)
        cut = SB_SUFFIX_SPLIT
        z = [_dot_nt(sqh_ref[h], sk_ref[pl.ds(start, tile), :]) for h in heads]
        s_pair = [_dot_nt(mk_ref[pl.ds(start, tile), half * 256:(half + 1) * 256], mqh_ref[half]) for half in (0, 1)]
        if prev is not None:
            add_values(prev, 1 - slot)

        for h in heads:
            s_h = s_pair[h // 2][:, (h % 2) * tile:(h % 2 + 1) * tile]
            if masked:
                s_h = jnp.where((start + _iota((tile, 1), 0)) <= q_pos, s_h, NEG_INF)
            m_old = m_ref[h]
            m_new = jnp.maximum(m_old, jnp.max(s_h, axis=0, keepdims=True))
            p = jnp.exp(s_h - m_new)
            alpha = jnp.exp(m_old - m_new)
            l_ref[h] = alpha * l_ref[h] + jnp.sum(p, axis=0, keepdims=True)
            m_ref[h] = m_new
            p_ref[slot, h * tile:(h + 1) * tile, :] = _bf(p)
            alpha_ref[slot, h] = alpha

        log_take = [jnp.minimum(z[h], 0.0) - jnp.log2(1.0 + jnp.exp2(-jnp.abs(z[h]))) for h in heads]
        log_keep = [log_take[h] - z[h] for h in heads]
        if masked:
            mask = (start + _iota((1, tile), 1)) < q_row
            log_keep = [jnp.where(mask, log_keep[h], 0.0) for h in heads]
        keep16 = [_bf(log_keep[h]) for h in heads]
        later_head = [jnp.dot(keep16[h][:, :cut], later_ref[...], preferred_element_type=F32) for h in heads]
        later_tail = [jnp.dot(keep16[h][:, cut:], later_ref[:tile - cut, :tile - cut], preferred_element_type=F32)
                      for h in heads]

        for h in heads:
            head_sum = jnp.sum(log_keep[h][:, :cut], axis=-1, keepdims=True)
            tail_sum = jnp.sum(log_keep[h][:, cut:], axis=-1, keepdims=True)
            c = c_ref[h]
            later = jnp.concatenate([later_head[h] + tail_sum, later_tail[h]], axis=1)
            w = jnp.exp2(log_take[h] + later + c)
            if masked:
                w = jnp.where(mask, w, 0.0)
            c_ref[h] = c + (head_sum + tail_sum)
            wts_ref[slot, :, h * tile:(h + 1) * tile] = _bf(w)

    block(i, True, None, 0)

    def below_pair(t, carry):
        j = i - 1 - 2 * t
        block(j, False, j + 1, 1)
        block(j - 1, False, j, 0)
        return carry

    lax.fori_loop(0, lax.shift_right_logical(i, 1), below_pair, 0)
    odd = (i & 1) == 1

    @pl.when(odd)
    def _():
        block(0, False, 1, 1)
        add_values(0, 1)

    @pl.when(jnp.logical_not(odd))
    def _():
        add_values(0, 0)
    ysb_ref[...] = _bf(sacc_ref[...])
    ymla_ref[...] = _bf((macc_ref[...] / _per_head_rows([l_ref[h] for h in heads], tile)).T)


def _attn(z_sb, sb_values, later_mat, q, k, vt, lp):
    t = z_sb.shape[0]
    tile = ATTN_TILE
    nb = lp // tile
    tiles = lambda w: pl.BlockSpec((tile, w), lambda b, i: (b * nb + i, 0))
    resident = lambda shape, index: pl.BlockSpec(shape, index, pipeline_mode=pl.Buffered(1))
    return pl.pallas_call(
        functools.partial(_attn_kernel, tile=tile),
        grid=(t // lp, nb),
        in_specs=[tiles(SB_W),
                  resident((lp, SB_W), lambda b, i: (b, 1)),
                  resident((nb, N_HEADS * tile, SB_W), lambda b, i: (b, 0, 0)),
                  _const_spec((SB_SUFFIX_SPLIT, SB_SUFFIX_SPLIT)),
                  tiles(512),
                  resident((lp, 512), lambda b, i: (b, 0)),
                  resident((nb, 256, N_HEADS * tile), lambda b, i: (b, 0, 0))],
        out_specs=[tiles(SB_W), tiles(256)],
        out_shape=[jax.ShapeDtypeStruct((t, SB_W), BF16), jax.ShapeDtypeStruct((t, 256), BF16)],
        scratch_shapes=[pltpu.VMEM((N_HEADS, tile, SB_W), BF16), pltpu.VMEM((N_HEADS, tile, 1), F32),
                        pltpu.VMEM((tile, SB_W), F32), pltpu.VMEM((2, tile, N_HEADS * tile), BF16),
                        pltpu.VMEM((2, 2 * tile, 256), BF16), pltpu.VMEM((N_HEADS, 1, tile), F32),
                        pltpu.VMEM((N_HEADS, 1, tile), F32), pltpu.VMEM((256, tile), F32),
                        pltpu.VMEM((2, N_HEADS * tile, tile), BF16), pltpu.VMEM((2, N_HEADS, 1, tile), F32)],
        compiler_params=_params(("parallel", "arbitrary")),
        name="attn",
    )(z_sb, z_sb, sb_values, later_mat, q, k, vt)


def _compact_eye(c):
    return jnp.where(_iota((c, N_HEADS * c), 0) == (_iota((c, N_HEADS * c), 1) & (c - 1)), 1.0, 0.0)


def _rw_prep_kernel(z_ref, zprev_ref, mu_ref, w0_ref, w2_ref, a0_ref, a2_ref, g2_ref, kk_ref, ka_ref, rk_ref,
                    w_o, rt_o, arb_o, kbar_o, bbar_o, v_o, uv_o, yv_o, dec_o, g_o, bonus_o):
    i = pl.program_id(1)
    z = z_ref[...]
    tm = z.shape[0]
    prev = jnp.where(i == 0, 0.0, zprev_ref[7:8, :])
    shifted = jnp.where(_iota((tm, 1), 0) == 0, prev, pltpu.roll(z, 1, 0))
    zz = z + (shifted - z) * mu_ref[...]
    r, k, v = zz[:, 0:256], zz[:, 256:512], zz[:, 512:768]
    lora_in = zz[:, 768:896]
    u = w0_ref[...] + _dot(jnp.tanh(lora_in), w2_ref[...])
    w = jnp.minimum(u, 0.0) - jnp.log1p(jnp.exp(-jnp.abs(u))) - 0.5
    a = jax.nn.sigmoid(a0_ref[...] + _dot(lora_in, a2_ref[...]))
    seg = _segment_matrix(RW_W, 64, 1.0)
    kx = k * kk_ref[...]
    kap = kx / jnp.maximum(jnp.sqrt(_dot_exact_rhs(kx * kx, seg)), 1e-12)
    kmod = k * (1.0 + (a - 1.0) * ka_ref[...])
    beta = kap * a
    lw = -jnp.exp(w)
    g_o[...] = _dot(jax.nn.sigmoid(zz[:, 896:1024]), g2_ref[...])
    bonus_o[...] = _dot_exact_rhs(r * kmod * rk_ref[...], seg) * v

    c = RW_CHUNK
    same_chunk = _same_segment((tm, tm), c, c)
    lw_hi, lw_lo = _split_hi_lo(lw)
    cum = jnp.where(same_chunk & (_iota((tm, tm), 0) >= _iota((tm, tm), 1)), 1.0, 0.0).astype(BF16)
    b = jnp.dot(cum, lw_hi, preferred_element_type=F32) + jnp.dot(cum, lw_lo, preferred_element_type=F32)
    b_end = jnp.concatenate(
        [jnp.broadcast_to(b[n * c + c - 1:(n + 1) * c, :], (c, RW_W)) for n in range(tm // c)], axis=0)
    grow = jnp.exp(-b)
    to_end = jnp.exp(b_end - b)
    kap_t = kap * jnp.exp(b - lw)
    r_t = r * jnp.exp(b)
    beta_g = beta * grow
    k_g = kmod * grow
    rt_o[...] = _bf(r_t)
    kbar_o[...] = _bf(kmod * to_end)
    bbar_o[...] = _bf(beta * to_end)
    v_o[...] = _bf(v)
    decay_end = jnp.exp(b_end)

    lane_pos = _iota((c, N_HEADS * c), 1) & (c - 1)
    strictly_earlier = _iota((c, N_HEADS * c), 0) > lane_pos
    not_later = _iota((c, N_HEADS * c), 0) >= lane_pos
    chunks = [slice(n * c, (n + 1) * c) for n in range(tm // c)]
    pair = [_dot_nt(jnp.concatenate([kap_t[rows], r_t[rows]], axis=0),
                    jnp.concatenate([_stack_heads(beta_g[rows], 64), _stack_heads(k_g[rows], 64)], axis=0))
            for rows in chunks]
    a_kb = [jnp.where(strictly_earlier, p[0:c, 0:4 * c], 0.0) for p in pair]
    a_kk = [jnp.where(strictly_earlier, p[0:c, 4 * c:8 * c], 0.0) for p in pair]
    a_rk = [jnp.where(not_later, p[c:2 * c, 4 * c:8 * c], 0.0) for p in pair]
    for rows, p in zip(chunks, pair):
        arb_o[rows, :] = _bf(jnp.where(not_later, p[c:2 * c, 0:4 * c], 0.0))
    inv = [_compact_eye(c) - a for a in a_kb]
    power = [_dot(a, _stack_heads(a, c)) for a in a_kb]
    span = 2
    while span < c:
        inv = [t + _dot(t, _stack_heads(p, c)) for t, p in zip(inv, power)]
        span *= 2
        if span < c:
            power = [_dot(p, _stack_heads(p, c)) for p in power]
    v_st = [_stack_heads(v[rows], 64) for rows in chunks]
    kk_v = [_dot(a, vs) for a, vs in zip(a_kk, v_st)]
    for n, rows in enumerate(chunks):
        w_o[rows, :] = _bf(_dot(inv[n], _stack_heads(kap_t[rows], 64)))
        uv_o[rows, :] = _dot(inv[n], _stack_heads(kk_v[n], 64))
        yv_o[rows, :] = _dot(a_rk[n], v_st[n])
        dec_o[n * 8:(n + 1) * 8, :] = decay_end[n * c:n * c + 8]


def _rw_prep(z_rw, p, lp):
    t = z_rw.shape[0]
    tm = ROW_TILE
    nb = lp // tm
    row = lambda b, i: (b * nb + i, 0)
    prev = lambda b, i: (jnp.maximum(b * (lp // 8) + i * (tm // 8) - 1, 0), 0)
    consts = [p["rw_mu"], p["rw_w0"], p["rw_w2"], p["rw_a0"], p["rw_a2"], p["rw_g2"],
              p["rw_k_k"], p["rw_k_a"], p["rw_r_k"]]
    return pl.pallas_call(
        _rw_prep_kernel,
        grid=(t // lp, nb),
        in_specs=[pl.BlockSpec((tm, W_RW), row), pl.BlockSpec((8, W_RW), prev)]
        + [_const_spec(c.shape) for c in consts],
        out_specs=[pl.BlockSpec((tm, RW_W), row)] * 8
        + [pl.BlockSpec((tm // RW_CHUNK * 8, RW_W), row)] + [pl.BlockSpec((tm, RW_W), row)] * 2,
        out_shape=[jax.ShapeDtypeStruct((t, RW_W), BF16)] * 6 + [jax.ShapeDtypeStruct((t, RW_W), F32)] * 2
        + [jax.ShapeDtypeStruct((t // RW_CHUNK * 8, RW_W), F32)] + [jax.ShapeDtypeStruct((t, RW_W), F32)] * 2,
        compiler_params=_params(("parallel", "parallel")),
        name="rw_prep",
    )(z_rw, z_rw, *consts)


def _recurrent_kernel(w_ref, rt_ref, arb_ref, kbar_ref, bbar_ref, rv_ref, uv_ref, yv_ref, dec_ref,
                      q_ref, k_ref, v_ref, al_ref, a2_ref, ab_ref, y_ref, o_ref, ht_ref, st_ref):
    c = GLA_CHUNK
    rc = RW_CHUNK
    seqs = range(st_ref.shape[0])

    @pl.when(pl.program_id(0) == 0)
    def _():
        ht_ref[...] = jnp.zeros_like(ht_ref)
        st_ref[...] = jnp.zeros_like(st_ref)

    rw_same_head = _same_segment((RW_W, RW_W), 64, 64)

    def rw_from_state(n, ht):
        rows = slice(n * rc, (n + 1) * rc)
        return [_dot_nt(jnp.concatenate([w_ref[s, rows], rt_ref[s, rows]], axis=0), ht[s]) for s in seqs]

    def rw_finish(n, from_state, ht):
        rows = slice(n * rc, (n + 1) * rc)
        u = [from_state[s][0:rc] + uv_ref[s, rows] for s in seqs]
        from_u = [_dot(arb_ref[s, rows], _stack_heads(u[s], 64)) for s in seqs]
        upd = [_dot_tn(jnp.concatenate([rv_ref[s, rows], _bf(u[s])], axis=0),
                       jnp.concatenate([kbar_ref[s, rows], -bbar_ref[s, rows]], axis=0)) for s in seqs]
        for s in seqs:
            y_ref[s, rows] = from_state[s][rc:2 * rc] + yv_ref[s, rows] - from_u[s]
        return [ht[s] * dec_ref[s, n * 8:n * 8 + 1, :] + jnp.where(rw_same_head, upd[s], 0.0) for s in seqs]

    sub = GLA_SUB
    nsub = c // sub
    row_i = _iota((c, c), 0)
    col_i = _iota((c, c), 1)
    tri = jnp.where(row_i >= col_i, 1.0, 0.0).astype(BF16)
    tri_sub = jnp.where((row_i >= col_i) & _same_segment((c, c), sub, sub), 1.0, 0.0).astype(BF16)
    key_pos = _iota((c, 1), 0)
    query_in_sub = _iota((N_HEADS * sub, c), 0) & (sub - 1)
    gla_same_head = _same_segment((GLA_W, GLA_QK), 64, GLA_DK)

    ht = [ht_ref[s] for s in seqs]
    rw_state_0 = rw_from_state(0, ht)

    x = [_dot(al_ref[s], a2_ref[...]) + ab_ref[...] for s in seqs]
    log_a = [(jnp.minimum(x[s], 0.0) - jnp.log1p(jnp.exp(-jnp.abs(x[s])))) * (1.0 / GLA_TAU) for s in seqs]
    b = [_dot_exact_lhs(tri, log_a[s]) for s in seqs]
    b_sub = [_dot_exact_lhs(tri_sub, log_a[s]) for s in seqs]
    q = [q_ref[s] * (GLA_DK ** -0.5) for s in seqs]
    st = [st_ref[s] for s in seqs]
    inter = [_dot_nt(q[s] * jnp.exp(b[s]), st[s]) for s in seqs]

    ht = rw_finish(0, rw_state_0, ht)
    rw_state_1 = rw_from_state(1, ht)

    scores = []
    for s in seqs:
        q_sub = q[s] * jnp.exp(b_sub[s])
        beta = b[s] - b_sub[s]
        k = k_ref[s]
        blocks = []
        for blk in range(nsub):
            lo, hi = blk * sub, (blk + 1) * sub
            expo = jnp.where(key_pos < hi, beta[lo:lo + 1, :] - b[s], NEG_INF)
            sc = _dot_nt(_stack_heads(q_sub[lo:hi], GLA_DK), k * jnp.exp(expo))
            blocks.append(jnp.where(_iota((N_HEADS * sub, c), 1) <= lo + query_in_sub, sc, 0.0))
        scores.append(jnp.concatenate(blocks, axis=0))

    ht = rw_finish(1, rw_state_1, ht)
    for s in seqs:
        ht_ref[s] = ht[s]

    per_head = [_dot(scores[s], v_ref[s]) for s in seqs]
    upd = [_dot_tn(v_ref[s], k_ref[s] * jnp.exp(b[s][c - 1:c, :] - b[s])) for s in seqs]
    for s in seqs:
        pieces = []
        for blk in range(nsub):
            piece = jnp.zeros((sub, GLA_W), F32)
            for h in range(N_HEADS):
                r0 = (blk * N_HEADS + h) * sub
                piece = piece + jnp.where(_head_lane_mask(GLA_W, 64, h), per_head[s][r0:r0 + sub], 0.0)
            pieces.append(piece)
        o_ref[s] = inter[s] + jnp.concatenate(pieces, axis=0)
        st_ref[s] = st[s] * jnp.exp(b[s][c - 1:c, :]) + jnp.where(gla_same_head, upd[s], 0.0)


def _recurrent(rw_terms, z_gla, a2, ab, batch):
    t = z_gla.shape[0]
    lp = t // batch
    c = GLA_CHUNK
    assert c == 2 * RW_CHUNK
    as_seq = lambda a: a.reshape(batch, a.shape[0] // batch, a.shape[1])
    rows = pl.BlockSpec((batch, c, RW_W), lambda i: (0, i, 0))
    z3 = as_seq(z_gla)
    y, o = pl.pallas_call(
        _recurrent_kernel,
        grid=(lp // c,),
        in_specs=[rows] * 8 + [pl.BlockSpec((batch, 16, RW_W), lambda i: (0, i, 0)),
                               pl.BlockSpec((batch, c, 128), lambda i: (0, i, 0)),
                               pl.BlockSpec((batch, c, 128), lambda i: (0, i, 1)),
                               pl.BlockSpec((batch, c, 256), lambda i: (0, i, 1)),
                               pl.BlockSpec((batch, c, 128), lambda i: (0, i, 6)),
                               _const_spec(a2.shape), _const_spec(ab.shape)],
        out_specs=[rows, pl.BlockSpec((batch, c, GLA_W), lambda i: (0, i, 0))],
        out_shape=[jax.ShapeDtypeStruct((batch, lp, RW_W), F32), jax.ShapeDtypeStruct((batch, lp, GLA_W), F32)],
        scratch_shapes=[pltpu.VMEM((batch, RW_W, RW_W), F32), pltpu.VMEM((batch, GLA_W, GLA_QK), F32)],
        compiler_params=_params(("arbitrary",)),
        name="recurrent",
    )(*map(as_seq, rw_terms), z3, z3, z3, z3, a2, ab)
    return y.reshape(t, RW_W), o.reshape(t, GLA_W)


def _merge_kernel(h_ref, ymla_ref, yrw_ref, bonus_ref, g_ref, ysb_ref, ogla_ref, rgla_ref,
                  nmix_ref, lnw_ref, lnb_ref, gn_ref, gb_ref, wg_ref, wb_ref, wo_ref, o_ref, *, tiles_per_seq):
    tm = h_ref.shape[0]
    avg = _segment_matrix(256, 64, 1.0 / 64)
    h = h_ref[...]
    n = _bf(_rms(h, EPS) * nmix_ref[...])

    y = yrw_ref[...]
    d = y - _dot_exact_rhs(y, avg)
    var = _dot_exact_rhs(d * d, avg)
    y_rw = (d * lax.rsqrt(var + RW_GN_EPS) * lnw_ref[...] + lnb_ref[...] + bonus_ref[...]) * g_ref[...]

    o = ogla_ref[...]
    r = rgla_ref[...]
    y_gla = o * lax.rsqrt(_dot_exact_rhs(o * o, avg) + EPS) * gn_ref[...] * (r * jax.nn.sigmoid(r))

    acc = jnp.zeros((tm, D_MODEL), F32)
    for m, y_m in enumerate((ymla_ref[...], y_rw, ysb_ref[...], y_gla)):
        logits = _dot_nt(n, wg_ref[m * D_MODEL:(m + 1) * D_MODEL, :])
        gate = jax.nn.sigmoid(logits + gb_ref[m:m + 1, :])
        acc = acc + gate * _dot(y_m, wb_ref[m])
    delta = _dot(acc, wo_ref[...])
    row = (pl.program_id(0) % tiles_per_seq) * tm + _iota((tm, 1), 0)
    o_ref[...] = h + jnp.where(row >= PAD, delta, 0.0)


def _merge(h, y_mla, y_rw, bonus, g, y_sb, o_gla, z_gla, p, lp):
    t = h.shape[0]
    tm = ROW_TILE
    row = lambda i: (i, 0)
    w256 = pl.BlockSpec((tm, 256), row)
    consts = [p["norm_mix"], p["rw_ln_w"], p["rw_ln_b"], p["gla_norm"], p["gate_b"]]
    stacked = [p["w_gate"], p["w_branch"], p["w_out"]]
    return pl.pallas_call(
        functools.partial(_merge_kernel, tiles_per_seq=lp // tm),
        grid=(t // tm,),
        in_specs=[pl.BlockSpec((tm, D_MODEL), row), w256, w256, w256, w256, w256, w256,
                  pl.BlockSpec((tm, 256), lambda i: (i, 2))]
        + [_const_spec(c.shape) for c in consts]
        + [_layer_spec(w, p["layer"]) for w in stacked],
        out_specs=pl.BlockSpec((tm, D_MODEL), row),
        out_shape=jax.ShapeDtypeStruct((t, D_MODEL), F32),
        compiler_params=_params(("parallel",)),
        name="merge",
    )(h, y_mla, y_rw, bonus, g, y_sb, o_gla, z_gla, *consts, *stacked)


def _ffn_kernel(h_ref, g_ref, win_ref, cw_ref, cb_ref, wout_ref, o_ref, tail_ref):
    @pl.when(pl.program_id(1) == 0)
    def _():
        tail_ref[...] = jnp.zeros_like(tail_ref)

    x = h_ref[...]
    tm = x.shape[0]
    n = _bf(_rms(x, EPS) * g_ref[...])
    rowi = _iota((tm, 1), 0)
    acc = jnp.zeros((tm, D_MODEL), F32)
    assert sum(FFN_COL_CHUNKS) == D_FF
    for c0, c1 in zip((0, FFN_COL_CHUNKS[0]), (FFN_COL_CHUNKS[0], D_FF)):
        a = jnp.dot(n, win_ref[:, c0:c1], preferred_element_type=F32)
        u = jnp.dot(n, win_ref[:, D_FF + c0:D_FF + c1], preferred_element_type=F32)
        prev1 = tail_ref[7:8, c0:c1]
        prev2 = tail_ref[6:7, c0:c1]
        a1 = jnp.where(rowi == 0, prev1, pltpu.roll(a, 1, 0))
        a2 = jnp.where(rowi == 0, prev2, jnp.where(rowi == 1, prev1, pltpu.roll(a, 2, 0)))
        tail_ref[:, c0:c1] = a[tm - 8:tm, :]
        conv = cb_ref[:, c0:c1] + cw_ref[0:1, c0:c1] * a2 + cw_ref[1:2, c0:c1] * a1 + cw_ref[2:3, c0:c1] * a
        acc = acc + _dot(conv * jax.nn.sigmoid(conv) * u, wout_ref[c0:c1, :])
    o_ref[...] = x + acc


def _ffn(h, g, w_in, conv_w, conv_b, w_out, layer, lp):
    t = h.shape[0]
    tm = ROW_TILE
    nb = lp // tm
    row = lambda b, i: (b * nb + i, 0)
    return pl.pallas_call(
        _ffn_kernel,
        grid=(t // lp, nb),
        in_specs=[pl.BlockSpec((tm, D_MODEL), row), _const_spec((1, D_MODEL)), _layer_spec(w_in, layer),
                  _const_spec(conv_w.shape), _const_spec(conv_b.shape), _layer_spec(w_out, layer)],
        out_specs=pl.BlockSpec((tm, D_MODEL), row),
        out_shape=jax.ShapeDtypeStruct((t, D_MODEL), F32),
        scratch_shapes=[pltpu.VMEM((8, D_FF), F32)],
        compiler_params=_params(("arbitrary", "arbitrary")),
        name="conv_ffn",
    )(h, g, w_in, conv_w, conv_b, w_out)


def _final_norm_kernel(h_ref, g_ref, o_ref):
    o_ref[0] = _rms(h_ref[...], EPS) * g_ref[...]


def _final_norm(h, g, batch, seq):
    lp = h.shape[0] // batch
    rows = FINAL_ROWS
    assert seq % rows == 0
    first = lambda bi, i: (pl.multiple_of(bi * lp + (PAD + N_META) + i * rows, BLOCK), 0)
    return pl.pallas_call(
        _final_norm_kernel,
        grid=(batch, seq // rows),
        in_specs=[pl.BlockSpec((pl.Element(rows), pl.Element(D_MODEL)), first), _const_spec((1, D_MODEL))],
        out_specs=pl.BlockSpec((1, rows, D_MODEL), lambda bi, i: (bi, i, 0)),
        out_shape=jax.ShapeDtypeStruct((batch, seq, D_MODEL), F32),
        compiler_params=_params(("parallel", "parallel")),
        name="final_norm",
    )(h, g)


def _rope_swap(w):
    half = w.shape[-1] // 2
    return jnp.concatenate([-w[..., half:], w[..., :half]], axis=-1)


def _in_weight_layout_kernel(w_ref, o_ref, gate_ref):
    def put(dst, val):
        o_ref[0, dst:dst + val.shape[0], :] = _bf(val)

    mla, sb, gla = IN_ROW_MLA, IN_ROW_SB, IN_ROW_GLA
    put(mla, w_ref[0, 0:384, :])
    half = MLA_ROPE // 2
    kr = w_ref[0, 384:384 + MLA_ROPE, :]
    kr_swapped = jnp.concatenate([-kr[half:], kr[:half]], axis=0)
    put(mla + 384, jnp.concatenate([kr] * 4, axis=0))
    put(mla + 512, jnp.concatenate([kr_swapped] * 4, axis=0))
    put(IN_ROW_RW, w_ref[0, 416:1440, :])
    put(sb, w_ref[0, 1440:1440 + SB_W, :] * (SB_HEAD ** -0.5 * LOG2_E))
    put(sb + SB_W, w_ref[0, 1440 + SB_W:2208, :])
    put(gla, w_ref[0, 2208:2720, :])
    put(gla + 512, w_ref[0, 2736:2992, :])
    lora = w_ref[0, 2720:2736, :]
    put(gla + 768, jnp.concatenate([lora, jnp.zeros((W_GLA - 784, lora.shape[1]), lora.dtype)], axis=0))
    gate_ref[0] = _bf(w_ref[0, 2992:2992 + W_GATE, :])


def _in_weight_layout(w_in):
    wt = jnp.swapaxes(w_in, 1, 2)
    depth, n, d = wt.shape
    cols = 256
    total = sum(IN_GROUP_WIDTHS)
    return pl.pallas_call(
        _in_weight_layout_kernel,
        grid=(depth, d // cols),
        in_specs=[pl.BlockSpec((1, n, cols), lambda l, i: (l, 0, i))],
        out_specs=[pl.BlockSpec((1, total, cols), lambda l, i: (l, 0, i)),
                   pl.BlockSpec((1, W_GATE, cols), lambda l, i: (l, 0, i))],
        out_shape=[jax.ShapeDtypeStruct((depth, total, d), BF16), jax.ShapeDtypeStruct((depth, W_GATE, d), BF16)],
        compiler_params=_params(("parallel", "parallel")),
        name="in_weight_layout",
    )(wt)


def _layout_params(w_in, mla_w_uq, mla_w_ukv, rw_w2, rw_a2, gla_a2):
    w_all, w_gate = _in_weight_layout(w_in)

    depth = w_in.shape[0]
    wuq = mla_w_uq.reshape(depth, MLA_Q_RANK, N_HEADS, MLA_NOPE + MLA_ROPE)
    nope, rope = wuq[..., :MLA_NOPE], wuq[..., MLA_NOPE:]
    rope_sw = _rope_swap(rope)
    zeros64 = jnp.zeros((depth, MLA_Q_RANK, 64), w_in.dtype)
    pair = lambda x, a, b: jnp.concatenate([x[:, :, a], x[:, :, b]], axis=-1)
    rope_pair = lambda x, a, b: jnp.concatenate([x[:, :, a], x[:, :, b], zeros64], axis=-1)
    wq = _bf(jnp.concatenate([pair(nope, 0, 1), rope_pair(rope, 0, 1), pair(nope, 2, 3), rope_pair(rope, 2, 3),
                              rope_pair(rope_sw, 0, 1), rope_pair(rope_sw, 2, 3)], axis=-1))
    wukv = mla_w_ukv.reshape(depth, MLA_KV_RANK, N_HEADS, 128)
    wkv = _bf(jnp.concatenate([wukv[..., :64].reshape(depth, MLA_KV_RANK, 256),
                               wukv[..., 64:].reshape(depth, MLA_KV_RANK, 256)], axis=-1))

    z64 = jnp.zeros_like(rw_w2)
    w2 = _bf(jnp.concatenate([rw_w2, z64], axis=1))
    a2 = _bf(jnp.concatenate([z64, rw_a2], axis=1))
    gla_a2p = _bf(jnp.concatenate([gla_a2, jnp.zeros((depth, 128 - gla_a2.shape[1], GLA_QK), gla_a2.dtype)], axis=1))
    return w_all, w_gate, wq, wkv, w2, a2, gla_a2p


def _rope_tables(lp):
    half = MLA_ROPE // 2
    freqs = ROPE_THETA ** (-jnp.arange(half, dtype=F32) / half)
    pos = (jnp.arange(lp) - PAD).astype(F32)
    ang = pos[:, None] * freqs[None, :]
    return jnp.tile(jnp.cos(ang), (1, 128 // half)), jnp.tile(jnp.sin(ang), (1, 128 // half))


def kernel(x, meta_tokens, norm_mix, w_in, mla_q_norm, mla_w_uq, mla_kv_norm, mla_w_ukv, rw_mu, rw_w0, rw_w2, rw_a0, rw_a2, rw_g2, rw_k_k, rw_k_a, rw_r_k, rw_ln_w, rw_ln_b, gla_a2, gla_a_b, gla_norm, gate_b, w_branch, w_out, norm_ffn, w_ffn_in, ffn_conv_w, ffn_conv_b, w_ffn_out, norm_final):
    batch, seq, _ = x.shape
    depth = w_in.shape[0]
    lp = PAD + N_META + seq
    t = batch * lp
    assert lp % ROW_TILE == 0 and lp % ATTN_TILE == 0 and lp % GLA_CHUNK == 0 and lp % RW_CHUNK == 0

    w_all, w_gate, wq, wkv, rw_w2p, rw_a2p, gla_a2p = _layout_params(
        w_in, mla_w_uq, mla_w_ukv, rw_w2, rw_a2, gla_a2)
    w_branch_b, w_out_b, w_ffn_in_b, w_ffn_out_b, rw_g2_b = map(_bf, (w_branch, w_out, w_ffn_in, w_ffn_out, rw_g2))
    vec = lambda a: a.reshape(depth, 1, -1)
    cos, sin = _rope_tables(lp)
    idx = jnp.arange(SB_SUFFIX_SPLIT)
    later_mat = jnp.where(idx[:, None] > idx[None, :], 1.0, 0.0).astype(BF16)

    h = _embed(x.reshape(batch * seq, D_MODEL), meta_tokens.astype(x.dtype), batch)

    for i in range(depth):
        z_rw, z_sb, z_gla, sb_values, q, k, v = _in_proj(
            h, vec(norm_mix)[i], w_all, vec(mla_q_norm)[i], vec(mla_kv_norm)[i], wq, wkv, i, cos, sin, lp)
        y_sb, y_mla = _attn(z_sb, sb_values, later_mat, q, k, v, lp)
        rw = {"rw_mu": vec(rw_mu)[i], "rw_w0": vec(rw_w0)[i], "rw_w2": rw_w2p[i], "rw_a0": vec(rw_a0)[i],
              "rw_a2": rw_a2p[i], "rw_g2": rw_g2_b[i], "rw_k_k": vec(rw_k_k)[i], "rw_k_a": vec(rw_k_a)[i],
              "rw_r_k": vec(rw_r_k)[i]}
        *chunk_terms, g, bonus = _rw_prep(z_rw, rw, lp)
        y_rw, o_gla = _recurrent(chunk_terms, z_gla, gla_a2p[i], vec(gla_a_b)[i], batch)
        mp = {"rw_ln_w": vec(rw_ln_w)[i], "rw_ln_b": vec(rw_ln_b)[i], "gla_norm": vec(gla_norm)[i],
              "gate_b": gate_b[i], "norm_mix": vec(norm_mix)[i], "w_gate": w_gate, "w_branch": w_branch_b,
              "w_out": w_out_b, "layer": i}
        h = _merge(h, y_mla, y_rw, bonus, g, y_sb, o_gla, z_gla, mp, lp)
        h = _ffn(h, vec(norm_ffn)[i], w_ffn_in_b, ffn_conv_w[i], vec(ffn_conv_b)[i], w_ffn_out_b, i, lp)
    return _final_norm(h, norm_final.reshape(1, D_MODEL), batch, seq)
```

```python
import functools

import jax
import jax.numpy as jnp
from jax import lax
from jax.experimental import pallas as pl
from jax.experimental.pallas import tpu as pltpu

F32 = jnp.float32
BF16 = jnp.bfloat16

D_MODEL = 1024
DEPTH = 4
N_META = 16
BLOCK = 128
PAD = (-N_META) % BLOCK
EPS = 1e-6
NEG_INF = -1e30
LOG2_E = 1.4426950408889634

N_HEADS = 4
MLA_NOPE = 64
MLA_ROPE = 32
MLA_Q_RANK = 256
MLA_KV_RANK = 128
MLA_PAD_LANE = 192
ROPE_THETA = 10000.0

RW_W = 256
RW_GN_EPS = 64e-5
RW_CHUNK = 64

SB_W = 256
SB_HEAD = 64
SB_SUFFIX_SPLIT = 256

GLA_DK = 32
GLA_QK = 128
GLA_W = 256
GLA_TAU = 16.0
GLA_CHUNK = 128
GLA_SUB = 16

D_FF = 2816
FFN_COL_CHUNKS = (1536, 1280)

W_MLA, W_RW, W_SB, W_GLA, W_GATE = 640, 1024, 768, 896, 4096
IN_GROUP_WIDTHS = (W_MLA, W_RW, W_SB, W_GLA)
IN_ROW_RW, IN_ROW_SB, IN_ROW_MLA, IN_ROW_GLA = 0, W_RW, W_RW + W_SB, W_RW + W_SB + W_MLA

ROW_TILE = 384
ATTN_TILE = 384
FINAL_ROWS = 1024
VMEM_LIMIT = 56 * 1024 * 1024


def _bf(x):
    return x.astype(BF16)


def _dot(a, b):
    return jnp.dot(_bf(a), _bf(b), preferred_element_type=F32)


def _dot_nt(a, b):
    return lax.dot_general(_bf(a), _bf(b), (((1,), (1,)), ((), ())), preferred_element_type=F32)


def _dot_tn(a, b):
    return lax.dot_general(_bf(a), _bf(b), (((0,), (0,)), ((), ())), preferred_element_type=F32)


def _split_hi_lo(x):
    hi = _bf(x)
    lo = _bf(x - hi.astype(F32))
    return hi, lo


def _dot_exact_rhs(x, m):
    hi, lo = _split_hi_lo(x)
    return jnp.dot(hi, m, preferred_element_type=F32) + jnp.dot(lo, m, preferred_element_type=F32)


def _dot_exact_lhs(m, x):
    hi, lo = _split_hi_lo(x)
    return jnp.dot(m, hi, preferred_element_type=F32) + jnp.dot(m, lo, preferred_element_type=F32)


def _iota(shape, dim):
    return lax.broadcasted_iota(jnp.int32, shape, dim)


def _div_pow2(x, d):
    assert d & (d - 1) == 0
    return lax.shift_right_logical(x, d.bit_length() - 1)


def _same_segment(shape, row_seg, col_seg):
    return _div_pow2(_iota(shape, 0), row_seg) == _div_pow2(_iota(shape, 1), col_seg)


def _segment_matrix(n, seg, value):
    return jnp.where(_same_segment((n, n), seg, seg), value, 0.0).astype(BF16)


def _head_lane_mask(width, head_width, h):
    lane = _iota((1, width), 1)
    return (lane >= h * head_width) & (lane < (h + 1) * head_width)


def _stack_heads(x, head_width):
    w = x.shape[1]
    return jnp.concatenate(
        [jnp.where(_head_lane_mask(w, head_width, h), x, 0.0) for h in range(N_HEADS)], axis=0)


def _rms(x, eps):
    return x * lax.rsqrt(jnp.mean(x * x, axis=-1, keepdims=True) + eps)


def _const_spec(shape):
    nd = len(shape)
    return pl.BlockSpec(shape, lambda *_: (0,) * nd)


def _layer_spec(stacked, layer):
    nd = stacked.ndim - 1
    return pl.BlockSpec((None,) + stacked.shape[1:], lambda *_: (layer,) + (0,) * nd)


def _params(sem, vmem=VMEM_LIMIT):
    return pltpu.CompilerParams(dimension_semantics=sem, vmem_limit_bytes=vmem)


def _embed_kernel(x_ref, meta_ref, o_ref):
    first_real = PAD + N_META

    @pl.when(pl.program_id(1) == 0)
    def _():
        o_ref[0:PAD, :] = jnp.zeros((PAD, D_MODEL), o_ref.dtype)
        o_ref[PAD:first_real, :] = meta_ref[...]
        o_ref[first_real:, :] = x_ref[0:o_ref.shape[0] - first_real, :]

    @pl.when(pl.program_id(1) > 0)
    def _():
        o_ref[...] = x_ref[...]


def _embed(x2, meta, batch):
    seq = x2.shape[0] // batch
    first_real = PAD + N_META
    lp = first_real + seq
    tm = ROW_TILE
    window = lambda b, i: (pl.multiple_of(b * seq + jnp.maximum(i * tm - first_real, 0), BLOCK), 0)
    return pl.pallas_call(
        _embed_kernel,
        grid=(batch, lp // tm),
        in_specs=[pl.BlockSpec((pl.Element(tm), pl.Element(D_MODEL)), window), _const_spec(meta.shape)],
        out_specs=pl.BlockSpec((tm, D_MODEL), lambda b, i: (b * (lp // tm) + i, 0)),
        out_shape=jax.ShapeDtypeStruct((batch * lp, D_MODEL), x2.dtype),
        compiler_params=_params(("parallel", "parallel")),
        name="embed",
    )(x2, meta)


def _mla_operands(z, qg, kvg, wq, wkv, cos, sin, seq_tile):
    nq = _rms(z[:, 0:256], EPS) * qg
    nkv = _rms(z[:, 256:384], EPS) * kvg
    ql = _dot(nq, wq)
    scale = (MLA_NOPE + MLA_ROPE) ** -0.5
    q = jnp.concatenate([ql[:, 0:128], ql[:, 128:256] * cos + ql[:, 512:640] * sin,
                         ql[:, 256:384], ql[:, 384:512] * cos + ql[:, 640:768] * sin], axis=1)
    pad_lane = (_iota((1, 512), 1) & 255) == MLA_PAD_LANE
    q = _bf(jnp.where(pad_lane, 1.0, q * scale))
    kvl = _dot(nkv, wkv)
    k_rope = z[:, 384:512] * cos + z[:, 512:640] * sin
    lane = _iota((1, 128), 1)
    is_pad_key = (seq_tile * z.shape[0] + _iota((z.shape[0], 1), 0)) < PAD
    k_rope = jnp.where(lane < 2 * MLA_ROPE, k_rope,
                       jnp.where((lane == MLA_PAD_LANE - 128) & is_pad_key, NEG_INF, 0.0))
    k = _bf(jnp.concatenate([kvl[:, 0:128], k_rope, kvl[:, 128:256], k_rope], axis=1))
    vt = _bf(kvl[:, 256:512].T)
    row_head = _div_pow2(_iota((256, 1), 0), 64)
    vt_heads = jnp.concatenate([jnp.where(row_head == h, vt, jnp.zeros((), BF16)) for h in range(N_HEADS)], axis=1)
    return q, k, vt_heads


def _in_proj_kernel(h_ref, g_ref, w_ref, qg_ref, kvg_ref, wq_ref, wkv_ref, cos_ref, sin_ref, *rest, tiles_per_seq):
    rw_consts, (sb_ref, gla_ref, sb_values_ref, q_ref, k_ref, vt_ref), rw_outs, rw_tail_ref = (
        rest[:9], rest[9:15], rest[15:26], rest[26])
    seq_tile = pl.program_id(0) % tiles_per_seq
    n = _bf(_rms(h_ref[...], EPS) * g_ref[...])
    z_rw = _dot_nt(n, w_ref[IN_ROW_RW:IN_ROW_SB, :])
    prev = jnp.where(seq_tile == 0, 0.0, rw_tail_ref[7:8, :])
    rw_tail_ref[...] = z_rw[z_rw.shape[0] - 8:, :]

    def stick_breaking_part():
        sb_ref[...] = _bf(_dot_nt(n, w_ref[IN_ROW_SB:IN_ROW_MLA, :]))
        sb_values_ref[0] = _head_stacked_values(sb_ref[:, 2 * SB_W:3 * SB_W])

    def mla_gla_part():
        both = _dot_nt(n, w_ref[IN_ROW_MLA:IN_ROW_GLA + W_GLA, :])
        gla_ref[...] = both[:, W_MLA:]
        q, k, vt_heads = _mla_operands(both[:, :W_MLA], qg_ref[...], kvg_ref[...], wq_ref[...], wkv_ref[...],
                                       cos_ref[...], sin_ref[...], seq_tile)
        q_ref[...] = q
        k_ref[...] = k
        vt_ref[0] = vt_heads

    _rw_chunk_terms(z_rw, prev, *rw_consts, *rw_outs, interleave=(stick_breaking_part, mla_gla_part))


def _in_proj(h, g, w_all, qg, kvg, wq, wkv, layer, cos, sin, rw, lp):
    t = h.shape[0]
    tm = ROW_TILE
    nb = lp // tm
    rows = lambda w: pl.BlockSpec((tm, w), lambda i: (i, 0))
    table = pl.BlockSpec((tm, 128), lambda i: (i % nb, 0))
    rw_consts = [rw["rw_mu"], rw["rw_w0"], rw["rw_w2"], rw["rw_a0"], rw["rw_a2"], rw["rw_g2"],
                 rw["rw_k_k"], rw["rw_k_a"], rw["rw_r_k"]]
    shape = lambda w, dt: jax.ShapeDtypeStruct((t, w), dt)
    return pl.pallas_call(
        functools.partial(_in_proj_kernel, tiles_per_seq=nb),
        grid=(t // tm,),
        in_specs=[rows(D_MODEL), _const_spec((1, D_MODEL)), _layer_spec(w_all, layer),
                  _const_spec((1, MLA_Q_RANK)), _const_spec((1, MLA_KV_RANK)),
                  _layer_spec(wq, layer), _layer_spec(wkv, layer), table, table]
        + [_const_spec(c.shape) for c in rw_consts],
        out_specs=[rows(W_SB), rows(W_GLA), pl.BlockSpec((1, N_HEADS * tm, SB_W), lambda i: (i, 0, 0)),
                   rows(512), rows(512), pl.BlockSpec((1, 256, N_HEADS * tm), lambda i: (i, 0, 0))]
        + [rows(RW_W)] * 8 + [pl.BlockSpec((tm // RW_CHUNK * 8, RW_W), lambda i: (i, 0))] + [rows(RW_W)] * 2,
        out_shape=[shape(W_SB, BF16), shape(W_GLA, F32),
                   jax.ShapeDtypeStruct((t // tm, N_HEADS * tm, SB_W), BF16),
                   shape(512, BF16), shape(512, BF16),
                   jax.ShapeDtypeStruct((t // tm, 256, N_HEADS * tm), BF16)]
        + [shape(RW_W, BF16)] * 6 + [shape(RW_W, F32)] * 2
        + [jax.ShapeDtypeStruct((t // RW_CHUNK * 8, RW_W), F32)] + [shape(RW_W, F32)] * 2,
        scratch_shapes=[pltpu.VMEM((8, W_RW), F32)],
        compiler_params=_params(("arbitrary",)),
        name="in_proj",
    )(h, g, w_all, qg, kvg, wq, wkv, cos, sin, *rw_consts)


def _head_stacked_values(vb):
    zero = jnp.zeros((), vb.dtype)
    return jnp.concatenate([jnp.where(_head_lane_mask(256, 64, h), vb, zero) for h in range(N_HEADS)], axis=0)


def _per_head_rows(rows, n):
    return jnp.concatenate([jnp.broadcast_to(r, (64, n)) for r in rows], axis=0)


def _attn_kernel(sq_ref, sk_ref, sv_ref, later_ref, mq_ref, mk_ref, mvt_ref, ysb_ref, ymla_ref,
                 sqh_ref, c_ref, sacc_ref, wts_ref, mqh_ref, m_ref, l_ref, macc_ref, p_ref, alpha_ref, *, tile):
    i = pl.program_id(1)
    heads = range(N_HEADS)
    q_row = i * tile + _iota((tile, 1), 0)
    q_pos = i * tile + _iota((1, tile), 1)
    for h in heads:
        sqh_ref[h] = jnp.where(_head_lane_mask(SB_W, SB_HEAD, h), sq_ref[...], jnp.zeros((), BF16))
    lane = _iota((1, 256), 1)
    for h in heads:
        half, slot = h // 2, h % 2
        head_lanes = (((lane >= slot * MLA_NOPE) & (lane < (slot + 1) * MLA_NOPE))
                      | ((lane >= 128 + slot * MLA_ROPE) & (lane < 128 + (slot + 1) * MLA_ROPE))
                      | (lane == MLA_PAD_LANE))
        mqh_ref[half, slot * tile:(slot + 1) * tile, :] = jnp.where(
            head_lanes, mq_ref[:, half * 256:(half + 1) * 256], jnp.zeros((), BF16))
    c_ref[...] = jnp.zeros(c_ref.shape, F32)
    sacc_ref[...] = jnp.zeros(sacc_ref.shape, F32)
    m_ref[...] = jnp.full(m_ref.shape, NEG_INF, F32)
    l_ref[...] = jnp.zeros(l_ref.shape, F32)
    macc_ref[...] = jnp.zeros(macc_ref.shape, F32)

    def add_values(j, slot):
        sacc_ref[...] += jnp.dot(wts_ref[slot], sv_ref[j], preferred_element_type=F32)
        pv = jnp.dot(mvt_ref[j], p_ref[slot], preferred_element_type=F32)
        macc_ref[...] = _per_head_rows([alpha_ref[slot, h] for h in heads], tile) * macc_ref[...] + pv

    def block(j, masked, prev, slot):
        start = pl.multiple_of(j * tile, tile)
        cut = SB_SUFFIX_SPLIT
        z = [_dot_nt(sqh_ref[h], sk_ref[pl.ds(start, tile), :]) for h in heads]
        s_pair = [_dot_nt(mk_ref[pl.ds(start, tile), half * 256:(half + 1) * 256], mqh_ref[half]) for half in (0, 1)]
        if prev is not None:
            add_values(prev, 1 - slot)

        for h in heads:
            s_h = s_pair[h // 2][:, (h % 2) * tile:(h % 2 + 1) * tile]
            if masked:
                s_h = jnp.where((start + _iota((tile, 1), 0)) <= q_pos, s_h, NEG_INF)
            m_old = m_ref[h]
            m_new = jnp.maximum(m_old, jnp.max(s_h, axis=0, keepdims=True))
            p = jnp.exp(s_h - m_new)
            alpha = jnp.exp(m_old - m_new)
            l_ref[h] = alpha * l_ref[h] + jnp.sum(p, axis=0, keepdims=True)
            m_ref[h] = m_new
            p_ref[slot, h * tile:(h + 1) * tile, :] = _bf(p)
            alpha_ref[slot, h] = alpha

        log_take = [jnp.minimum(z[h], 0.0) - jnp.log2(1.0 + jnp.exp2(-jnp.abs(z[h]))) for h in heads]
        log_keep = [log_take[h] - z[h] for h in heads]
        if masked:
            mask = (start + _iota((1, tile), 1)) < q_row
            log_keep = [jnp.where(mask, log_keep[h], 0.0) for h in heads]
        keep16 = [_bf(log_keep[h]) for h in heads]
        later_head = [jnp.dot(keep16[h][:, :cut], later_ref[...], preferred_element_type=F32) for h in heads]
        later_tail = [jnp.dot(keep16[h][:, cut:], later_ref[:tile - cut, :tile - cut], preferred_element_type=F32)
                      for h in heads]

        for h in heads:
            head_sum = jnp.sum(log_keep[h][:, :cut], axis=-1, keepdims=True)
            tail_sum = jnp.sum(log_keep[h][:, cut:], axis=-1, keepdims=True)
            c = c_ref[h]
            later = jnp.concatenate([later_head[h] + tail_sum, later_tail[h]], axis=1)
            w = jnp.exp2(log_take[h] + later + c)
            if masked:
                w = jnp.where(mask, w, 0.0)
            c_ref[h] = c + (head_sum + tail_sum)
            wts_ref[slot, :, h * tile:(h + 1) * tile] = _bf(w)

    block(i, True, None, 0)

    def below_pair(t, carry):
        j = i - 1 - 2 * t
        block(j, False, j + 1, 1)
        block(j - 1, False, j, 0)
        return carry

    lax.fori_loop(0, lax.shift_right_logical(i, 1), below_pair, 0)
    odd = (i & 1) == 1

    @pl.when(odd)
    def _():
        block(0, False, 1, 1)
        add_values(0, 1)

    @pl.when(jnp.logical_not(odd))
    def _():
        add_values(0, 0)
    ysb_ref[...] = _bf(sacc_ref[...])
    ymla_ref[...] = _bf((macc_ref[...] / _per_head_rows([l_ref[h] for h in heads], tile)).T)


def _attn(z_sb, sb_values, later_mat, q, k, vt, lp):
    t = z_sb.shape[0]
    tile = ATTN_TILE
    nb = lp // tile
    tiles = lambda w: pl.BlockSpec((tile, w), lambda b, i: (b * nb + i, 0))
    resident = lambda shape, index: pl.BlockSpec(shape, index, pipeline_mode=pl.Buffered(1))
    return pl.pallas_call(
        functools.partial(_attn_kernel, tile=tile),
        grid=(t // lp, nb),
        in_specs=[tiles(SB_W),
                  resident((lp, SB_W), lambda b, i: (b, 1)),
                  resident((nb, N_HEADS * tile, SB_W), lambda b, i: (b, 0, 0)),
                  _const_spec((SB_SUFFIX_SPLIT, SB_SUFFIX_SPLIT)),
                  tiles(512),
                  resident((lp, 512), lambda b, i: (b, 0)),
                  resident((nb, 256, N_HEADS * tile), lambda b, i: (b, 0, 0))],
        out_specs=[tiles(SB_W), tiles(256)],
        out_shape=[jax.ShapeDtypeStruct((t, SB_W), BF16), jax.ShapeDtypeStruct((t, 256), BF16)],
        scratch_shapes=[pltpu.VMEM((N_HEADS, tile, SB_W), BF16), pltpu.VMEM((N_HEADS, tile, 1), F32),
                        pltpu.VMEM((tile, SB_W), F32), pltpu.VMEM((2, tile, N_HEADS * tile), BF16),
                        pltpu.VMEM((2, 2 * tile, 256), BF16), pltpu.VMEM((N_HEADS, 1, tile), F32),
                        pltpu.VMEM((N_HEADS, 1, tile), F32), pltpu.VMEM((256, tile), F32),
                        pltpu.VMEM((2, N_HEADS * tile, tile), BF16), pltpu.VMEM((2, N_HEADS, 1, tile), F32)],
        compiler_params=_params(("parallel", "arbitrary")),
        name="attn",
    )(z_sb, z_sb, sb_values, later_mat, q, k, vt)


def _compact_eye(c):
    return jnp.where(_iota((c, N_HEADS * c), 0) == (_iota((c, N_HEADS * c), 1) & (c - 1)), 1.0, 0.0)


def _rw_chunk_terms(z, prev, mu_ref, w0_ref, w2_ref, a0_ref, a2_ref, g2_ref, kk_ref, ka_ref, rk_ref,
                    w_o, rt_o, arb_o, kbar_o, bbar_o, v_o, uv_o, yv_o, dec_o, g_o, bonus_o, interleave):
    tm = z.shape[0]
    shifted = jnp.where(_iota((tm, 1), 0) == 0, prev, pltpu.roll(z, 1, 0))
    zz = z + (shifted - z) * mu_ref[...]
    r, k, v = zz[:, 0:256], zz[:, 256:512], zz[:, 512:768]
    lora_in = zz[:, 768:896]
    u = w0_ref[...] + _dot(jnp.tanh(lora_in), w2_ref[...])
    w = jnp.minimum(u, 0.0) - jnp.log1p(jnp.exp(-jnp.abs(u))) - 0.5
    a = jax.nn.sigmoid(a0_ref[...] + _dot(lora_in, a2_ref[...]))
    seg = _segment_matrix(RW_W, 64, 1.0)
    kx = k * kk_ref[...]
    kap = kx / jnp.maximum(jnp.sqrt(_dot_exact_rhs(kx * kx, seg)), 1e-12)
    kmod = k * (1.0 + (a - 1.0) * ka_ref[...])
    beta = kap * a
    lw = -jnp.exp(w)
    g_o[...] = _dot(jax.nn.sigmoid(zz[:, 896:1024]), g2_ref[...])
    bonus_o[...] = _dot_exact_rhs(r * kmod * rk_ref[...], seg) * v

    c = RW_CHUNK
    same_chunk = _same_segment((tm, tm), c, c)
    lw_hi, lw_lo = _split_hi_lo(lw)
    cum = jnp.where(same_chunk & (_iota((tm, tm), 0) >= _iota((tm, tm), 1)), 1.0, 0.0).astype(BF16)
    b = jnp.dot(cum, lw_hi, preferred_element_type=F32) + jnp.dot(cum, lw_lo, preferred_element_type=F32)
    b_end = jnp.concatenate(
        [jnp.broadcast_to(b[n * c + c - 1:(n + 1) * c, :], (c, RW_W)) for n in range(tm // c)], axis=0)
    grow = jnp.exp(-b)
    to_end = jnp.exp(b_end - b)
    kap_t = kap * jnp.exp(b - lw)
    r_t = r * jnp.exp(b)
    beta_g = beta * grow
    k_g = kmod * grow
    rt_o[...] = _bf(r_t)
    kbar_o[...] = _bf(kmod * to_end)
    bbar_o[...] = _bf(beta * to_end)
    v_o[...] = _bf(v)
    decay_end = jnp.exp(b_end)

    lane_pos = _iota((c, N_HEADS * c), 1) & (c - 1)
    strictly_earlier = _iota((c, N_HEADS * c), 0) > lane_pos
    not_later = _iota((c, N_HEADS * c), 0) >= lane_pos
    chunks = [slice(n * c, (n + 1) * c) for n in range(tm // c)]
    pair = [_dot_nt(jnp.concatenate([kap_t[rows], r_t[rows]], axis=0),
                    jnp.concatenate([_stack_heads(beta_g[rows], 64), _stack_heads(k_g[rows], 64)], axis=0))
            for rows in chunks]
    a_kb = [jnp.where(strictly_earlier, p[0:c, 0:4 * c], 0.0) for p in pair]
    a_kk = [jnp.where(strictly_earlier, p[0:c, 4 * c:8 * c], 0.0) for p in pair]
    a_rk = [jnp.where(not_later, p[c:2 * c, 4 * c:8 * c], 0.0) for p in pair]
    for rows, p in zip(chunks, pair):
        arb_o[rows, :] = _bf(jnp.where(not_later, p[c:2 * c, 0:4 * c], 0.0))
    interleave[0]()
    inv = [_compact_eye(c) - a for a in a_kb]
    power = [_dot(a, _stack_heads(a, c)) for a in a_kb]
    span = 2
    while span < c:
        inv = [t + _dot(t, _stack_heads(p, c)) for t, p in zip(inv, power)]
        span *= 2
        if span == 8:
            interleave[1]()
        if span < c:
            power = [_dot(p, _stack_heads(p, c)) for p in power]
    v_st = [_stack_heads(v[rows], 64) for rows in chunks]
    kk_v = [_dot(a, vs) for a, vs in zip(a_kk, v_st)]
    for n, rows in enumerate(chunks):
        w_o[rows, :] = _bf(_dot(inv[n], _stack_heads(kap_t[rows], 64)))
        uv_o[rows, :] = _dot(inv[n], _stack_heads(kk_v[n], 64))
        yv_o[rows, :] = _dot(a_rk[n], v_st[n])
        dec_o[n * 8:(n + 1) * 8, :] = decay_end[n * c:n * c + 8]


def _recurrent_kernel(w_ref, rt_ref, arb_ref, kbar_ref, bbar_ref, rv_ref, uv_ref, yv_ref, dec_ref,
                      q_ref, k_ref, v_ref, al_ref, a2_ref, ab_ref, y_ref, o_ref, ht_ref, st_ref):
    c = GLA_CHUNK
    rc = RW_CHUNK
    seqs = range(st_ref.shape[0])

    @pl.when(pl.program_id(0) == 0)
    def _():
        ht_ref[...] = jnp.zeros_like(ht_ref)
        st_ref[...] = jnp.zeros_like(st_ref)

    rw_same_head = _same_segment((RW_W, RW_W), 64, 64)

    def rw_from_state(n, ht):
        rows = slice(n * rc, (n + 1) * rc)
        return [_dot_nt(jnp.concatenate([w_ref[s, rows], rt_ref[s, rows]], axis=0), ht[s]) for s in seqs]

    def rw_finish(n, from_state, ht):
        rows = slice(n * rc, (n + 1) * rc)
        u = [from_state[s][0:rc] + uv_ref[s, rows] for s in seqs]
        from_u = [_dot(arb_ref[s, rows], _stack_heads(u[s], 64)) for s in seqs]
        upd = [_dot_tn(jnp.concatenate([rv_ref[s, rows], _bf(u[s])], axis=0),
                       jnp.concatenate([kbar_ref[s, rows], -bbar_ref[s, rows]], axis=0)) for s in seqs]
        for s in seqs:
            y_ref[s, rows] = from_state[s][rc:2 * rc] + yv_ref[s, rows] - from_u[s]
        return [ht[s] * dec_ref[s, n * 8:n * 8 + 1, :] + jnp.where(rw_same_head, upd[s], 0.0) for s in seqs]

    sub = GLA_SUB
    nsub = c // sub
    row_i = _iota((c, c), 0)
    col_i = _iota((c, c), 1)
    tri = jnp.where(row_i >= col_i, 1.0, 0.0).astype(BF16)
    tri_sub = jnp.where((row_i >= col_i) & _same_segment((c, c), sub, sub), 1.0, 0.0).astype(BF16)
    key_pos = _iota((c, 1), 0)
    query_in_sub = _iota((N_HEADS * sub, c), 0) & (sub - 1)
    gla_same_head = _same_segment((GLA_W, GLA_QK), 64, GLA_DK)

    ht = [ht_ref[s] for s in seqs]
    rw_state_0 = rw_from_state(0, ht)

    x = [_dot(al_ref[s], a2_ref[...]) + ab_ref[...] for s in seqs]
    log_a = [(jnp.minimum(x[s], 0.0) - jnp.log1p(jnp.exp(-jnp.abs(x[s])))) * (1.0 / GLA_TAU) for s in seqs]
    b = [_dot_exact_lhs(tri, log_a[s]) for s in seqs]
    b_sub = [_dot_exact_lhs(tri_sub, log_a[s]) for s in seqs]
    q = [q_ref[s] * (GLA_DK ** -0.5) for s in seqs]
    st = [st_ref[s] for s in seqs]
    inter = [_dot_nt(q[s] * jnp.exp(b[s]), st[s]) for s in seqs]

    ht = rw_finish(0, rw_state_0, ht)
    rw_state_1 = rw_from_state(1, ht)

    scores = []
    for s in seqs:
        q_sub = q[s] * jnp.exp(b_sub[s])
        beta = b[s] - b_sub[s]
        k = k_ref[s]
        blocks = []
        for blk in range(nsub):
            lo, hi = blk * sub, (blk + 1) * sub
            expo = jnp.where(key_pos < hi, beta[lo:lo + 1, :] - b[s], NEG_INF)
            sc = _dot_nt(_stack_heads(q_sub[lo:hi], GLA_DK), k * jnp.exp(expo))
            blocks.append(jnp.where(_iota((N_HEADS * sub, c), 1) <= lo + query_in_sub, sc, 0.0))
        scores.append(jnp.concatenate(blocks, axis=0))

    ht = rw_finish(1, rw_state_1, ht)
    for s in seqs:
        ht_ref[s] = ht[s]

    per_head = [_dot(scores[s], v_ref[s]) for s in seqs]
    upd = [_dot_tn(v_ref[s], k_ref[s] * jnp.exp(b[s][c - 1:c, :] - b[s])) for s in seqs]
    for s in seqs:
        pieces = []
        for blk in range(nsub):
            piece = jnp.zeros((sub, GLA_W), F32)
            for h in range(N_HEADS):
                r0 = (blk * N_HEADS + h) * sub
                piece = piece + jnp.where(_head_lane_mask(GLA_W, 64, h), per_head[s][r0:r0 + sub], 0.0)
            pieces.append(piece)
        o_ref[s] = inter[s] + jnp.concatenate(pieces, axis=0)
        st_ref[s] = st[s] * jnp.exp(b[s][c - 1:c, :]) + jnp.where(gla_same_head, upd[s], 0.0)


def _recurrent(rw_terms, z_gla, a2, ab, batch):
    t = z_gla.shape[0]
    lp = t // batch
    c = GLA_CHUNK
    assert c == 2 * RW_CHUNK
    as_seq = lambda a: a.reshape(batch, a.shape[0] // batch, a.shape[1])
    rows = pl.BlockSpec((batch, c, RW_W), lambda i: (0, i, 0))
    z3 = as_seq(z_gla)
    y, o = pl.pallas_call(
        _recurrent_kernel,
        grid=(lp // c,),
        in_specs=[rows] * 8 + [pl.BlockSpec((batch, 16, RW_W), lambda i: (0, i, 0)),
                               pl.BlockSpec((batch, c, 128), lambda i: (0, i, 0)),
                               pl.BlockSpec((batch, c, 128), lambda i: (0, i, 1)),
                               pl.BlockSpec((batch, c, 256), lambda i: (0, i, 1)),
                               pl.BlockSpec((batch, c, 128), lambda i: (0, i, 6)),
                               _const_spec(a2.shape), _const_spec(ab.shape)],
        out_specs=[rows, pl.BlockSpec((batch, c, GLA_W), lambda i: (0, i, 0))],
        out_shape=[jax.ShapeDtypeStruct((batch, lp, RW_W), F32), jax.ShapeDtypeStruct((batch, lp, GLA_W), F32)],
        scratch_shapes=[pltpu.VMEM((batch, RW_W, RW_W), F32), pltpu.VMEM((batch, GLA_W, GLA_QK), F32)],
        compiler_params=_params(("arbitrary",)),
        name="recurrent",
    )(*map(as_seq, rw_terms), z3, z3, z3, z3, a2, ab)
    return y.reshape(t, RW_W), o.reshape(t, GLA_W)


def _merge_kernel(h_ref, ymla_ref, yrw_ref, bonus_ref, g_ref, ysb_ref, ogla_ref, rgla_ref,
                  nmix_ref, lnw_ref, lnb_ref, gn_ref, gb_ref, wg_ref, wb_ref, wo_ref, o_ref, *, tiles_per_seq):
    tm = h_ref.shape[0]
    avg = _segment_matrix(256, 64, 1.0 / 64)
    h = h_ref[...]
    n = _bf(_rms(h, EPS) * nmix_ref[...])

    y = yrw_ref[...]
    d = y - _dot_exact_rhs(y, avg)
    var = _dot_exact_rhs(d * d, avg)
    y_rw = (d * lax.rsqrt(var + RW_GN_EPS) * lnw_ref[...] + lnb_ref[...] + bonus_ref[...]) * g_ref[...]

    o = ogla_ref[...]
    r = rgla_ref[...]
    y_gla = o * lax.rsqrt(_dot_exact_rhs(o * o, avg) + EPS) * gn_ref[...] * (r * jax.nn.sigmoid(r))

    acc = jnp.zeros((tm, D_MODEL), F32)
    for m, y_m in enumerate((ymla_ref[...], y_rw, ysb_ref[...], y_gla)):
        logits = _dot_nt(n, wg_ref[m * D_MODEL:(m + 1) * D_MODEL, :])
        gate = jax.nn.sigmoid(logits + gb_ref[m:m + 1, :])
        acc = acc + gate * _dot(y_m, wb_ref[m])
    delta = _dot(acc, wo_ref[...])
    row = (pl.program_id(0) % tiles_per_seq) * tm + _iota((tm, 1), 0)
    o_ref[...] = h + jnp.where(row >= PAD, delta, 0.0)


def _merge(h, y_mla, y_rw, bonus, g, y_sb, o_gla, z_gla, p, lp):
    t = h.shape[0]
    tm = ROW_TILE
    row = lambda i: (i, 0)
    w256 = pl.BlockSpec((tm, 256), row)
    consts = [p["norm_mix"], p["rw_ln_w"], p["rw_ln_b"], p["gla_norm"], p["gate_b"]]
    stacked = [p["w_gate"], p["w_branch"], p["w_out"]]
    return pl.pallas_call(
        functools.partial(_merge_kernel, tiles_per_seq=lp // tm),
        grid=(t // tm,),
        in_specs=[pl.BlockSpec((tm, D_MODEL), row), w256, w256, w256, w256, w256, w256,
                  pl.BlockSpec((tm, 256), lambda i: (i, 2))]
        + [_const_spec(c.shape) for c in consts]
        + [_layer_spec(w, p["layer"]) for w in stacked],
        out_specs=pl.BlockSpec((tm, D_MODEL), row),
        out_shape=jax.ShapeDtypeStruct((t, D_MODEL), F32),
        compiler_params=_params(("parallel",)),
        name="merge",
    )(h, y_mla, y_rw, bonus, g, y_sb, o_gla, z_gla, *consts, *stacked)


def _ffn_kernel(h_ref, g_ref, win_ref, cw_ref, cb_ref, wout_ref, o_ref, tail_ref):
    @pl.when(pl.program_id(1) == 0)
    def _():
        tail_ref[...] = jnp.zeros_like(tail_ref)

    x = h_ref[...]
    tm = x.shape[0]
    n = _bf(_rms(x, EPS) * g_ref[...])
    rowi = _iota((tm, 1), 0)
    acc = jnp.zeros((tm, D_MODEL), F32)
    assert sum(FFN_COL_CHUNKS) == D_FF
    for c0, c1 in zip((0, FFN_COL_CHUNKS[0]), (FFN_COL_CHUNKS[0], D_FF)):
        a = jnp.dot(n, win_ref[:, c0:c1], preferred_element_type=F32)
        u = jnp.dot(n, win_ref[:, D_FF + c0:D_FF + c1], preferred_element_type=F32)
        prev1 = tail_ref[7:8, c0:c1]
        prev2 = tail_ref[6:7, c0:c1]
        a1 = jnp.where(rowi == 0, prev1, pltpu.roll(a, 1, 0))
        a2 = jnp.where(rowi == 0, prev2, jnp.where(rowi == 1, prev1, pltpu.roll(a, 2, 0)))
        tail_ref[:, c0:c1] = a[tm - 8:tm, :]
        conv = cb_ref[:, c0:c1] + cw_ref[0:1, c0:c1] * a2 + cw_ref[1:2, c0:c1] * a1 + cw_ref[2:3, c0:c1] * a
        acc = acc + _dot(conv * jax.nn.sigmoid(conv) * u, wout_ref[c0:c1, :])
    o_ref[...] = x + acc


def _ffn(h, g, w_in, conv_w, conv_b, w_out, layer, lp):
    t = h.shape[0]
    tm = ROW_TILE
    nb = lp // tm
    row = lambda b, i: (b * nb + i, 0)
    return pl.pallas_call(
        _ffn_kernel,
        grid=(t // lp, nb),
        in_specs=[pl.BlockSpec((tm, D_MODEL), row), _const_spec((1, D_MODEL)), _layer_spec(w_in, layer),
                  _const_spec(conv_w.shape), _const_spec(conv_b.shape), _layer_spec(w_out, layer)],
        out_specs=pl.BlockSpec((tm, D_MODEL), row),
        out_shape=jax.ShapeDtypeStruct((t, D_MODEL), F32),
        scratch_shapes=[pltpu.VMEM((8, D_FF), F32)],
        compiler_params=_params(("arbitrary", "arbitrary")),
        name="conv_ffn",
    )(h, g, w_in, conv_w, conv_b, w_out)


def _final_norm_kernel(h_ref, g_ref, o_ref):
    o_ref[0] = _rms(h_ref[...], EPS) * g_ref[...]


def _final_norm(h, g, batch, seq):
    lp = h.shape[0] // batch
    rows = FINAL_ROWS
    assert seq % rows == 0
    first = lambda bi, i: (pl.multiple_of(bi * lp + (PAD + N_META) + i * rows, BLOCK), 0)
    return pl.pallas_call(
        _final_norm_kernel,
        grid=(batch, seq // rows),
        in_specs=[pl.BlockSpec((pl.Element(rows), pl.Element(D_MODEL)), first), _const_spec((1, D_MODEL))],
        out_specs=pl.BlockSpec((1, rows, D_MODEL), lambda bi, i: (bi, i, 0)),
        out_shape=jax.ShapeDtypeStruct((batch, seq, D_MODEL), F32),
        compiler_params=_params(("parallel", "parallel")),
        name="final_norm",
    )(h, g)


def _rope_swap(w):
    half = w.shape[-1] // 2
    return jnp.concatenate([-w[..., half:], w[..., :half]], axis=-1)


def _in_weight_layout_kernel(w_ref, o_ref, gate_ref):
    def put(dst, val):
        o_ref[0, dst:dst + val.shape[0], :] = _bf(val)

    mla, sb, gla = IN_ROW_MLA, IN_ROW_SB, IN_ROW_GLA
    put(mla, w_ref[0, 0:384, :])
    half = MLA_ROPE // 2
    kr = w_ref[0, 384:384 + MLA_ROPE, :]
    kr_swapped = jnp.concatenate([-kr[half:], kr[:half]], axis=0)
    put(mla + 384, jnp.concatenate([kr] * 4, axis=0))
    put(mla + 512, jnp.concatenate([kr_swapped] * 4, axis=0))
    put(IN_ROW_RW, w_ref[0, 416:1440, :])
    put(sb, w_ref[0, 1440:1440 + SB_W, :] * (SB_HEAD ** -0.5 * LOG2_E))
    put(sb + SB_W, w_ref[0, 1440 + SB_W:2208, :])
    put(gla, w_ref[0, 2208:2720, :])
    put(gla + 512, w_ref[0, 2736:2992, :])
    lora = w_ref[0, 2720:2736, :]
    put(gla + 768, jnp.concatenate([lora, jnp.zeros((W_GLA - 784, lora.shape[1]), lora.dtype)], axis=0))
    gate_ref[0] = _bf(w_ref[0, 2992:2992 + W_GATE, :])


def _in_weight_layout(w_in):
    wt = jnp.swapaxes(w_in, 1, 2)
    depth, n, d = wt.shape
    cols = 256
    total = sum(IN_GROUP_WIDTHS)
    return pl.pallas_call(
        _in_weight_layout_kernel,
        grid=(depth, d // cols),
        in_specs=[pl.BlockSpec((1, n, cols), lambda l, i: (l, 0, i))],
        out_specs=[pl.BlockSpec((1, total, cols), lambda l, i: (l, 0, i)),
                   pl.BlockSpec((1, W_GATE, cols), lambda l, i: (l, 0, i))],
        out_shape=[jax.ShapeDtypeStruct((depth, total, d), BF16), jax.ShapeDtypeStruct((depth, W_GATE, d), BF16)],
        compiler_params=_params(("parallel", "parallel")),
        name="in_weight_layout",
    )(wt)


def _layout_params(w_in, mla_w_uq, mla_w_ukv, rw_w2, rw_a2, gla_a2):
    w_all, w_gate = _in_weight_layout(w_in)

    depth = w_in.shape[0]
    wuq = mla_w_uq.reshape(depth, MLA_Q_RANK, N_HEADS, MLA_NOPE + MLA_ROPE)
    nope, rope = wuq[..., :MLA_NOPE], wuq[..., MLA_NOPE:]
    rope_sw = _rope_swap(rope)
    zeros64 = jnp.zeros((depth, MLA_Q_RANK, 64), w_in.dtype)
    pair = lambda x, a, b: jnp.concatenate([x[:, :, a], x[:, :, b]], axis=-1)
    rope_pair = lambda x, a, b: jnp.concatenate([x[:, :, a], x[:, :, b], zeros64], axis=-1)
    wq = _bf(jnp.concatenate([pair(nope, 0, 1), rope_pair(rope, 0, 1), pair(nope, 2, 3), rope_pair(rope, 2, 3),
                              rope_pair(rope_sw, 0, 1), rope_pair(rope_sw, 2, 3)], axis=-1))
    wukv = mla_w_ukv.reshape(depth, MLA_KV_RANK, N_HEADS, 128)
    wkv = _bf(jnp.concatenate([wukv[..., :64].reshape(depth, MLA_KV_RANK, 256),
                               wukv[..., 64:].reshape(depth, MLA_KV_RANK, 256)], axis=-1))

    z64 = jnp.zeros_like(rw_w2)
    w2 = _bf(jnp.concatenate([rw_w2, z64], axis=1))
    a2 = _bf(jnp.concatenate([z64, rw_a2], axis=1))
    gla_a2p = _bf(jnp.concatenate([gla_a2, jnp.zeros((depth, 128 - gla_a2.shape[1], GLA_QK), gla_a2.dtype)], axis=1))
    return w_all, w_gate, wq, wkv, w2, a2, gla_a2p


def _rope_tables(lp):
    half = MLA_ROPE // 2
    freqs = ROPE_THETA ** (-jnp.arange(half, dtype=F32) / half)
    pos = (jnp.arange(lp) - PAD).astype(F32)
    ang = pos[:, None] * freqs[None, :]
    return jnp.tile(jnp.cos(ang), (1, 128 // half)), jnp.tile(jnp.sin(ang), (1, 128 // half))


def kernel(x, meta_tokens, norm_mix, w_in, mla_q_norm, mla_w_uq, mla_kv_norm, mla_w_ukv, rw_mu, rw_w0, rw_w2, rw_a0, rw_a2, rw_g2, rw_k_k, rw_k_a, rw_r_k, rw_ln_w, rw_ln_b, gla_a2, gla_a_b, gla_norm, gate_b, w_branch, w_out, norm_ffn, w_ffn_in, ffn_conv_w, ffn_conv_b, w_ffn_out, norm_final):
    batch, seq, _ = x.shape
    depth = w_in.shape[0]
    lp = PAD + N_META + seq
    t = batch * lp
    assert lp % ROW_TILE == 0 and lp % ATTN_TILE == 0 and lp % GLA_CHUNK == 0 and lp % RW_CHUNK == 0

    w_all, w_gate, wq, wkv, rw_w2p, rw_a2p, gla_a2p = _layout_params(
        w_in, mla_w_uq, mla_w_ukv, rw_w2, rw_a2, gla_a2)
    w_branch_b, w_out_b, w_ffn_in_b, w_ffn_out_b, rw_g2_b = map(_bf, (w_branch, w_out, w_ffn_in, w_ffn_out, rw_g2))
    vec = lambda a: a.reshape(depth, 1, -1)
    cos, sin = _rope_tables(lp)
    idx = jnp.arange(SB_SUFFIX_SPLIT)
    later_mat = jnp.where(idx[:, None] > idx[None, :], 1.0, 0.0).astype(BF16)

    h = _embed(x.reshape(batch * seq, D_MODEL), meta_tokens.astype(x.dtype), batch)

    for i in range(depth):
        rw = {"rw_mu": vec(rw_mu)[i], "rw_w0": vec(rw_w0)[i], "rw_w2": rw_w2p[i], "rw_a0": vec(rw_a0)[i],
              "rw_a2": rw_a2p[i], "rw_g2": rw_g2_b[i], "rw_k_k": vec(rw_k_k)[i], "rw_k_a": vec(rw_k_a)[i],
              "rw_r_k": vec(rw_r_k)[i]}
        z_sb, z_gla, sb_values, q, k, v, *chunk_terms, g, bonus = _in_proj(
            h, vec(norm_mix)[i], w_all, vec(mla_q_norm)[i], vec(mla_kv_norm)[i], wq, wkv, i, cos, sin, rw, lp)
        y_sb, y_mla = _attn(z_sb, sb_values, later_mat, q, k, v, lp)
        y_rw, o_gla = _recurrent(chunk_terms, z_gla, gla_a2p[i], vec(gla_a_b)[i], batch)
        mp = {"rw_ln_w": vec(rw_ln_w)[i], "rw_ln_b": vec(rw_ln_b)[i], "gla_norm": vec(gla_norm)[i],
              "gate_b": gate_b[i], "norm_mix": vec(norm_mix)[i], "w_gate": w_gate, "w_branch": w_branch_b,
              "w_out": w_out_b, "layer": i}
        h = _merge(h, y_mla, y_rw, bonus, g, y_sb, o_gla, z_gla, mp, lp)
        h = _ffn(h, vec(norm_ffn)[i], w_ffn_in_b, ffn_conv_w[i], vec(ffn_conv_b)[i], w_ffn_out_b, i, lp)
    return _final_norm(h, norm_final.reshape(1, D_MODEL), batch, seq)
```

```python
import functools

import jax
import jax.numpy as jnp
from jax import lax
from jax.experimental import pallas as pl
from jax.experimental.pallas import tpu as pltpu

F32 = jnp.float32
BF16 = jnp.bfloat16

D_MODEL = 1024
DEPTH = 4
N_META = 16
BLOCK = 128
PAD = (-N_META) % BLOCK
EPS = 1e-6
NEG_INF = -1e30
LOG2_E = 1.4426950408889634

N_HEADS = 4
MLA_NOPE = 64
MLA_ROPE = 32
MLA_Q_RANK = 256
MLA_KV_RANK = 128
MLA_PAD_LANE = 192
ROPE_THETA = 10000.0

RW_W = 256
RW_GN_EPS = 64e-5
RW_CHUNK = 64

SB_W = 256
SB_HEAD = 64
SB_SUFFIX_SPLIT = 256

GLA_DK = 32
GLA_QK = 128
GLA_W = 256
GLA_TAU = 16.0
GLA_CHUNK = 128
GLA_SUB = 16

D_FF = 2816
FFN_COL_CHUNKS = (1536, 1280)

W_MLA, W_RW, W_SB, W_GLA, W_GATE = 640, 1024, 768, 896, 4096
IN_GROUP_WIDTHS = (W_MLA, W_RW, W_SB, W_GLA)
IN_ROW_RW, IN_ROW_SB, IN_ROW_MLA, IN_ROW_GLA = 0, W_RW, W_RW + W_SB, W_RW + W_SB + W_MLA

ROW_TILE = 384
ATTN_TILE = 384
FINAL_ROWS = 1024
VMEM_LIMIT = 56 * 1024 * 1024


def _bf(x):
    return x.astype(BF16)


def _dot(a, b):
    return jnp.dot(_bf(a), _bf(b), preferred_element_type=F32)


def _dot_nt(a, b):
    return lax.dot_general(_bf(a), _bf(b), (((1,), (1,)), ((), ())), preferred_element_type=F32)


def _dot_tn(a, b):
    return lax.dot_general(_bf(a), _bf(b), (((0,), (0,)), ((), ())), preferred_element_type=F32)


def _split_hi_lo(x):
    hi = _bf(x)
    lo = _bf(x - hi.astype(F32))
    return hi, lo


def _dot_exact_rhs(x, m):
    hi, lo = _split_hi_lo(x)
    return jnp.dot(hi, m, preferred_element_type=F32) + jnp.dot(lo, m, preferred_element_type=F32)


def _dot_exact_lhs(m, x):
    hi, lo = _split_hi_lo(x)
    return jnp.dot(m, hi, preferred_element_type=F32) + jnp.dot(m, lo, preferred_element_type=F32)


def _iota(shape, dim):
    return lax.broadcasted_iota(jnp.int32, shape, dim)


def _div_pow2(x, d):
    assert d & (d - 1) == 0
    return lax.shift_right_logical(x, d.bit_length() - 1)


def _same_segment(shape, row_seg, col_seg):
    return _div_pow2(_iota(shape, 0), row_seg) == _div_pow2(_iota(shape, 1), col_seg)


def _segment_matrix(n, seg, value):
    return jnp.where(_same_segment((n, n), seg, seg), value, 0.0).astype(BF16)


def _head_lane_mask(width, head_width, h):
    lane = _iota((1, width), 1)
    return (lane >= h * head_width) & (lane < (h + 1) * head_width)


def _stack_heads(x, head_width):
    w = x.shape[1]
    return jnp.concatenate(
        [jnp.where(_head_lane_mask(w, head_width, h), x, 0.0) for h in range(N_HEADS)], axis=0)


def _rms(x, eps):
    return x * lax.rsqrt(jnp.mean(x * x, axis=-1, keepdims=True) + eps)


def _const_spec(shape):
    nd = len(shape)
    return pl.BlockSpec(shape, lambda *_: (0,) * nd)


def _layer_spec(stacked, layer):
    nd = stacked.ndim - 1
    return pl.BlockSpec((None,) + stacked.shape[1:], lambda *_: (layer,) + (0,) * nd)


def _params(sem, vmem=VMEM_LIMIT):
    return pltpu.CompilerParams(dimension_semantics=sem, vmem_limit_bytes=vmem)


def _embed_kernel(x_ref, meta_ref, o_ref):
    first_real = PAD + N_META

    @pl.when(pl.program_id(1) == 0)
    def _():
        o_ref[0:PAD, :] = jnp.zeros((PAD, D_MODEL), o_ref.dtype)
        o_ref[PAD:first_real, :] = meta_ref[...]
        o_ref[first_real:, :] = x_ref[0:o_ref.shape[0] - first_real, :]

    @pl.when(pl.program_id(1) > 0)
    def _():
        o_ref[...] = x_ref[...]


def _embed(x2, meta, batch):
    seq = x2.shape[0] // batch
    first_real = PAD + N_META
    lp = first_real + seq
    tm = ROW_TILE
    window = lambda b, i: (pl.multiple_of(b * seq + jnp.maximum(i * tm - first_real, 0), BLOCK), 0)
    return pl.pallas_call(
        _embed_kernel,
        grid=(batch, lp // tm),
        in_specs=[pl.BlockSpec((pl.Element(tm), pl.Element(D_MODEL)), window), _const_spec(meta.shape)],
        out_specs=pl.BlockSpec((tm, D_MODEL), lambda b, i: (b * (lp // tm) + i, 0)),
        out_shape=jax.ShapeDtypeStruct((batch * lp, D_MODEL), x2.dtype),
        compiler_params=_params(("parallel", "parallel")),
        name="embed",
    )(x2, meta)


def _mla_operands(z, qg, kvg, wq, wkv, cos, sin, seq_tile):
    nq = _rms(z[:, 0:256], EPS) * qg
    nkv = _rms(z[:, 256:384], EPS) * kvg
    ql = _dot(nq, wq)
    scale = (MLA_NOPE + MLA_ROPE) ** -0.5
    q = jnp.concatenate([ql[:, 0:128], ql[:, 128:256] * cos + ql[:, 512:640] * sin,
                         ql[:, 256:384], ql[:, 384:512] * cos + ql[:, 640:768] * sin], axis=1)
    pad_lane = (_iota((1, 512), 1) & 255) == MLA_PAD_LANE
    q = _bf(jnp.where(pad_lane, 1.0, q * scale))
    kvl = _dot(nkv, wkv)
    k_rope = z[:, 384:512] * cos + z[:, 512:640] * sin
    lane = _iota((1, 128), 1)
    is_pad_key = (seq_tile * z.shape[0] + _iota((z.shape[0], 1), 0)) < PAD
    k_rope = jnp.where(lane < 2 * MLA_ROPE, k_rope,
                       jnp.where((lane == MLA_PAD_LANE - 128) & is_pad_key, NEG_INF, 0.0))
    k = _bf(jnp.concatenate([kvl[:, 0:128], k_rope, kvl[:, 128:256], k_rope], axis=1))
    vt = _bf(kvl[:, 256:512].T)
    row_head = _div_pow2(_iota((256, 1), 0), 64)
    vt_heads = jnp.concatenate([jnp.where(row_head == h, vt, jnp.zeros((), BF16)) for h in range(N_HEADS)], axis=1)
    return q, k, vt_heads


def _in_proj_kernel(h_ref, g_ref, w_ref, qg_ref, kvg_ref, wq_ref, wkv_ref, cos_ref, sin_ref,
                    rw_ref, sb_ref, gla_ref, sb_values_ref, q_ref, k_ref, vt_ref, *, tiles_per_seq):
    n = _bf(_rms(h_ref[...], EPS) * g_ref[...])
    rw_ref[...] = _dot_nt(n, w_ref[IN_ROW_RW:IN_ROW_SB, :])
    sb_ref[...] = _bf(_dot_nt(n, w_ref[IN_ROW_SB:IN_ROW_MLA, :]))
    both = _dot_nt(n, w_ref[IN_ROW_MLA:IN_ROW_GLA + W_GLA, :])
    gla_ref[...] = both[:, W_MLA:]
    sb_values_ref[0] = _head_stacked_values(sb_ref[:, 2 * SB_W:3 * SB_W])
    q, k, vt_heads = _mla_operands(both[:, :W_MLA], qg_ref[...], kvg_ref[...], wq_ref[...], wkv_ref[...],
                                   cos_ref[...], sin_ref[...], pl.program_id(0) % tiles_per_seq)
    q_ref[...] = q
    k_ref[...] = k
    vt_ref[0] = vt_heads


def _in_proj(h, g, w_all, qg, kvg, wq, wkv, layer, cos, sin, lp):
    t = h.shape[0]
    tm = ROW_TILE
    nb = lp // tm
    rows = lambda w: pl.BlockSpec((tm, w), lambda i: (i, 0))
    table = pl.BlockSpec((tm, 128), lambda i: (i % nb, 0))
    return pl.pallas_call(
        functools.partial(_in_proj_kernel, tiles_per_seq=nb),
        grid=(t // tm,),
        in_specs=[rows(D_MODEL), _const_spec((1, D_MODEL)), _layer_spec(w_all, layer),
                  _const_spec((1, MLA_Q_RANK)), _const_spec((1, MLA_KV_RANK)),
                  _layer_spec(wq, layer), _layer_spec(wkv, layer), table, table],
        out_specs=[rows(W_RW), rows(W_SB), rows(W_GLA),
                   pl.BlockSpec((1, N_HEADS * tm, SB_W), lambda i: (i, 0, 0)),
                   rows(512), rows(512), pl.BlockSpec((1, 256, N_HEADS * tm), lambda i: (i, 0, 0))],
        out_shape=[jax.ShapeDtypeStruct((t, W_RW), F32), jax.ShapeDtypeStruct((t, W_SB), BF16),
                   jax.ShapeDtypeStruct((t, W_GLA), F32),
                   jax.ShapeDtypeStruct((t // tm, N_HEADS * tm, SB_W), BF16),
                   jax.ShapeDtypeStruct((t, 512), BF16), jax.ShapeDtypeStruct((t, 512), BF16),
                   jax.ShapeDtypeStruct((t // tm, 256, N_HEADS * tm), BF16)],
        compiler_params=_params(("parallel",)),
        name="in_proj",
    )(h, g, w_all, qg, kvg, wq, wkv, cos, sin)


def _head_stacked_values(vb):
    zero = jnp.zeros((), vb.dtype)
    return jnp.concatenate([jnp.where(_head_lane_mask(256, 64, h), vb, zero) for h in range(N_HEADS)], axis=0)


def _per_head_rows(rows, n):
    return jnp.concatenate([jnp.broadcast_to(r, (64, n)) for r in rows], axis=0)


def _attn_kernel(sq_ref, sk_ref, sv_ref, later_ref, mq_ref, mk_ref, mvt_ref, ysb_ref, ymla_ref,
                 sqh_ref, c_ref, sacc_ref, wts_ref, mqh_ref, m_ref, l_ref, macc_ref, p_ref, alpha_ref, *, tile):
    i = pl.program_id(1)
    heads = range(N_HEADS)
    q_row = i * tile + _iota((tile, 1), 0)
    q_pos = i * tile + _iota((1, tile), 1)
    for h in heads:
        sqh_ref[h] = jnp.where(_head_lane_mask(SB_W, SB_HEAD, h), sq_ref[...], jnp.zeros((), BF16))
    lane = _iota((1, 256), 1)
    for h in heads:
        half, slot = h // 2, h % 2
        head_lanes = (((lane >= slot * MLA_NOPE) & (lane < (slot + 1) * MLA_NOPE))
                      | ((lane >= 128 + slot * MLA_ROPE) & (lane < 128 + (slot + 1) * MLA_ROPE))
                      | (lane == MLA_PAD_LANE))
        mqh_ref[half, slot * tile:(slot + 1) * tile, :] = jnp.where(
            head_lanes, mq_ref[:, half * 256:(half + 1) * 256], jnp.zeros((), BF16))
    c_ref[...] = jnp.zeros(c_ref.shape, F32)
    sacc_ref[...] = jnp.zeros(sacc_ref.shape, F32)
    m_ref[...] = jnp.full(m_ref.shape, NEG_INF, F32)
    l_ref[...] = jnp.zeros(l_ref.shape, F32)
    macc_ref[...] = jnp.zeros(macc_ref.shape, F32)

    def add_values(j, slot):
        sacc_ref[...] += jnp.dot(wts_ref[slot], sv_ref[j], preferred_element_type=F32)
        pv = jnp.dot(mvt_ref[j], p_ref[slot], preferred_element_type=F32)
        macc_ref[...] = _per_head_rows([alpha_ref[slot, h] for h in heads], tile) * macc_ref[...] + pv

    def block(j, masked, prev, slot):
        start = pl.multiple_of(j * tile, tile)
        cut = SB_SUFFIX_SPLIT
        z = [_dot_nt(sqh_ref[h], sk_ref[pl.ds(start, tile), :]) for h in heads]
        s_pair = [_dot_nt(mk_ref[pl.ds(start, tile), half * 256:(half + 1) * 256], mqh_ref[half]) for half in (0, 1)]
        if prev is not None:
            add_values(prev, 1 - slot)

        for h in heads:
            s_h = s_pair[h // 2][:, (h % 2) * tile:(h % 2 + 1) * tile]
            if masked:
                s_h = jnp.where((start + _iota((tile, 1), 0)) <= q_pos, s_h, NEG_INF)
            m_old = m_ref[h]
            m_new = jnp.maximum(m_old, jnp.max(s_h, axis=0, keepdims=True))
            p = jnp.exp(s_h - m_new)
            alpha = jnp.exp(m_old - m_new)
            l_ref[h] = alpha * l_ref[h] + jnp.sum(p, axis=0, keepdims=True)
            m_ref[h] = m_new
            p_ref[slot, h * tile:(h + 1) * tile, :] = _bf(p)
            alpha_ref[slot, h] = alpha

        log_take = [jnp.minimum(z[h], 0.0) - jnp.log2(1.0 + jnp.exp2(-jnp.abs(z[h]))) for h in heads]
        log_keep = [log_take[h] - z[h] for h in heads]
        if masked:
            mask = (start + _iota((1, tile), 1)) < q_row
            log_keep = [jnp.where(mask, log_keep[h], 0.0) for h in heads]
        keep16 = [_bf(log_keep[h]) for h in heads]
        later_head = [jnp.dot(keep16[h][:, :cut], later_ref[...], preferred_element_type=F32) for h in heads]
        later_tail = [jnp.dot(keep16[h][:, cut:], later_ref[:tile - cut, :tile - cut], preferred_element_type=F32)
                      for h in heads]

        for h in heads:
            head_sum = jnp.sum(log_keep[h][:, :cut], axis=-1, keepdims=True)
            tail_sum = jnp.sum(log_keep[h][:, cut:], axis=-1, keepdims=True)
            c = c_ref[h]
            later = jnp.concatenate([later_head[h] + tail_sum, later_tail[h]], axis=1)
            w = jnp.exp2(log_take[h] + later + c)
            if masked:
                w = jnp.where(mask, w, 0.0)
            c_ref[h] = c + (head_sum + tail_sum)
            wts_ref[slot, :, h * tile:(h + 1) * tile] = _bf(w)

    block(i, True, None, 0)

    def below_pair(t, carry):
        j = i - 1 - 2 * t
        block(j, False, j + 1, 1)
        block(j - 1, False, j, 0)
        return carry

    lax.fori_loop(0, lax.shift_right_logical(i, 1), below_pair, 0)
    odd = (i & 1) == 1

    @pl.when(odd)
    def _():
        block(0, False, 1, 1)
        add_values(0, 1)

    @pl.when(jnp.logical_not(odd))
    def _():
        add_values(0, 0)
    ysb_ref[...] = _bf(sacc_ref[...])
    ymla_ref[...] = _bf((macc_ref[...] / _per_head_rows([l_ref[h] for h in heads], tile)).T)


def _attn(z_sb, sb_values, later_mat, q, k, vt, lp):
    t = z_sb.shape[0]
    tile = ATTN_TILE
    nb = lp // tile
    tiles = lambda w: pl.BlockSpec((tile, w), lambda b, i: (b * nb + i, 0))
    resident = lambda shape, index: pl.BlockSpec(shape, index, pipeline_mode=pl.Buffered(1))
    return pl.pallas_call(
        functools.partial(_attn_kernel, tile=tile),
        grid=(t // lp, nb),
        in_specs=[tiles(SB_W),
                  pl.BlockSpec((lp, SB_W), lambda b, i: (b, 1)),
                  resident((nb, N_HEADS * tile, SB_W), lambda b, i: (b, 0, 0)),
                  _const_spec((SB_SUFFIX_SPLIT, SB_SUFFIX_SPLIT)),
                  tiles(512),
                  pl.BlockSpec((lp, 512), lambda b, i: (b, 0)),
                  resident((nb, 256, N_HEADS * tile), lambda b, i: (b, 0, 0))],
        out_specs=[tiles(SB_W), tiles(256)],
        out_shape=[jax.ShapeDtypeStruct((t, SB_W), BF16), jax.ShapeDtypeStruct((t, 256), BF16)],
        scratch_shapes=[pltpu.VMEM((N_HEADS, tile, SB_W), BF16), pltpu.VMEM((N_HEADS, tile, 1), F32),
                        pltpu.VMEM((tile, SB_W), F32), pltpu.VMEM((2, tile, N_HEADS * tile), BF16),
                        pltpu.VMEM((2, 2 * tile, 256), BF16), pltpu.VMEM((N_HEADS, 1, tile), F32),
                        pltpu.VMEM((N_HEADS, 1, tile), F32), pltpu.VMEM((256, tile), F32),
                        pltpu.VMEM((2, N_HEADS * tile, tile), BF16), pltpu.VMEM((2, N_HEADS, 1, tile), F32)],
        compiler_params=_params(("parallel", "arbitrary")),
        name="attn",
    )(z_sb, z_sb, sb_values, later_mat, q, k, vt)


def _compact_eye(c):
    return jnp.where(_iota((c, N_HEADS * c), 0) == (_iota((c, N_HEADS * c), 1) & (c - 1)), 1.0, 0.0)


def _rw_prep_kernel(z_ref, zprev_ref, mu_ref, w0_ref, w2_ref, a0_ref, a2_ref, g2_ref, kk_ref, ka_ref, rk_ref,
                    w_o, rt_o, arb_o, kbar_o, bbar_o, v_o, uv_o, yv_o, dec_o, g_o, bonus_o):
    i = pl.program_id(1)
    z = z_ref[...]
    tm = z.shape[0]
    prev = jnp.where(i == 0, 0.0, zprev_ref[7:8, :])
    shifted = jnp.where(_iota((tm, 1), 0) == 0, prev, pltpu.roll(z, 1, 0))
    zz = z + (shifted - z) * mu_ref[...]
    r, k, v = zz[:, 0:256], zz[:, 256:512], zz[:, 512:768]
    lora_in = zz[:, 768:896]
    u = w0_ref[...] + _dot(jnp.tanh(lora_in), w2_ref[...])
    w = jnp.minimum(u, 0.0) - jnp.log1p(jnp.exp(-jnp.abs(u))) - 0.5
    a = jax.nn.sigmoid(a0_ref[...] + _dot(lora_in, a2_ref[...]))
    seg = _segment_matrix(RW_W, 64, 1.0)
    kx = k * kk_ref[...]
    kap = kx / jnp.maximum(jnp.sqrt(_dot_exact_rhs(kx * kx, seg)), 1e-12)
    kmod = k * (1.0 + (a - 1.0) * ka_ref[...])
    beta = kap * a
    lw = -jnp.exp(w)
    g_o[...] = _dot(jax.nn.sigmoid(zz[:, 896:1024]), g2_ref[...])
    bonus_o[...] = _dot_exact_rhs(r * kmod * rk_ref[...], seg) * v

    c = RW_CHUNK
    same_chunk = _same_segment((tm, tm), c, c)
    lw_hi, lw_lo = _split_hi_lo(lw)
    cum = jnp.where(same_chunk & (_iota((tm, tm), 0) >= _iota((tm, tm), 1)), 1.0, 0.0).astype(BF16)
    b = jnp.dot(cum, lw_hi, preferred_element_type=F32) + jnp.dot(cum, lw_lo, preferred_element_type=F32)
    b_end = jnp.concatenate(
        [jnp.broadcast_to(b[n * c + c - 1:(n + 1) * c, :], (c, RW_W)) for n in range(tm // c)], axis=0)
    grow = jnp.exp(-b)
    to_end = jnp.exp(b_end - b)
    kap_t = kap * jnp.exp(b - lw)
    r_t = r * jnp.exp(b)
    beta_g = beta * grow
    k_g = kmod * grow
    rt_o[...] = _bf(r_t)
    kbar_o[...] = _bf(kmod * to_end)
    bbar_o[...] = _bf(beta * to_end)
    v_o[...] = _bf(v)
    decay_end = jnp.exp(b_end)

    lane_pos = _iota((c, N_HEADS * c), 1) & (c - 1)
    strictly_earlier = _iota((c, N_HEADS * c), 0) > lane_pos
    not_later = _iota((c, N_HEADS * c), 0) >= lane_pos
    chunks = [slice(n * c, (n + 1) * c) for n in range(tm // c)]
    pair = [_dot_nt(jnp.concatenate([kap_t[rows], r_t[rows]], axis=0),
                    jnp.concatenate([_stack_heads(beta_g[rows], 64), _stack_heads(k_g[rows], 64)], axis=0))
            for rows in chunks]
    a_kb = [jnp.where(strictly_earlier, p[0:c, 0:4 * c], 0.0) for p in pair]
    a_kk = [jnp.where(strictly_earlier, p[0:c, 4 * c:8 * c], 0.0) for p in pair]
    a_rk = [jnp.where(not_later, p[c:2 * c, 4 * c:8 * c], 0.0) for p in pair]
    for rows, p in zip(chunks, pair):
        arb_o[rows, :] = _bf(jnp.where(not_later, p[c:2 * c, 0:4 * c], 0.0))
    inv = [_compact_eye(c) - a for a in a_kb]
    power = [_dot(a, _stack_heads(a, c)) for a in a_kb]
    span = 2
    while span < c:
        inv = [t + _dot(t, _stack_heads(p, c)) for t, p in zip(inv, power)]
        span *= 2
        if span < c:
            power = [_dot(p, _stack_heads(p, c)) for p in power]
    v_st = [_stack_heads(v[rows], 64) for rows in chunks]
    kk_v = [_dot(a, vs) for a, vs in zip(a_kk, v_st)]
    for n, rows in enumerate(chunks):
        w_o[rows, :] = _bf(_dot(inv[n], _stack_heads(kap_t[rows], 64)))
        uv_o[rows, :] = _dot(inv[n], _stack_heads(kk_v[n], 64))
        yv_o[rows, :] = _dot(a_rk[n], v_st[n])
        dec_o[n * 8:(n + 1) * 8, :] = decay_end[n * c:n * c + 8]


def _rw_prep(z_rw, p, lp):
    t = z_rw.shape[0]
    tm = ROW_TILE
    nb = lp // tm
    row = lambda b, i: (b * nb + i, 0)
    prev = lambda b, i: (jnp.maximum(b * (lp // 8) + i * (tm // 8) - 1, 0), 0)
    consts = [p["rw_mu"], p["rw_w0"], p["rw_w2"], p["rw_a0"], p["rw_a2"], p["rw_g2"],
              p["rw_k_k"], p["rw_k_a"], p["rw_r_k"]]
    return pl.pallas_call(
        _rw_prep_kernel,
        grid=(t // lp, nb),
        in_specs=[pl.BlockSpec((tm, W_RW), row), pl.BlockSpec((8, W_RW), prev)]
        + [_const_spec(c.shape) for c in consts],
        out_specs=[pl.BlockSpec((tm, RW_W), row)] * 8
        + [pl.BlockSpec((tm // RW_CHUNK * 8, RW_W), row)] + [pl.BlockSpec((tm, RW_W), row)] * 2,
        out_shape=[jax.ShapeDtypeStruct((t, RW_W), BF16)] * 6 + [jax.ShapeDtypeStruct((t, RW_W), F32)] * 2
        + [jax.ShapeDtypeStruct((t // RW_CHUNK * 8, RW_W), F32)] + [jax.ShapeDtypeStruct((t, RW_W), F32)] * 2,
        compiler_params=_params(("parallel", "parallel")),
        name="rw_prep",
    )(z_rw, z_rw, *consts)


def _recurrent_kernel(w_ref, rt_ref, arb_ref, kbar_ref, bbar_ref, rv_ref, uv_ref, yv_ref, dec_ref,
                      q_ref, k_ref, v_ref, al_ref, a2_ref, ab_ref, y_ref, o_ref, ht_ref, st_ref):
    c = GLA_CHUNK
    rc = RW_CHUNK
    seqs = range(st_ref.shape[0])

    @pl.when(pl.program_id(0) == 0)
    def _():
        ht_ref[...] = jnp.zeros_like(ht_ref)
        st_ref[...] = jnp.zeros_like(st_ref)

    rw_same_head = _same_segment((RW_W, RW_W), 64, 64)

    def rw_from_state(n, ht):
        rows = slice(n * rc, (n + 1) * rc)
        return [_dot_nt(jnp.concatenate([w_ref[s, rows], rt_ref[s, rows]], axis=0), ht[s]) for s in seqs]

    def rw_finish(n, from_state, ht):
        rows = slice(n * rc, (n + 1) * rc)
        u = [from_state[s][0:rc] + uv_ref[s, rows] for s in seqs]
        from_u = [_dot(arb_ref[s, rows], _stack_heads(u[s], 64)) for s in seqs]
        upd = [_dot_tn(jnp.concatenate([rv_ref[s, rows], _bf(u[s])], axis=0),
                       jnp.concatenate([kbar_ref[s, rows], -bbar_ref[s, rows]], axis=0)) for s in seqs]
        for s in seqs:
            y_ref[s, rows] = from_state[s][rc:2 * rc] + yv_ref[s, rows] - from_u[s]
        return [ht[s] * dec_ref[s, n * 8:n * 8 + 1, :] + jnp.where(rw_same_head, upd[s], 0.0) for s in seqs]

    sub = GLA_SUB
    nsub = c // sub
    row_i = _iota((c, c), 0)
    col_i = _iota((c, c), 1)
    tri = jnp.where(row_i >= col_i, 1.0, 0.0).astype(BF16)
    tri_sub = jnp.where((row_i >= col_i) & _same_segment((c, c), sub, sub), 1.0, 0.0).astype(BF16)
    key_pos = _iota((c, 1), 0)
    query_in_sub = _iota((N_HEADS * sub, c), 0) & (sub - 1)
    gla_same_head = _same_segment((GLA_W, GLA_QK), 64, GLA_DK)

    ht = [ht_ref[s] for s in seqs]
    rw_state_0 = rw_from_state(0, ht)

    x = [_dot(al_ref[s], a2_ref[...]) + ab_ref[...] for s in seqs]
    log_a = [(jnp.minimum(x[s], 0.0) - jnp.log1p(jnp.exp(-jnp.abs(x[s])))) * (1.0 / GLA_TAU) for s in seqs]
    b = [_dot_exact_lhs(tri, log_a[s]) for s in seqs]
    b_sub = [_dot_exact_lhs(tri_sub, log_a[s]) for s in seqs]
    q = [q_ref[s] * (GLA_DK ** -0.5) for s in seqs]
    st = [st_ref[s] for s in seqs]
    inter = [_dot_nt(q[s] * jnp.exp(b[s]), st[s]) for s in seqs]

    ht = rw_finish(0, rw_state_0, ht)
    rw_state_1 = rw_from_state(1, ht)

    scores = []
    for s in seqs:
        q_sub = q[s] * jnp.exp(b_sub[s])
        beta = b[s] - b_sub[s]
        k = k_ref[s]
        blocks = []
        for blk in range(nsub):
            lo, hi = blk * sub, (blk + 1) * sub
            expo = jnp.where(key_pos < hi, beta[lo:lo + 1, :] - b[s], NEG_INF)
            sc = _dot_nt(_stack_heads(q_sub[lo:hi], GLA_DK), k * jnp.exp(expo))
            blocks.append(jnp.where(_iota((N_HEADS * sub, c), 1) <= lo + query_in_sub, sc, 0.0))
        scores.append(jnp.concatenate(blocks, axis=0))

    ht = rw_finish(1, rw_state_1, ht)
    for s in seqs:
        ht_ref[s] = ht[s]

    per_head = [_dot(scores[s], v_ref[s]) for s in seqs]
    upd = [_dot_tn(v_ref[s], k_ref[s] * jnp.exp(b[s][c - 1:c, :] - b[s])) for s in seqs]
    for s in seqs:
        pieces = []
        for blk in range(nsub):
            piece = jnp.zeros((sub, GLA_W), F32)
            for h in range(N_HEADS):
                r0 = (blk * N_HEADS + h) * sub
                piece = piece + jnp.where(_head_lane_mask(GLA_W, 64, h), per_head[s][r0:r0 + sub], 0.0)
            pieces.append(piece)
        o_ref[s] = inter[s] + jnp.concatenate(pieces, axis=0)
        st_ref[s] = st[s] * jnp.exp(b[s][c - 1:c, :]) + jnp.where(gla_same_head, upd[s], 0.0)


def _recurrent(rw_terms, z_gla, a2, ab, batch):
    t = z_gla.shape[0]
    lp = t // batch
    c = GLA_CHUNK
    assert c == 2 * RW_CHUNK
    as_seq = lambda a: a.reshape(batch, a.shape[0] // batch, a.shape[1])
    rows = pl.BlockSpec((batch, c, RW_W), lambda i: (0, i, 0))
    z3 = as_seq(z_gla)
    y, o = pl.pallas_call(
        _recurrent_kernel,
        grid=(lp // c,),
        in_specs=[rows] * 8 + [pl.BlockSpec((batch, 16, RW_W), lambda i: (0, i, 0)),
                               pl.BlockSpec((batch, c, 128), lambda i: (0, i, 0)),
                               pl.BlockSpec((batch, c, 128), lambda i: (0, i, 1)),
                               pl.BlockSpec((batch, c, 256), lambda i: (0, i, 1)),
                               pl.BlockSpec((batch, c, 128), lambda i: (0, i, 6)),
                               _const_spec(a2.shape), _const_spec(ab.shape)],
        out_specs=[rows, pl.BlockSpec((batch, c, GLA_W), lambda i: (0, i, 0))],
        out_shape=[jax.ShapeDtypeStruct((batch, lp, RW_W), F32), jax.ShapeDtypeStruct((batch, lp, GLA_W), F32)],
        scratch_shapes=[pltpu.VMEM((batch, RW_W, RW_W), F32), pltpu.VMEM((batch, GLA_W, GLA_QK), F32)],
        compiler_params=_params(("arbitrary",)),
        name="recurrent",
    )(*map(as_seq, rw_terms), z3, z3, z3, z3, a2, ab)
    return y.reshape(t, RW_W), o.reshape(t, GLA_W)


def _merge_kernel(h_ref, ymla_ref, yrw_ref, bonus_ref, g_ref, ysb_ref, ogla_ref, rgla_ref,
                  nmix_ref, lnw_ref, lnb_ref, gn_ref, gb_ref, wg_ref, wb_ref, wo_ref, o_ref, *, tiles_per_seq):
    tm = h_ref.shape[0]
    avg = _segment_matrix(256, 64, 1.0 / 64)
    h = h_ref[...]
    n = _bf(_rms(h, EPS) * nmix_ref[...])

    y = yrw_ref[...]
    d = y - _dot_exact_rhs(y, avg)
    var = _dot_exact_rhs(d * d, avg)
    y_rw = (d * lax.rsqrt(var + RW_GN_EPS) * lnw_ref[...] + lnb_ref[...] + bonus_ref[...]) * g_ref[...]

    o = ogla_ref[...]
    r = rgla_ref[...]
    y_gla = o * lax.rsqrt(_dot_exact_rhs(o * o, avg) + EPS) * gn_ref[...] * (r * jax.nn.sigmoid(r))

    acc = jnp.zeros((tm, D_MODEL), F32)
    for m, y_m in enumerate((ymla_ref[...], y_rw, ysb_ref[...], y_gla)):
        logits = _dot_nt(n, wg_ref[m * D_MODEL:(m + 1) * D_MODEL, :])
        gate = jax.nn.sigmoid(logits + gb_ref[m:m + 1, :])
        acc = acc + gate * _dot(y_m, wb_ref[m])
    delta = _dot(acc, wo_ref[...])
    row = (pl.program_id(0) % tiles_per_seq) * tm + _iota((tm, 1), 0)
    o_ref[...] = h + jnp.where(row >= PAD, delta, 0.0)


def _merge(h, y_mla, y_rw, bonus, g, y_sb, o_gla, z_gla, p, lp):
    t = h.shape[0]
    tm = ROW_TILE
    row = lambda i: (i, 0)
    w256 = pl.BlockSpec((tm, 256), row)
    consts = [p["norm_mix"], p["rw_ln_w"], p["rw_ln_b"], p["gla_norm"], p["gate_b"]]
    stacked = [p["w_gate"], p["w_branch"], p["w_out"]]
    return pl.pallas_call(
        functools.partial(_merge_kernel, tiles_per_seq=lp // tm),
        grid=(t // tm,),
        in_specs=[pl.BlockSpec((tm, D_MODEL), row), w256, w256, w256, w256, w256, w256,
                  pl.BlockSpec((tm, 256), lambda i: (i, 2))]
        + [_const_spec(c.shape) for c in consts]
        + [_layer_spec(w, p["layer"]) for w in stacked],
        out_specs=pl.BlockSpec((tm, D_MODEL), row),
        out_shape=jax.ShapeDtypeStruct((t, D_MODEL), F32),
        compiler_params=_params(("parallel",)),
        name="merge",
    )(h, y_mla, y_rw, bonus, g, y_sb, o_gla, z_gla, *consts, *stacked)


def _ffn_kernel(h_ref, g_ref, win_ref, cw_ref, cb_ref, wout_ref, o_ref, tail_ref):
    @pl.when(pl.program_id(1) == 0)
    def _():
        tail_ref[...] = jnp.zeros_like(tail_ref)

    x = h_ref[...]
    tm = x.shape[0]
    n = _bf(_rms(x, EPS) * g_ref[...])
    rowi = _iota((tm, 1), 0)
    acc = jnp.zeros((tm, D_MODEL), F32)
    assert sum(FFN_COL_CHUNKS) == D_FF
    for c0, c1 in zip((0, FFN_COL_CHUNKS[0]), (FFN_COL_CHUNKS[0], D_FF)):
        a = jnp.dot(n, win_ref[:, c0:c1], preferred_element_type=F32)
        u = jnp.dot(n, win_ref[:, D_FF + c0:D_FF + c1], preferred_element_type=F32)
        prev1 = tail_ref[7:8, c0:c1]
        prev2 = tail_ref[6:7, c0:c1]
        a1 = jnp.where(rowi == 0, prev1, pltpu.roll(a, 1, 0))
        a2 = jnp.where(rowi == 0, prev2, jnp.where(rowi == 1, prev1, pltpu.roll(a, 2, 0)))
        tail_ref[:, c0:c1] = a[tm - 8:tm, :]
        conv = cb_ref[:, c0:c1] + cw_ref[0:1, c0:c1] * a2 + cw_ref[1:2, c0:c1] * a1 + cw_ref[2:3, c0:c1] * a
        acc = acc + _dot(conv * jax.nn.sigmoid(conv) * u, wout_ref[c0:c1, :])
    o_ref[...] = x + acc


def _ffn(h, g, w_in, conv_w, conv_b, w_out, layer, lp):
    t = h.shape[0]
    tm = ROW_TILE
    nb = lp // tm
    row = lambda b, i: (b * nb + i, 0)
    return pl.pallas_call(
        _ffn_kernel,
        grid=(t // lp, nb),
        in_specs=[pl.BlockSpec((tm, D_MODEL), row), _const_spec((1, D_MODEL)), _layer_spec(w_in, layer),
                  _const_spec(conv_w.shape), _const_spec(conv_b.shape), _layer_spec(w_out, layer)],
        out_specs=pl.BlockSpec((tm, D_MODEL), row),
        out_shape=jax.ShapeDtypeStruct((t, D_MODEL), F32),
        scratch_shapes=[pltpu.VMEM((8, D_FF), F32)],
        compiler_params=_params(("arbitrary", "arbitrary")),
        name="conv_ffn",
    )(h, g, w_in, conv_w, conv_b, w_out)


def _final_norm_kernel(h_ref, g_ref, o_ref):
    o_ref[0] = _rms(h_ref[...], EPS) * g_ref[...]


def _final_norm(h, g, batch, seq):
    lp = h.shape[0] // batch
    rows = FINAL_ROWS
    assert seq % rows == 0
    first = lambda bi, i: (pl.multiple_of(bi * lp + (PAD + N_META) + i * rows, BLOCK), 0)
    return pl.pallas_call(
        _final_norm_kernel,
        grid=(batch, seq // rows),
        in_specs=[pl.BlockSpec((pl.Element(rows), pl.Element(D_MODEL)), first), _const_spec((1, D_MODEL))],
        out_specs=pl.BlockSpec((1, rows, D_MODEL), lambda bi, i: (bi, i, 0)),
        out_shape=jax.ShapeDtypeStruct((batch, seq, D_MODEL), F32),
        compiler_params=_params(("parallel", "parallel")),
        name="final_norm",
    )(h, g)


def _rope_swap(w):
    half = w.shape[-1] // 2
    return jnp.concatenate([-w[..., half:], w[..., :half]], axis=-1)


def _in_weight_layout_kernel(w_ref, o_ref, gate_ref):
    def put(dst, val):
        o_ref[0, dst:dst + val.shape[0], :] = _bf(val)

    mla, sb, gla = IN_ROW_MLA, IN_ROW_SB, IN_ROW_GLA
    put(mla, w_ref[0, 0:384, :])
    half = MLA_ROPE // 2
    kr = w_ref[0, 384:384 + MLA_ROPE, :]
    kr_swapped = jnp.concatenate([-kr[half:], kr[:half]], axis=0)
    put(mla + 384, jnp.concatenate([kr] * 4, axis=0))
    put(mla + 512, jnp.concatenate([kr_swapped] * 4, axis=0))
    put(IN_ROW_RW, w_ref[0, 416:1440, :])
    put(sb, w_ref[0, 1440:1440 + SB_W, :] * (SB_HEAD ** -0.5 * LOG2_E))
    put(sb + SB_W, w_ref[0, 1440 + SB_W:2208, :])
    put(gla, w_ref[0, 2208:2720, :])
    put(gla + 512, w_ref[0, 2736:2992, :])
    lora = w_ref[0, 2720:2736, :]
    put(gla + 768, jnp.concatenate([lora, jnp.zeros((W_GLA - 784, lora.shape[1]), lora.dtype)], axis=0))
    gate_ref[0] = _bf(w_ref[0, 2992:2992 + W_GATE, :])


def _in_weight_layout(w_in):
    wt = jnp.swapaxes(w_in, 1, 2)
    depth, n, d = wt.shape
    cols = 256
    total = sum(IN_GROUP_WIDTHS)
    return pl.pallas_call(
        _in_weight_layout_kernel,
        grid=(depth, d // cols),
        in_specs=[pl.BlockSpec((1, n, cols), lambda l, i: (l, 0, i))],
        out_specs=[pl.BlockSpec((1, total, cols), lambda l, i: (l, 0, i)),
                   pl.BlockSpec((1, W_GATE, cols), lambda l, i: (l, 0, i))],
        out_shape=[jax.ShapeDtypeStruct((depth, total, d), BF16), jax.ShapeDtypeStruct((depth, W_GATE, d), BF16)],
        compiler_params=_params(("parallel", "parallel")),
        name="in_weight_layout",
    )(wt)


def _layout_params(w_in, mla_w_uq, mla_w_ukv, rw_w2, rw_a2, gla_a2):
    w_all, w_gate = _in_weight_layout(w_in)

    depth = w_in.shape[0]
    wuq = mla_w_uq.reshape(depth, MLA_Q_RANK, N_HEADS, MLA_NOPE + MLA_ROPE)
    nope, rope = wuq[..., :MLA_NOPE], wuq[..., MLA_NOPE:]
    rope_sw = _rope_swap(rope)
    zeros64 = jnp.zeros((depth, MLA_Q_RANK, 64), w_in.dtype)
    pair = lambda x, a, b: jnp.concatenate([x[:, :, a], x[:, :, b]], axis=-1)
    rope_pair = lambda x, a, b: jnp.concatenate([x[:, :, a], x[:, :, b], zeros64], axis=-1)
    wq = _bf(jnp.concatenate([pair(nope, 0, 1), rope_pair(rope, 0, 1), pair(nope, 2, 3), rope_pair(rope, 2, 3),
                              rope_pair(rope_sw, 0, 1), rope_pair(rope_sw, 2, 3)], axis=-1))
    wukv = mla_w_ukv.reshape(depth, MLA_KV_RANK, N_HEADS, 128)
    wkv = _bf(jnp.concatenate([wukv[..., :64].reshape(depth, MLA_KV_RANK, 256),
                               wukv[..., 64:].reshape(depth, MLA_KV_RANK, 256)], axis=-1))

    z64 = jnp.zeros_like(rw_w2)
    w2 = _bf(jnp.concatenate([rw_w2, z64], axis=1))
    a2 = _bf(jnp.concatenate([z64, rw_a2], axis=1))
    gla_a2p = _bf(jnp.concatenate([gla_a2, jnp.zeros((depth, 128 - gla_a2.shape[1], GLA_QK), gla_a2.dtype)], axis=1))
    return w_all, w_gate, wq, wkv, w2, a2, gla_a2p


def _rope_tables(lp):
    half = MLA_ROPE // 2
    freqs = ROPE_THETA ** (-jnp.arange(half, dtype=F32) / half)
    pos = (jnp.arange(lp) - PAD).astype(F32)
    ang = pos[:, None] * freqs[None, :]
    return jnp.tile(jnp.cos(ang), (1, 128 // half)), jnp.tile(jnp.sin(ang), (1, 128 // half))


def kernel(x, meta_tokens, norm_mix, w_in, mla_q_norm, mla_w_uq, mla_kv_norm, mla_w_ukv, rw_mu, rw_w0, rw_w2, rw_a0, rw_a2, rw_g2, rw_k_k, rw_k_a, rw_r_k, rw_ln_w, rw_ln_b, gla_a2, gla_a_b, gla_norm, gate_b, w_branch, w_out, norm_ffn, w_ffn_in, ffn_conv_w, ffn_conv_b, w_ffn_out, norm_final):
    batch, seq, _ = x.shape
    depth = w_in.shape[0]
    lp = PAD + N_META + seq
    t = batch * lp
    assert lp % ROW_TILE == 0 and lp % ATTN_TILE == 0 and lp % GLA_CHUNK == 0 and lp % RW_CHUNK == 0

    w_all, w_gate, wq, wkv, rw_w2p, rw_a2p, gla_a2p = _layout_params(
        w_in, mla_w_uq, mla_w_ukv, rw_w2, rw_a2, gla_a2)
    w_branch_b, w_out_b, w_ffn_in_b, w_ffn_out_b, rw_g2_b = map(_bf, (w_branch, w_out, w_ffn_in, w_ffn_out, rw_g2))
    vec = lambda a: a.reshape(depth, 1, -1)
    cos, sin = _rope_tables(lp)
    idx = jnp.arange(SB_SUFFIX_SPLIT)
    later_mat = jnp.where(idx[:, None] > idx[None, :], 1.0, 0.0).astype(BF16)

    h = _embed(x.reshape(batch * seq, D_MODEL), meta_tokens.astype(x.dtype), batch)

    for i in range(depth):
        z_rw, z_sb, z_gla, sb_values, q, k, v = _in_proj(
            h, vec(norm_mix)[i], w_all, vec(mla_q_norm)[i], vec(mla_kv_norm)[i], wq, wkv, i, cos, sin, lp)
        y_sb, y_mla = _attn(z_sb, sb_values, later_mat, q, k, v, lp)
        rw = {"rw_mu": vec(rw_mu)[i], "rw_w0": vec(rw_w0)[i], "rw_w2": rw_w2p[i], "rw_a0": vec(rw_a0)[i],
              "rw_a2": rw_a2p[i], "rw_g2": rw_g2_b[i], "rw_k_k": vec(rw_k_k)[i], "rw_k_a": vec(rw_k_a)[i],
              "rw_r_k": vec(rw_r_k)[i]}
        *chunk_terms, g, bonus = _rw_prep(z_rw, rw, lp)
        y_rw, o_gla = _recurrent(chunk_terms, z_gla, gla_a2p[i], vec(gla_a_b)[i], batch)
        mp = {"rw_ln_w": vec(rw_ln_w)[i], "rw_ln_b": vec(rw_ln_b)[i], "gla_norm": vec(gla_norm)[i],
              "gate_b": gate_b[i], "norm_mix": vec(norm_mix)[i], "w_gate": w_gate, "w_branch": w_branch_b,
              "w_out": w_out_b, "layer": i}
        h = _merge(h, y_mla, y_rw, bonus, g, y_sb, o_gla, z_gla, mp, lp)
        h = _ffn(h, vec(norm_ffn)[i], w_ffn_in_b, ffn_conv_w[i], vec(ffn_conv_b)[i], w_ffn_out_b, i, lp)
    return _final_norm(h, norm_final.reshape(1, D_MODEL), batch, seq)
```

```python
import functools

import jax
import jax.numpy as jnp
from jax import lax
from jax.experimental import pallas as pl
from jax.experimental.pallas import tpu as pltpu

F32 = jnp.float32
BF16 = jnp.bfloat16

D_MODEL = 1024
DEPTH = 4
N_META = 16
BLOCK = 128
PAD = (-N_META) % BLOCK
EPS = 1e-6
NEG_INF = -1e30
LOG2_E = 1.4426950408889634

N_HEADS = 4
MLA_NOPE = 64
MLA_ROPE = 32
MLA_Q_RANK = 256
MLA_KV_RANK = 128
MLA_PAD_LANE = 192
ROPE_THETA = 10000.0

RW_W = 256
RW_GN_EPS = 64e-5
RW_CHUNK = 64

SB_W = 256
SB_HEAD = 64
SB_SUFFIX_SPLIT = 256

GLA_DK = 32
GLA_QK = 128
GLA_W = 256
GLA_TAU = 16.0
GLA_CHUNK = 128
GLA_SUB = 16

D_FF = 2816
FFN_COL_CHUNKS = (1536, 1280)

W_MLA, W_RW, W_SB, W_GLA, W_GATE = 640, 1024, 768, 896, 4096
IN_GROUP_WIDTHS = (W_MLA, W_RW, W_SB, W_GLA)
IN_ROW_RW, IN_ROW_SB, IN_ROW_MLA, IN_ROW_GLA = 0, W_RW, W_RW + W_SB, W_RW + W_SB + W_MLA

ROW_TILE = 384
ATTN_TILE = 384
FINAL_ROWS = 1024
VMEM_LIMIT = 56 * 1024 * 1024


def _bf(x):
    return x.astype(BF16)


def _dot(a, b):
    return jnp.dot(_bf(a), _bf(b), preferred_element_type=F32)


def _dot_nt(a, b):
    return lax.dot_general(_bf(a), _bf(b), (((1,), (1,)), ((), ())), preferred_element_type=F32)


def _dot_tn(a, b):
    return lax.dot_general(_bf(a), _bf(b), (((0,), (0,)), ((), ())), preferred_element_type=F32)


def _split_hi_lo(x):
    hi = _bf(x)
    lo = _bf(x - hi.astype(F32))
    return hi, lo


def _dot_exact_rhs(x, m):
    hi, lo = _split_hi_lo(x)
    return jnp.dot(hi, m, preferred_element_type=F32) + jnp.dot(lo, m, preferred_element_type=F32)


def _dot_exact_lhs(m, x):
    hi, lo = _split_hi_lo(x)
    return jnp.dot(m, hi, preferred_element_type=F32) + jnp.dot(m, lo, preferred_element_type=F32)


def _iota(shape, dim):
    return lax.broadcasted_iota(jnp.int32, shape, dim)


def _div_pow2(x, d):
    assert d & (d - 1) == 0
    return lax.shift_right_logical(x, d.bit_length() - 1)


def _same_segment(shape, row_seg, col_seg):
    return _div_pow2(_iota(shape, 0), row_seg) == _div_pow2(_iota(shape, 1), col_seg)


def _segment_matrix(n, seg, value):
    return jnp.where(_same_segment((n, n), seg, seg), value, 0.0).astype(BF16)


def _head_lane_mask(width, head_width, h):
    lane = _iota((1, width), 1)
    return (lane >= h * head_width) & (lane < (h + 1) * head_width)


def _stack_heads(x, head_width):
    w = x.shape[1]
    return jnp.concatenate(
        [jnp.where(_head_lane_mask(w, head_width, h), x, 0.0) for h in range(N_HEADS)], axis=0)


def _rms(x, eps):
    return x * lax.rsqrt(jnp.mean(x * x, axis=-1, keepdims=True) + eps)


def _const_spec(shape):
    nd = len(shape)
    return pl.BlockSpec(shape, lambda *_: (0,) * nd)


def _layer_spec(stacked, layer):
    nd = stacked.ndim - 1
    return pl.BlockSpec((None,) + stacked.shape[1:], lambda *_: (layer,) + (0,) * nd)


def _params(sem, vmem=VMEM_LIMIT):
    return pltpu.CompilerParams(dimension_semantics=sem, vmem_limit_bytes=vmem)


def _embed_kernel(x_ref, meta_ref, o_ref):
    first_real = PAD + N_META

    @pl.when(pl.program_id(1) == 0)
    def _():
        o_ref[0:PAD, :] = jnp.zeros((PAD, D_MODEL), o_ref.dtype)
        o_ref[PAD:first_real, :] = meta_ref[...]
        o_ref[first_real:, :] = x_ref[0:o_ref.shape[0] - first_real, :]

    @pl.when(pl.program_id(1) > 0)
    def _():
        o_ref[...] = x_ref[...]


def _embed(x2, meta, batch):
    seq = x2.shape[0] // batch
    first_real = PAD + N_META
    lp = first_real + seq
    tm = ROW_TILE
    window = lambda b, i: (pl.multiple_of(b * seq + jnp.maximum(i * tm - first_real, 0), BLOCK), 0)
    return pl.pallas_call(
        _embed_kernel,
        grid=(batch, lp // tm),
        in_specs=[pl.BlockSpec((pl.Element(tm), pl.Element(D_MODEL)), window), _const_spec(meta.shape)],
        out_specs=pl.BlockSpec((tm, D_MODEL), lambda b, i: (b * (lp // tm) + i, 0)),
        out_shape=jax.ShapeDtypeStruct((batch * lp, D_MODEL), x2.dtype),
        compiler_params=_params(("parallel", "parallel")),
        name="embed",
    )(x2, meta)


def _mla_operands(z, qg, kvg, wq, wkv, cos, sin, seq_tile):
    nq = _rms(z[:, 0:256], EPS) * qg
    nkv = _rms(z[:, 256:384], EPS) * kvg
    ql = _dot(nq, wq)
    scale = (MLA_NOPE + MLA_ROPE) ** -0.5
    q = jnp.concatenate([ql[:, 0:128], ql[:, 128:256] * cos + ql[:, 512:640] * sin,
                         ql[:, 256:384], ql[:, 384:512] * cos + ql[:, 640:768] * sin], axis=1)
    pad_lane = (_iota((1, 512), 1) & 255) == MLA_PAD_LANE
    q = _bf(jnp.where(pad_lane, 1.0, q * scale))
    kvl = _dot(nkv, wkv)
    k_rope = z[:, 384:512] * cos + z[:, 512:640] * sin
    lane = _iota((1, 128), 1)
    is_pad_key = (seq_tile * z.shape[0] + _iota((z.shape[0], 1), 0)) < PAD
    k_rope = jnp.where(lane < 2 * MLA_ROPE, k_rope,
                       jnp.where((lane == MLA_PAD_LANE - 128) & is_pad_key, NEG_INF, 0.0))
    k = _bf(jnp.concatenate([kvl[:, 0:128], k_rope, kvl[:, 128:256], k_rope], axis=1))
    vt = _bf(kvl[:, 256:512].T)
    row_head = _div_pow2(_iota((256, 1), 0), 64)
    vt_heads = jnp.concatenate([jnp.where(row_head == h, vt, jnp.zeros((), BF16)) for h in range(N_HEADS)], axis=1)
    return q, k, vt_heads


def _in_proj_kernel(h_ref, g_ref, w_ref, qg_ref, kvg_ref, wq_ref, wkv_ref, cos_ref, sin_ref,
                    rw_ref, sb_ref, gla_ref, sb_values_ref, q_ref, k_ref, vt_ref, *, tiles_per_seq):
    n = _bf(_rms(h_ref[...], EPS) * g_ref[...])
    rw_ref[...] = _dot_nt(n, w_ref[IN_ROW_RW:IN_ROW_SB, :])
    sb_ref[...] = _bf(_dot_nt(n, w_ref[IN_ROW_SB:IN_ROW_MLA, :]))
    both = _dot_nt(n, w_ref[IN_ROW_MLA:IN_ROW_GLA + W_GLA, :])
    gla_ref[...] = both[:, W_MLA:]
    sb_values_ref[0] = _head_stacked_values(sb_ref[:, 2 * SB_W:3 * SB_W])
    q, k, vt_heads = _mla_operands(both[:, :W_MLA], qg_ref[...], kvg_ref[...], wq_ref[...], wkv_ref[...],
                                   cos_ref[...], sin_ref[...], pl.program_id(0) % tiles_per_seq)
    q_ref[...] = q
    k_ref[...] = k
    vt_ref[0] = vt_heads


def _in_proj(h, g, w_all, qg, kvg, wq, wkv, layer, cos, sin, lp):
    t = h.shape[0]
    tm = ROW_TILE
    nb = lp // tm
    rows = lambda w: pl.BlockSpec((tm, w), lambda i: (i, 0))
    table = pl.BlockSpec((tm, 128), lambda i: (i % nb, 0))
    return pl.pallas_call(
        functools.partial(_in_proj_kernel, tiles_per_seq=nb),
        grid=(t // tm,),
        in_specs=[rows(D_MODEL), _const_spec((1, D_MODEL)), _layer_spec(w_all, layer),
                  _const_spec((1, MLA_Q_RANK)), _const_spec((1, MLA_KV_RANK)),
                  _layer_spec(wq, layer), _layer_spec(wkv, layer), table, table],
        out_specs=[rows(W_RW), rows(W_SB), rows(W_GLA),
                   pl.BlockSpec((1, N_HEADS * tm, SB_W), lambda i: (i, 0, 0)),
                   rows(512), rows(512), pl.BlockSpec((1, 256, N_HEADS * tm), lambda i: (i, 0, 0))],
        out_shape=[jax.ShapeDtypeStruct((t, W_RW), F32), jax.ShapeDtypeStruct((t, W_SB), BF16),
                   jax.ShapeDtypeStruct((t, W_GLA), F32),
                   jax.ShapeDtypeStruct((t // tm, N_HEADS * tm, SB_W), BF16),
                   jax.ShapeDtypeStruct((t, 512), BF16), jax.ShapeDtypeStruct((t, 512), BF16),
                   jax.ShapeDtypeStruct((t // tm, 256, N_HEADS * tm), BF16)],
        compiler_params=_params(("parallel",)),
        name="in_proj",
    )(h, g, w_all, qg, kvg, wq, wkv, cos, sin)


def _head_stacked_values(vb):
    zero = jnp.zeros((), vb.dtype)
    return jnp.concatenate([jnp.where(_head_lane_mask(256, 64, h), vb, zero) for h in range(N_HEADS)], axis=0)


def _per_head_rows(rows, n):
    return jnp.concatenate([jnp.broadcast_to(r, (64, n)) for r in rows], axis=0)


def _attn_kernel(sq_ref, sk_ref, sv_ref, later_ref, mq_ref, mk_ref, mvt_ref, ysb_ref, ymla_ref,
                 sqh_ref, c_ref, sacc_ref, wts_ref, mqh_ref, m_ref, l_ref, macc_ref, p_ref, alpha_ref, *, tile):
    i = pl.program_id(1)
    heads = range(N_HEADS)
    q_row = i * tile + _iota((tile, 1), 0)
    q_pos = i * tile + _iota((1, tile), 1)
    for h in heads:
        sqh_ref[h] = jnp.where(_head_lane_mask(SB_W, SB_HEAD, h), sq_ref[...], jnp.zeros((), BF16))
    lane = _iota((1, 256), 1)
    for h in heads:
        half, slot = h // 2, h % 2
        head_lanes = (((lane >= slot * MLA_NOPE) & (lane < (slot + 1) * MLA_NOPE))
                      | ((lane >= 128 + slot * MLA_ROPE) & (lane < 128 + (slot + 1) * MLA_ROPE))
                      | (lane == MLA_PAD_LANE))
        mqh_ref[half, slot * tile:(slot + 1) * tile, :] = jnp.where(
            head_lanes, mq_ref[:, half * 256:(half + 1) * 256], jnp.zeros((), BF16))
    c_ref[...] = jnp.zeros(c_ref.shape, F32)
    sacc_ref[...] = jnp.zeros(sacc_ref.shape, F32)
    m_ref[...] = jnp.full(m_ref.shape, NEG_INF, F32)
    l_ref[...] = jnp.zeros(l_ref.shape, F32)
    macc_ref[...] = jnp.zeros(macc_ref.shape, F32)

    def add_values(j, slot):
        sacc_ref[...] += jnp.dot(wts_ref[slot], sv_ref[j], preferred_element_type=F32)
        pv = jnp.dot(mvt_ref[j], p_ref[slot], preferred_element_type=F32)
        macc_ref[...] = _per_head_rows([alpha_ref[slot, h] for h in heads], tile) * macc_ref[...] + pv

    def block(j, masked, prev, slot):
        start = pl.multiple_of(j * tile, tile)
        cut = SB_SUFFIX_SPLIT
        z = [_dot_nt(sqh_ref[h], sk_ref[pl.ds(start, tile), :]) for h in heads]
        s_pair = [_dot_nt(mk_ref[pl.ds(start, tile), half * 256:(half + 1) * 256], mqh_ref[half]) for half in (0, 1)]
        if prev is not None:
            add_values(prev, 1 - slot)

        for h in heads:
            s_h = s_pair[h // 2][:, (h % 2) * tile:(h % 2 + 1) * tile]
            if masked:
                s_h = jnp.where((start + _iota((tile, 1), 0)) <= q_pos, s_h, NEG_INF)
            m_old = m_ref[h]
            m_new = jnp.maximum(m_old, jnp.max(s_h, axis=0, keepdims=True))
            p = jnp.exp(s_h - m_new)
            alpha = jnp.exp(m_old - m_new)
            l_ref[h] = alpha * l_ref[h] + jnp.sum(p, axis=0, keepdims=True)
            m_ref[h] = m_new
            p_ref[slot, h * tile:(h + 1) * tile, :] = _bf(p)
            alpha_ref[slot, h] = alpha

        log_take = [jnp.minimum(z[h], 0.0) - jnp.log2(1.0 + jnp.exp2(-jnp.abs(z[h]))) for h in heads]
        log_keep = [log_take[h] - z[h] for h in heads]
        if masked:
            mask = (start + _iota((1, tile), 1)) < q_row
            log_keep = [jnp.where(mask, log_keep[h], 0.0) for h in heads]
        keep16 = [_bf(log_keep[h]) for h in heads]
        later_head = [jnp.dot(keep16[h][:, :cut], later_ref[...], preferred_element_type=F32) for h in heads]
        later_tail = [jnp.dot(keep16[h][:, cut:], later_ref[:tile - cut, :tile - cut], preferred_element_type=F32)
                      for h in heads]

        for h in heads:
            head_sum = jnp.sum(log_keep[h][:, :cut], axis=-1, keepdims=True)
            tail_sum = jnp.sum(log_keep[h][:, cut:], axis=-1, keepdims=True)
            c = c_ref[h]
            later = jnp.concatenate([later_head[h] + tail_sum, later_tail[h]], axis=1)
            w = jnp.exp2(log_take[h] + later + c)
            if masked:
                w = jnp.where(mask, w, 0.0)
            c_ref[h] = c + (head_sum + tail_sum)
            wts_ref[slot, :, h * tile:(h + 1) * tile] = _bf(w)

    block(i, True, None, 0)

    def below_pair(t, carry):
        j = i - 1 - 2 * t
        block(j, False, j + 1, 1)
        block(j - 1, False, j, 0)
        return carry

    lax.fori_loop(0, lax.shift_right_logical(i, 1), below_pair, 0)
    odd = (i & 1) == 1

    @pl.when(odd)
    def _():
        block(0, False, 1, 1)
        add_values(0, 1)

    @pl.when(jnp.logical_not(odd))
    def _():
        add_values(0, 0)
    ysb_ref[...] = _bf(sacc_ref[...])
    ymla_ref[...] = _bf((macc_ref[...] / _per_head_rows([l_ref[h] for h in heads], tile)).T)


def _attn(z_sb, sb_values, later_mat, q, k, vt, lp):
    t = z_sb.shape[0]
    tile = ATTN_TILE
    nb = lp // tile
    tiles = lambda w: pl.BlockSpec((tile, w), lambda b, i: (b * nb + i, 0))
    resident = lambda shape, index: pl.BlockSpec(shape, index, pipeline_mode=pl.Buffered(1))
    return pl.pallas_call(
        functools.partial(_attn_kernel, tile=tile),
        grid=(t // lp, nb),
        in_specs=[tiles(SB_W),
                  pl.BlockSpec((lp, SB_W), lambda b, i: (b, 1)),
                  resident((nb, N_HEADS * tile, SB_W), lambda b, i: (b, 0, 0)),
                  _const_spec((SB_SUFFIX_SPLIT, SB_SUFFIX_SPLIT)),
                  tiles(512),
                  pl.BlockSpec((lp, 512), lambda b, i: (b, 0)),
                  resident((nb, 256, N_HEADS * tile), lambda b, i: (b, 0, 0))],
        out_specs=[tiles(SB_W), tiles(256)],
        out_shape=[jax.ShapeDtypeStruct((t, SB_W), BF16), jax.ShapeDtypeStruct((t, 256), BF16)],
        scratch_shapes=[pltpu.VMEM((N_HEADS, tile, SB_W), BF16), pltpu.VMEM((N_HEADS, tile, 1), F32),
                        pltpu.VMEM((tile, SB_W), F32), pltpu.VMEM((2, tile, N_HEADS * tile), BF16),
                        pltpu.VMEM((2, 2 * tile, 256), BF16), pltpu.VMEM((N_HEADS, 1, tile), F32),
                        pltpu.VMEM((N_HEADS, 1, tile), F32), pltpu.VMEM((256, tile), F32),
                        pltpu.VMEM((2, N_HEADS * tile, tile), BF16), pltpu.VMEM((2, N_HEADS, 1, tile), F32)],
        compiler_params=_params(("parallel", "arbitrary")),
        name="attn",
    )(z_sb, z_sb, sb_values, later_mat, q, k, vt)


def _compact_eye(c):
    return jnp.where(_iota((c, N_HEADS * c), 0) == (_iota((c, N_HEADS * c), 1) & (c - 1)), 1.0, 0.0)


def _rw_prep_kernel(z_ref, zprev_ref, mu_ref, w0_ref, w2_ref, a0_ref, a2_ref, g2_ref, kk_ref, ka_ref, rk_ref,
                    w_o, rt_o, arb_o, kbar_o, bbar_o, v_o, uv_o, yv_o, dec_o, g_o, bonus_o):
    i = pl.program_id(1)
    z = z_ref[...]
    tm = z.shape[0]
    prev = jnp.where(i == 0, 0.0, zprev_ref[7:8, :])
    shifted = jnp.where(_iota((tm, 1), 0) == 0, prev, pltpu.roll(z, 1, 0))
    zz = z + (shifted - z) * mu_ref[...]
    r, k, v = zz[:, 0:256], zz[:, 256:512], zz[:, 512:768]
    lora_in = zz[:, 768:896]
    u = w0_ref[...] + _dot(jnp.tanh(lora_in), w2_ref[...])
    w = jnp.minimum(u, 0.0) - jnp.log1p(jnp.exp(-jnp.abs(u))) - 0.5
    a = jax.nn.sigmoid(a0_ref[...] + _dot(lora_in, a2_ref[...]))
    seg = _segment_matrix(RW_W, 64, 1.0)
    kx = k * kk_ref[...]
    kap = kx / jnp.maximum(jnp.sqrt(_dot_exact_rhs(kx * kx, seg)), 1e-12)
    kmod = k * (1.0 + (a - 1.0) * ka_ref[...])
    beta = kap * a
    lw = -jnp.exp(w)
    g_o[...] = _dot(jax.nn.sigmoid(zz[:, 896:1024]), g2_ref[...])
    bonus_o[...] = _dot_exact_rhs(r * kmod * rk_ref[...], seg) * v

    c = RW_CHUNK
    same_chunk = _same_segment((tm, tm), c, c)
    lw_hi, lw_lo = _split_hi_lo(lw)
    cum = jnp.where(same_chunk & (_iota((tm, tm), 0) >= _iota((tm, tm), 1)), 1.0, 0.0).astype(BF16)
    b = jnp.dot(cum, lw_hi, preferred_element_type=F32) + jnp.dot(cum, lw_lo, preferred_element_type=F32)
    b_end = jnp.concatenate(
        [jnp.broadcast_to(b[n * c + c - 1:(n + 1) * c, :], (c, RW_W)) for n in range(tm // c)], axis=0)
    grow = jnp.exp(-b)
    to_end = jnp.exp(b_end - b)
    kap_t = kap * jnp.exp(b - lw)
    r_t = r * jnp.exp(b)
    beta_g = beta * grow
    k_g = kmod * grow
    rt_o[...] = _bf(r_t)
    kbar_o[...] = _bf(kmod * to_end)
    bbar_o[...] = _bf(beta * to_end)
    v_o[...] = _bf(v)
    decay_end = jnp.exp(b_end)

    lane_pos = _iota((c, N_HEADS * c), 1) & (c - 1)
    strictly_earlier = _iota((c, N_HEADS * c), 0) > lane_pos
    not_later = _iota((c, N_HEADS * c), 0) >= lane_pos
    chunks = [slice(n * c, (n + 1) * c) for n in range(tm // c)]
    pair = [_dot_nt(jnp.concatenate([kap_t[rows], r_t[rows]], axis=0),
                    jnp.concatenate([_stack_heads(beta_g[rows], 64), _stack_heads(k_g[rows], 64)], axis=0))
            for rows in chunks]
    a_kb = [jnp.where(strictly_earlier, p[0:c, 0:4 * c], 0.0) for p in pair]
    a_kk = [jnp.where(strictly_earlier, p[0:c, 4 * c:8 * c], 0.0) for p in pair]
    a_rk = [jnp.where(not_later, p[c:2 * c, 4 * c:8 * c], 0.0) for p in pair]
    for rows, p in zip(chunks, pair):
        arb_o[rows, :] = _bf(jnp.where(not_later, p[c:2 * c, 0:4 * c], 0.0))
    inv = [_compact_eye(c) - a for a in a_kb]
    power = [_dot(a, _stack_heads(a, c)) for a in a_kb]
    span = 2
    while span < c:
        inv = [t + _dot(t, _stack_heads(p, c)) for t, p in zip(inv, power)]
        span *= 2
        if span < c:
            power = [_dot(p, _stack_heads(p, c)) for p in power]
    v_st = [_stack_heads(v[rows], 64) for rows in chunks]
    kk_v = [_dot(a, vs) for a, vs in zip(a_kk, v_st)]
    for n, rows in enumerate(chunks):
        w_o[rows, :] = _bf(_dot(inv[n], _stack_heads(kap_t[rows], 64)))
        uv_o[rows, :] = _dot(inv[n], _stack_heads(kk_v[n], 64))
        yv_o[rows, :] = _dot(a_rk[n], v_st[n])
        dec_o[n * 8:(n + 1) * 8, :] = decay_end[n * c:n * c + 8]


def _rw_prep(z_rw, p, lp):
    t = z_rw.shape[0]
    tm = ROW_TILE
    nb = lp // tm
    row = lambda b, i: (b * nb + i, 0)
    prev = lambda b, i: (jnp.maximum(b * (lp // 8) + i * (tm // 8) - 1, 0), 0)
    consts = [p["rw_mu"], p["rw_w0"], p["rw_w2"], p["rw_a0"], p["rw_a2"], p["rw_g2"],
              p["rw_k_k"], p["rw_k_a"], p["rw_r_k"]]
    return pl.pallas_call(
        _rw_prep_kernel,
        grid=(t // lp, nb),
        in_specs=[pl.BlockSpec((tm, W_RW), row), pl.BlockSpec((8, W_RW), prev)]
        + [_const_spec(c.shape) for c in consts],
        out_specs=[pl.BlockSpec((tm, RW_W), row)] * 8
        + [pl.BlockSpec((tm // RW_CHUNK * 8, RW_W), row)] + [pl.BlockSpec((tm, RW_W), row)] * 2,
        out_shape=[jax.ShapeDtypeStruct((t, RW_W), BF16)] * 6 + [jax.ShapeDtypeStruct((t, RW_W), F32)] * 2
        + [jax.ShapeDtypeStruct((t // RW_CHUNK * 8, RW_W), F32)] + [jax.ShapeDtypeStruct((t, RW_W), F32)] * 2,
        compiler_params=_params(("parallel", "parallel")),
        name="rw_prep",
    )(z_rw, z_rw, *consts)


def _recurrent_kernel(w_ref, rt_ref, arb_ref, kbar_ref, bbar_ref, rv_ref, uv_ref, yv_ref, dec_ref,
                      q_ref, k_ref, v_ref, al_ref, a2_ref, ab_ref, y_ref, o_ref, ht_ref, st_ref):
    c = GLA_CHUNK
    rc = RW_CHUNK
    seqs = range(st_ref.shape[0])

    @pl.when(pl.program_id(0) == 0)
    def _():
        ht_ref[...] = jnp.zeros_like(ht_ref)
        st_ref[...] = jnp.zeros_like(st_ref)

    rw_same_head = _same_segment((RW_W, RW_W), 64, 64)

    def rw_from_state(n, ht):
        rows = slice(n * rc, (n + 1) * rc)
        return [_dot_nt(jnp.concatenate([w_ref[s, rows], rt_ref[s, rows]], axis=0), ht[s]) for s in seqs]

    def rw_finish(n, from_state, ht):
        rows = slice(n * rc, (n + 1) * rc)
        u = [from_state[s][0:rc] + uv_ref[s, rows] for s in seqs]
        from_u = [_dot(arb_ref[s, rows], _stack_heads(u[s], 64)) for s in seqs]
        upd = [_dot_tn(jnp.concatenate([rv_ref[s, rows], _bf(u[s])], axis=0),
                       jnp.concatenate([kbar_ref[s, rows], -bbar_ref[s, rows]], axis=0)) for s in seqs]
        for s in seqs:
            y_ref[s, rows] = from_state[s][rc:2 * rc] + yv_ref[s, rows] - from_u[s]
        return [ht[s] * dec_ref[s, n * 8:n * 8 + 1, :] + jnp.where(rw_same_head, upd[s], 0.0) for s in seqs]

    sub = GLA_SUB
    nsub = c // sub
    row_i = _iota((c, c), 0)
    col_i = _iota((c, c), 1)
    tri = jnp.where(row_i >= col_i, 1.0, 0.0).astype(BF16)
    tri_sub = jnp.where((row_i >= col_i) & _same_segment((c, c), sub, sub), 1.0, 0.0).astype(BF16)
    key_pos = _iota((c, 1), 0)
    query_in_sub = _iota((N_HEADS * sub, c), 0) & (sub - 1)
    gla_same_head = _same_segment((GLA_W, GLA_QK), 64, GLA_DK)

    ht = [ht_ref[s] for s in seqs]
    rw_state_0 = rw_from_state(0, ht)

    x = [_dot(al_ref[s], a2_ref[...]) + ab_ref[...] for s in seqs]
    log_a = [(jnp.minimum(x[s], 0.0) - jnp.log1p(jnp.exp(-jnp.abs(x[s])))) * (1.0 / GLA_TAU) for s in seqs]
    b = [_dot_exact_lhs(tri, log_a[s]) for s in seqs]
    b_sub = [_dot_exact_lhs(tri_sub, log_a[s]) for s in seqs]
    q = [q_ref[s] * (GLA_DK ** -0.5) for s in seqs]
    st = [st_ref[s] for s in seqs]
    inter = [_dot_nt(q[s] * jnp.exp(b[s]), st[s]) for s in seqs]

    ht = rw_finish(0, rw_state_0, ht)
    rw_state_1 = rw_from_state(1, ht)

    scores = []
    for s in seqs:
        q_sub = q[s] * jnp.exp(b_sub[s])
        beta = b[s] - b_sub[s]
        k = k_ref[s]
        blocks = []
        for blk in range(nsub):
            lo, hi = blk * sub, (blk + 1) * sub
            expo = jnp.where(key_pos < hi, beta[lo:lo + 1, :] - b[s], NEG_INF)
            sc = _dot_nt(_stack_heads(q_sub[lo:hi], GLA_DK), k * jnp.exp(expo))
            blocks.append(jnp.where(_iota((N_HEADS * sub, c), 1) <= lo + query_in_sub, sc, 0.0))
        scores.append(jnp.concatenate(blocks, axis=0))

    ht = rw_finish(1, rw_state_1, ht)
    for s in seqs:
        ht_ref[s] = ht[s]

    per_head = [_dot(scores[s], v_ref[s]) for s in seqs]
    upd = [_dot_tn(v_ref[s], k_ref[s] * jnp.exp(b[s][c - 1:c, :] - b[s])) for s in seqs]
    for s in seqs:
        pieces = []
        for blk in range(nsub):
            piece = jnp.zeros((sub, GLA_W), F32)
            for h in range(N_HEADS):
                r0 = (blk * N_HEADS + h) * sub
                piece = piece + jnp.where(_head_lane_mask(GLA_W, 64, h), per_head[s][r0:r0 + sub], 0.0)
            pieces.append(piece)
        o_ref[s] = inter[s] + jnp.concatenate(pieces, axis=0)
        st_ref[s] = st[s] * jnp.exp(b[s][c - 1:c, :]) + jnp.where(gla_same_head, upd[s], 0.0)


def _recurrent(rw_terms, z_gla, a2, ab, batch):
    t = z_gla.shape[0]
    lp = t // batch
    c = GLA_CHUNK
    assert c == 2 * RW_CHUNK
    as_seq = lambda a: a.reshape(batch, a.shape[0] // batch, a.shape[1])
    rows = pl.BlockSpec((batch, c, RW_W), lambda i: (0, i, 0))
    z3 = as_seq(z_gla)
    y, o = pl.pallas_call(
        _recurrent_kernel,
        grid=(lp // c,),
        in_specs=[rows] * 8 + [pl.BlockSpec((batch, 16, RW_W), lambda i: (0, i, 0)),
                               pl.BlockSpec((batch, c, 128), lambda i: (0, i, 0)),
                               pl.BlockSpec((batch, c, 128), lambda i: (0, i, 1)),
                               pl.BlockSpec((batch, c, 256), lambda i: (0, i, 1)),
                               pl.BlockSpec((batch, c, 128), lambda i: (0, i, 6)),
                               _const_spec(a2.shape), _const_spec(ab.shape)],
        out_specs=[rows, pl.BlockSpec((batch, c, GLA_W), lambda i: (0, i, 0))],
        out_shape=[jax.ShapeDtypeStruct((batch, lp, RW_W), F32), jax.ShapeDtypeStruct((batch, lp, GLA_W), F32)],
        scratch_shapes=[pltpu.VMEM((batch, RW_W, RW_W), F32), pltpu.VMEM((batch, GLA_W, GLA_QK), F32)],
        compiler_params=_params(("arbitrary",)),
        name="recurrent",
    )(*map(as_seq, rw_terms), z3, z3, z3, z3, a2, ab)
    return y.reshape(t, RW_W), o.reshape(t, GLA_W)


def _merge_kernel(h_ref, ymla_ref, yrw_ref, bonus_ref, g_ref, ysb_ref, ogla_ref, rgla_ref,
                  nmix_ref, lnw_ref, lnb_ref, gn_ref, gb_ref, wg_ref, wb_ref, wo_ref, o_ref, *, seq_tile):
    tm = h_ref.shape[0]
    avg = _segment_matrix(256, 64, 1.0 / 64)
    h = h_ref[...]
    n = _bf(_rms(h, EPS) * nmix_ref[...])

    y = yrw_ref[...]
    d = y - _dot_exact_rhs(y, avg)
    var = _dot_exact_rhs(d * d, avg)
    y_rw = (d * lax.rsqrt(var + RW_GN_EPS) * lnw_ref[...] + lnb_ref[...] + bonus_ref[...]) * g_ref[...]

    o = ogla_ref[...]
    r = rgla_ref[...]
    y_gla = o * lax.rsqrt(_dot_exact_rhs(o * o, avg) + EPS) * gn_ref[...] * (r * jax.nn.sigmoid(r))

    acc = jnp.zeros((tm, D_MODEL), F32)
    for m, y_m in enumerate((ymla_ref[...], y_rw, ysb_ref[...], y_gla)):
        logits = _dot_nt(n, wg_ref[m * D_MODEL:(m + 1) * D_MODEL, :])
        gate = jax.nn.sigmoid(logits + gb_ref[m:m + 1, :])
        acc = acc + gate * _dot(y_m, wb_ref[m])
    delta = _dot(acc, wo_ref[...])
    row = seq_tile * tm + _iota((tm, 1), 0)
    o_ref[...] = h + jnp.where(row >= PAD, delta, 0.0)


def _ffn_kernel(h_ref, g_ref, win_ref, cw_ref, cb_ref, wout_ref, o_ref, tail_ref):
    @pl.when(pl.program_id(1) == 0)
    def _():
        tail_ref[...] = jnp.zeros_like(tail_ref)

    x = h_ref[...]
    tm = x.shape[0]
    n = _bf(_rms(x, EPS) * g_ref[...])
    rowi = _iota((tm, 1), 0)
    acc = jnp.zeros((tm, D_MODEL), F32)
    assert sum(FFN_COL_CHUNKS) == D_FF
    for c0, c1 in zip((0, FFN_COL_CHUNKS[0]), (FFN_COL_CHUNKS[0], D_FF)):
        a = jnp.dot(n, win_ref[:, c0:c1], preferred_element_type=F32)
        u = jnp.dot(n, win_ref[:, D_FF + c0:D_FF + c1], preferred_element_type=F32)
        prev1 = tail_ref[7:8, c0:c1]
        prev2 = tail_ref[6:7, c0:c1]
        a1 = jnp.where(rowi == 0, prev1, pltpu.roll(a, 1, 0))
        a2 = jnp.where(rowi == 0, prev2, jnp.where(rowi == 1, prev1, pltpu.roll(a, 2, 0)))
        tail_ref[:, c0:c1] = a[tm - 8:tm, :]
        conv = cb_ref[:, c0:c1] + cw_ref[0:1, c0:c1] * a2 + cw_ref[1:2, c0:c1] * a1 + cw_ref[2:3, c0:c1] * a
        acc = acc + _dot(conv * jax.nn.sigmoid(conv) * u, wout_ref[c0:c1, :])
    o_ref[...] = x + acc


N_MERGE_INPUTS = 16


def _merge_ffn_kernel(*refs):
    merge_in, ffn_consts = refs[:N_MERGE_INPUTS], refs[N_MERGE_INPUTS:N_MERGE_INPUTS + 5]
    o_ref, mid_ref, tail_ref = refs[N_MERGE_INPUTS + 5:]
    _merge_kernel(*merge_in, mid_ref, seq_tile=pl.program_id(1))
    _ffn_kernel(mid_ref, *ffn_consts, o_ref, tail_ref)


def _merge_ffn(h, y_mla, y_rw, bonus, g, y_sb, o_gla, z_gla, p, g_ffn, w_in, conv_w, conv_b, w_out, lp):
    t = h.shape[0]
    tm = ROW_TILE
    nb = lp // tm
    layer = p["layer"]
    row = lambda b, i: (b * nb + i, 0)
    w256 = pl.BlockSpec((tm, 256), row)
    consts = [p["norm_mix"], p["rw_ln_w"], p["rw_ln_b"], p["gla_norm"], p["gate_b"]]
    resident = lambda w: pl.BlockSpec((None,) + w.shape[1:], lambda *_: (layer,) + (0,) * (w.ndim - 1),
                                      pipeline_mode=pl.Buffered(1))
    return pl.pallas_call(
        _merge_ffn_kernel,
        grid=(t // lp, nb),
        in_specs=[pl.BlockSpec((tm, D_MODEL), row), w256, w256, w256, w256, w256, w256,
                  pl.BlockSpec((tm, 256), lambda b, i: (b * nb + i, 2))]
        + [_const_spec(c.shape) for c in consts]
        + [resident(p["w_gate"]), resident(p["w_branch"]), resident(p["w_out"])]
        + [_const_spec((1, D_MODEL)), resident(w_in), _const_spec(conv_w.shape), _const_spec(conv_b.shape),
           resident(w_out)],
        out_specs=pl.BlockSpec((tm, D_MODEL), row),
        out_shape=jax.ShapeDtypeStruct((t, D_MODEL), F32),
        scratch_shapes=[pltpu.VMEM((tm, D_MODEL), F32), pltpu.VMEM((8, D_FF), F32)],
        compiler_params=_params(("arbitrary", "arbitrary")),
        name="merge_ffn",
    )(h, y_mla, y_rw, bonus, g, y_sb, o_gla, z_gla, *consts, p["w_gate"], p["w_branch"], p["w_out"],
      g_ffn, w_in, conv_w, conv_b, w_out)


def _final_norm_kernel(h_ref, g_ref, o_ref):
    o_ref[0] = _rms(h_ref[...], EPS) * g_ref[...]


def _final_norm(h, g, batch, seq):
    lp = h.shape[0] // batch
    rows = FINAL_ROWS
    assert seq % rows == 0
    first = lambda bi, i: (pl.multiple_of(bi * lp + (PAD + N_META) + i * rows, BLOCK), 0)
    return pl.pallas_call(
        _final_norm_kernel,
        grid=(batch, seq // rows),
        in_specs=[pl.BlockSpec((pl.Element(rows), pl.Element(D_MODEL)), first), _const_spec((1, D_MODEL))],
        out_specs=pl.BlockSpec((1, rows, D_MODEL), lambda bi, i: (bi, i, 0)),
        out_shape=jax.ShapeDtypeStruct((batch, seq, D_MODEL), F32),
        compiler_params=_params(("parallel", "parallel")),
        name="final_norm",
    )(h, g)


def _rope_swap(w):
    half = w.shape[-1] // 2
    return jnp.concatenate([-w[..., half:], w[..., :half]], axis=-1)


def _in_weight_layout_kernel(w_ref, o_ref, gate_ref):
    def put(dst, val):
        o_ref[0, dst:dst + val.shape[0], :] = _bf(val)

    mla, sb, gla = IN_ROW_MLA, IN_ROW_SB, IN_ROW_GLA
    put(mla, w_ref[0, 0:384, :])
    half = MLA_ROPE // 2
    kr = w_ref[0, 384:384 + MLA_ROPE, :]
    kr_swapped = jnp.concatenate([-kr[half:], kr[:half]], axis=0)
    put(mla + 384, jnp.concatenate([kr] * 4, axis=0))
    put(mla + 512, jnp.concatenate([kr_swapped] * 4, axis=0))
    put(IN_ROW_RW, w_ref[0, 416:1440, :])
    put(sb, w_ref[0, 1440:1440 + SB_W, :] * (SB_HEAD ** -0.5 * LOG2_E))
    put(sb + SB_W, w_ref[0, 1440 + SB_W:2208, :])
    put(gla, w_ref[0, 2208:2720, :])
    put(gla + 512, w_ref[0, 2736:2992, :])
    lora = w_ref[0, 2720:2736, :]
    put(gla + 768, jnp.concatenate([lora, jnp.zeros((W_GLA - 784, lora.shape[1]), lora.dtype)], axis=0))
    gate_ref[0] = _bf(w_ref[0, 2992:2992 + W_GATE, :])


def _in_weight_layout(w_in):
    wt = jnp.swapaxes(w_in, 1, 2)
    depth, n, d = wt.shape
    cols = 256
    total = sum(IN_GROUP_WIDTHS)
    return pl.pallas_call(
        _in_weight_layout_kernel,
        grid=(depth, d // cols),
        in_specs=[pl.BlockSpec((1, n, cols), lambda l, i: (l, 0, i))],
        out_specs=[pl.BlockSpec((1, total, cols), lambda l, i: (l, 0, i)),
                   pl.BlockSpec((1, W_GATE, cols), lambda l, i: (l, 0, i))],
        out_shape=[jax.ShapeDtypeStruct((depth, total, d), BF16), jax.ShapeDtypeStruct((depth, W_GATE, d), BF16)],
        compiler_params=_params(("parallel", "parallel")),
        name="in_weight_layout",
    )(wt)


def _layout_params(w_in, mla_w_uq, mla_w_ukv, rw_w2, rw_a2, gla_a2):
    w_all, w_gate = _in_weight_layout(w_in)

    depth = w_in.shape[0]
    wuq = mla_w_uq.reshape(depth, MLA_Q_RANK, N_HEADS, MLA_NOPE + MLA_ROPE)
    nope, rope = wuq[..., :MLA_NOPE], wuq[..., MLA_NOPE:]
    rope_sw = _rope_swap(rope)
    zeros64 = jnp.zeros((depth, MLA_Q_RANK, 64), w_in.dtype)
    pair = lambda x, a, b: jnp.concatenate([x[:, :, a], x[:, :, b]], axis=-1)
    rope_pair = lambda x, a, b: jnp.concatenate([x[:, :, a], x[:, :, b], zeros64], axis=-1)
    wq = _bf(jnp.concatenate([pair(nope, 0, 1), rope_pair(rope, 0, 1), pair(nope, 2, 3), rope_pair(rope, 2, 3),
                              rope_pair(rope_sw, 0, 1), rope_pair(rope_sw, 2, 3)], axis=-1))
    wukv = mla_w_ukv.reshape(depth, MLA_KV_RANK, N_HEADS, 128)
    wkv = _bf(jnp.concatenate([wukv[..., :64].reshape(depth, MLA_KV_RANK, 256),
                               wukv[..., 64:].reshape(depth, MLA_KV_RANK, 256)], axis=-1))

    z64 = jnp.zeros_like(rw_w2)
    w2 = _bf(jnp.concatenate([rw_w2, z64], axis=1))
    a2 = _bf(jnp.concatenate([z64, rw_a2], axis=1))
    gla_a2p = _bf(jnp.concatenate([gla_a2, jnp.zeros((depth, 128 - gla_a2.shape[1], GLA_QK), gla_a2.dtype)], axis=1))
    return w_all, w_gate, wq, wkv, w2, a2, gla_a2p


def _rope_tables(lp):
    half = MLA_ROPE // 2
    freqs = ROPE_THETA ** (-jnp.arange(half, dtype=F32) / half)
    pos = (jnp.arange(lp) - PAD).astype(F32)
    ang = pos[:, None] * freqs[None, :]
    return jnp.tile(jnp.cos(ang), (1, 128 // half)), jnp.tile(jnp.sin(ang), (1, 128 // half))


def kernel(x, meta_tokens, norm_mix, w_in, mla_q_norm, mla_w_uq, mla_kv_norm, mla_w_ukv, rw_mu, rw_w0, rw_w2, rw_a0, rw_a2, rw_g2, rw_k_k, rw_k_a, rw_r_k, rw_ln_w, rw_ln_b, gla_a2, gla_a_b, gla_norm, gate_b, w_branch, w_out, norm_ffn, w_ffn_in, ffn_conv_w, ffn_conv_b, w_ffn_out, norm_final):
    batch, seq, _ = x.shape
    depth = w_in.shape[0]
    lp = PAD + N_META + seq
    t = batch * lp
    assert lp % ROW_TILE == 0 and lp % ATTN_TILE == 0 and lp % GLA_CHUNK == 0 and lp % RW_CHUNK == 0

    w_all, w_gate, wq, wkv, rw_w2p, rw_a2p, gla_a2p = _layout_params(
        w_in, mla_w_uq, mla_w_ukv, rw_w2, rw_a2, gla_a2)
    w_branch_b, w_out_b, w_ffn_in_b, w_ffn_out_b, rw_g2_b = map(_bf, (w_branch, w_out, w_ffn_in, w_ffn_out, rw_g2))
    vec = lambda a: a.reshape(depth, 1, -1)
    cos, sin = _rope_tables(lp)
    idx = jnp.arange(SB_SUFFIX_SPLIT)
    later_mat = jnp.where(idx[:, None] > idx[None, :], 1.0, 0.0).astype(BF16)

    h = _embed(x.reshape(batch * seq, D_MODEL), meta_tokens.astype(x.dtype), batch)

    for i in range(depth):
        z_rw, z_sb, z_gla, sb_values, q, k, v = _in_proj(
            h, vec(norm_mix)[i], w_all, vec(mla_q_norm)[i], vec(mla_kv_norm)[i], wq, wkv, i, cos, sin, lp)
        y_sb, y_mla = _attn(z_sb, sb_values, later_mat, q, k, v, lp)
        rw = {"rw_mu": vec(rw_mu)[i], "rw_w0": vec(rw_w0)[i], "rw_w2": rw_w2p[i], "rw_a0": vec(rw_a0)[i],
              "rw_a2": rw_a2p[i], "rw_g2": rw_g2_b[i], "rw_k_k": vec(rw_k_k)[i], "rw_k_a": vec(rw_k_a)[i],
              "rw_r_k": vec(rw_r_k)[i]}
        *chunk_terms, g, bonus = _rw_prep(z_rw, rw, lp)
        y_rw, o_gla = _recurrent(chunk_terms, z_gla, gla_a2p[i], vec(gla_a_b)[i], batch)
        mp = {"rw_ln_w": vec(rw_ln_w)[i], "rw_ln_b": vec(rw_ln_b)[i], "gla_norm": vec(gla_norm)[i],
              "gate_b": gate_b[i], "norm_mix": vec(norm_mix)[i], "w_gate": w_gate, "w_branch": w_branch_b,
              "w_out": w_out_b, "layer": i}
        h = _merge_ffn(h, y_mla, y_rw, bonus, g, y_sb, o_gla, z_gla, mp, vec(norm_ffn)[i], w_ffn_in_b,
                       ffn_conv_w[i], vec(ffn_conv_b)[i], w_ffn_out_b, lp)
    return _final_norm(h, norm_final.reshape(1, D_MODEL), batch, seq)
```
